```python
import math
import jax
import jax.numpy as jnp
from jax import lax
import numpy as np

D_MODEL = 1024
BATCH = 2
SEQ = 8192
DEPTH = 1

CTX_LEN = 256
GRID_W = 64
POS_BASE = 10000.0
NORM_EPS = 1e-6
N_ADA = 6

HY_WIDTH = 512
HY_ORDER = 2
HY_PROJ = (HY_ORDER + 1) * HY_WIDTH
HY_SHORT = 3
HY_EMB = 33
HY_BANDS = (HY_EMB - 1) // 2
HY_FILTER_HIDDEN = 64
HY_N_FILTERS = HY_ORDER * 2 * HY_WIDTH
HY_DECAY_TARGET = 1e-2
HY_FAST_PCT = 0.3
HY_SLOW_PCT = 1.5
HY_MIN_DECAY = math.log(HY_DECAY_TARGET) / HY_SLOW_PCT
HY_MAX_DECAY = math.log(HY_DECAY_TARGET) / HY_FAST_PCT

HG_HEADS = 4
HG_DK = 128
HG_DV = 128
HG_KW = HG_HEADS * HG_DK
HG_WIDTH = HG_HEADS * HG_DV
HG_CHUNK = 64

IN_SIZES = (HY_PROJ, HG_KW, HG_WIDTH, HG_KW, HG_KW, HG_WIDTH, D_MODEL, D_MODEL)
IN_COLS = sum(IN_SIZES)
IN_SPLITS = tuple(int(v) for v in np.cumsum(IN_SIZES)[:-1])

N_EXPERTS = 256
TOP_K = 8
N_GROUPS = 8
TOPK_GROUPS = 4
EXPERT_HIDDEN = 256
SHARED_HIDDEN = 256
ROUTED_SCALE = 2.5
MOE_BLOCK = 128

kernel_name = 'hyena_hgrn2_moe_prefix_dit_block'


def rmsnorm(x, g):
    xf = x.astype(jnp.float32)
    y = xf * lax.rsqrt(jnp.mean(xf * xf, axis=-1, keepdims=True) + NORM_EPS)
    return (y * g.astype(jnp.float32)).astype(x.dtype)


def modulate(h, shift, scale):
    return h * (1 + scale) + shift


def grid_pos_embed(rows, cols, dim):
    quarter = dim // 4
    omega = 1.0 / (POS_BASE ** (jnp.arange(quarter, dtype=jnp.float32) / quarter))
    ang_r = jnp.arange(rows, dtype=jnp.float32)[:, None] * omega
    ang_c = jnp.arange(cols, dtype=jnp.float32)[:, None] * omega
    emb_r = jnp.concatenate([jnp.sin(ang_r), jnp.cos(ang_r)], axis=-1)
    emb_c = jnp.concatenate([jnp.sin(ang_c), jnp.cos(ang_c)], axis=-1)
    emb = jnp.concatenate([jnp.broadcast_to(emb_r[:, None], (rows, cols, dim // 2)),
                           jnp.broadcast_to(emb_c[None], (rows, cols, dim // 2))], axis=-1)
    return emb.reshape(rows * cols, dim)


def short_conv(u, w, b):
    L = u.shape[1]
    half = HY_SHORT // 2
    up = jnp.pad(u, ((0, 0), (half, half), (0, 0)))
    return sum(up[:, j:j + L] * w[j] for j in range(HY_SHORT)) + b


def hyena_filters(L, w1, b1, w2, b2, w3, b3, w4, freq):
    f32 = lambda a: a.astype(jnp.float32)
    t = jnp.arange(L, dtype=jnp.float32)
    tn = t / max(L - 1, 1)
    ang = (2 * math.pi / L) * t[:, None] * jnp.linspace(1e-4, HY_BANDS - 1, HY_BANDS, dtype=jnp.float32)
    z = jnp.concatenate([tn[:, None], jnp.cos(ang), -jnp.sin(ang)], axis=-1)
    fr = f32(freq)
    a = jnp.sin(fr * (z @ f32(w1) + f32(b1)))
    a = jnp.sin(fr * (a @ f32(w2) + f32(b2)))
    a = jnp.sin(fr * (a @ f32(w3) + f32(b3)))
    h = (a @ f32(w4)).reshape(L, HY_ORDER, 2, HY_WIDTH)
    deltas = jnp.abs(jnp.linspace(HY_MIN_DECAY, HY_MAX_DECAY, HY_WIDTH, dtype=jnp.float32))
    return h * jnp.exp(-tn[:, None] * deltas)[:, None, None, :]


def long_conv(u, h_fwd, h_bwd, skip):
    L = u.shape[1]
    kern = jnp.concatenate([h_fwd, jnp.zeros_like(h_fwd[:1]), h_bwd[:0:-1]], axis=0)
    kern = kern / jnp.sum(jnp.abs(kern), axis=0, keepdims=True)
    uf = u.astype(jnp.float32)
    spec = jnp.fft.rfft(uf, n=2 * L, axis=1) * jnp.fft.rfft(kern, n=2 * L, axis=0)[None]
    y = jnp.fft.irfft(spec, n=2 * L, axis=1)[:, :L]
    return (y + uf * skip.astype(jnp.float32)).astype(u.dtype)


def hyena_branch(p_hy, conv_w, conv_b, filt, skip):
    u = short_conv(p_hy, conv_w, conv_b)
    v, x1, x2 = jnp.split(u, HY_ORDER + 1, axis=-1)
    z = x1 * long_conv(v, filt[:, 0, 0], filt[:, 0, 1], skip[0])
    return x2 * long_conv(z, filt[:, 1, 0], filt[:, 1, 1], skip[1])


def hgrn2_chunk_scan(q, k, v, logf, s0):
    B, L, H, DK = q.shape
    DV = v.shape[-1]
    C = HG_CHUNK
    n = L // C
    q, k, v, logf = [a.reshape(B, n, C, H, a.shape[-1]) for a in (q, k, v, logf)]
    b = jnp.cumsum(logf, axis=2)
    b_end = b[:, :, -1]
    b_ref = b[:, :, C // 2 - 1][:, :, None]
    qd = q * jnp.exp(b - b_ref)
    kd = k * jnp.exp(b_ref - b)
    tri = jnp.tril(jnp.ones((C, C), dtype=bool))
    att = jnp.where(tri, jnp.einsum('bnthd,bnshd->bnhts', qd, kd), 0.0)
    o_intra = jnp.einsum('bnhts,bnshe->bnthe', att, v)
    delta = jnp.einsum('bnshd,bnshe->bnhde', k * jnp.exp(b_end[:, :, None] - b), v)
    decay = jnp.exp(b_end)

    def step(S, xs):
        dec, dlt = xs
        return dec[..., None] * S + dlt, S

    s_fin, s_start = lax.scan(step, s0, (jnp.moveaxis(decay, 1, 0), jnp.moveaxis(delta, 1, 0)))
    o_inter = jnp.einsum('bnthd,nbhde->bnthe', q * jnp.exp(b), s_start)
    return (o_intra + o_inter).reshape(B, L, H, DV), s_fin


def hgrn2_branch(q_raw, i_raw, ff_raw, fb_raw, g_raw, lb_f, lb_b, norm_g, s0_f, s0_b):
    B, L, _ = q_raw.shape
    f32 = lambda a: a.astype(jnp.float32)
    q = jax.nn.silu(f32(q_raw)).reshape(B, L, HG_HEADS, HG_DK)
    v = f32(i_raw).reshape(B, L, HG_HEADS, HG_DV)

    def gates(f_raw, lb):
        f = (lb + (1 - lb) * jax.nn.sigmoid(f32(f_raw))).reshape(B, L, HG_HEADS, HG_DK)
        return 1 - f, jnp.log(f)

    k_f, lf_f = gates(ff_raw, lb_f)
    k_b, lf_b = gates(fb_raw, lb_b)
    rev = lambda a: jnp.flip(a, axis=1)
    o_f, s_f = hgrn2_chunk_scan(q, k_f, v, lf_f, s0_f)
    o_b, s_b = hgrn2_chunk_scan(rev(q), rev(k_b), rev(v), rev(lf_b), s0_b)
    o = o_f + rev(o_b)
    o = o * lax.rsqrt(jnp.mean(o * o, axis=-1, keepdims=True) + NORM_EPS) * f32(norm_g)
    o = o.reshape(B, L, HG_WIDTH) * jax.nn.silu(f32(g_raw))
    return o.astype(q_raw.dtype), s_f, s_b


def merge_branches(y_hy, y_hg, gate_hy, gate_hg, w_hy_out, w_hg_out, w_out):
    m = jax.nn.sigmoid(gate_hy) * (y_hy @ w_hy_out) + jax.nn.sigmoid(gate_hg) * (y_hg @ w_hg_out)
    return m @ w_out


def swiglu(t, wg, wu, wd):
    return (jax.nn.silu(t @ wg) * (t @ wu)) @ wd


def moe_ffn(h, router_w, router_bias, w_gate, w_up, w_down, sh_gate, sh_up, sh_down):
    Bh, Lh, D = h.shape
    t = h.reshape(-1, D)
    T = t.shape[0]
    scores = jax.nn.sigmoid(t.astype(jnp.float32) @ router_w.astype(jnp.float32))
    biased = scores + router_bias.astype(jnp.float32)
    grp_score = lax.top_k(biased.reshape(T, N_GROUPS, -1), 2)[0].sum(-1)
    _, gidx = lax.top_k(grp_score, TOPK_GROUPS)
    gmask = jnp.any(gidx[..., None] == jnp.arange(N_GROUPS), axis=1)
    emask = jnp.repeat(gmask, N_EXPERTS // N_GROUPS, axis=1)
    _, eidx = lax.top_k(jnp.where(emask, biased, -jnp.inf), TOP_K)
    wsel = jnp.take_along_axis(scores, eidx, axis=1)
    wsel = wsel / jnp.sum(wsel, axis=-1, keepdims=True) * ROUTED_SCALE

    A = T * TOP_K
    eid = eidx.reshape(-1)
    tok = jnp.repeat(jnp.arange(T, dtype=jnp.int32), TOP_K)
    order = jnp.argsort(eid)
    eid_s = eid[order]
    counts = jnp.bincount(eid, length=N_EXPERTS)
    padded = (counts + MOE_BLOCK - 1) // MOE_BLOCK * MOE_BLOCK
    starts = jnp.cumsum(counts) - counts
    pad_ends = jnp.cumsum(padded)
    pad_starts = pad_ends - padded
    dest = pad_starts[eid_s] + jnp.arange(A) - starts[eid_s]
    n_blocks = (A + N_EXPERTS * (MOE_BLOCK - 1) + MOE_BLOCK - 1) // MOE_BLOCK
    P = n_blocks * MOE_BLOCK
    tok_buf = jnp.full((P,), T, dtype=jnp.int32).at[dest].set(tok[order])
    w_buf = jnp.zeros((P,), jnp.float32).at[dest].set(wsel.reshape(-1)[order])
    block_e = jnp.minimum(jnp.searchsorted(pad_ends, jnp.arange(n_blocks) * MOE_BLOCK, side='right'),
                          N_EXPERTS - 1)
    t_pad = jnp.concatenate([t, jnp.zeros((1, D), t.dtype)], axis=0)

    def block_fn(args):
        e, toks, wb = args
        xb = t_pad[toks]
        return swiglu(xb, w_gate[e], w_up[e], w_down[e]) * wb[:, None].astype(xb.dtype)

    y = lax.map(block_fn, (block_e, tok_buf.reshape(n_blocks, MOE_BLOCK), w_buf.reshape(n_blocks, MOE_BLOCK)))
    routed = jax.ops.segment_sum(y.reshape(P, D), tok_buf, num_segments=T + 1)[:T]
    out = routed + swiglu(t, sh_gate, sh_up, sh_down)
    return out.reshape(Bh, Lh, D)


def setup_inputs(seed: int = 0) -> dict:
    key = jax.random.key(seed)
    ks = iter(jax.random.split(key, 40))
    nrm = lambda shape, scale: jax.random.normal(next(ks), shape, jnp.float32) * scale
    D = D_MODEL
    NL = DEPTH
    F = EXPERT_HIDDEN
    FS = SHARED_HIDDEN
    HF = HY_FILTER_HIDDEN
    return {
        'x': nrm((BATCH, SEQ, D), 1.0),
        'c': nrm((BATCH, D), 1.0),
        'ctx': nrm((BATCH, CTX_LEN, D), 1.0),
        'c_ctx': nrm((D,), 1.0),
        'norm1_g': 1.0 + nrm((NL, D), 0.02),
        'norm2_g': 1.0 + nrm((NL, D), 0.02),
        'ada_w': nrm((NL, D, N_ADA * D), 0.5 * D ** -0.5),
        'ada_b': nrm((NL, N_ADA * D), 0.02),
        'w_in': nrm((NL, D, IN_COLS), D ** -0.5),
        'hy_conv_w': nrm((NL, HY_SHORT, HY_PROJ), HY_SHORT ** -0.5),
        'hy_conv_b': nrm((NL, HY_PROJ), 0.02),
        'hy_f_w1': nrm((NL, HY_EMB, HF), HY_EMB ** -0.5),
        'hy_f_b1': nrm((NL, HF), 0.02),
        'hy_f_w2': nrm((NL, HF, HF), HF ** -0.5),
        'hy_f_b2': nrm((NL, HF), 0.02),
        'hy_f_w3': nrm((NL, HF, HF), HF ** -0.5),
        'hy_f_b3': nrm((NL, HF), 0.02),
        'hy_f_w4': nrm((NL, HF, HY_N_FILTERS), HF ** -0.5),
        'hy_f_freq': 1.0 + nrm((NL, HF), 0.02),
        'hy_skip': nrm((NL, HY_ORDER, HY_WIDTH), 1.0),
        'hg_lb_logits': nrm((NL + 1, 2, HG_KW), 0.1),
        'hg_norm_g': 1.0 + nrm((NL, HG_DV), 0.02),
        'w_hy_out': nrm((NL, HY_WIDTH, D), HY_WIDTH ** -0.5),
        'w_hg_out': nrm((NL, HG_WIDTH, D), HG_WIDTH ** -0.5),
        'w_out': nrm((NL, D, D), D ** -0.5),
        'router_w': nrm((NL, D, N_EXPERTS), D ** -0.5),
        'router_bias': nrm((NL, N_EXPERTS), 0.01),
        'exp_w_gate': nrm((NL, N_EXPERTS, D, F), D ** -0.5),
        'exp_w_up': nrm((NL, N_EXPERTS, D, F), D ** -0.5),
        'exp_w_down': nrm((NL, N_EXPERTS, F, D), F ** -0.5),
        'sh_w_gate': nrm((NL, D, FS), D ** -0.5),
        'sh_w_up': nrm((NL, D, FS), D ** -0.5),
        'sh_w_down': nrm((NL, FS, D), FS ** -0.5),
        'final_g': 1.0 + nrm((D,), 0.02),
    }


def reference(x, c, ctx, c_ctx, norm1_g, norm2_g, ada_w, ada_b, w_in, hy_conv_w, hy_conv_b,
              hy_f_w1, hy_f_b1, hy_f_w2, hy_f_b2, hy_f_w3, hy_f_b3, hy_f_w4, hy_f_freq, hy_skip,
              hg_lb_logits, hg_norm_g, w_hy_out, w_hg_out, w_out, router_w, router_bias,
              exp_w_gate, exp_w_up, exp_w_down, sh_w_gate, sh_w_up, sh_w_down, final_g):
    B, S, D = x.shape
    rows = S // GRID_W
    x = x + grid_pos_embed(rows, GRID_W, D).astype(x.dtype)
    xc = ctx
    n_ctx = xc.shape[1]
    s_lat = jax.nn.silu(c)
    s_ctx = jax.nn.silu(c_ctx)
    lbs = jnp.cumsum(jax.nn.softmax(hg_lb_logits.astype(jnp.float32), axis=0), axis=0)
    zero_state = jnp.zeros((B, HG_HEADS, HG_DK, HG_DV), jnp.float32)

    for l in range(DEPTH):
        last = l == DEPTH - 1
        sh1, sc1, g1, sh2, sc2, g2 = jnp.split((s_lat @ ada_w[l] + ada_b[l])[:, None, :], N_ADA, axis=-1)
        csh1, csc1, cg1, csh2, csc2, cg2 = jnp.split(s_ctx @ ada_w[l] + ada_b[l], N_ADA, axis=-1)
        filt_w = (hy_f_w1[l], hy_f_b1[l], hy_f_w2[l], hy_f_b2[l], hy_f_w3[l], hy_f_b3[l], hy_f_w4[l], hy_f_freq[l])

        p = jnp.split(modulate(rmsnorm(x, norm1_g[l]), sh1, sc1) @ w_in[l], IN_SPLITS, axis=-1)
        pc = jnp.split(modulate(rmsnorm(xc, norm1_g[l]), csh1, csc1) @ w_in[l], IN_SPLITS, axis=-1)
        lb_f, lb_b = lbs[l, 0], lbs[l, 1]
        hg_c, st_f, st_b = hgrn2_branch(*pc[1:6], lb_f, lb_b, hg_norm_g[l], zero_state, zero_state)
        hg_x, _, _ = hgrn2_branch(*p[1:6], lb_f, lb_b, hg_norm_g[l], st_f, st_b)
        hy_x = hyena_branch(p[0], hy_conv_w[l], hy_conv_b[l], hyena_filters(S, *filt_w), hy_skip[l])
        x = x + g1 * merge_branches(hy_x, hg_x, p[6], p[7], w_hy_out[l], w_hg_out[l], w_out[l])
        if not last:
            hy_c = hyena_branch(pc[0], hy_conv_w[l], hy_conv_b[l], hyena_filters(n_ctx, *filt_w), hy_skip[l])
            xc = xc + cg1 * merge_branches(hy_c, hg_c, pc[6], pc[7], w_hy_out[l], w_hg_out[l], w_out[l])

        moe_w = (router_w[l], router_bias[l], exp_w_gate[l], exp_w_up[l], exp_w_down[l],
                 sh_w_gate[l], sh_w_up[l], sh_w_down[l])
        h2 = modulate(rmsnorm(x, norm2_g[l]), sh2, sc2)
        if last:
            x = x + g2 * moe_ffn(h2, *moe_w)
        else:
            hc2 = modulate(rmsnorm(xc, norm2_g[l]), csh2, csc2)
            y = moe_ffn(jnp.concatenate([hc2, h2], axis=1), *moe_w)
            xc = xc + cg2 * y[:, :n_ctx]
            x = x + g2 * y[:, n_ctx:]

    return rmsnorm(x, final_g)
```

```python
import functools
import math

import numpy as np
import jax
import jax.numpy as jnp
from jax import lax
from jax.experimental import pallas as pl
from jax.experimental.pallas import tpu as pltpu

F32 = jnp.float32
BF16 = jnp.bfloat16
U32 = jnp.uint32
I32 = jnp.int32
HIGHEST = lax.Precision.HIGHEST

GRID_W = 64
POS_BASE = 10000.0
NORM_EPS = 1e-6
N_ADA = 6
HY_ORDER = 2
HY_SHORT = 3
HY_DECAY_TARGET = 1e-2
HY_FAST_PCT = 0.3
HY_SLOW_PCT = 1.5
HG_HEADS = 4
HG_CHUNK = 64
N_GROUPS = 8
TOPK_GROUPS = 4
TOP_K = 8
ROUTED_SCALE = 2.5

LANES = 128
SUBLANES = 8
VMEM_LIMIT = 56 * 1024 * 1024

TOKEN_TILE = 512
HG_TIME_BLOCK = 512
HALO_ROWS = 16
DFT_P = 128
DFT_GROUP = 8
MOE_ROWS = 256
DMA_TILE = 256


def _cparams(*sem):
    return pltpu.CompilerParams(dimension_semantics=sem, vmem_limit_bytes=VMEM_LIMIT)


def _dot(a, b):
    return jnp.dot(a, b, preferred_element_type=F32)


def _dot_hi(a, b):
    return jnp.dot(a, b, preferred_element_type=F32, precision=HIGHEST)


def _silu(x):
    return x * jax.nn.sigmoid(x)


def _ada_kernel(c_ref, w_ref, b_ref, o_ref):
    o_ref[...] = _dot_hi(_silu(c_ref[...]), w_ref[...]) + b_ref[...]


def ada_vectors(c_rows, ada_w, ada_b):
    r, d = c_rows.shape
    n = ada_w.shape[1]
    bn = 1024
    return pl.pallas_call(
        _ada_kernel,
        out_shape=jax.ShapeDtypeStruct((r, n), F32),
        grid=(n // bn,),
        in_specs=[pl.BlockSpec((r, d), lambda j: (0, 0)),
                  pl.BlockSpec((d, bn), lambda j: (0, j)),
                  pl.BlockSpec((1, bn), lambda j: (0, j))],
        out_specs=pl.BlockSpec((r, bn), lambda j: (0, j)),
        compiler_params=_cparams("arbitrary"),
        name="ada_vectors",
    )(c_rows, ada_w, ada_b.reshape(1, n))


def _inproj_kernel(x_ref, er_ref, ec_ref, g_ref, sh_ref, sc_ref, w_ref, o_ref, *, col_chunk):
    x = x_ref[...]
    rows, gw, d = x.shape
    half = d // 2
    xp = jnp.concatenate([x[:, :, :half] + er_ref[...], x[:, :, half:] + ec_ref[...]], axis=-1)
    xp = xp.reshape(rows * gw, d)
    ms = jnp.mean(xp * xp, axis=-1, keepdims=True)
    y = xp * lax.rsqrt(ms + NORM_EPS) * g_ref[...]
    h = (y * (1.0 + sc_ref[...]) + sh_ref[...]).astype(BF16)
    n = o_ref.shape[1]
    for j in range(n // col_chunk):
        sl = slice(j * col_chunk, (j + 1) * col_chunk)
        o_ref[:, sl] = _dot(h, w_ref[:, sl]).astype(o_ref.dtype)


def in_projection(x, emb_r, emb_c, norm_g, shift, scale, w_bf16, tm):
    b, s, d = x.shape
    n = w_bf16.shape[1]
    rows_per_batch = s // GRID_W
    rt = tm // GRID_W
    tiles_per_batch = rows_per_batch // rt
    x3 = x.reshape(b * rows_per_batch, GRID_W, d)
    col_chunk = 512
    return pl.pallas_call(
        functools.partial(_inproj_kernel, col_chunk=col_chunk),
        out_shape=jax.ShapeDtypeStruct((b * s, n), BF16),
        grid=(b * tiles_per_batch,),
        in_specs=[pl.BlockSpec((rt, GRID_W, d), lambda i: (i, 0, 0)),
                  pl.BlockSpec((rt, 1, d // 2), lambda i: (i % tiles_per_batch, 0, 0)),
                  pl.BlockSpec((GRID_W, d // 2), lambda i: (0, 0)),
                  pl.BlockSpec((1, d), lambda i: (0, 0)),
                  pl.BlockSpec((None, 1, d), lambda i: (i // tiles_per_batch, 0, 0)),
                  pl.BlockSpec((None, 1, d), lambda i: (i // tiles_per_batch, 0, 0)),
                  pl.BlockSpec((d, n), lambda i: (0, 0))],
        out_specs=pl.BlockSpec((tm, n), lambda i: (i, 0)),
        compiler_params=_cparams("arbitrary"),
        name="in_projection",
    )(x3, emb_r, emb_c, norm_g.reshape(1, d), shift, scale, w_bf16)


def _hgrn_kernel(*refs, reverse, n_chunks, final):
    if final:
        (q_ref, i_ref, f_ref, lb_ref, s0_ref, of_ref, gate_ref, ng_ref, o_ref, sfin_ref, s_scr) = refs
    else:
        (q_ref, i_ref, f_ref, lb_ref, s0_ref, o_ref, sfin_ref, s_scr) = refs
    cs = HG_CHUNK
    n_heads = s_scr.shape[0]
    dk = s_scr.shape[2]

    @pl.when(pl.program_id(1) == 0)
    def _():
        s_scr[...] = s0_ref[...]

    row = lax.broadcasted_iota(I32, (cs, cs), 0)
    col = lax.broadcasted_iota(I32, (cs, cs), 1)
    tri = (col >= row) if reverse else (col <= row)
    tri_f = tri.astype(F32)
    end_row = 0 if reverse else cs - 1
    mid_row = cs // 2 if reverse else cs // 2 - 1

    def chunk_body(ci, carry):
        c = (n_chunks - 1 - ci) if reverse else ci
        r0 = pl.multiple_of(c * cs, cs)
        rows = pl.ds(r0, cs)
        lb = lb_ref[...]
        f = lb + (1.0 - lb) * jax.nn.sigmoid(f_ref[rows, :].astype(F32))
        lf = jnp.log(f)
        b_all = _dot_hi(tri_f, lf)
        k_all = 1.0 - f
        q_all = _silu(q_ref[rows, :].astype(F32))
        for h in range(n_heads):
            sl = slice(h * dk, (h + 1) * dk)
            b = b_all[:, sl]
            q = q_all[:, sl]
            k = k_all[:, sl]
            v = i_ref[rows, sl]
            b_end = b[end_row:end_row + 1]
            b_mid = b[mid_row:mid_row + 1]
            qd = (q * jnp.exp(b - b_mid)).astype(BF16)
            kd = (k * jnp.exp(b_mid - b)).astype(BF16)
            att = lax.dot_general(qd, kd, (((1,), (1,)), ((), ())), preferred_element_type=F32)
            att = jnp.where(tri, att, 0.0).astype(BF16)
            st = s_scr[h]
            qe = (q * jnp.exp(b)).astype(BF16)
            o = _dot(att, v) + lax.dot_general(qe, st.astype(BF16), (((1,), (1,)), ((), ())),
                                               preferred_element_type=F32)
            ke = (k * jnp.exp(b_end - b)).astype(BF16)
            delta_t = lax.dot_general(v, ke, (((0,), (0,)), ((), ())), preferred_element_type=F32)
            s_scr[h] = st * jnp.exp(b_end) + delta_t
            if final:
                o = o + of_ref[rows, sl].astype(F32)
                o = o * lax.rsqrt(jnp.mean(o * o, axis=-1, keepdims=True) + NORM_EPS) * ng_ref[...]
                o = o * _silu(gate_ref[rows, sl].astype(F32))
            o_ref[rows, sl] = o.astype(o_ref.dtype)
        return carry

    lax.fori_loop(0, n_chunks, chunk_body, 0)
    sfin_ref[...] = s_scr[...]


def hgrn_scan(p, cols, lb, s0, seq, tb, *, reverse, o_fwd=None, gate_col=None, norm_g=None):
    bsz, n_heads, dv, dk = s0.shape
    width = n_heads * dk
    nt = seq // tb
    final = o_fwd is not None

    def tmap(b, t):
        return b * nt + ((nt - 1 - t) if reverse else t)

    def colspec(cb):
        return pl.BlockSpec((tb, width), lambda b, t: (tmap(b, t), cb))

    in_specs = [colspec(cols[0]), colspec(cols[1]), colspec(cols[2]),
                pl.BlockSpec((1, width), lambda b, t: (0, 0)),
                pl.BlockSpec((None, n_heads, dv, dk), lambda b, t: (b, 0, 0, 0))]
    args = [p, p, p, lb.reshape(1, width), s0]
    if final:
        in_specs += [pl.BlockSpec((tb, width), lambda b, t: (tmap(b, t), 0)),
                     colspec(gate_col),
                     pl.BlockSpec((1, dk), lambda b, t: (0, 0))]
        args += [o_fwd, p, norm_g.reshape(1, dk)]
    return pl.pallas_call(
        functools.partial(_hgrn_kernel, reverse=reverse, n_chunks=tb // HG_CHUNK, final=final),
        out_shape=(jax.ShapeDtypeStruct((bsz * seq, width), BF16),
                   jax.ShapeDtypeStruct((bsz, n_heads, dv, dk), F32)),
        grid=(bsz, nt),
        in_specs=in_specs,
        out_specs=(pl.BlockSpec((tb, width), lambda b, t: (tmap(b, t), 0)),
                   pl.BlockSpec((None, n_heads, dv, dk), lambda b, t: (b, 0, 0, 0))),
        scratch_shapes=[pltpu.VMEM((n_heads, dv, dk), F32)],
        compiler_params=_cparams("arbitrary", "arbitrary"),
        name="hgrn_bwd" if reverse else "hgrn_fwd",
    )(*args)


def _shortconv_kernel(p_ref, prev_ref, next_ref, w_ref, b_ref, o_ref, *, tiles_per_batch):
    i = pl.program_id(0)
    ti = i % tiles_per_batch
    p = p_ref[...].astype(F32)
    tm = p.shape[0]
    row = lax.broadcasted_iota(I32, (tm, 1), 0)
    prev_row = jnp.where(ti == 0, 0.0, prev_ref[HALO_ROWS - 1:HALO_ROWS, :].astype(F32))
    next_row = jnp.where(ti == tiles_per_batch - 1, 0.0, next_ref[0:1, :].astype(F32))
    p_prev = jnp.where(row == 0, prev_row, pltpu.roll(p, 1, axis=0))
    p_next = jnp.where(row == tm - 1, next_row, pltpu.roll(p, tm - 1, axis=0))
    u = w_ref[0:1, :] * p_prev + w_ref[1:2, :] * p + w_ref[2:3, :] * p_next + b_ref[...]
    o_ref[...] = u.astype(o_ref.dtype)


def short_conv(p, width, conv_w, conv_b, seq, tm):
    t = p.shape[0]
    nt = t // tm
    tiles_per_batch = seq // tm
    sub = tm // HALO_ROWS
    return pl.pallas_call(
        functools.partial(_shortconv_kernel, tiles_per_batch=tiles_per_batch),
        out_shape=jax.ShapeDtypeStruct((t, width), BF16),
        grid=(nt,),
        in_specs=[pl.BlockSpec((tm, width), lambda i: (i, 0)),
                  pl.BlockSpec((HALO_ROWS, width), lambda i: (jnp.maximum(i * sub - 1, 0), 0)),
                  pl.BlockSpec((HALO_ROWS, width), lambda i: (jnp.minimum((i + 1) * sub, t // HALO_ROWS - 1), 0)),
                  pl.BlockSpec((HY_SHORT, width), lambda i: (0, 0)),
                  pl.BlockSpec((1, width), lambda i: (0, 0))],
        out_specs=pl.BlockSpec((tm, width), lambda i: (i, 0)),
        compiler_params=_cparams("arbitrary"),
        name="short_conv",
    )(p, p, p, conv_w, conv_b.reshape(1, width))


def _filter_kernel(band_ref, iscos_ref, issin_ref, w1_ref, b1_ref, w2_ref, b2_ref, w3_ref, b3_ref,
                   w4f_ref, w4b_ref, fr_ref, delta_ref, k_ref, s_ref, *, seq):
    step = pl.program_id(0)
    gb, q, ncol = k_ref.shape
    half = q // 2
    width = delta_ref.shape[1]
    nrow = gb * q
    r = lax.broadcasted_iota(I32, (nrow, 1), 0)
    is_bwd = r >= gb * half
    rr = jnp.where(is_bwd, r - gb * half, r)
    j = lax.shift_right_logical(rr, int(math.log2(half)))
    a = (rr & (half - 1)) + jnp.where(is_bwd, half, 0)
    n = (a * DFT_P + step * gb + j).astype(F32)
    t = jnp.where(is_bwd, 2.0 * seq - n, n)
    tn = t / float(max(seq - 1, 1))
    ang = (2.0 * math.pi / seq) * t * band_ref[...]
    z = jnp.where(iscos_ref[...] > 0, jnp.cos(ang), jnp.where(issin_ref[...] > 0, -jnp.sin(ang), 0.0))
    lane = lax.broadcasted_iota(I32, z.shape, 1)
    z = jnp.where(lane == 0, tn, z)
    fr = fr_ref[...]
    act = jnp.sin(fr * (_dot_hi(z, w1_ref[...]) + b1_ref[...]))
    act = jnp.sin(fr * (_dot_hi(act, w2_ref[...]) + b2_ref[...]))
    act = jnp.sin(fr * (_dot_hi(act, w3_ref[...]) + b3_ref[...]))
    delta = jnp.concatenate([delta_ref[...]] * (ncol // width), axis=1)
    nf = gb * half
    hf = _dot_hi(act[:nf], w4f_ref[...]) * jnp.exp(-tn[:nf] * delta)
    hb = _dot_hi(act[nf:], w4b_ref[...]) * jnp.exp(-tn[nf:] * delta)
    hb = jnp.where(n[nf:] == float(seq), 0.0, hb)
    k_ref[:, :half, :] = hf.reshape(gb, half, ncol).astype(k_ref.dtype)
    k_ref[:, half:, :] = hb.reshape(gb, half, ncol).astype(k_ref.dtype)
    tot = jnp.sum(jnp.abs(hf), axis=0, keepdims=True) + jnp.sum(jnp.abs(hb), axis=0, keepdims=True)

    @pl.when(step == 0)
    def _():
        s_ref[...] = jnp.zeros_like(s_ref)

    s_ref[...] += tot


def hyena_filter_taps(seq, w1, b1, w2, b2, w3, b3, w4, freq, width):
    emb = w1.shape[0]
    hid = w1.shape[1]
    bands = (emb - 1) // 2
    q = 2 * seq // DFT_P
    ncol = HY_ORDER * width
    band = np.zeros((1, LANES), np.float32)
    iscos = np.zeros((1, LANES), np.float32)
    issin = np.zeros((1, LANES), np.float32)
    lin = np.linspace(1e-4, bands - 1, bands, dtype=np.float32)
    band[0, 1:1 + bands] = lin
    band[0, 1 + bands:1 + 2 * bands] = lin
    iscos[0, 1:1 + bands] = 1.0
    issin[0, 1 + bands:1 + 2 * bands] = 1.0
    min_decay = math.log(HY_DECAY_TARGET) / HY_SLOW_PCT
    max_decay = math.log(HY_DECAY_TARGET) / HY_FAST_PCT
    delta = np.abs(np.linspace(min_decay, max_decay, width, dtype=np.float32)).reshape(1, width)
    w1p = jnp.zeros((LANES, hid), F32).at[:emb].set(w1.astype(F32))
    w4r = w4.astype(F32).reshape(hid, HY_ORDER, 2, width)
    w4f = w4r[:, :, 0, :].reshape(hid, ncol)
    w4b = w4r[:, :, 1, :].reshape(hid, ncol)
    gb = DFT_GROUP
    const = lambda shape: pl.BlockSpec(shape, lambda i: tuple(0 for _ in shape))
    return pl.pallas_call(
        functools.partial(_filter_kernel, seq=seq),
        out_shape=(jax.ShapeDtypeStruct((DFT_P, q, ncol), BF16),
                   jax.ShapeDtypeStruct((1, ncol), F32)),
        grid=(DFT_P // gb,),
        in_specs=[const((1, LANES)), const((1, LANES)), const((1, LANES)),
                  const((LANES, hid)), const((1, hid)), const((hid, hid)), const((1, hid)),
                  const((hid, hid)), const((1, hid)), const((hid, ncol)), const((hid, ncol)),
                  const((1, hid)), const((1, width))],
        out_specs=(pl.BlockSpec((gb, q, ncol), lambda i: (i, 0, 0)),
                   pl.BlockSpec((1, ncol), lambda i: (0, 0))),
        compiler_params=_cparams("arbitrary"),
        name="hyena_filter",
    )(jnp.asarray(band), jnp.asarray(iscos), jnp.asarray(issin), w1p, b1.reshape(1, hid).astype(F32),
      w2.astype(F32), b2.reshape(1, hid).astype(F32), w3.astype(F32), b3.reshape(1, hid).astype(F32),
      w4f, w4b, freq.reshape(1, hid).astype(F32), jnp.asarray(delta))


def _dft_tables(seq):
    p = DFT_P
    n_fft = 2 * seq
    q = n_fft // p
    qh = q // 2
    ka = jnp.arange(q, dtype=I32)
    a = jnp.arange(q, dtype=I32)
    b = jnp.arange(p, dtype=I32)
    nn = a[None, :] * p + b[:, None]
    ph = (ka[None, :, None] * nn[:, None, :]) % n_fft
    ang = ph.astype(F32) * (2.0 * math.pi / n_fft)
    mr, mi = jnp.cos(ang), -jnp.sin(ang)
    m1c = jnp.concatenate([jnp.concatenate([mr[:, :, :qh], -mi[:, :, :qh]], axis=2),
                           jnp.concatenate([mi[:, :, :qh], mr[:, :, :qh]], axis=2)], axis=1)
    m1r = jnp.concatenate([mr, mi], axis=1)
    gr = jnp.swapaxes(mr[:, :, :qh], 1, 2) / n_fft
    gi = -jnp.swapaxes(mi[:, :, :qh], 1, 2) / n_fft
    m4 = jnp.concatenate([jnp.concatenate([gr, -gi], axis=2), jnp.concatenate([gi, gr], axis=2)], axis=1)
    kb = np.arange(p)
    ang2 = 2.0 * np.pi * ((kb[:, None] * kb[None, :]) % p) / p
    fr, fi = np.cos(ang2), -np.sin(ang2)
    m2 = np.block([[fr, -fi], [fi, fr]]).astype(np.float32)
    m3 = np.block([[fr, fi], [-fi, fr]]).astype(np.float32)
    return (m1c.astype(BF16), m1r.astype(BF16), jnp.asarray(m2, BF16), jnp.asarray(m3, BF16), m4.astype(BF16))


def _bmm_kernel(w_ref, x_ref, o_ref, *, shared_w):
    for j in range(x_ref.shape[0]):
        w = w_ref[...] if shared_w else w_ref[j]
        o_ref[j] = _dot(w, x_ref[j]).astype(o_ref.dtype)


def batched_left_matmul(w, x, col_block, ncols, name):
    g, k = x.shape[0], x.shape[1]
    shared = w.ndim == 2
    m = w.shape[-2]
    gb = DFT_GROUP
    wspec = (pl.BlockSpec((m, k), lambda i: (0, 0)) if shared
             else pl.BlockSpec((gb, m, k), lambda i: (i, 0, 0)))
    return pl.pallas_call(
        functools.partial(_bmm_kernel, shared_w=shared),
        out_shape=jax.ShapeDtypeStruct((g, m, ncols), BF16),
        grid=(g // gb,),
        in_specs=[wspec, pl.BlockSpec((gb, k, ncols), lambda i: (i, 0, col_block))],
        out_specs=pl.BlockSpec((gb, m, ncols), lambda i: (i, 0, 0)),
        compiler_params=_cparams("arbitrary"),
        name=name,
    )(w, x)


def _dft_mid_kernel(m2_ref, m3_ref, x_ref, k_ref, o_ref):
    half = x_ref.shape[1] // 2
    for j in range(x_ref.shape[0]):
        xf = _dot(m2_ref[...], x_ref[j])
        kf = k_ref[j].astype(F32)
        xr, xi = xf[:half], xf[half:]
        kr, ki = kf[:half], kf[half:]
        z = jnp.concatenate([xr * kr - xi * ki, xr * ki + xi * kr], axis=0).astype(BF16)
        o_ref[j] = _dot(m3_ref[...], z).astype(o_ref.dtype)


def dft_mid(m2, m3, x, kspec, kcol, ncols):
    g, r = x.shape[0], x.shape[1]
    gb = DFT_GROUP
    return pl.pallas_call(
        _dft_mid_kernel,
        out_shape=jax.ShapeDtypeStruct((g, r, ncols), BF16),
        grid=(g // gb,),
        in_specs=[pl.BlockSpec((r, r), lambda i: (0, 0)),
                  pl.BlockSpec((r, r), lambda i: (0, 0)),
                  pl.BlockSpec((gb, r, ncols), lambda i: (i, 0, 0)),
                  pl.BlockSpec((gb, r, ncols), lambda i: (i, 0, kcol))],
        out_specs=pl.BlockSpec((gb, r, ncols), lambda i: (i, 0, 0)),
        compiler_params=_cparams("arbitrary"),
        name="dft_mid",
    )(m2, m3, x, kspec)


def _dft_out_kernel(m4_ref, y_ref, inv_ref, skip_ref, v_ref, mul_ref, o_ref):
    for j in range(y_ref.shape[0]):
        conv = _dot(m4_ref[j], y_ref[j]) * inv_ref[...] + v_ref[j].astype(F32) * skip_ref[...]
        o_ref[j] = (mul_ref[j].astype(F32) * conv).astype(o_ref.dtype)


def dft_out(m4, y, inv_l1, skip, u, v_col, mul, mul_col, ncols):
    g, r = y.shape[0], y.shape[1]
    rows = m4.shape[1]
    gb = DFT_GROUP
    return pl.pallas_call(
        _dft_out_kernel,
        out_shape=jax.ShapeDtypeStruct((g, rows, ncols), BF16),
        grid=(g // gb,),
        in_specs=[pl.BlockSpec((gb, rows, r), lambda i: (i, 0, 0)),
                  pl.BlockSpec((gb, r, ncols), lambda i: (i, 0, 0)),
                  pl.BlockSpec((1, ncols), lambda i: (0, 0)),
                  pl.BlockSpec((1, ncols), lambda i: (0, 0)),
                  pl.BlockSpec((gb, rows, ncols), lambda i: (i, 0, v_col)),
                  pl.BlockSpec((gb, rows, ncols), lambda i: (i, 0, mul_col))],
        out_specs=pl.BlockSpec((gb, rows, ncols), lambda i: (i, 0, 0)),
        compiler_params=_cparams("arbitrary"),
        name="dft_out",
    )(m4, y, inv_l1, skip, u, mul)


def _swap_ab(x):
    g1, r, c = x.shape
    g2 = r // 2
    return x.reshape(g1, 2, g2, c).transpose(2, 1, 0, 3).reshape(g2, 2 * g1, c)


def hyena_branch(u_nat, bsz, seq, width, taps, l1, skip):
    p = DFT_P
    qh = seq // p
    m1c, m1r, m2, m3, m4 = _dft_tables(seq)
    ncol = HY_ORDER * width
    ks1 = batched_left_matmul(m1r, taps, 0, ncol, "dft_k1")
    kspec = batched_left_matmul(m2, _swap_ab(ks1), 0, ncol, "dft_k2")
    inv_l1 = 1.0 / l1
    u = u_nat.reshape(bsz, qh, p, 3 * width).transpose(2, 0, 1, 3).reshape(p, bsz * qh, 3 * width)
    z = None
    for order in range(HY_ORDER):
        src, src_col = (u, 0) if order == 0 else (z, 0)
        s1 = batched_left_matmul(m1c, src, src_col, width, "dft_s1")
        mid = dft_mid(m2, m3, _swap_ab(s1), kspec, order, width)
        z = dft_out(m4, _swap_ab(mid), inv_l1[:, order * width:(order + 1) * width],
                    skip[order].reshape(1, width).astype(F32), src, src_col, u, order + 1, width)
    return z.reshape(p, bsz, qh, width).transpose(1, 2, 0, 3).reshape(bsz * seq, width)


def _pack_pairs(x):
    w = x.shape[1] // 2
    u = lax.bitcast_convert_type(x, U32)
    r = (u + U32(0x7FFF) + ((u >> 16) & U32(1))) >> 16
    return r[:, :w] | (r[:, w:] << 16)


def _unpack_pairs(p):
    lo = lax.bitcast_convert_type(p << 16, F32)
    hi = lax.bitcast_convert_type(p & U32(0xFFFF0000), F32)
    return jnp.concatenate([lo, hi], axis=1)


def _merge_kernel(x_ref, er_ref, ec_ref, yhy_ref, yhg_ref, ghy_ref, ghg_ref, why_ref, whg_ref, wo_ref,
                  g1_ref, n2_ref, sh2_ref, sc2_ref, g2_ref, rwt_ref, sgu_ref, sd_ref,
                  xres_ref, h2p_ref, lg_ref):
    x = x_ref[...]
    rows, gw, d = x.shape
    half = d // 2
    xp = jnp.concatenate([x[:, :, :half] + er_ref[...], x[:, :, half:] + ec_ref[...]], axis=-1)
    xp = xp.reshape(rows * gw, d)
    m = (jax.nn.sigmoid(ghy_ref[...].astype(F32)) * _dot(yhy_ref[...], why_ref[...])
         + jax.nn.sigmoid(ghg_ref[...].astype(F32)) * _dot(yhg_ref[...], whg_ref[...]))
    x1 = xp + g1_ref[...] * _dot(m.astype(BF16), wo_ref[...])
    ms = jnp.mean(x1 * x1, axis=-1, keepdims=True)
    h2 = x1 * lax.rsqrt(ms + NORM_EPS) * n2_ref[...] * (1.0 + sc2_ref[...]) + sh2_ref[...]
    lg_ref[...] = lax.dot_general(rwt_ref[...], h2, (((1,), (1,)), ((), ())),
                                  preferred_element_type=F32, precision=HIGHEST)
    h2b = h2.astype(BF16)
    gu = _dot(h2b, sgu_ref[...])
    fs = gu.shape[1] // 2
    shared = _dot((_silu(gu[:, :fs]) * gu[:, fs:]).astype(BF16), sd_ref[...])
    xres_ref[...] = x1 + g2_ref[...] * shared
    h2p_ref[...] = _pack_pairs(h2)


def merge_stage(x, emb_r, emb_c, y_hy, y_hg, p, gate_cols, w_hy_out, w_hg_out, w_out, g1, norm2_g,
                sh2, sc2, g2, router_wt, sh_gate_up, sh_down, tm):
    b, s, d = x.shape
    rows_per_batch = s // GRID_W
    rt = tm // GRID_W
    tiles_per_batch = rows_per_batch // rt
    x3 = x.reshape(b * rows_per_batch, GRID_W, d)
    wb = y_hy.shape[1]
    ne = router_wt.shape[0]
    fs2 = sh_gate_up.shape[1]
    tok = lambda cb, w: pl.BlockSpec((tm, w), lambda i: (i, cb))
    const = lambda shape: pl.BlockSpec(shape, lambda i: tuple(0 for _ in shape))
    per_b = pl.BlockSpec((None, 1, d), lambda i: (i // tiles_per_batch, 0, 0))
    return pl.pallas_call(
        _merge_kernel,
        out_shape=(jax.ShapeDtypeStruct((b * s, d), F32),
                   jax.ShapeDtypeStruct((b * s, d // 2), U32),
                   jax.ShapeDtypeStruct((ne, b * s), F32)),
        grid=(b * tiles_per_batch,),
        in_specs=[pl.BlockSpec((rt, GRID_W, d), lambda i: (i, 0, 0)),
                  pl.BlockSpec((rt, 1, d // 2), lambda i: (i % tiles_per_batch, 0, 0)),
                  const((GRID_W, d // 2)),
                  tok(0, wb), tok(0, wb), tok(gate_cols[0], d), tok(gate_cols[1], d),
                  const((wb, d)), const((wb, d)), const((d, d)),
                  per_b, const((1, d)), per_b, per_b, per_b,
                  const((ne, d)), const((d, fs2)), const((fs2 // 2, d))],
        out_specs=(pl.BlockSpec((tm, d), lambda i: (i, 0)),
                   pl.BlockSpec((tm, d // 2), lambda i: (i, 0)),
                   pl.BlockSpec((ne, tm), lambda i: (0, i))),
        compiler_params=_cparams("arbitrary"),
        name="merge",
    )(x3, emb_r, emb_c, y_hy, y_hg, p, p, w_hy_out, w_hg_out, w_out, g1, norm2_g.reshape(1, d),
      sh2, sc2, g2, router_wt, sh_gate_up, sh_down)


def _route_kernel(lg_ref, bias_ref, eidx_ref, wsel_ref, rank_ref, cnt_ref, carry):
    ne, tr = lg_ref.shape
    gsz = ne // N_GROUPS
    neg = -jnp.inf

    @pl.when(pl.program_id(0) == 0)
    def _():
        carry[...] = jnp.zeros_like(carry)

    scores = jax.nn.sigmoid(lg_ref[...])
    biased = scores + bias_ref[...]
    riota = lax.broadcasted_iota(I32, (gsz, tr), 0).astype(F32)
    gs = []
    for g in range(N_GROUPS):
        vg = biased[g * gsz:(g + 1) * gsz]
        m1 = jnp.max(vg, axis=0, keepdims=True)
        i1 = jnp.min(jnp.where(vg == m1, riota, float(gsz)), axis=0, keepdims=True)
        m2 = jnp.max(jnp.where(riota == i1, neg, vg), axis=0, keepdims=True)
        gs.append(m1 + m2)
    cur = jnp.concatenate(gs, axis=0)
    giota = lax.broadcasted_iota(I32, (N_GROUPS, tr), 0).astype(F32)
    gsel = jnp.zeros((N_GROUPS, tr), F32)
    for _ in range(TOPK_GROUPS):
        m = jnp.max(cur, axis=0, keepdims=True)
        idx = jnp.min(jnp.where(cur == m, giota, float(N_GROUPS)), axis=0, keepdims=True)
        hit = giota == idx
        gsel = jnp.where(hit, 1.0, gsel)
        cur = jnp.where(hit, neg, cur)
    cur = jnp.concatenate([jnp.where(gsel[g:g + 1] > 0.0, biased[g * gsz:(g + 1) * gsz], neg)
                           for g in range(N_GROUPS)], axis=0)
    eiota = lax.broadcasted_iota(I32, (ne, tr), 0).astype(F32)
    chosen = jnp.zeros((ne, tr), F32)
    idxs, ws = [], []
    for _ in range(TOP_K):
        m = jnp.max(cur, axis=0, keepdims=True)
        idx = jnp.min(jnp.where(cur == m, eiota, float(ne)), axis=0, keepdims=True)
        hit = eiota == idx
        idxs.append(idx)
        ws.append(jnp.sum(jnp.where(hit, scores, 0.0), axis=0, keepdims=True))
        chosen = jnp.where(hit, 1.0, chosen)
        cur = jnp.where(hit, neg, cur)
    w = jnp.concatenate(ws, axis=0)
    wsel_ref[...] = w / jnp.sum(w, axis=0, keepdims=True) * ROUTED_SCALE
    eidx_ref[...] = jnp.concatenate(idxs, axis=0).astype(I32)
    srow = lax.broadcasted_iota(I32, (tr, tr), 0)
    scol = lax.broadcasted_iota(I32, (tr, tr), 1)
    before = (srow < scol).astype(BF16)
    base = carry[...] + _dot(chosen.astype(BF16), before)
    ranks = [jnp.sum(jnp.where(eiota == idx, base, 0.0), axis=0, keepdims=True) for idx in idxs]
    rank_ref[...] = jnp.concatenate(ranks, axis=0).astype(I32)
    carry[...] += jnp.sum(chosen, axis=1, keepdims=True)
    cnt_ref[...] = carry[...]


def route(logits_t, router_bias, tr):
    ne, t = logits_t.shape
    return pl.pallas_call(
        _route_kernel,
        out_shape=(jax.ShapeDtypeStruct((TOP_K, t), I32),
                   jax.ShapeDtypeStruct((TOP_K, t), F32),
                   jax.ShapeDtypeStruct((TOP_K, t), I32),
                   jax.ShapeDtypeStruct((ne, 1), F32)),
        grid=(t // tr,),
        in_specs=[pl.BlockSpec((ne, tr), lambda i: (0, i)),
                  pl.BlockSpec((ne, 1), lambda i: (0, 0))],
        out_specs=(pl.BlockSpec((TOP_K, tr), lambda i: (0, i)),
                   pl.BlockSpec((TOP_K, tr), lambda i: (0, i)),
                   pl.BlockSpec((TOP_K, tr), lambda i: (0, i)),
                   pl.BlockSpec((ne, 1), lambda i: (0, 0))),
        scratch_shapes=[pltpu.VMEM((ne, 1), F32)],
        compiler_params=_cparams("arbitrary"),
        name="route",
    )(logits_t, router_bias.reshape(ne, 1).astype(F32))


def _dest_kernel(cnt_ref, eidx_ref, rank_ref, dest_ref, be_ref, nv_ref, nb_ref, start_scr, *, n_blocks):
    ne = cnt_ref.shape[0]
    tr = eidx_ref.shape[1]

    @pl.when(pl.program_id(0) == 0)
    def _():
        cnt = jnp.broadcast_to(cnt_ref[...], (ne, LANES))
        padded = jnp.floor((cnt + float(MOE_ROWS - 1)) / float(MOE_ROWS)) * float(MOE_ROWS)
        r = lax.broadcasted_iota(I32, (ne, ne), 0)
        c = lax.broadcasted_iota(I32, (ne, ne), 1)
        start = _dot_hi((c < r).astype(F32), padded)
        start_scr[...] = start
        end = start[:, 0:1] + padded[:, 0:1]
        used = start[:, 0:1] + cnt[:, 0:1]
        nbl = be_ref.shape[1]
        blk_row = (lax.broadcasted_iota(I32, (1, nbl), 1) * MOE_ROWS).astype(F32)
        total = jnp.max(end, axis=0, keepdims=True)
        last_row = total - float(MOE_ROWS)
        blk_row_c = jnp.minimum(blk_row, last_row)
        e_of = jnp.sum((end <= blk_row_c).astype(F32), axis=0, keepdims=True)
        e_of = jnp.minimum(e_of, float(ne - 1))
        eio = lax.broadcasted_iota(I32, (ne, nbl), 0).astype(F32)
        used_e = jnp.sum(jnp.where(eio == e_of, used, 0.0), axis=0, keepdims=True)
        valid = jnp.clip(used_e - blk_row_c, 0.0, float(MOE_ROWS))
        be_ref[...] = e_of.astype(I32)
        nv_ref[...] = jnp.where(blk_row <= last_row, valid, 0.0).astype(I32)
        nb_ref[...] = jnp.broadcast_to(total / float(MOE_ROWS), nb_ref.shape).astype(I32)

    eiota = lax.broadcasted_iota(I32, (ne, tr), 0)
    start_col = start_scr[:, 0:1]
    rows = []
    for k in range(TOP_K):
        hit = eiota == eidx_ref[k:k + 1, :]
        rows.append(jnp.sum(jnp.where(hit, start_col, 0.0), axis=0, keepdims=True))
    dest_ref[...] = jnp.concatenate(rows, axis=0).astype(I32) + rank_ref[...]


def dispatch_plan(counts, eidx, rank, tr, n_blocks):
    ne = counts.shape[0]
    t = eidx.shape[1]
    nbl = pl.cdiv(n_blocks, LANES) * LANES
    return pl.pallas_call(
        functools.partial(_dest_kernel, n_blocks=n_blocks),
        out_shape=(jax.ShapeDtypeStruct((TOP_K, t), I32),
                   jax.ShapeDtypeStruct((1, nbl), I32),
                   jax.ShapeDtypeStruct((1, nbl), I32),
                   jax.ShapeDtypeStruct((1, LANES), I32)),
        grid=(t // tr,),
        in_specs=[pl.BlockSpec((ne, 1), lambda i: (0, 0)),
                  pl.BlockSpec((TOP_K, tr), lambda i: (0, i)),
                  pl.BlockSpec((TOP_K, tr), lambda i: (0, i))],
        out_specs=(pl.BlockSpec((TOP_K, tr), lambda i: (0, i)),
                   pl.BlockSpec((1, nbl), lambda i: (0, 0)),
                   pl.BlockSpec((1, nbl), lambda i: (0, 0)),
                   pl.BlockSpec((1, LANES), lambda i: (0, 0))),
        scratch_shapes=[pltpu.VMEM((ne, LANES), F32)],
        compiler_params=_cparams("arbitrary"),
        name="dispatch_plan",
    )(counts, eidx, rank)


def _scatter_kernel(dest_ref, h_ref, xs_ref, sem):
    tm = h_ref.shape[0]

    def row_copy(t, d):
        return pltpu.make_async_copy(h_ref.at[pl.ds(t, 1)], xs_ref.at[pl.ds(d, 1)], sem)

    def issue(t, carry):
        for k in range(TOP_K):
            row_copy(t, dest_ref[k, t]).start()
        return carry

    lax.fori_loop(0, tm, issue, 0)

    def drain(t, carry):
        for k in range(TOP_K):
            row_copy(0, 0).wait()
        return carry

    lax.fori_loop(0, tm, drain, 0)


def scatter_rows(dest, h2p, n_rows, tm):
    t, w = h2p.shape
    return pl.pallas_call(
        _scatter_kernel,
        out_shape=jax.ShapeDtypeStruct((n_rows, w), U32),
        grid=(t // tm,),
        in_specs=[pl.BlockSpec((TOP_K, tm), lambda i: (0, i), memory_space=pltpu.SMEM),
                  pl.BlockSpec((tm, w), lambda i: (i, 0))],
        out_specs=pl.BlockSpec(memory_space=pl.ANY),
        scratch_shapes=[pltpu.SemaphoreType.DMA(())],
        compiler_params=_cparams("arbitrary"),
        name="scatter_rows",
    )(dest, h2p)


def _gmm_kernel(be_ref, nv_ref, nb_ref, xs_ref, wg_ref, wu_ref, wd_ref, y_ref, wgu_s, wd_s):
    j = pl.program_id(0)
    f = wg_ref.shape[1]
    e_cur = be_ref[j]
    e_prev = be_ref[jnp.maximum(j - 1, 0)]

    @pl.when(j < nb_ref[0])
    def _():
        @pl.when((j == 0) | (e_cur != e_prev))
        def _():
            wgu_s[:, :f] = wg_ref[...].astype(BF16)
            wgu_s[:, f:] = wu_ref[...].astype(BF16)
            wd_s[...] = wd_ref[...].astype(BF16)

        x = _unpack_pairs(xs_ref[...])
        row = lax.broadcasted_iota(I32, (x.shape[0], 1), 0)
        x = jnp.where(row < nv_ref[j], x, 0.0).astype(BF16)
        gu = _dot(x, wgu_s[...])
        hmid = (_silu(gu[:, :f]) * gu[:, f:]).astype(BF16)
        y_ref[...] = _pack_pairs(_dot(hmid, wd_s[...]))


def grouped_mlp(block_e, block_valid, n_used, xs, w_gate, w_up, w_down, n_blocks):
    ne, d, f = w_gate.shape
    w = xs.shape[1]
    last = lambda j, nb: jnp.minimum(j, nb[0] - 1)
    grid_spec = pltpu.PrefetchScalarGridSpec(
        num_scalar_prefetch=3,
        grid=(n_blocks,),
        in_specs=[pl.BlockSpec((MOE_ROWS, w), lambda j, be, nv, nb: (last(j, nb), 0)),
                  pl.BlockSpec((None, d, f), lambda j, be, nv, nb: (be[j], 0, 0)),
                  pl.BlockSpec((None, d, f), lambda j, be, nv, nb: (be[j], 0, 0)),
                  pl.BlockSpec((None, f, d), lambda j, be, nv, nb: (be[j], 0, 0))],
        out_specs=pl.BlockSpec((MOE_ROWS, w), lambda j, be, nv, nb: (last(j, nb), 0)),
        scratch_shapes=[pltpu.VMEM((d, 2 * f), BF16), pltpu.VMEM((f, d), BF16)],
    )
    return pl.pallas_call(
        _gmm_kernel,
        out_shape=jax.ShapeDtypeStruct(xs.shape, U32),
        grid_spec=grid_spec,
        compiler_params=_cparams("arbitrary"),
        name="grouped_mlp",
    )(block_e, block_valid, n_used, xs, w_gate, w_up, w_down)


def _combine_kernel(dest_ref, xres_ref, wt_ref, g2_ref, fg_ref, ys_ref, o_ref, buf, sem):
    tm = xres_ref.shape[0]

    def row_copy(k, t, d):
        return pltpu.make_async_copy(ys_ref.at[pl.ds(d, 1)], buf.at[k, pl.ds(t, 1)], sem)

    def issue(t, carry):
        for k in range(TOP_K):
            row_copy(k, t, dest_ref[k, t]).start()
        return carry

    lax.fori_loop(0, tm, issue, 0)

    def drain(t, carry):
        for k in range(TOP_K):
            row_copy(0, 0, 0).wait()
        return carry

    lax.fori_loop(0, tm, drain, 0)

    wt = wt_ref[...]
    routed = jnp.zeros(xres_ref.shape, F32)
    for k in range(TOP_K):
        routed = routed + wt[:, k:k + 1] * _unpack_pairs(buf[k])
    x2 = xres_ref[...] + g2_ref[...] * routed
    ms = jnp.mean(x2 * x2, axis=-1, keepdims=True)
    o_ref[...] = x2 * lax.rsqrt(ms + NORM_EPS) * fg_ref[...]


def combine(dest, xres, wsel_t, g2, final_g, ys, seq, tm):
    t, d = xres.shape
    tiles_per_batch = seq // tm
    return pl.pallas_call(
        _combine_kernel,
        out_shape=jax.ShapeDtypeStruct((t, d), F32),
        grid=(t // tm,),
        in_specs=[pl.BlockSpec((TOP_K, tm), lambda i: (0, i), memory_space=pltpu.SMEM),
                  pl.BlockSpec((tm, d), lambda i: (i, 0)),
                  pl.BlockSpec((tm, TOP_K), lambda i: (i, 0)),
                  pl.BlockSpec((None, 1, d), lambda i: (i // tiles_per_batch, 0, 0)),
                  pl.BlockSpec((1, d), lambda i: (0, 0)),
                  pl.BlockSpec(memory_space=pl.ANY)],
        out_specs=pl.BlockSpec((tm, d), lambda i: (i, 0)),
        scratch_shapes=[pltpu.VMEM((TOP_K, tm, d // 2), U32), pltpu.SemaphoreType.DMA(())],
        compiler_params=_cparams("arbitrary"),
        name="combine",
    )(dest, xres, wsel_t, g2, final_g.reshape(1, d), ys)


def _pos_tables(rows, cols, dim):
    quarter = dim // 4
    omega = 1.0 / (POS_BASE ** (np.arange(quarter, dtype=np.float32) / quarter))
    ang_r = np.arange(rows, dtype=np.float32)[:, None] * omega
    ang_c = np.arange(cols, dtype=np.float32)[:, None] * omega
    emb_r = np.concatenate([np.sin(ang_r), np.cos(ang_r)], axis=-1).astype(np.float32)
    emb_c = np.concatenate([np.sin(ang_c), np.cos(ang_c)], axis=-1).astype(np.float32)
    return jnp.asarray(emb_r.reshape(rows, 1, dim // 2)), jnp.asarray(emb_c)


def kernel(x, c, ctx, c_ctx, norm1_g, norm2_g, ada_w, ada_b, w_in, hy_conv_w, hy_conv_b, hy_f_w1, hy_f_b1, hy_f_w2, hy_f_b2, hy_f_w3, hy_f_b3, hy_f_w4, hy_f_freq, hy_skip, hg_lb_logits, hg_norm_g, w_hy_out, w_hg_out, w_out, router_w, router_bias, exp_w_gate, exp_w_up, exp_w_down, sh_w_gate, sh_w_up, sh_w_down, final_g):
    bsz, seq, d = x.shape
    n_ctx = ctx.shape[1]
    hy_w = w_hy_out.shape[1]
    hg_w = w_hg_out.shape[1]
    dk = hg_norm_g.shape[1]
    n_heads = hg_w // dk
    ne = router_w.shape[2]
    l = 0

    c_rows = jnp.zeros((SUBLANES, d), F32).at[:bsz].set(c).at[bsz].set(c_ctx)
    mods = ada_vectors(c_rows, ada_w[l], ada_b[l])
    sh1, sc1, g1, sh2, sc2, g2 = [mods[:bsz, j * d:(j + 1) * d].reshape(bsz, 1, d) for j in range(N_ADA)]
    csh1 = jnp.broadcast_to(mods[bsz, 0:d].reshape(1, 1, d), (bsz, 1, d))
    csc1 = jnp.broadcast_to(mods[bsz, d:2 * d].reshape(1, 1, d), (bsz, 1, d))

    emb_r, emb_c = _pos_tables(seq // GRID_W, GRID_W, d)
    w_in_b = w_in[l].astype(BF16)
    hy_proj = 3 * hy_w
    p = in_projection(x, emb_r, emb_c, norm1_g[l], sh1, sc1, w_in_b, TOKEN_TILE)
    hg_cols = slice(hy_proj, hy_proj + 5 * hg_w)
    zero_r = jnp.zeros((n_ctx // GRID_W, 1, d // 2), F32)
    zero_c = jnp.zeros((GRID_W, d // 2), F32)
    pc = in_projection(ctx, zero_r, zero_c, norm1_g[l], csh1, csc1, w_in_b[:, hg_cols], n_ctx)

    lbs = jnp.cumsum(jax.nn.softmax(hg_lb_logits.astype(F32), axis=0), axis=0)
    lb_f, lb_b = lbs[l, 0], lbs[l, 1]
    zero_state = jnp.zeros((bsz, n_heads, dk, dk), F32)
    base = hy_proj // hg_w
    _, st_f = hgrn_scan(pc, (0, 1, 2), lb_f, zero_state, n_ctx, n_ctx, reverse=False)
    _, st_b = hgrn_scan(pc, (0, 1, 3), lb_b, zero_state, n_ctx, n_ctx, reverse=True)
    o_f, _ = hgrn_scan(p, (base, base + 1, base + 2), lb_f, st_f, seq, HG_TIME_BLOCK, reverse=False)
    y_hg, _ = hgrn_scan(p, (base, base + 1, base + 3), lb_b, st_b, seq, HG_TIME_BLOCK, reverse=True,
                        o_fwd=o_f, gate_col=base + 4, norm_g=hg_norm_g[l])

    u = short_conv(p, hy_proj, hy_conv_w[l], hy_conv_b[l], seq, TOKEN_TILE)
    taps, l1 = hyena_filter_taps(seq, hy_f_w1[l], hy_f_b1[l], hy_f_w2[l], hy_f_b2[l], hy_f_w3[l], hy_f_b3[l],
                                 hy_f_w4[l], hy_f_freq[l], hy_w)
    y_hy = hyena_branch(u, bsz, seq, hy_w, taps, l1, hy_skip[l])

    gate_base = (hy_proj + 5 * hg_w) // d
    sh_gu = jnp.concatenate([sh_w_gate[l], sh_w_up[l]], axis=1).astype(BF16)
    xres, h2p, logits_t = merge_stage(
        x, emb_r, emb_c, y_hy, y_hg, p, (gate_base, gate_base + 1),
        w_hy_out[l].astype(BF16), w_hg_out[l].astype(BF16), w_out[l].astype(BF16), g1, norm2_g[l],
        sh2, sc2, g2, router_w[l].T.astype(F32), sh_gu, sh_w_down[l].astype(BF16), TOKEN_TILE)

    t = bsz * seq
    eidx, wsel, rank, counts = route(logits_t, router_bias[l], TOKEN_TILE)
    n_rows = t * TOP_K + ne * (MOE_ROWS - 1)
    n_blocks = pl.cdiv(n_rows, MOE_ROWS)
    dest, block_e, block_valid, n_used = dispatch_plan(counts, eidx, rank, TOKEN_TILE, n_blocks)

    xs = scatter_rows(dest, h2p, n_blocks * MOE_ROWS, DMA_TILE)
    ys = grouped_mlp(block_e.reshape(-1), block_valid.reshape(-1), n_used.reshape(-1)[:1], xs,
                     exp_w_gate[l], exp_w_up[l], exp_w_down[l], n_blocks)
    out = combine(dest, xres, wsel.T, g2, final_g, ys, seq, DMA_TILE)
    return out.reshape(bsz, seq, d)
```

```python
import functools
import math

import numpy as np
import jax
import jax.numpy as jnp
from jax import lax
from jax.experimental import pallas as pl
from jax.experimental.pallas import tpu as pltpu

F32 = jnp.float32
BF16 = jnp.bfloat16
U32 = jnp.uint32
I32 = jnp.int32
HIGHEST = lax.Precision.HIGHEST

GRID_W = 64
POS_BASE = 10000.0
NORM_EPS = 1e-6
N_ADA = 6
HY_ORDER = 2
HY_SHORT = 3
HY_DECAY_TARGET = 1e-2
HY_FAST_PCT = 0.3
HY_SLOW_PCT = 1.5
HG_HEADS = 4
HG_CHUNK = 64
N_GROUPS = 8
TOPK_GROUPS = 4
TOP_K = 8
ROUTED_SCALE = 2.5

LANES = 128
SUBLANES = 8
VMEM_LIMIT = 56 * 1024 * 1024

TOKEN_TILE = 512
HG_TIME_BLOCK = 512
HALO_ROWS = 16
DFT_P = 128
DFT_GROUP = 8
MOE_ROWS = 256
DMA_TILE = 256


def _cparams(*sem):
    return pltpu.CompilerParams(dimension_semantics=sem, vmem_limit_bytes=VMEM_LIMIT)


def _dot(a, b):
    return jnp.dot(a, b, preferred_element_type=F32)


def _dot_hi(a, b):
    return jnp.dot(a, b, preferred_element_type=F32, precision=HIGHEST)


def _silu(x):
    return x * jax.nn.sigmoid(x)


def _ada_kernel(c_ref, w_ref, b_ref, o_ref):
    o_ref[...] = _dot_hi(_silu(c_ref[...]), w_ref[...]) + b_ref[...]


def ada_vectors(c_rows, ada_w, ada_b):
    r, d = c_rows.shape
    n = ada_w.shape[1]
    bn = 1024
    return pl.pallas_call(
        _ada_kernel,
        out_shape=jax.ShapeDtypeStruct((r, n), F32),
        grid=(n // bn,),
        in_specs=[pl.BlockSpec((r, d), lambda j: (0, 0)),
                  pl.BlockSpec((d, bn), lambda j: (0, j)),
                  pl.BlockSpec((1, bn), lambda j: (0, j))],
        out_specs=pl.BlockSpec((r, bn), lambda j: (0, j)),
        compiler_params=_cparams("arbitrary"),
        name="ada_vectors",
    )(c_rows, ada_w, ada_b.reshape(1, n))


def _inproj_kernel(x_ref, er_ref, ec_ref, g_ref, sh_ref, sc_ref, w_ref, o_ref, *, col_chunk):
    x = x_ref[...]
    rows, gw, d = x.shape
    half = d // 2
    xp = jnp.concatenate([x[:, :, :half] + er_ref[...], x[:, :, half:] + ec_ref[...]], axis=-1)
    xp = xp.reshape(rows * gw, d)
    ms = jnp.mean(xp * xp, axis=-1, keepdims=True)
    y = xp * lax.rsqrt(ms + NORM_EPS) * g_ref[...]
    h = (y * (1.0 + sc_ref[...]) + sh_ref[...]).astype(BF16)
    n = o_ref.shape[1]
    for j in range(n // col_chunk):
        sl = slice(j * col_chunk, (j + 1) * col_chunk)
        o_ref[:, sl] = _dot(h, w_ref[:, sl]).astype(o_ref.dtype)


def in_projection(x, emb_r, emb_c, norm_g, shift, scale, w_bf16, tm):
    b, s, d = x.shape
    n = w_bf16.shape[1]
    rows_per_batch = s // GRID_W
    rt = tm // GRID_W
    tiles_per_batch = rows_per_batch // rt
    x3 = x.reshape(b * rows_per_batch, GRID_W, d)
    col_chunk = 512
    return pl.pallas_call(
        functools.partial(_inproj_kernel, col_chunk=col_chunk),
        out_shape=jax.ShapeDtypeStruct((b * s, n), BF16),
        grid=(b * tiles_per_batch,),
        in_specs=[pl.BlockSpec((rt, GRID_W, d), lambda i: (i, 0, 0)),
                  pl.BlockSpec((rt, 1, d // 2), lambda i: (i % tiles_per_batch, 0, 0)),
                  pl.BlockSpec((GRID_W, d // 2), lambda i: (0, 0)),
                  pl.BlockSpec((1, d), lambda i: (0, 0)),
                  pl.BlockSpec((None, 1, d), lambda i: (i // tiles_per_batch, 0, 0)),
                  pl.BlockSpec((None, 1, d), lambda i: (i // tiles_per_batch, 0, 0)),
                  pl.BlockSpec((d, n), lambda i: (0, 0))],
        out_specs=pl.BlockSpec((tm, n), lambda i: (i, 0)),
        compiler_params=_cparams("arbitrary"),
        name="in_projection",
    )(x3, emb_r, emb_c, norm_g.reshape(1, d), shift, scale, w_bf16)


def _hgrn_kernel(*refs, reverse, n_chunks, final):
    if final:
        (q_ref, i_ref, f_ref, lb_ref, s0_ref, of_ref, gate_ref, ng_ref, o_ref, sfin_ref, s_scr) = refs
    else:
        (q_ref, i_ref, f_ref, lb_ref, s0_ref, o_ref, sfin_ref, s_scr) = refs
    cs = HG_CHUNK
    n_heads = s_scr.shape[0]
    dk = s_scr.shape[2]

    @pl.when(pl.program_id(1) == 0)
    def _():
        s_scr[...] = s0_ref[...]

    row = lax.broadcasted_iota(I32, (cs, cs), 0)
    col = lax.broadcasted_iota(I32, (cs, cs), 1)
    tri = (col >= row) if reverse else (col <= row)
    tri_b = tri.astype(BF16)
    end_row = 0 if reverse else cs - 1
    mid_row = cs // 2 if reverse else cs // 2 - 1

    def chunk_body(ci):
        c = (n_chunks - 1 - ci) if reverse else ci
        rows = slice(c * cs, (c + 1) * cs)
        lb = lb_ref[...]
        f = lb + (1.0 - lb) * jax.nn.sigmoid(f_ref[rows, :].astype(F32))
        lf = jnp.log(f)
        lf_hi = lf.astype(BF16)
        lf_lo = (lf - lf_hi.astype(F32)).astype(BF16)
        b_all = _dot(tri_b, lf_hi) + _dot(tri_b, lf_lo)
        k_all = 1.0 - f
        q_all = _silu(q_ref[rows, :].astype(F32))
        for h in range(n_heads):
            sl = slice(h * dk, (h + 1) * dk)
            b = b_all[:, sl]
            q = q_all[:, sl]
            k = k_all[:, sl]
            v = i_ref[rows, sl]
            b_end = b[end_row:end_row + 1]
            b_mid = b[mid_row:mid_row + 1]
            qd = (q * jnp.exp(b - b_mid)).astype(BF16)
            kd = (k * jnp.exp(b_mid - b)).astype(BF16)
            att = lax.dot_general(qd, kd, (((1,), (1,)), ((), ())), preferred_element_type=F32)
            att = jnp.where(tri, att, 0.0).astype(BF16)
            st = s_scr[h]
            qe = (q * jnp.exp(b)).astype(BF16)
            o = _dot(att, v) + lax.dot_general(qe, st.astype(BF16), (((1,), (1,)), ((), ())),
                                               preferred_element_type=F32)
            ke = (k * jnp.exp(b_end - b)).astype(BF16)
            delta_t = lax.dot_general(v, ke, (((0,), (0,)), ((), ())), preferred_element_type=F32)
            s_scr[h] = st * jnp.exp(b_end) + delta_t
            if final:
                o = o + of_ref[rows, sl].astype(F32)
                o = o * lax.rsqrt(jnp.mean(o * o, axis=-1, keepdims=True) + NORM_EPS) * ng_ref[...]
                o = o * _silu(gate_ref[rows, sl].astype(F32))
            o_ref[rows, sl] = o.astype(o_ref.dtype)

    for ci in range(n_chunks):
        chunk_body(ci)
    sfin_ref[...] = s_scr[...]


def hgrn_scan(p, cols, lb, s0, seq, tb, *, reverse, o_fwd=None, gate_col=None, norm_g=None):
    bsz, n_heads, dv, dk = s0.shape
    width = n_heads * dk
    nt = seq // tb
    final = o_fwd is not None

    def tmap(b, t):
        return b * nt + ((nt - 1 - t) if reverse else t)

    def colspec(cb):
        return pl.BlockSpec((tb, width), lambda b, t: (tmap(b, t), cb))

    in_specs = [colspec(cols[0]), colspec(cols[1]), colspec(cols[2]),
                pl.BlockSpec((1, width), lambda b, t: (0, 0)),
                pl.BlockSpec((None, n_heads, dv, dk), lambda b, t: (b, 0, 0, 0))]
    args = [p, p, p, lb.reshape(1, width), s0]
    if final:
        in_specs += [pl.BlockSpec((tb, width), lambda b, t: (tmap(b, t), 0)),
                     colspec(gate_col),
                     pl.BlockSpec((1, dk), lambda b, t: (0, 0))]
        args += [o_fwd, p, norm_g.reshape(1, dk)]
    return pl.pallas_call(
        functools.partial(_hgrn_kernel, reverse=reverse, n_chunks=tb // HG_CHUNK, final=final),
        out_shape=(jax.ShapeDtypeStruct((bsz * seq, width), BF16),
                   jax.ShapeDtypeStruct((bsz, n_heads, dv, dk), F32)),
        grid=(bsz, nt),
        in_specs=in_specs,
        out_specs=(pl.BlockSpec((tb, width), lambda b, t: (tmap(b, t), 0)),
                   pl.BlockSpec((None, n_heads, dv, dk), lambda b, t: (b, 0, 0, 0))),
        scratch_shapes=[pltpu.VMEM((n_heads, dv, dk), F32)],
        compiler_params=_cparams("arbitrary", "arbitrary"),
        name="hgrn_bwd" if reverse else "hgrn_fwd",
    )(*args)


def _shortconv_kernel(p_ref, prev_ref, next_ref, w_ref, b_ref, o_ref, *, tiles_per_batch):
    i = pl.program_id(0)
    ti = i % tiles_per_batch
    p = p_ref[...].astype(F32)
    tm = p.shape[0]
    row = lax.broadcasted_iota(I32, (tm, 1), 0)
    prev_row = jnp.where(ti == 0, 0.0, prev_ref[HALO_ROWS - 1:HALO_ROWS, :].astype(F32))
    next_row = jnp.where(ti == tiles_per_batch - 1, 0.0, next_ref[0:1, :].astype(F32))
    p_prev = jnp.where(row == 0, prev_row, pltpu.roll(p, 1, axis=0))
    p_next = jnp.where(row == tm - 1, next_row, pltpu.roll(p, tm - 1, axis=0))
    u = w_ref[0:1, :] * p_prev + w_ref[1:2, :] * p + w_ref[2:3, :] * p_next + b_ref[...]
    o_ref[...] = u.astype(o_ref.dtype)


def short_conv(p, width, conv_w, conv_b, seq, tm):
    t = p.shape[0]
    nt = t // tm
    tiles_per_batch = seq // tm
    sub = tm // HALO_ROWS
    return pl.pallas_call(
        functools.partial(_shortconv_kernel, tiles_per_batch=tiles_per_batch),
        out_shape=jax.ShapeDtypeStruct((t, width), BF16),
        grid=(nt,),
        in_specs=[pl.BlockSpec((tm, width), lambda i: (i, 0)),
                  pl.BlockSpec((HALO_ROWS, width), lambda i: (jnp.maximum(i * sub - 1, 0), 0)),
                  pl.BlockSpec((HALO_ROWS, width), lambda i: (jnp.minimum((i + 1) * sub, t // HALO_ROWS - 1), 0)),
                  pl.BlockSpec((HY_SHORT, width), lambda i: (0, 0)),
                  pl.BlockSpec((1, width), lambda i: (0, 0))],
        out_specs=pl.BlockSpec((tm, width), lambda i: (i, 0)),
        compiler_params=_cparams("arbitrary"),
        name="short_conv",
    )(p, p, p, conv_w, conv_b.reshape(1, width))


def _filter_kernel(band_ref, w1t_ref, w1c_ref, w1s_ref, b1_ref, w2_ref, b2_ref, w3_ref, b3_ref,
                   w4f_ref, w4b_ref, fr_ref, delta_ref, k_ref, s_ref, *, seq):
    step = pl.program_id(0)
    gb, q, ncol = k_ref.shape
    half = q // 2
    width = delta_ref.shape[1]
    nrow = gb * q
    nf = gb * half

    def positions(shape, axis):
        r = lax.broadcasted_iota(I32, shape, axis)
        is_bwd = r >= nf
        rr = jnp.where(is_bwd, r - nf, r)
        j = lax.shift_right_logical(rr, int(math.log2(half)))
        a = (rr & (half - 1)) + jnp.where(is_bwd, half, 0)
        n = (a * DFT_P + step * gb + j).astype(F32)
        t = jnp.where(is_bwd, 2.0 * seq - n, n)
        return n, t, t / float(max(seq - 1, 1))

    _, t_l, tn_l = positions((1, nrow), 1)
    ang = (2.0 * math.pi / seq) * t_l * band_ref[...]
    fr = fr_ref[...]
    pre = (w1t_ref[...] * tn_l + _dot_hi(w1c_ref[...], jnp.cos(ang)) - _dot_hi(w1s_ref[...], jnp.sin(ang))
           + b1_ref[...])
    act = jnp.sin(fr * pre)
    act = jnp.sin(fr * (_dot_hi(w2_ref[...], act) + b2_ref[...]))
    act = jnp.sin(fr * (_dot_hi(w3_ref[...], act) + b3_ref[...])).astype(BF16)
    n_s, _, tn_s = positions((nrow, 1), 0)
    delta = jnp.concatenate([delta_ref[...]] * (ncol // width), axis=1)
    tdot = lambda x, w: lax.dot_general(x, w, (((0,), (0,)), ((), ())), preferred_element_type=F32)
    hf = tdot(act[:, :nf], w4f_ref[...]) * jnp.exp(-tn_s[:nf] * delta)
    hb = tdot(act[:, nf:], w4b_ref[...]) * jnp.exp(-tn_s[nf:] * delta)
    hb = jnp.where(n_s[nf:] == float(seq), 0.0, hb)
    k_ref[:, :half, :] = hf.reshape(gb, half, ncol).astype(k_ref.dtype)
    k_ref[:, half:, :] = hb.reshape(gb, half, ncol).astype(k_ref.dtype)
    tot = jnp.sum(jnp.abs(hf), axis=0, keepdims=True) + jnp.sum(jnp.abs(hb), axis=0, keepdims=True)

    @pl.when(step == 0)
    def _():
        s_ref[...] = jnp.zeros_like(s_ref)

    s_ref[...] += tot


def hyena_filter_taps(seq, w1, b1, w2, b2, w3, b3, w4, freq, width):
    emb = w1.shape[0]
    hid = w1.shape[1]
    bands = (emb - 1) // 2
    q = 2 * seq // DFT_P
    ncol = HY_ORDER * width
    band = np.linspace(1e-4, bands - 1, bands, dtype=np.float32).reshape(bands, 1)
    min_decay = math.log(HY_DECAY_TARGET) / HY_SLOW_PCT
    max_decay = math.log(HY_DECAY_TARGET) / HY_FAST_PCT
    delta = np.abs(np.linspace(min_decay, max_decay, width, dtype=np.float32)).reshape(1, width)
    w1t = w1.astype(F32).T
    col = lambda v: v.reshape(hid, 1).astype(F32)
    w4r = w4.astype(BF16).reshape(hid, HY_ORDER, 2, width)
    w4f = w4r[:, :, 0, :].reshape(hid, ncol)
    w4b = w4r[:, :, 1, :].reshape(hid, ncol)
    gb = DFT_GROUP
    const = lambda shape: pl.BlockSpec(shape, lambda i: tuple(0 for _ in shape))
    return pl.pallas_call(
        functools.partial(_filter_kernel, seq=seq),
        out_shape=(jax.ShapeDtypeStruct((DFT_P, q, ncol), BF16),
                   jax.ShapeDtypeStruct((1, ncol), F32)),
        grid=(DFT_P // gb,),
        in_specs=[const((bands, 1)), const((hid, 1)), const((hid, bands)), const((hid, bands)), const((hid, 1)),
                  const((hid, hid)), const((hid, 1)), const((hid, hid)), const((hid, 1)),
                  const((hid, ncol)), const((hid, ncol)), const((hid, 1)), const((1, width))],
        out_specs=(pl.BlockSpec((gb, q, ncol), lambda i: (i, 0, 0)),
                   pl.BlockSpec((1, ncol), lambda i: (0, 0))),
        compiler_params=_cparams("arbitrary"),
        name="hyena_filter",
    )(jnp.asarray(band), w1t[:, 0:1], w1t[:, 1:1 + bands], w1t[:, 1 + bands:1 + 2 * bands], col(b1),
      w2.astype(F32).T, col(b2), w3.astype(F32).T, col(b3), w4f, w4b, col(freq), jnp.asarray(delta))


def _dft_tables(seq):
    p = DFT_P
    n_fft = 2 * seq
    q = n_fft // p
    qh = q // 2
    ka = jnp.arange(q, dtype=I32)
    a = jnp.arange(q, dtype=I32)
    b = jnp.arange(p, dtype=I32)
    nn = a[None, :] * p + b[:, None]
    ph = (ka[None, :, None] * nn[:, None, :]) % n_fft
    ang = ph.astype(F32) * (2.0 * math.pi / n_fft)
    mr, mi = jnp.cos(ang), -jnp.sin(ang)
    m1c = jnp.concatenate([jnp.concatenate([mr[:, :, :qh], -mi[:, :, :qh]], axis=2),
                           jnp.concatenate([mi[:, :, :qh], mr[:, :, :qh]], axis=2)], axis=1)
    m1r = jnp.concatenate([mr, mi], axis=1)
    gr = jnp.swapaxes(mr[:, :, :qh], 1, 2) / n_fft
    gi = -jnp.swapaxes(mi[:, :, :qh], 1, 2) / n_fft
    m4 = jnp.concatenate([jnp.concatenate([gr, -gi], axis=2), jnp.concatenate([gi, gr], axis=2)], axis=1)
    kb = np.arange(p)
    ang2 = 2.0 * np.pi * ((kb[:, None] * kb[None, :]) % p) / p
    fr, fi = np.cos(ang2), -np.sin(ang2)
    m2 = np.block([[fr, -fi], [fi, fr]]).astype(np.float32)
    m3 = np.block([[fr, fi], [-fi, fr]]).astype(np.float32)
    return (m1c.astype(BF16), m1r.astype(BF16), jnp.asarray(m2, BF16), jnp.asarray(m3, BF16), m4.astype(BF16))


def _bmm_kernel(w_ref, x_ref, o_ref, *, shared_w):
    for j in range(x_ref.shape[0]):
        w = w_ref[...] if shared_w else w_ref[j]
        o_ref[j] = _dot(w, x_ref[j]).astype(o_ref.dtype)


def batched_left_matmul(w, x, col_block, ncols, name):
    g, k = x.shape[0], x.shape[1]
    shared = w.ndim == 2
    m = w.shape[-2]
    gb = DFT_GROUP
    wspec = (pl.BlockSpec((m, k), lambda i: (0, 0)) if shared
             else pl.BlockSpec((gb, m, k), lambda i: (i, 0, 0)))
    return pl.pallas_call(
        functools.partial(_bmm_kernel, shared_w=shared),
        out_shape=jax.ShapeDtypeStruct((g, m, ncols), BF16),
        grid=(g // gb,),
        in_specs=[wspec, pl.BlockSpec((gb, k, ncols), lambda i: (i, 0, col_block))],
        out_specs=pl.BlockSpec((gb, m, ncols), lambda i: (i, 0, 0)),
        compiler_params=_cparams("arbitrary"),
        name=name,
    )(w, x)


def _dft_mid_kernel(m2_ref, m3_ref, x_ref, k_ref, o_ref):
    half = x_ref.shape[1] // 2
    for j in range(x_ref.shape[0]):
        xf = _dot(m2_ref[...], x_ref[j])
        kf = k_ref[j].astype(F32)
        xr, xi = xf[:half], xf[half:]
        kr, ki = kf[:half], kf[half:]
        z = jnp.concatenate([xr * kr - xi * ki, xr * ki + xi * kr], axis=0).astype(BF16)
        o_ref[j] = _dot(m3_ref[...], z).astype(o_ref.dtype)


def dft_mid(m2, m3, x, kspec, kcol, ncols):
    g, r = x.shape[0], x.shape[1]
    gb = DFT_GROUP
    return pl.pallas_call(
        _dft_mid_kernel,
        out_shape=jax.ShapeDtypeStruct((g, r, ncols), BF16),
        grid=(g // gb,),
        in_specs=[pl.BlockSpec((r, r), lambda i: (0, 0)),
                  pl.BlockSpec((r, r), lambda i: (0, 0)),
                  pl.BlockSpec((gb, r, ncols), lambda i: (i, 0, 0)),
                  pl.BlockSpec((gb, r, ncols), lambda i: (i, 0, kcol))],
        out_specs=pl.BlockSpec((gb, r, ncols), lambda i: (i, 0, 0)),
        compiler_params=_cparams("arbitrary"),
        name="dft_mid",
    )(m2, m3, x, kspec)


def _dft_out_kernel(m4_ref, y_ref, inv_ref, skip_ref, v_ref, mul_ref, o_ref):
    for j in range(y_ref.shape[0]):
        conv = _dot(m4_ref[j], y_ref[j]) * inv_ref[...] + v_ref[j].astype(F32) * skip_ref[...]
        o_ref[j] = (mul_ref[j].astype(F32) * conv).astype(o_ref.dtype)


def dft_out(m4, y, inv_l1, skip, u, v_col, mul, mul_col, ncols):
    g, r = y.shape[0], y.shape[1]
    rows = m4.shape[1]
    gb = DFT_GROUP
    return pl.pallas_call(
        _dft_out_kernel,
        out_shape=jax.ShapeDtypeStruct((g, rows, ncols), BF16),
        grid=(g // gb,),
        in_specs=[pl.BlockSpec((gb, rows, r), lambda i: (i, 0, 0)),
                  pl.BlockSpec((gb, r, ncols), lambda i: (i, 0, 0)),
                  pl.BlockSpec((1, ncols), lambda i: (0, 0)),
                  pl.BlockSpec((1, ncols), lambda i: (0, 0)),
                  pl.BlockSpec((gb, rows, ncols), lambda i: (i, 0, v_col)),
                  pl.BlockSpec((gb, rows, ncols), lambda i: (i, 0, mul_col))],
        out_specs=pl.BlockSpec((gb, rows, ncols), lambda i: (i, 0, 0)),
        compiler_params=_cparams("arbitrary"),
        name="dft_out",
    )(m4, y, inv_l1, skip, u, mul)


def _swap_ab(x):
    g1, r, c = x.shape
    g2 = r // 2
    return x.reshape(g1, 2, g2, c).transpose(2, 1, 0, 3).reshape(g2, 2 * g1, c)


def hyena_branch(u_nat, bsz, seq, width, taps, l1, skip):
    p = DFT_P
    qh = seq // p
    m1c, m1r, m2, m3, m4 = _dft_tables(seq)
    ncol = HY_ORDER * width
    ks1 = batched_left_matmul(m1r, taps, 0, ncol, "dft_k1")
    kspec = batched_left_matmul(m2, _swap_ab(ks1), 0, ncol, "dft_k2")
    inv_l1 = 1.0 / l1
    u = u_nat.reshape(bsz, qh, p, 3 * width).transpose(2, 0, 1, 3).reshape(p, bsz * qh, 3 * width)
    z = None
    for order in range(HY_ORDER):
        src, src_col = (u, 0) if order == 0 else (z, 0)
        s1 = batched_left_matmul(m1c, src, src_col, width, "dft_s1")
        mid = dft_mid(m2, m3, _swap_ab(s1), kspec, order, width)
        z = dft_out(m4, _swap_ab(mid), inv_l1[:, order * width:(order + 1) * width],
                    skip[order].reshape(1, width).astype(F32), src, src_col, u, order + 1, width)
    return z.reshape(p, bsz, qh, width).transpose(1, 2, 0, 3).reshape(bsz * seq, width)


def _pack_pairs(x):
    w = x.shape[1] // 2
    u = lax.bitcast_convert_type(x, U32)
    r = (u + U32(0x7FFF) + ((u >> 16) & U32(1))) >> 16
    return r[:, :w] | (r[:, w:] << 16)


def _unpack_pairs(p):
    lo = lax.bitcast_convert_type(p << 16, F32)
    hi = lax.bitcast_convert_type(p & U32(0xFFFF0000), F32)
    return jnp.concatenate([lo, hi], axis=1)


def _merge_kernel(x_ref, er_ref, ec_ref, yhy_ref, yhg_ref, ghy_ref, ghg_ref, why_ref, whg_ref, wo_ref,
                  g1_ref, n2_ref, sh2_ref, sc2_ref, g2_ref, rwt_ref, sgu_ref, sd_ref,
                  xres_ref, h2p_ref, lg_ref):
    x = x_ref[...]
    rows, gw, d = x.shape
    half = d // 2
    xp = jnp.concatenate([x[:, :, :half] + er_ref[...], x[:, :, half:] + ec_ref[...]], axis=-1)
    xp = xp.reshape(rows * gw, d)
    m = (jax.nn.sigmoid(ghy_ref[...].astype(F32)) * _dot(yhy_ref[...], why_ref[...])
         + jax.nn.sigmoid(ghg_ref[...].astype(F32)) * _dot(yhg_ref[...], whg_ref[...]))
    x1 = xp + g1_ref[...] * _dot(m.astype(BF16), wo_ref[...])
    ms = jnp.mean(x1 * x1, axis=-1, keepdims=True)
    h2 = x1 * lax.rsqrt(ms + NORM_EPS) * n2_ref[...] * (1.0 + sc2_ref[...]) + sh2_ref[...]
    lg_ref[...] = lax.dot_general(rwt_ref[...], h2, (((1,), (1,)), ((), ())),
                                  preferred_element_type=F32, precision=HIGHEST)
    h2b = h2.astype(BF16)
    gu = _dot(h2b, sgu_ref[...])
    fs = gu.shape[1] // 2
    shared = _dot((_silu(gu[:, :fs]) * gu[:, fs:]).astype(BF16), sd_ref[...])
    xres_ref[...] = x1 + g2_ref[...] * shared
    h2p_ref[...] = _pack_pairs(h2)


def merge_stage(x, emb_r, emb_c, y_hy, y_hg, p, gate_cols, w_hy_out, w_hg_out, w_out, g1, norm2_g,
                sh2, sc2, g2, router_wt, sh_gate_up, sh_down, tm):
    b, s, d = x.shape
    rows_per_batch = s // GRID_W
    rt = tm // GRID_W
    tiles_per_batch = rows_per_batch // rt
    x3 = x.reshape(b * rows_per_batch, GRID_W, d)
    wb = y_hy.shape[1]
    ne = router_wt.shape[0]
    fs2 = sh_gate_up.shape[1]
    tok = lambda cb, w: pl.BlockSpec((tm, w), lambda i: (i, cb))
    const = lambda shape: pl.BlockSpec(shape, lambda i: tuple(0 for _ in shape))
    per_b = pl.BlockSpec((None, 1, d), lambda i: (i // tiles_per_batch, 0, 0))
    return pl.pallas_call(
        _merge_kernel,
        out_shape=(jax.ShapeDtypeStruct((b * s, d), F32),
                   jax.ShapeDtypeStruct((b * s, d // 2), U32),
                   jax.ShapeDtypeStruct((ne, b * s), F32)),
        grid=(b * tiles_per_batch,),
        in_specs=[pl.BlockSpec((rt, GRID_W, d), lambda i: (i, 0, 0)),
                  pl.BlockSpec((rt, 1, d // 2), lambda i: (i % tiles_per_batch, 0, 0)),
                  const((GRID_W, d // 2)),
                  tok(0, wb), tok(0, wb), tok(gate_cols[0], d), tok(gate_cols[1], d),
                  const((wb, d)), const((wb, d)), const((d, d)),
                  per_b, const((1, d)), per_b, per_b, per_b,
                  const((ne, d)), const((d, fs2)), const((fs2 // 2, d))],
        out_specs=(pl.BlockSpec((tm, d), lambda i: (i, 0)),
                   pl.BlockSpec((tm, d // 2), lambda i: (i, 0)),
                   pl.BlockSpec((ne, tm), lambda i: (0, i))),
        compiler_params=_cparams("arbitrary"),
        name="merge",
    )(x3, emb_r, emb_c, y_hy, y_hg, p, p, w_hy_out, w_hg_out, w_out, g1, norm2_g.reshape(1, d),
      sh2, sc2, g2, router_wt, sh_gate_up, sh_down)


def _route_kernel(lg_ref, bias_ref, eidx_ref, wsel_ref, rank_ref, cnt_ref, carry):
    ne, tr = lg_ref.shape
    gsz = ne // N_GROUPS
    neg = -jnp.inf

    @pl.when(pl.program_id(0) == 0)
    def _():
        carry[...] = jnp.zeros_like(carry)

    scores = jax.nn.sigmoid(lg_ref[...])
    biased = scores + bias_ref[...]
    riota = lax.broadcasted_iota(I32, (gsz, tr), 0).astype(F32)
    gs = []
    for g in range(N_GROUPS):
        vg = biased[g * gsz:(g + 1) * gsz]
        m1 = jnp.max(vg, axis=0, keepdims=True)
        i1 = jnp.min(jnp.where(vg == m1, riota, float(gsz)), axis=0, keepdims=True)
        m2 = jnp.max(jnp.where(riota == i1, neg, vg), axis=0, keepdims=True)
        gs.append(m1 + m2)
    cur = jnp.concatenate(gs, axis=0)
    giota = lax.broadcasted_iota(I32, (N_GROUPS, tr), 0).astype(F32)
    gsel = jnp.zeros((N_GROUPS, tr), F32)
    for _ in range(TOPK_GROUPS):
        m = jnp.max(cur, axis=0, keepdims=True)
        idx = jnp.min(jnp.where(cur == m, giota, float(N_GROUPS)), axis=0, keepdims=True)
        hit = giota == idx
        gsel = jnp.where(hit, 1.0, gsel)
        cur = jnp.where(hit, neg, cur)
    cur = jnp.concatenate([jnp.where(gsel[g:g + 1] > 0.0, biased[g * gsz:(g + 1) * gsz], neg)
                           for g in range(N_GROUPS)], axis=0)
    eiota = lax.broadcasted_iota(I32, (ne, tr), 0).astype(F32)
    chosen = jnp.zeros((ne, tr), F32)
    idxs, ws = [], []
    for _ in range(TOP_K):
        m = jnp.max(cur, axis=0, keepdims=True)
        idx = jnp.min(jnp.where(cur == m, eiota, float(ne)), axis=0, keepdims=True)
        hit = eiota == idx
        idxs.append(idx)
        ws.append(jnp.sum(jnp.where(hit, scores, 0.0), axis=0, keepdims=True))
        chosen = jnp.where(hit, 1.0, chosen)
        cur = jnp.where(hit, neg, cur)
    w = jnp.concatenate(ws, axis=0)
    wsel_ref[...] = w / jnp.sum(w, axis=0, keepdims=True) * ROUTED_SCALE
    eidx_ref[...] = jnp.concatenate(idxs, axis=0).astype(I32)
    srow = lax.broadcasted_iota(I32, (tr, tr), 0)
    scol = lax.broadcasted_iota(I32, (tr, tr), 1)
    before = (srow < scol).astype(BF16)
    base = carry[...] + _dot(chosen.astype(BF16), before)
    ranks = [jnp.sum(jnp.where(eiota == idx, base, 0.0), axis=0, keepdims=True) for idx in idxs]
    rank_ref[...] = jnp.concatenate(ranks, axis=0).astype(I32)
    carry[...] += jnp.sum(chosen, axis=1, keepdims=True)
    cnt_ref[...] = carry[...]


def route(logits_t, router_bias, tr):
    ne, t = logits_t.shape
    return pl.pallas_call(
        _route_kernel,
        out_shape=(jax.ShapeDtypeStruct((TOP_K, t), I32),
                   jax.ShapeDtypeStruct((TOP_K, t), F32),
                   jax.ShapeDtypeStruct((TOP_K, t), I32),
                   jax.ShapeDtypeStruct((ne, 1), F32)),
        grid=(t // tr,),
        in_specs=[pl.BlockSpec((ne, tr), lambda i: (0, i)),
                  pl.BlockSpec((ne, 1), lambda i: (0, 0))],
        out_specs=(pl.BlockSpec((TOP_K, tr), lambda i: (0, i)),
                   pl.BlockSpec((TOP_K, tr), lambda i: (0, i)),
                   pl.BlockSpec((TOP_K, tr), lambda i: (0, i)),
                   pl.BlockSpec((ne, 1), lambda i: (0, 0))),
        scratch_shapes=[pltpu.VMEM((ne, 1), F32)],
        compiler_params=_cparams("arbitrary"),
        name="route",
    )(logits_t, router_bias.reshape(ne, 1).astype(F32))


def _dest_kernel(cnt_ref, eidx_ref, rank_ref, dest_ref, be_ref, nv_ref, nb_ref, start_scr, *, n_blocks):
    ne = cnt_ref.shape[0]
    tr = eidx_ref.shape[1]

    @pl.when(pl.program_id(0) == 0)
    def _():
        cnt = jnp.broadcast_to(cnt_ref[...], (ne, LANES))
        padded = jnp.floor((cnt + float(MOE_ROWS - 1)) / float(MOE_ROWS)) * float(MOE_ROWS)
        r = lax.broadcasted_iota(I32, (ne, ne), 0)
        c = lax.broadcasted_iota(I32, (ne, ne), 1)
        start = _dot_hi((c < r).astype(F32), padded)
        start_scr[...] = start
        end = start[:, 0:1] + padded[:, 0:1]
        used = start[:, 0:1] + cnt[:, 0:1]
        nbl = be_ref.shape[1]
        blk_row = (lax.broadcasted_iota(I32, (1, nbl), 1) * MOE_ROWS).astype(F32)
        total = jnp.max(end, axis=0, keepdims=True)
        last_row = total - float(MOE_ROWS)
        blk_row_c = jnp.minimum(blk_row, last_row)
        e_of = jnp.sum((end <= blk_row_c).astype(F32), axis=0, keepdims=True)
        e_of = jnp.minimum(e_of, float(ne - 1))
        eio = lax.broadcasted_iota(I32, (ne, nbl), 0).astype(F32)
        used_e = jnp.sum(jnp.where(eio == e_of, used, 0.0), axis=0, keepdims=True)
        valid = jnp.clip(used_e - blk_row_c, 0.0, float(MOE_ROWS))
        be_ref[...] = e_of.astype(I32)
        nv_ref[...] = jnp.where(blk_row <= last_row, valid, 0.0).astype(I32)
        nb_ref[...] = jnp.broadcast_to(total / float(MOE_ROWS), nb_ref.shape).astype(I32)

    eiota = lax.broadcasted_iota(I32, (ne, tr), 0)
    start_col = start_scr[:, 0:1]
    rows = []
    for k in range(TOP_K):
        hit = eiota == eidx_ref[k:k + 1, :]
        rows.append(jnp.sum(jnp.where(hit, start_col, 0.0), axis=0, keepdims=True))
    dest_ref[...] = jnp.concatenate(rows, axis=0).astype(I32) + rank_ref[...]


def dispatch_plan(counts, eidx, rank, tr, n_blocks):
    ne = counts.shape[0]
    t = eidx.shape[1]
    nbl = pl.cdiv(n_blocks, LANES) * LANES
    return pl.pallas_call(
        functools.partial(_dest_kernel, n_blocks=n_blocks),
        out_shape=(jax.ShapeDtypeStruct((TOP_K, t), I32),
                   jax.ShapeDtypeStruct((1, nbl), I32),
                   jax.ShapeDtypeStruct((1, nbl), I32),
                   jax.ShapeDtypeStruct((1, LANES), I32)),
        grid=(t // tr,),
        in_specs=[pl.BlockSpec((ne, 1), lambda i: (0, 0)),
                  pl.BlockSpec((TOP_K, tr), lambda i: (0, i)),
                  pl.BlockSpec((TOP_K, tr), lambda i: (0, i))],
        out_specs=(pl.BlockSpec((TOP_K, tr), lambda i: (0, i)),
                   pl.BlockSpec((1, nbl), lambda i: (0, 0)),
                   pl.BlockSpec((1, nbl), lambda i: (0, 0)),
                   pl.BlockSpec((1, LANES), lambda i: (0, 0))),
        scratch_shapes=[pltpu.VMEM((ne, LANES), F32)],
        compiler_params=_cparams("arbitrary"),
        name="dispatch_plan",
    )(counts, eidx, rank)


def _scatter_kernel(dest_ref, h_ref, xs_ref, sem):
    tm = h_ref.shape[0]

    def row_copy(t, d):
        return pltpu.make_async_copy(h_ref.at[pl.ds(t, 1)], xs_ref.at[pl.ds(d, 1)], sem)

    def issue(t, carry):
        for k in range(TOP_K):
            row_copy(t, dest_ref[k, t]).start(priority=k % 2)
        return carry

    lax.fori_loop(0, tm, issue, 0, unroll=2)

    def drain(t, carry):
        for k in range(TOP_K):
            row_copy(0, 0).wait()
        return carry

    lax.fori_loop(0, tm, drain, 0)


def scatter_rows(dest, h2p, n_rows, tm):
    t, w = h2p.shape
    return pl.pallas_call(
        _scatter_kernel,
        out_shape=jax.ShapeDtypeStruct((n_rows, w), U32),
        grid=(t // tm,),
        in_specs=[pl.BlockSpec((TOP_K, tm), lambda i: (0, i), memory_space=pltpu.SMEM),
                  pl.BlockSpec((tm, w), lambda i: (i, 0))],
        out_specs=pl.BlockSpec(memory_space=pl.ANY),
        scratch_shapes=[pltpu.SemaphoreType.DMA(())],
        compiler_params=_cparams("arbitrary"),
        name="scatter_rows",
    )(dest, h2p)


def _gmm_kernel(be_ref, nv_ref, nb_ref, xs_ref, wg_ref, wu_ref, wd_ref, y_ref):
    j = pl.program_id(0)

    @pl.when(j < nb_ref[0])
    def _():
        x = _unpack_pairs(xs_ref[...])
        row = lax.broadcasted_iota(I32, (x.shape[0], 1), 0)
        x = jnp.where(row < nv_ref[j], x, 0.0)
        hmid = _silu(_dot(x, wg_ref[...])) * _dot(x, wu_ref[...])
        y_ref[...] = _pack_pairs(_dot(hmid, wd_ref[...]))


def grouped_mlp(block_e, block_valid, n_used, xs, w_gate, w_up, w_down, n_blocks):
    ne, d, f = w_gate.shape
    w = xs.shape[1]
    last = lambda j, nb: jnp.minimum(j, nb[0] - 1)
    grid_spec = pltpu.PrefetchScalarGridSpec(
        num_scalar_prefetch=3,
        grid=(n_blocks,),
        in_specs=[pl.BlockSpec((MOE_ROWS, w), lambda j, be, nv, nb: (last(j, nb), 0)),
                  pl.BlockSpec((None, d, f), lambda j, be, nv, nb: (be[j], 0, 0)),
                  pl.BlockSpec((None, d, f), lambda j, be, nv, nb: (be[j], 0, 0)),
                  pl.BlockSpec((None, f, d), lambda j, be, nv, nb: (be[j], 0, 0))],
        out_specs=pl.BlockSpec((MOE_ROWS, w), lambda j, be, nv, nb: (last(j, nb), 0)),
    )
    return pl.pallas_call(
        _gmm_kernel,
        out_shape=jax.ShapeDtypeStruct(xs.shape, U32),
        grid_spec=grid_spec,
        compiler_params=_cparams("arbitrary"),
        name="grouped_mlp",
    )(block_e, block_valid, n_used, xs, w_gate, w_up, w_down)


def _combine_kernel(dest_ref, xres_ref, wt_ref, g2_ref, fg_ref, ys_ref, o_ref, buf, sem):
    tm = xres_ref.shape[0]

    def row_copy(k, t, d):
        return pltpu.make_async_copy(ys_ref.at[pl.ds(d, 1)], buf.at[k, pl.ds(t, 1)], sem)

    def issue(t, carry):
        for k in range(TOP_K):
            row_copy(k, t, dest_ref[k, t]).start(priority=k % 2)
        return carry

    lax.fori_loop(0, tm, issue, 0, unroll=2)

    def drain(t, carry):
        for k in range(TOP_K):
            row_copy(0, 0, 0).wait()
        return carry

    lax.fori_loop(0, tm, drain, 0)

    wt = wt_ref[...]
    routed = jnp.zeros(xres_ref.shape, F32)
    for k in range(TOP_K):
        routed = routed + wt[:, k:k + 1] * _unpack_pairs(buf[k])
    x2 = xres_ref[...] + g2_ref[...] * routed
    ms = jnp.mean(x2 * x2, axis=-1, keepdims=True)
    o_ref[...] = x2 * lax.rsqrt(ms + NORM_EPS) * fg_ref[...]


def combine(dest, xres, wsel_t, g2, final_g, ys, seq, tm):
    t, d = xres.shape
    tiles_per_batch = seq // tm
    return pl.pallas_call(
        _combine_kernel,
        out_shape=jax.ShapeDtypeStruct((t, d), F32),
        grid=(t // tm,),
        in_specs=[pl.BlockSpec((TOP_K, tm), lambda i: (0, i), memory_space=pltpu.SMEM),
                  pl.BlockSpec((tm, d), lambda i: (i, 0)),
                  pl.BlockSpec((tm, TOP_K), lambda i: (i, 0)),
                  pl.BlockSpec((None, 1, d), lambda i: (i // tiles_per_batch, 0, 0)),
                  pl.BlockSpec((1, d), lambda i: (0, 0)),
                  pl.BlockSpec(memory_space=pl.ANY)],
        out_specs=pl.BlockSpec((tm, d), lambda i: (i, 0)),
        scratch_shapes=[pltpu.VMEM((TOP_K, tm, d // 2), U32), pltpu.SemaphoreType.DMA(())],
        compiler_params=_cparams("arbitrary"),
        name="combine",
    )(dest, xres, wsel_t, g2, final_g.reshape(1, d), ys)


def _pos_tables(rows, cols, dim):
    quarter = dim // 4
    omega = 1.0 / (POS_BASE ** (np.arange(quarter, dtype=np.float32) / quarter))
    ang_r = np.arange(rows, dtype=np.float32)[:, None] * omega
    ang_c = np.arange(cols, dtype=np.float32)[:, None] * omega
    emb_r = np.concatenate([np.sin(ang_r), np.cos(ang_r)], axis=-1).astype(np.float32)
    emb_c = np.concatenate([np.sin(ang_c), np.cos(ang_c)], axis=-1).astype(np.float32)
    return jnp.asarray(emb_r.reshape(rows, 1, dim // 2)), jnp.asarray(emb_c)


def kernel(x, c, ctx, c_ctx, norm1_g, norm2_g, ada_w, ada_b, w_in, hy_conv_w, hy_conv_b, hy_f_w1, hy_f_b1, hy_f_w2, hy_f_b2, hy_f_w3, hy_f_b3, hy_f_w4, hy_f_freq, hy_skip, hg_lb_logits, hg_norm_g, w_hy_out, w_hg_out, w_out, router_w, router_bias, exp_w_gate, exp_w_up, exp_w_down, sh_w_gate, sh_w_up, sh_w_down, final_g):
    bsz, seq, d = x.shape
    n_ctx = ctx.shape[1]
    hy_w = w_hy_out.shape[1]
    hg_w = w_hg_out.shape[1]
    dk = hg_norm_g.shape[1]
    n_heads = hg_w // dk
    ne = router_w.shape[2]
    l = 0

    c_rows = jnp.zeros((SUBLANES, d), F32).at[:bsz].set(c).at[bsz].set(c_ctx)
    mods = ada_vectors(c_rows, ada_w[l], ada_b[l])
    sh1, sc1, g1, sh2, sc2, g2 = [mods[:bsz, j * d:(j + 1) * d].reshape(bsz, 1, d) for j in range(N_ADA)]
    csh1 = jnp.broadcast_to(mods[bsz, 0:d].reshape(1, 1, d), (bsz, 1, d))
    csc1 = jnp.broadcast_to(mods[bsz, d:2 * d].reshape(1, 1, d), (bsz, 1, d))

    emb_r, emb_c = _pos_tables(seq // GRID_W, GRID_W, d)
    w_in_b = w_in[l].astype(BF16)
    hy_proj = 3 * hy_w
    p = in_projection(x, emb_r, emb_c, norm1_g[l], sh1, sc1, w_in_b, TOKEN_TILE)
    hg_cols = slice(hy_proj, hy_proj + 5 * hg_w)
    zero_r = jnp.zeros((n_ctx // GRID_W, 1, d // 2), F32)
    zero_c = jnp.zeros((GRID_W, d // 2), F32)
    pc = in_projection(ctx, zero_r, zero_c, norm1_g[l], csh1, csc1, w_in_b[:, hg_cols], n_ctx)

    lbs = jnp.cumsum(jax.nn.softmax(hg_lb_logits.astype(F32), axis=0), axis=0)
    lb_f, lb_b = lbs[l, 0], lbs[l, 1]
    zero_state = jnp.zeros((bsz, n_heads, dk, dk), F32)
    base = hy_proj // hg_w
    _, st_f = hgrn_scan(pc, (0, 1, 2), lb_f, zero_state, n_ctx, n_ctx, reverse=False)
    _, st_b = hgrn_scan(pc, (0, 1, 3), lb_b, zero_state, n_ctx, n_ctx, reverse=True)
    o_f, _ = hgrn_scan(p, (base, base + 1, base + 2), lb_f, st_f, seq, HG_TIME_BLOCK, reverse=False)
    y_hg, _ = hgrn_scan(p, (base, base + 1, base + 3), lb_b, st_b, seq, HG_TIME_BLOCK, reverse=True,
                        o_fwd=o_f, gate_col=base + 4, norm_g=hg_norm_g[l])

    u = short_conv(p, hy_proj, hy_conv_w[l], hy_conv_b[l], seq, TOKEN_TILE)
    taps, l1 = hyena_filter_taps(seq, hy_f_w1[l], hy_f_b1[l], hy_f_w2[l], hy_f_b2[l], hy_f_w3[l], hy_f_b3[l],
                                 hy_f_w4[l], hy_f_freq[l], hy_w)
    y_hy = hyena_branch(u, bsz, seq, hy_w, taps, l1, hy_skip[l])

    gate_base = (hy_proj + 5 * hg_w) // d
    sh_gu = jnp.concatenate([sh_w_gate[l], sh_w_up[l]], axis=1).astype(BF16)
    xres, h2p, logits_t = merge_stage(
        x, emb_r, emb_c, y_hy, y_hg, p, (gate_base, gate_base + 1),
        w_hy_out[l].astype(BF16), w_hg_out[l].astype(BF16), w_out[l].astype(BF16), g1, norm2_g[l],
        sh2, sc2, g2, router_w[l].T.astype(F32), sh_gu, sh_w_down[l].astype(BF16), TOKEN_TILE)

    t = bsz * seq
    eidx, wsel, rank, counts = route(logits_t, router_bias[l], TOKEN_TILE)
    n_rows = t * TOP_K + ne * (MOE_ROWS - 1)
    n_blocks = pl.cdiv(n_rows, MOE_ROWS)
    dest, block_e, block_valid, n_used = dispatch_plan(counts, eidx, rank, TOKEN_TILE, n_blocks)

    xs = scatter_rows(dest, h2p, n_blocks * MOE_ROWS, DMA_TILE)
    ys = grouped_mlp(block_e.reshape(-1), block_valid.reshape(-1), n_used.reshape(-1)[:1], xs,
                     exp_w_gate[l], exp_w_up[l], exp_w_down[l], n_blocks)
    out = combine(dest, xres, wsel.T, g2, final_g, ys, seq, DMA_TILE)
    return out.reshape(bsz, seq, d)
```

```python
import functools
import math

import numpy as np
import jax
import jax.numpy as jnp
from jax import lax
from jax.experimental import pallas as pl
from jax.experimental.pallas import tpu as pltpu

F32 = jnp.float32
BF16 = jnp.bfloat16
U32 = jnp.uint32
I32 = jnp.int32
HIGHEST = lax.Precision.HIGHEST

GRID_W = 64
POS_BASE = 10000.0
NORM_EPS = 1e-6
N_ADA = 6
HY_ORDER = 2
HY_SHORT = 3
HY_DECAY_TARGET = 1e-2
HY_FAST_PCT = 0.3
HY_SLOW_PCT = 1.5
HG_HEADS = 4
HG_CHUNK = 64
N_GROUPS = 8
TOPK_GROUPS = 4
TOP_K = 8
ROUTED_SCALE = 2.5

LANES = 128
SUBLANES = 8
VMEM_LIMIT = 56 * 1024 * 1024

TOKEN_TILE = 512
HG_TIME_BLOCK = 512
HALO_ROWS = 16
DFT_P = 128
DFT_GROUP = 8
MOE_ROWS = 256
MOE_STEP_BLOCKS = 4
SCATTER_TILE = 512
GATHER_TILE = 256


def _cparams(*sem):
    return pltpu.CompilerParams(dimension_semantics=sem, vmem_limit_bytes=VMEM_LIMIT)


def _dot(a, b):
    return jnp.dot(a, b, preferred_element_type=F32)


def _dot_hi(a, b):
    return jnp.dot(a, b, preferred_element_type=F32, precision=HIGHEST)


def _silu(x):
    return x * jax.nn.sigmoid(x)


def _ada_kernel(c_ref, w_ref, b_ref, o_ref):
    o_ref[...] = _dot_hi(_silu(c_ref[...]), w_ref[...]) + b_ref[...]


def ada_vectors(c_rows, ada_w, ada_b):
    r, d = c_rows.shape
    n = ada_w.shape[1]
    bn = 1024
    return pl.pallas_call(
        _ada_kernel,
        out_shape=jax.ShapeDtypeStruct((r, n), F32),
        grid=(n // bn,),
        in_specs=[pl.BlockSpec((r, d), lambda j: (0, 0)),
                  pl.BlockSpec((d, bn), lambda j: (0, j)),
                  pl.BlockSpec((1, bn), lambda j: (0, j))],
        out_specs=pl.BlockSpec((r, bn), lambda j: (0, j)),
        compiler_params=_cparams("arbitrary"),
        name="ada_vectors",
    )(c_rows, ada_w, ada_b.reshape(1, n))


def _inproj_kernel(x_ref, er_ref, ec_ref, g_ref, sh_ref, sc_ref, w_ref, o_ref, *, col_chunk):
    x = x_ref[...]
    rows, gw, d = x.shape
    half = d // 2
    xp = jnp.concatenate([x[:, :, :half] + er_ref[...], x[:, :, half:] + ec_ref[...]], axis=-1)
    xp = xp.reshape(rows * gw, d)
    ms = jnp.mean(xp * xp, axis=-1, keepdims=True)
    y = xp * lax.rsqrt(ms + NORM_EPS) * g_ref[...]
    h = (y * (1.0 + sc_ref[...]) + sh_ref[...]).astype(BF16)
    n = o_ref.shape[1]
    for j in range(n // col_chunk):
        sl = slice(j * col_chunk, (j + 1) * col_chunk)
        o_ref[:, sl] = _dot(h, w_ref[:, sl]).astype(o_ref.dtype)


def in_projection(x, emb_r, emb_c, norm_g, shift, scale, w_bf16, tm):
    b, s, d = x.shape
    n = w_bf16.shape[1]
    rows_per_batch = s // GRID_W
    rt = tm // GRID_W
    tiles_per_batch = rows_per_batch // rt
    x3 = x.reshape(b * rows_per_batch, GRID_W, d)
    col_chunk = 512
    return pl.pallas_call(
        functools.partial(_inproj_kernel, col_chunk=col_chunk),
        out_shape=jax.ShapeDtypeStruct((b * s, n), BF16),
        grid=(b * tiles_per_batch,),
        in_specs=[pl.BlockSpec((rt, GRID_W, d), lambda i: (i, 0, 0)),
                  pl.BlockSpec((rt, 1, d // 2), lambda i: (i % tiles_per_batch, 0, 0)),
                  pl.BlockSpec((GRID_W, d // 2), lambda i: (0, 0)),
                  pl.BlockSpec((1, d), lambda i: (0, 0)),
                  pl.BlockSpec((None, 1, d), lambda i: (i // tiles_per_batch, 0, 0)),
                  pl.BlockSpec((None, 1, d), lambda i: (i // tiles_per_batch, 0, 0)),
                  pl.BlockSpec((d, n), lambda i: (0, 0))],
        out_specs=pl.BlockSpec((tm, n), lambda i: (i, 0)),
        compiler_params=_cparams("arbitrary"),
        name="in_projection",
    )(x3, emb_r, emb_c, norm_g.reshape(1, d), shift, scale, w_bf16)


def _hgrn_kernel(*refs, reverse, n_chunks, final):
    if final:
        (q_ref, i_ref, f_ref, lb_ref, s0_ref, of_ref, gate_ref, ng_ref, o_ref, sfin_ref, s_scr) = refs
    else:
        (q_ref, i_ref, f_ref, lb_ref, s0_ref, o_ref, sfin_ref, s_scr) = refs
    cs = HG_CHUNK
    n_heads = s_scr.shape[0]
    dk = s_scr.shape[2]

    @pl.when(pl.program_id(1) == 0)
    def _():
        s_scr[...] = s0_ref[...]

    row = lax.broadcasted_iota(I32, (cs, cs), 0)
    col = lax.broadcasted_iota(I32, (cs, cs), 1)
    tri = (col >= row) if reverse else (col <= row)
    tri_b = tri.astype(BF16)
    end_row = 0 if reverse else cs - 1
    mid_row = cs // 2 if reverse else cs // 2 - 1

    def chunk_body(ci):
        c = (n_chunks - 1 - ci) if reverse else ci
        rows = slice(c * cs, (c + 1) * cs)
        lb = lb_ref[...]
        f = lb + (1.0 - lb) * jax.nn.sigmoid(f_ref[rows, :].astype(F32))
        lf = jnp.log(f)
        lf_hi = lf.astype(BF16)
        lf_lo = (lf - lf_hi.astype(F32)).astype(BF16)
        b_all = _dot(tri_b, lf_hi) + _dot(tri_b, lf_lo)
        k_all = 1.0 - f
        q_all = _silu(q_ref[rows, :].astype(F32))
        for h in range(n_heads):
            sl = slice(h * dk, (h + 1) * dk)
            b = b_all[:, sl]
            q = q_all[:, sl]
            k = k_all[:, sl]
            v = i_ref[rows, sl]
            b_end = b[end_row:end_row + 1]
            b_mid = b[mid_row:mid_row + 1]
            qd = (q * jnp.exp(b - b_mid)).astype(BF16)
            kd = (k * jnp.exp(b_mid - b)).astype(BF16)
            att = lax.dot_general(qd, kd, (((1,), (1,)), ((), ())), preferred_element_type=F32)
            att = jnp.where(tri, att, 0.0).astype(BF16)
            st = s_scr[h]
            qe = (q * jnp.exp(b)).astype(BF16)
            o = _dot(att, v) + lax.dot_general(qe, st.astype(BF16), (((1,), (1,)), ((), ())),
                                               preferred_element_type=F32)
            ke = (k * jnp.exp(b_end - b)).astype(BF16)
            delta_t = lax.dot_general(v, ke, (((0,), (0,)), ((), ())), preferred_element_type=F32)
            s_scr[h] = st * jnp.exp(b_end) + delta_t
            if final:
                o = o + of_ref[rows, sl].astype(F32)
                o = o * lax.rsqrt(jnp.mean(o * o, axis=-1, keepdims=True) + NORM_EPS) * ng_ref[...]
                o = o * _silu(gate_ref[rows, sl].astype(F32))
            o_ref[rows, sl] = o.astype(o_ref.dtype)

    for ci in range(n_chunks):
        chunk_body(ci)
    sfin_ref[...] = s_scr[...]


def hgrn_scan(p, cols, lb, s0, seq, tb, *, reverse, o_fwd=None, gate_col=None, norm_g=None):
    bsz, n_heads, dv, dk = s0.shape
    width = n_heads * dk
    nt = seq // tb
    final = o_fwd is not None

    def tmap(b, t):
        return b * nt + ((nt - 1 - t) if reverse else t)

    def colspec(cb):
        return pl.BlockSpec((tb, width), lambda b, t: (tmap(b, t), cb))

    in_specs = [colspec(cols[0]), colspec(cols[1]), colspec(cols[2]),
                pl.BlockSpec((1, width), lambda b, t: (0, 0)),
                pl.BlockSpec((None, n_heads, dv, dk), lambda b, t: (b, 0, 0, 0))]
    args = [p, p, p, lb.reshape(1, width), s0]
    if final:
        in_specs += [pl.BlockSpec((tb, width), lambda b, t: (tmap(b, t), 0)),
                     colspec(gate_col),
                     pl.BlockSpec((1, dk), lambda b, t: (0, 0))]
        args += [o_fwd, p, norm_g.reshape(1, dk)]
    return pl.pallas_call(
        functools.partial(_hgrn_kernel, reverse=reverse, n_chunks=tb // HG_CHUNK, final=final),
        out_shape=(jax.ShapeDtypeStruct((bsz * seq, width), BF16),
                   jax.ShapeDtypeStruct((bsz, n_heads, dv, dk), F32)),
        grid=(bsz, nt),
        in_specs=in_specs,
        out_specs=(pl.BlockSpec((tb, width), lambda b, t: (tmap(b, t), 0)),
                   pl.BlockSpec((None, n_heads, dv, dk), lambda b, t: (b, 0, 0, 0))),
        scratch_shapes=[pltpu.VMEM((n_heads, dv, dk), F32)],
        compiler_params=_cparams("arbitrary", "arbitrary"),
        name="hgrn_bwd" if reverse else "hgrn_fwd",
    )(*args)


def _shortconv_kernel(p_ref, prev_ref, next_ref, w_ref, b_ref, o_ref, *, tiles_per_batch):
    i = pl.program_id(0)
    ti = i % tiles_per_batch
    p = p_ref[...].astype(F32)
    tm = p.shape[0]
    row = lax.broadcasted_iota(I32, (tm, 1), 0)
    prev_row = jnp.where(ti == 0, 0.0, prev_ref[HALO_ROWS - 1:HALO_ROWS, :].astype(F32))
    next_row = jnp.where(ti == tiles_per_batch - 1, 0.0, next_ref[0:1, :].astype(F32))
    p_prev = jnp.where(row == 0, prev_row, pltpu.roll(p, 1, axis=0))
    p_next = jnp.where(row == tm - 1, next_row, pltpu.roll(p, tm - 1, axis=0))
    u = w_ref[0:1, :] * p_prev + w_ref[1:2, :] * p + w_ref[2:3, :] * p_next + b_ref[...]
    o_ref[...] = u.astype(o_ref.dtype)


def short_conv(p, width, conv_w, conv_b, seq, tm):
    t = p.shape[0]
    nt = t // tm
    tiles_per_batch = seq // tm
    sub = tm // HALO_ROWS
    return pl.pallas_call(
        functools.partial(_shortconv_kernel, tiles_per_batch=tiles_per_batch),
        out_shape=jax.ShapeDtypeStruct((t, width), BF16),
        grid=(nt,),
        in_specs=[pl.BlockSpec((tm, width), lambda i: (i, 0)),
                  pl.BlockSpec((HALO_ROWS, width), lambda i: (jnp.maximum(i * sub - 1, 0), 0)),
                  pl.BlockSpec((HALO_ROWS, width), lambda i: (jnp.minimum((i + 1) * sub, t // HALO_ROWS - 1), 0)),
                  pl.BlockSpec((HY_SHORT, width), lambda i: (0, 0)),
                  pl.BlockSpec((1, width), lambda i: (0, 0))],
        out_specs=pl.BlockSpec((tm, width), lambda i: (i, 0)),
        compiler_params=_cparams("arbitrary"),
        name="short_conv",
    )(p, p, p, conv_w, conv_b.reshape(1, width))


def _filter_kernel(band_ref, w1t_ref, w1c_ref, w1s_ref, b1_ref, w2_ref, b2_ref, w3_ref, b3_ref,
                   w4f_ref, w4b_ref, fr_ref, delta_ref, k_ref, s_ref, *, seq):
    step = pl.program_id(0)
    gb, q, ncol = k_ref.shape
    half = q // 2
    width = delta_ref.shape[1]
    nrow = gb * q
    nf = gb * half

    def positions(shape, axis):
        r = lax.broadcasted_iota(I32, shape, axis)
        is_bwd = r >= nf
        rr = jnp.where(is_bwd, r - nf, r)
        j = lax.shift_right_logical(rr, int(math.log2(half)))
        a = (rr & (half - 1)) + jnp.where(is_bwd, half, 0)
        n = (a * DFT_P + step * gb + j).astype(F32)
        t = jnp.where(is_bwd, 2.0 * seq - n, n)
        return n, t, t / float(max(seq - 1, 1))

    _, t_l, tn_l = positions((1, nrow), 1)
    ang = (2.0 * math.pi / seq) * t_l * band_ref[...]
    fr = fr_ref[...]
    pre = (w1t_ref[...] * tn_l + _dot_hi(w1c_ref[...], jnp.cos(ang)) - _dot_hi(w1s_ref[...], jnp.sin(ang))
           + b1_ref[...])
    act = jnp.sin(fr * pre)
    act = jnp.sin(fr * (_dot_hi(w2_ref[...], act) + b2_ref[...]))
    act = jnp.sin(fr * (_dot_hi(w3_ref[...], act) + b3_ref[...])).astype(BF16)
    n_s, _, tn_s = positions((nrow, 1), 0)
    delta = jnp.concatenate([delta_ref[...]] * (ncol // width), axis=1)
    tdot = lambda x, w: lax.dot_general(x, w, (((0,), (0,)), ((), ())), preferred_element_type=F32)
    hf = tdot(act[:, :nf], w4f_ref[...]) * jnp.exp(-tn_s[:nf] * delta)
    hb = tdot(act[:, nf:], w4b_ref[...]) * jnp.exp(-tn_s[nf:] * delta)
    hb = jnp.where(n_s[nf:] == float(seq), 0.0, hb)
    k_ref[:, :half, :] = hf.reshape(gb, half, ncol).astype(k_ref.dtype)
    k_ref[:, half:, :] = hb.reshape(gb, half, ncol).astype(k_ref.dtype)
    tot = jnp.sum(jnp.abs(hf), axis=0, keepdims=True) + jnp.sum(jnp.abs(hb), axis=0, keepdims=True)

    @pl.when(step == 0)
    def _():
        s_ref[...] = jnp.zeros_like(s_ref)

    s_ref[...] += tot


def hyena_filter_taps(seq, w1, b1, w2, b2, w3, b3, w4, freq, width):
    emb = w1.shape[0]
    hid = w1.shape[1]
    bands = (emb - 1) // 2
    q = 2 * seq // DFT_P
    ncol = HY_ORDER * width
    band = np.linspace(1e-4, bands - 1, bands, dtype=np.float32).reshape(bands, 1)
    min_decay = math.log(HY_DECAY_TARGET) / HY_SLOW_PCT
    max_decay = math.log(HY_DECAY_TARGET) / HY_FAST_PCT
    delta = np.abs(np.linspace(min_decay, max_decay, width, dtype=np.float32)).reshape(1, width)
    w1t = w1.astype(F32).T
    col = lambda v: v.reshape(hid, 1).astype(F32)
    w4r = w4.astype(BF16).reshape(hid, HY_ORDER, 2, width)
    w4f = w4r[:, :, 0, :].reshape(hid, ncol)
    w4b = w4r[:, :, 1, :].reshape(hid, ncol)
    gb = DFT_GROUP
    const = lambda shape: pl.BlockSpec(shape, lambda i: tuple(0 for _ in shape))
    return pl.pallas_call(
        functools.partial(_filter_kernel, seq=seq),
        out_shape=(jax.ShapeDtypeStruct((DFT_P, q, ncol), BF16),
                   jax.ShapeDtypeStruct((1, ncol), F32)),
        grid=(DFT_P // gb,),
        in_specs=[const((bands, 1)), const((hid, 1)), const((hid, bands)), const((hid, bands)), const((hid, 1)),
                  const((hid, hid)), const((hid, 1)), const((hid, hid)), const((hid, 1)),
                  const((hid, ncol)), const((hid, ncol)), const((hid, 1)), const((1, width))],
        out_specs=(pl.BlockSpec((gb, q, ncol), lambda i: (i, 0, 0)),
                   pl.BlockSpec((1, ncol), lambda i: (0, 0))),
        compiler_params=_cparams("arbitrary"),
        name="hyena_filter",
    )(jnp.asarray(band), w1t[:, 0:1], w1t[:, 1:1 + bands], w1t[:, 1 + bands:1 + 2 * bands], col(b1),
      w2.astype(F32).T, col(b2), w3.astype(F32).T, col(b3), w4f, w4b, col(freq), jnp.asarray(delta))


def _dft_tables(seq):
    p = DFT_P
    n_fft = 2 * seq
    q = n_fft // p
    qh = q // 2
    ka = jnp.arange(q, dtype=I32)
    a = jnp.arange(q, dtype=I32)
    b = jnp.arange(p, dtype=I32)
    nn = a[None, :] * p + b[:, None]
    ph = (ka[None, :, None] * nn[:, None, :]) % n_fft
    ang = ph.astype(F32) * (2.0 * math.pi / n_fft)
    mr, mi = jnp.cos(ang), -jnp.sin(ang)
    m1c = jnp.concatenate([jnp.concatenate([mr[:, :, :qh], -mi[:, :, :qh]], axis=2),
                           jnp.concatenate([mi[:, :, :qh], mr[:, :, :qh]], axis=2)], axis=1)
    m1r = jnp.concatenate([mr, mi], axis=1)
    gr = jnp.swapaxes(mr[:, :, :qh], 1, 2) / n_fft
    gi = -jnp.swapaxes(mi[:, :, :qh], 1, 2) / n_fft
    m4 = jnp.concatenate([jnp.concatenate([gr, -gi], axis=2), jnp.concatenate([gi, gr], axis=2)], axis=1)
    kb = np.arange(p)
    ang2 = 2.0 * np.pi * ((kb[:, None] * kb[None, :]) % p) / p
    fr, fi = np.cos(ang2), -np.sin(ang2)
    m2 = np.block([[fr, -fi], [fi, fr]]).astype(np.float32)
    m3 = np.block([[fr, fi], [-fi, fr]]).astype(np.float32)
    return (m1c.astype(BF16), m1r.astype(BF16), jnp.asarray(m2, BF16), jnp.asarray(m3, BF16), m4.astype(BF16))


def _bmm_kernel(w_ref, x_ref, o_ref, *, shared_w):
    for j in range(x_ref.shape[0]):
        w = w_ref[...] if shared_w else w_ref[j]
        o_ref[j] = _dot(w, x_ref[j]).astype(o_ref.dtype)


def batched_left_matmul(w, x, col_block, ncols, name):
    g, k = x.shape[0], x.shape[1]
    shared = w.ndim == 2
    m = w.shape[-2]
    gb = DFT_GROUP
    wspec = (pl.BlockSpec((m, k), lambda i: (0, 0)) if shared
             else pl.BlockSpec((gb, m, k), lambda i: (i, 0, 0)))
    return pl.pallas_call(
        functools.partial(_bmm_kernel, shared_w=shared),
        out_shape=jax.ShapeDtypeStruct((g, m, ncols), BF16),
        grid=(g // gb,),
        in_specs=[wspec, pl.BlockSpec((gb, k, ncols), lambda i: (i, 0, col_block))],
        out_specs=pl.BlockSpec((gb, m, ncols), lambda i: (i, 0, 0)),
        compiler_params=_cparams("arbitrary"),
        name=name,
    )(w, x)


def _dft_mid_kernel(m2_ref, m3_ref, x_ref, k_ref, o_ref):
    half = x_ref.shape[1] // 2
    for j in range(x_ref.shape[0]):
        xf = _dot(m2_ref[...], x_ref[j])
        kf = k_ref[j].astype(F32)
        xr, xi = xf[:half], xf[half:]
        kr, ki = kf[:half], kf[half:]
        z = jnp.concatenate([xr * kr - xi * ki, xr * ki + xi * kr], axis=0).astype(BF16)
        o_ref[j] = _dot(m3_ref[...], z).astype(o_ref.dtype)


def dft_mid(m2, m3, x, kspec, kcol, ncols):
    g, r = x.shape[0], x.shape[1]
    gb = DFT_GROUP
    return pl.pallas_call(
        _dft_mid_kernel,
        out_shape=jax.ShapeDtypeStruct((g, r, ncols), BF16),
        grid=(g // gb,),
        in_specs=[pl.BlockSpec((r, r), lambda i: (0, 0)),
                  pl.BlockSpec((r, r), lambda i: (0, 0)),
                  pl.BlockSpec((gb, r, ncols), lambda i: (i, 0, 0)),
                  pl.BlockSpec((gb, r, ncols), lambda i: (i, 0, kcol))],
        out_specs=pl.BlockSpec((gb, r, ncols), lambda i: (i, 0, 0)),
        compiler_params=_cparams("arbitrary"),
        name="dft_mid",
    )(m2, m3, x, kspec)


def _dft_out_kernel(m4_ref, y_ref, inv_ref, skip_ref, v_ref, mul_ref, o_ref):
    for j in range(y_ref.shape[0]):
        conv = _dot(m4_ref[j], y_ref[j]) * inv_ref[...] + v_ref[j].astype(F32) * skip_ref[...]
        o_ref[j] = (mul_ref[j].astype(F32) * conv).astype(o_ref.dtype)


def dft_out(m4, y, inv_l1, skip, u, v_col, mul, mul_col, ncols):
    g, r = y.shape[0], y.shape[1]
    rows = m4.shape[1]
    gb = DFT_GROUP
    return pl.pallas_call(
        _dft_out_kernel,
        out_shape=jax.ShapeDtypeStruct((g, rows, ncols), BF16),
        grid=(g // gb,),
        in_specs=[pl.BlockSpec((gb, rows, r), lambda i: (i, 0, 0)),
                  pl.BlockSpec((gb, r, ncols), lambda i: (i, 0, 0)),
                  pl.BlockSpec((1, ncols), lambda i: (0, 0)),
                  pl.BlockSpec((1, ncols), lambda i: (0, 0)),
                  pl.BlockSpec((gb, rows, ncols), lambda i: (i, 0, v_col)),
                  pl.BlockSpec((gb, rows, ncols), lambda i: (i, 0, mul_col))],
        out_specs=pl.BlockSpec((gb, rows, ncols), lambda i: (i, 0, 0)),
        compiler_params=_cparams("arbitrary"),
        name="dft_out",
    )(m4, y, inv_l1, skip, u, mul)


def _swap_ab(x):
    g1, r, c = x.shape
    g2 = r // 2
    return x.reshape(g1, 2, g2, c).transpose(2, 1, 0, 3).reshape(g2, 2 * g1, c)


def hyena_branch(u_nat, bsz, seq, width, taps, l1, skip):
    p = DFT_P
    qh = seq // p
    m1c, m1r, m2, m3, m4 = _dft_tables(seq)
    ncol = HY_ORDER * width
    ks1 = batched_left_matmul(m1r, taps, 0, ncol, "dft_k1")
    kspec = batched_left_matmul(m2, _swap_ab(ks1), 0, ncol, "dft_k2")
    inv_l1 = 1.0 / l1
    u = u_nat.reshape(bsz, qh, p, 3 * width).transpose(2, 0, 1, 3).reshape(p, bsz * qh, 3 * width)
    z = None
    for order in range(HY_ORDER):
        src, src_col = (u, 0) if order == 0 else (z, 0)
        s1 = batched_left_matmul(m1c, src, src_col, width, "dft_s1")
        mid = dft_mid(m2, m3, _swap_ab(s1), kspec, order, width)
        z = dft_out(m4, _swap_ab(mid), inv_l1[:, order * width:(order + 1) * width],
                    skip[order].reshape(1, width).astype(F32), src, src_col, u, order + 1, width)
    return z.reshape(p, bsz, qh, width).transpose(1, 2, 0, 3).reshape(bsz * seq, width)


def _pack_pairs(x):
    w = x.shape[1] // 2
    u = lax.bitcast_convert_type(x, U32)
    r = (u + U32(0x7FFF) + ((u >> 16) & U32(1))) >> 16
    return r[:, :w] | (r[:, w:] << 16)


def _unpack_pairs(p):
    lo = lax.bitcast_convert_type(p << 16, F32)
    hi = lax.bitcast_convert_type(p & U32(0xFFFF0000), F32)
    return jnp.concatenate([lo, hi], axis=1)


def _merge_kernel(x_ref, er_ref, ec_ref, yhy_ref, yhg_ref, ghy_ref, ghg_ref, why_ref, whg_ref, wo_ref,
                  g1_ref, n2_ref, sh2_ref, sc2_ref, g2_ref, rwt_ref, sgu_ref, sd_ref,
                  xres_ref, h2p_ref, lg_ref):
    x = x_ref[...]
    rows, gw, d = x.shape
    half = d // 2
    xp = jnp.concatenate([x[:, :, :half] + er_ref[...], x[:, :, half:] + ec_ref[...]], axis=-1)
    xp = xp.reshape(rows * gw, d)
    m = (jax.nn.sigmoid(ghy_ref[...].astype(F32)) * _dot(yhy_ref[...], why_ref[...])
         + jax.nn.sigmoid(ghg_ref[...].astype(F32)) * _dot(yhg_ref[...], whg_ref[...]))
    x1 = xp + g1_ref[...] * _dot(m.astype(BF16), wo_ref[...])
    ms = jnp.mean(x1 * x1, axis=-1, keepdims=True)
    h2 = x1 * lax.rsqrt(ms + NORM_EPS) * n2_ref[...] * (1.0 + sc2_ref[...]) + sh2_ref[...]
    lg_ref[...] = lax.dot_general(rwt_ref[...], h2, (((1,), (1,)), ((), ())),
                                  preferred_element_type=F32, precision=HIGHEST)
    h2b = h2.astype(BF16)
    gu = _dot(h2b, sgu_ref[...])
    fs = gu.shape[1] // 2
    shared = _dot((_silu(gu[:, :fs]) * gu[:, fs:]).astype(BF16), sd_ref[...])
    xres_ref[...] = x1 + g2_ref[...] * shared
    h2p_ref[...] = _pack_pairs(h2)


def merge_stage(x, emb_r, emb_c, y_hy, y_hg, p, gate_cols, w_hy_out, w_hg_out, w_out, g1, norm2_g,
                sh2, sc2, g2, router_wt, sh_gate_up, sh_down, tm):
    b, s, d = x.shape
    rows_per_batch = s // GRID_W
    rt = tm // GRID_W
    tiles_per_batch = rows_per_batch // rt
    x3 = x.reshape(b * rows_per_batch, GRID_W, d)
    wb = y_hy.shape[1]
    ne = router_wt.shape[0]
    fs2 = sh_gate_up.shape[1]
    tok = lambda cb, w: pl.BlockSpec((tm, w), lambda i: (i, cb))
    const = lambda shape: pl.BlockSpec(shape, lambda i: tuple(0 for _ in shape))
    per_b = pl.BlockSpec((None, 1, d), lambda i: (i // tiles_per_batch, 0, 0))
    return pl.pallas_call(
        _merge_kernel,
        out_shape=(jax.ShapeDtypeStruct((b * s, d), F32),
                   jax.ShapeDtypeStruct((b * s, d // 2), U32),
                   jax.ShapeDtypeStruct((ne, b * s), F32)),
        grid=(b * tiles_per_batch,),
        in_specs=[pl.BlockSpec((rt, GRID_W, d), lambda i: (i, 0, 0)),
                  pl.BlockSpec((rt, 1, d // 2), lambda i: (i % tiles_per_batch, 0, 0)),
                  const((GRID_W, d // 2)),
                  tok(0, wb), tok(0, wb), tok(gate_cols[0], d), tok(gate_cols[1], d),
                  const((wb, d)), const((wb, d)), const((d, d)),
                  per_b, const((1, d)), per_b, per_b, per_b,
                  const((ne, d)), const((d, fs2)), const((fs2 // 2, d))],
        out_specs=(pl.BlockSpec((tm, d), lambda i: (i, 0)),
                   pl.BlockSpec((tm, d // 2), lambda i: (i, 0)),
                   pl.BlockSpec((ne, tm), lambda i: (0, i))),
        compiler_params=_cparams("arbitrary"),
        name="merge",
    )(x3, emb_r, emb_c, y_hy, y_hg, p, p, w_hy_out, w_hg_out, w_out, g1, norm2_g.reshape(1, d),
      sh2, sc2, g2, router_wt, sh_gate_up, sh_down)


def _route_kernel(lg_ref, bias_ref, eidx_ref, wsel_ref, rank_ref, cnt_ref, carry):
    ne, tr = lg_ref.shape
    gsz = ne // N_GROUPS
    neg = -jnp.inf

    @pl.when(pl.program_id(0) == 0)
    def _():
        carry[...] = jnp.zeros_like(carry)

    scores = jax.nn.sigmoid(lg_ref[...])
    biased = scores + bias_ref[...]
    riota = lax.broadcasted_iota(I32, (gsz, tr), 0).astype(F32)
    gs = []
    for g in range(N_GROUPS):
        vg = biased[g * gsz:(g + 1) * gsz]
        m1 = jnp.max(vg, axis=0, keepdims=True)
        i1 = jnp.min(jnp.where(vg == m1, riota, float(gsz)), axis=0, keepdims=True)
        m2 = jnp.max(jnp.where(riota == i1, neg, vg), axis=0, keepdims=True)
        gs.append(m1 + m2)
    cur = jnp.concatenate(gs, axis=0)
    giota = lax.broadcasted_iota(I32, (N_GROUPS, tr), 0).astype(F32)
    gsel = jnp.zeros((N_GROUPS, tr), F32)
    for _ in range(TOPK_GROUPS):
        m = jnp.max(cur, axis=0, keepdims=True)
        idx = jnp.min(jnp.where(cur == m, giota, float(N_GROUPS)), axis=0, keepdims=True)
        hit = giota == idx
        gsel = jnp.where(hit, 1.0, gsel)
        cur = jnp.where(hit, neg, cur)
    cur = jnp.concatenate([jnp.where(gsel[g:g + 1] > 0.0, biased[g * gsz:(g + 1) * gsz], neg)
                           for g in range(N_GROUPS)], axis=0)
    eiota = lax.broadcasted_iota(I32, (ne, tr), 0).astype(F32)
    chosen = jnp.zeros((ne, tr), F32)
    idxs, ws = [], []
    for _ in range(TOP_K):
        m = jnp.max(cur, axis=0, keepdims=True)
        idx = jnp.min(jnp.where(cur == m, eiota, float(ne)), axis=0, keepdims=True)
        hit = eiota == idx
        idxs.append(idx)
        ws.append(jnp.sum(jnp.where(hit, scores, 0.0), axis=0, keepdims=True))
        chosen = jnp.where(hit, 1.0, chosen)
        cur = jnp.where(hit, neg, cur)
    w = jnp.concatenate(ws, axis=0)
    wsel_ref[...] = w / jnp.sum(w, axis=0, keepdims=True) * ROUTED_SCALE
    eidx_ref[...] = jnp.concatenate(idxs, axis=0).astype(I32)
    srow = lax.broadcasted_iota(I32, (tr, tr), 0)
    scol = lax.broadcasted_iota(I32, (tr, tr), 1)
    before = (srow < scol).astype(BF16)
    base = carry[...] + _dot(chosen.astype(BF16), before)
    ranks = [jnp.sum(jnp.where(eiota == idx, base, 0.0), axis=0, keepdims=True) for idx in idxs]
    rank_ref[...] = jnp.concatenate(ranks, axis=0).astype(I32)
    carry[...] += jnp.sum(chosen, axis=1, keepdims=True)
    cnt_ref[...] = carry[...]


def route(logits_t, router_bias, tr):
    ne, t = logits_t.shape
    return pl.pallas_call(
        _route_kernel,
        out_shape=(jax.ShapeDtypeStruct((TOP_K, t), I32),
                   jax.ShapeDtypeStruct((TOP_K, t), F32),
                   jax.ShapeDtypeStruct((TOP_K, t), I32),
                   jax.ShapeDtypeStruct((ne, 1), F32)),
        grid=(t // tr,),
        in_specs=[pl.BlockSpec((ne, tr), lambda i: (0, i)),
                  pl.BlockSpec((ne, 1), lambda i: (0, 0))],
        out_specs=(pl.BlockSpec((TOP_K, tr), lambda i: (0, i)),
                   pl.BlockSpec((TOP_K, tr), lambda i: (0, i)),
                   pl.BlockSpec((TOP_K, tr), lambda i: (0, i)),
                   pl.BlockSpec((ne, 1), lambda i: (0, 0))),
        scratch_shapes=[pltpu.VMEM((ne, 1), F32)],
        compiler_params=_cparams("arbitrary"),
        name="route",
    )(logits_t, router_bias.reshape(ne, 1).astype(F32))


def _dest_kernel(cnt_ref, eidx_ref, rank_ref, dest_ref, be_ref, nv_ref, nb_ref, start_scr, *, n_blocks):
    ne = cnt_ref.shape[0]
    tr = eidx_ref.shape[1]

    @pl.when(pl.program_id(0) == 0)
    def _():
        cnt = jnp.broadcast_to(cnt_ref[...], (ne, LANES))
        padded = jnp.floor((cnt + float(MOE_ROWS - 1)) / float(MOE_ROWS)) * float(MOE_ROWS)
        r = lax.broadcasted_iota(I32, (ne, ne), 0)
        c = lax.broadcasted_iota(I32, (ne, ne), 1)
        start = _dot_hi((c < r).astype(F32), padded)
        start_scr[...] = start
        end = start[:, 0:1] + padded[:, 0:1]
        used = start[:, 0:1] + cnt[:, 0:1]
        nbl = be_ref.shape[1]
        blk_row = (lax.broadcasted_iota(I32, (1, nbl), 1) * MOE_ROWS).astype(F32)
        total = jnp.max(end, axis=0, keepdims=True)
        last_row = total - float(MOE_ROWS)
        blk_row_c = jnp.minimum(blk_row, last_row)
        e_of = jnp.sum((end <= blk_row_c).astype(F32), axis=0, keepdims=True)
        e_of = jnp.minimum(e_of, float(ne - 1))
        eio = lax.broadcasted_iota(I32, (ne, nbl), 0).astype(F32)
        used_e = jnp.sum(jnp.where(eio == e_of, used, 0.0), axis=0, keepdims=True)
        valid = jnp.clip(used_e - blk_row_c, 0.0, float(MOE_ROWS))
        be_ref[...] = e_of.astype(I32)
        nv_ref[...] = jnp.where(blk_row <= last_row, valid, 0.0).astype(I32)
        nb_ref[...] = jnp.broadcast_to(total / float(MOE_ROWS), nb_ref.shape).astype(I32)

    eiota = lax.broadcasted_iota(I32, (ne, tr), 0)
    start_col = start_scr[:, 0:1]
    rows = []
    for k in range(TOP_K):
        hit = eiota == eidx_ref[k:k + 1, :]
        rows.append(jnp.sum(jnp.where(hit, start_col, 0.0), axis=0, keepdims=True))
    dest_ref[...] = jnp.concatenate(rows, axis=0).astype(I32) + rank_ref[...]


def dispatch_plan(counts, eidx, rank, tr, n_blocks):
    ne = counts.shape[0]
    t = eidx.shape[1]
    nbl = pl.cdiv(n_blocks, LANES) * LANES
    return pl.pallas_call(
        functools.partial(_dest_kernel, n_blocks=n_blocks),
        out_shape=(jax.ShapeDtypeStruct((TOP_K, t), I32),
                   jax.ShapeDtypeStruct((1, nbl), I32),
                   jax.ShapeDtypeStruct((1, nbl), I32),
                   jax.ShapeDtypeStruct((1, LANES), I32)),
        grid=(t // tr,),
        in_specs=[pl.BlockSpec((ne, 1), lambda i: (0, 0)),
                  pl.BlockSpec((TOP_K, tr), lambda i: (0, i)),
                  pl.BlockSpec((TOP_K, tr), lambda i: (0, i))],
        out_specs=(pl.BlockSpec((TOP_K, tr), lambda i: (0, i)),
                   pl.BlockSpec((1, nbl), lambda i: (0, 0)),
                   pl.BlockSpec((1, nbl), lambda i: (0, 0)),
                   pl.BlockSpec((1, LANES), lambda i: (0, 0))),
        scratch_shapes=[pltpu.VMEM((ne, LANES), F32)],
        compiler_params=_cparams("arbitrary"),
        name="dispatch_plan",
    )(counts, eidx, rank)


def _tile_dest(dest, tm):
    k, t = dest.shape
    return dest.reshape(k, t // tm, tm).transpose(1, 0, 2).reshape(t // tm, 1, k * tm)


def _scatter_kernel(dest_ref, h_ref, xs_ref, sem):
    tm = h_ref.shape[0]

    def row_copy(t, d):
        return pltpu.make_async_copy(h_ref.at[pl.ds(t, 1)], xs_ref.at[pl.ds(d, 1)], sem)

    def issue(t, carry):
        for k in range(TOP_K):
            row_copy(t, dest_ref[0, k * tm + t]).start(priority=k % 2)
        return carry

    lax.fori_loop(0, tm, issue, 0, unroll=2)

    def drain(t, carry):
        for k in range(TOP_K):
            row_copy(0, 0).wait()
        return carry

    lax.fori_loop(0, tm, drain, 0)


def scatter_rows(dest, h2p, n_rows, tm):
    t, w = h2p.shape
    return pl.pallas_call(
        _scatter_kernel,
        out_shape=jax.ShapeDtypeStruct((n_rows, w), U32),
        grid=(t // tm,),
        in_specs=[pl.BlockSpec((None, 1, TOP_K * tm), lambda i: (i, 0, 0), memory_space=pltpu.SMEM),
                  pl.BlockSpec((tm, w), lambda i: (i, 0))],
        out_specs=pl.BlockSpec(memory_space=pl.ANY),
        scratch_shapes=[pltpu.SemaphoreType.DMA(())],
        compiler_params=_cparams("arbitrary"),
        name="scatter_rows",
    )(dest, h2p)


def _gmm_kernel(be_ref, nv_ref, nb_ref, first_ref, slot_ref, nxt_ref, xs_ref, wg_hbm, wu_hbm, wd_hbm, y_ref,
                wg_buf, wu_buf, wd_buf, sem):
    def weight_copies(e, s):
        return (pltpu.make_async_copy(wg_hbm.at[e], wg_buf.at[s], sem.at[s]),
                pltpu.make_async_copy(wu_hbm.at[e], wu_buf.at[s], sem.at[s]),
                pltpu.make_async_copy(wd_hbm.at[e], wd_buf.at[s], sem.at[s]))

    def one_block(j, rows):
        s = slot_ref[j]

        @pl.when(j == 0)
        def _():
            for c in weight_copies(be_ref[0], 0):
                c.start()

        @pl.when(first_ref[j] == 1)
        def _():
            for c in weight_copies(be_ref[j], s):
                c.wait()

            @pl.when(nxt_ref[j] >= 0)
            def _():
                for c in weight_copies(nxt_ref[j], 1 - s):
                    c.start()

        x = _unpack_pairs(xs_ref[rows, :])
        row = lax.broadcasted_iota(I32, (x.shape[0], 1), 0)
        x = jnp.where(row < nv_ref[j], x, 0.0)
        hmid = _silu(_dot(x, wg_buf[s])) * _dot(x, wu_buf[s])
        y_ref[rows, :] = _pack_pairs(_dot(hmid, wd_buf[s]))

    for sb in range(MOE_STEP_BLOCKS):
        j = pl.program_id(0) * MOE_STEP_BLOCKS + sb
        pl.when(j < nb_ref[0])(functools.partial(one_block, j, slice(sb * MOE_ROWS, (sb + 1) * MOE_ROWS)))


def grouped_mlp(block_e, block_valid, n_used, xs, w_gate, w_up, w_down, n_blocks):
    ne, d, f = w_gate.shape
    w = xs.shape[1]
    jj = jnp.arange(block_e.shape[0], dtype=I32)
    active = jj < n_used[0]
    first = (active & ((jj == 0) | (block_e != jnp.roll(block_e, 1)))).astype(I32)
    run = jnp.cumsum(first) - 1
    slot = (run % 2).astype(I32)
    nbl = block_e.shape[0]
    run_expert = jnp.full((nbl + 1,), -1, I32).at[jnp.where(first == 1, run, nbl)].set(
        jnp.where(first == 1, block_e, -1))
    nxt = run_expert[jnp.minimum(run + 1, nbl)]
    step_rows = MOE_STEP_BLOCKS * MOE_ROWS
    last = lambda g, nb: jnp.minimum(g, (nb[0] - 1) // MOE_STEP_BLOCKS)
    grid_spec = pltpu.PrefetchScalarGridSpec(
        num_scalar_prefetch=6,
        grid=(n_blocks // MOE_STEP_BLOCKS,),
        in_specs=[pl.BlockSpec((step_rows, w), lambda g, be, nv, nb, fi, sl, nx: (last(g, nb), 0)),
                  pl.BlockSpec(memory_space=pl.ANY),
                  pl.BlockSpec(memory_space=pl.ANY),
                  pl.BlockSpec(memory_space=pl.ANY)],
        out_specs=pl.BlockSpec((step_rows, w), lambda g, be, nv, nb, fi, sl, nx: (last(g, nb), 0)),
        scratch_shapes=[pltpu.VMEM((2, d, f), F32), pltpu.VMEM((2, d, f), F32), pltpu.VMEM((2, f, d), F32),
                        pltpu.SemaphoreType.DMA((2,))],
    )
    return pl.pallas_call(
        _gmm_kernel,
        out_shape=jax.ShapeDtypeStruct(xs.shape, U32),
        grid_spec=grid_spec,
        compiler_params=_cparams("arbitrary"),
        name="grouped_mlp",
    )(block_e, block_valid, n_used, first, slot, nxt, xs, w_gate, w_up, w_down)


def _combine_kernel(dcur_ref, dnext_ref, xres_ref, wt_ref, g2_ref, fg_ref, ys_ref, o_ref, buf, sem):
    tm = xres_ref.shape[0]
    i = pl.program_id(0)
    slot = i % 2

    def row_copy(s, k, t, d):
        return pltpu.make_async_copy(ys_ref.at[pl.ds(d, 1)], buf.at[s, k, pl.ds(t, 1)], sem.at[s])

    def issue_tile(d_ref, s):
        def issue(t, carry):
            for k in range(TOP_K):
                row_copy(s, k, t, d_ref[0, k * tm + t]).start(priority=k % 2)
            return carry

        lax.fori_loop(0, tm, issue, 0, unroll=2)

    @pl.when(i == 0)
    def _():
        issue_tile(dcur_ref, 0)

    @pl.when(i + 1 < pl.num_programs(0))
    def _():
        issue_tile(dnext_ref, 1 - slot)

    def drain(t, carry):
        for k in range(TOP_K):
            row_copy(slot, 0, 0, 0).wait()
        return carry

    lax.fori_loop(0, tm, drain, 0)

    wt = wt_ref[...]
    routed = jnp.zeros(xres_ref.shape, F32)
    for k in range(TOP_K):
        routed = routed + wt[:, k:k + 1] * _unpack_pairs(buf[slot, k])
    x2 = xres_ref[...] + g2_ref[...] * routed
    ms = jnp.mean(x2 * x2, axis=-1, keepdims=True)
    o_ref[...] = x2 * lax.rsqrt(ms + NORM_EPS) * fg_ref[...]


def combine(dest, xres, wsel_t, g2, final_g, ys, seq, tm):
    t, d = xres.shape
    tiles_per_batch = seq // tm
    nt = t // tm
    return pl.pallas_call(
        _combine_kernel,
        out_shape=jax.ShapeDtypeStruct((t, d), F32),
        grid=(nt,),
        in_specs=[pl.BlockSpec((None, 1, TOP_K * tm), lambda i: (i, 0, 0), memory_space=pltpu.SMEM),
                  pl.BlockSpec((None, 1, TOP_K * tm), lambda i: (jnp.minimum(i + 1, nt - 1), 0, 0),
                               memory_space=pltpu.SMEM),
                  pl.BlockSpec((tm, d), lambda i: (i, 0)),
                  pl.BlockSpec((tm, TOP_K), lambda i: (i, 0)),
                  pl.BlockSpec((None, 1, d), lambda i: (i // tiles_per_batch, 0, 0)),
                  pl.BlockSpec((1, d), lambda i: (0, 0)),
                  pl.BlockSpec(memory_space=pl.ANY)],
        out_specs=pl.BlockSpec((tm, d), lambda i: (i, 0)),
        scratch_shapes=[pltpu.VMEM((2, TOP_K, tm, d // 2), U32), pltpu.SemaphoreType.DMA((2,))],
        compiler_params=_cparams("arbitrary"),
        name="combine",
    )(dest, dest, xres, wsel_t, g2, final_g.reshape(1, d), ys)


def _pos_tables(rows, cols, dim):
    quarter = dim // 4
    omega = 1.0 / (POS_BASE ** (np.arange(quarter, dtype=np.float32) / quarter))
    ang_r = np.arange(rows, dtype=np.float32)[:, None] * omega
    ang_c = np.arange(cols, dtype=np.float32)[:, None] * omega
    emb_r = np.concatenate([np.sin(ang_r), np.cos(ang_r)], axis=-1).astype(np.float32)
    emb_c = np.concatenate([np.sin(ang_c), np.cos(ang_c)], axis=-1).astype(np.float32)
    return jnp.asarray(emb_r.reshape(rows, 1, dim // 2)), jnp.asarray(emb_c)


def kernel(x, c, ctx, c_ctx, norm1_g, norm2_g, ada_w, ada_b, w_in, hy_conv_w, hy_conv_b, hy_f_w1, hy_f_b1, hy_f_w2, hy_f_b2, hy_f_w3, hy_f_b3, hy_f_w4, hy_f_freq, hy_skip, hg_lb_logits, hg_norm_g, w_hy_out, w_hg_out, w_out, router_w, router_bias, exp_w_gate, exp_w_up, exp_w_down, sh_w_gate, sh_w_up, sh_w_down, final_g):
    bsz, seq, d = x.shape
    n_ctx = ctx.shape[1]
    hy_w = w_hy_out.shape[1]
    hg_w = w_hg_out.shape[1]
    dk = hg_norm_g.shape[1]
    n_heads = hg_w // dk
    ne = router_w.shape[2]
    l = 0

    c_rows = jnp.zeros((SUBLANES, d), F32).at[:bsz].set(c).at[bsz].set(c_ctx)
    mods = ada_vectors(c_rows, ada_w[l], ada_b[l])
    sh1, sc1, g1, sh2, sc2, g2 = [mods[:bsz, j * d:(j + 1) * d].reshape(bsz, 1, d) for j in range(N_ADA)]
    csh1 = jnp.broadcast_to(mods[bsz, 0:d].reshape(1, 1, d), (bsz, 1, d))
    csc1 = jnp.broadcast_to(mods[bsz, d:2 * d].reshape(1, 1, d), (bsz, 1, d))

    emb_r, emb_c = _pos_tables(seq // GRID_W, GRID_W, d)
    w_in_b = w_in[l].astype(BF16)
    hy_proj = 3 * hy_w
    p = in_projection(x, emb_r, emb_c, norm1_g[l], sh1, sc1, w_in_b, TOKEN_TILE)
    hg_cols = slice(hy_proj, hy_proj + 5 * hg_w)
    zero_r = jnp.zeros((n_ctx // GRID_W, 1, d // 2), F32)
    zero_c = jnp.zeros((GRID_W, d // 2), F32)
    pc = in_projection(ctx, zero_r, zero_c, norm1_g[l], csh1, csc1, w_in_b[:, hg_cols], n_ctx)

    lbs = jnp.cumsum(jax.nn.softmax(hg_lb_logits.astype(F32), axis=0), axis=0)
    lb_f, lb_b = lbs[l, 0], lbs[l, 1]
    zero_state = jnp.zeros((bsz, n_heads, dk, dk), F32)
    base = hy_proj // hg_w
    _, st_f = hgrn_scan(pc, (0, 1, 2), lb_f, zero_state, n_ctx, n_ctx, reverse=False)
    _, st_b = hgrn_scan(pc, (0, 1, 3), lb_b, zero_state, n_ctx, n_ctx, reverse=True)
    o_f, _ = hgrn_scan(p, (base, base + 1, base + 2), lb_f, st_f, seq, HG_TIME_BLOCK, reverse=False)
    y_hg, _ = hgrn_scan(p, (base, base + 1, base + 3), lb_b, st_b, seq, HG_TIME_BLOCK, reverse=True,
                        o_fwd=o_f, gate_col=base + 4, norm_g=hg_norm_g[l])

    u = short_conv(p, hy_proj, hy_conv_w[l], hy_conv_b[l], seq, TOKEN_TILE)
    taps, l1 = hyena_filter_taps(seq, hy_f_w1[l], hy_f_b1[l], hy_f_w2[l], hy_f_b2[l], hy_f_w3[l], hy_f_b3[l],
                                 hy_f_w4[l], hy_f_freq[l], hy_w)
    y_hy = hyena_branch(u, bsz, seq, hy_w, taps, l1, hy_skip[l])

    gate_base = (hy_proj + 5 * hg_w) // d
    sh_gu = jnp.concatenate([sh_w_gate[l], sh_w_up[l]], axis=1).astype(BF16)
    xres, h2p, logits_t = merge_stage(
        x, emb_r, emb_c, y_hy, y_hg, p, (gate_base, gate_base + 1),
        w_hy_out[l].astype(BF16), w_hg_out[l].astype(BF16), w_out[l].astype(BF16), g1, norm2_g[l],
        sh2, sc2, g2, router_w[l].T.astype(F32), sh_gu, sh_w_down[l].astype(BF16), TOKEN_TILE)

    t = bsz * seq
    eidx, wsel, rank, counts = route(logits_t, router_bias[l], TOKEN_TILE)
    n_rows = t * TOP_K + ne * (MOE_ROWS - 1)
    n_blocks = pl.cdiv(pl.cdiv(n_rows, MOE_ROWS), MOE_STEP_BLOCKS) * MOE_STEP_BLOCKS
    dest, block_e, block_valid, n_used = dispatch_plan(counts, eidx, rank, TOKEN_TILE, n_blocks)

    xs = scatter_rows(_tile_dest(dest, SCATTER_TILE), h2p, n_blocks * MOE_ROWS, SCATTER_TILE)
    ys = grouped_mlp(block_e.reshape(-1), block_valid.reshape(-1), n_used.reshape(-1)[:1], xs,
                     exp_w_gate[l], exp_w_up[l], exp_w_down[l], n_blocks)
    out = combine(_tile_dest(dest, GATHER_TILE), xres, wsel.T, g2, final_g, ys, seq, GATHER_TILE)
    return out.reshape(bsz, seq, d)
```

```python
import functools
import math

import numpy as np
import jax
import jax.numpy as jnp
from jax import lax
from jax.experimental import pallas as pl
from jax.experimental.pallas import tpu as pltpu

F32 = jnp.float32
BF16 = jnp.bfloat16
U32 = jnp.uint32
I32 = jnp.int32
HIGHEST = lax.Precision.HIGHEST

GRID_W = 64
POS_BASE = 10000.0
NORM_EPS = 1e-6
N_ADA = 6
HY_ORDER = 2
HY_SHORT = 3
HY_DECAY_TARGET = 1e-2
HY_FAST_PCT = 0.3
HY_SLOW_PCT = 1.5
HG_HEADS = 4
HG_CHUNK = 64
N_GROUPS = 8
TOPK_GROUPS = 4
TOP_K = 8
ROUTED_SCALE = 2.5

LANES = 128
SUBLANES = 8
VMEM_LIMIT = 56 * 1024 * 1024

TOKEN_TILE = 512
HG_TIME_BLOCK = 512
HALO_ROWS = 16
DFT_P = 128
FILTER_GROUP = 8
DFT_GROUP = 16
MOE_ROWS = 256
MOE_STEP_BLOCKS = 8
SCATTER_TILE = 512
GATHER_TILE = 256


def _cparams(*sem):
    return pltpu.CompilerParams(dimension_semantics=sem, vmem_limit_bytes=VMEM_LIMIT)


def _dot(a, b):
    return jnp.dot(a, b, preferred_element_type=F32)


def _dot_hi(a, b):
    return jnp.dot(a, b, preferred_element_type=F32, precision=HIGHEST)


def _dot_nt(a, b):
    return lax.dot_general(a, b, (((1,), (1,)), ((), ())), preferred_element_type=F32)


def _dot_tn(a, b):
    return lax.dot_general(a, b, (((0,), (0,)), ((), ())), preferred_element_type=F32)


def _silu(x):
    return x * jax.nn.sigmoid(x)


def _split_bf16(x):
    hi = x.astype(BF16)
    return hi, (x - hi.astype(F32)).astype(BF16)


def _ada_kernel(c_ref, w_ref, b_ref, o_ref):
    o_ref[...] = _dot_hi(_silu(c_ref[...]), w_ref[...]) + b_ref[...]


def ada_vectors(c_rows, ada_w, ada_b):
    r, d = c_rows.shape
    n = ada_w.shape[1]
    bn = 1024
    return pl.pallas_call(
        _ada_kernel,
        out_shape=jax.ShapeDtypeStruct((r, n), F32),
        grid=(n // bn,),
        in_specs=[pl.BlockSpec((r, d), lambda j: (0, 0)),
                  pl.BlockSpec((d, bn), lambda j: (0, j)),
                  pl.BlockSpec((1, bn), lambda j: (0, j))],
        out_specs=pl.BlockSpec((r, bn), lambda j: (0, j)),
        compiler_params=_cparams("arbitrary"),
        name="ada_vectors",
    )(c_rows, ada_w, ada_b.reshape(1, n))


def _inproj_kernel(x_ref, er_ref, ec_ref, g_ref, sh_ref, sc_ref, w_ref, o_ref, *, col_chunk):
    x = x_ref[...]
    rows, gw, d = x.shape
    half = d // 2
    xp = jnp.concatenate([x[:, :, :half] + er_ref[...], x[:, :, half:] + ec_ref[...]], axis=-1)
    xp = xp.reshape(rows * gw, d)
    ms = jnp.mean(xp * xp, axis=-1, keepdims=True)
    y = xp * lax.rsqrt(ms + NORM_EPS) * g_ref[...]
    h = (y * (1.0 + sc_ref[...]) + sh_ref[...]).astype(BF16)
    n = o_ref.shape[1]
    for j in range(n // col_chunk):
        sl = slice(j * col_chunk, (j + 1) * col_chunk)
        o_ref[:, sl] = _dot(h, w_ref[:, sl]).astype(o_ref.dtype)


def in_projection(x, emb_r, emb_c, norm_g, shift, scale, w_bf16, tm):
    b, s, d = x.shape
    n = w_bf16.shape[1]
    rows_per_batch = s // GRID_W
    rt = tm // GRID_W
    tiles_per_batch = rows_per_batch // rt
    x3 = x.reshape(b * rows_per_batch, GRID_W, d)
    col_chunk = 512
    return pl.pallas_call(
        functools.partial(_inproj_kernel, col_chunk=col_chunk),
        out_shape=jax.ShapeDtypeStruct((b * s, n), BF16),
        grid=(b * tiles_per_batch,),
        in_specs=[pl.BlockSpec((rt, GRID_W, d), lambda i: (i, 0, 0)),
                  pl.BlockSpec((rt, 1, d // 2), lambda i: (i % tiles_per_batch, 0, 0)),
                  pl.BlockSpec((GRID_W, d // 2), lambda i: (0, 0)),
                  pl.BlockSpec((1, d), lambda i: (0, 0)),
                  pl.BlockSpec((None, 1, d), lambda i: (i // tiles_per_batch, 0, 0)),
                  pl.BlockSpec((None, 1, d), lambda i: (i // tiles_per_batch, 0, 0)),
                  pl.BlockSpec((d, n), lambda i: (0, 0))],
        out_specs=pl.BlockSpec((tm, n), lambda i: (i, 0)),
        compiler_params=_cparams("arbitrary"),
        name="in_projection",
    )(x3, emb_r, emb_c, norm_g.reshape(1, d), shift, scale, w_bf16)


def _hgrn_kernel(*refs, reverse, n_chunks, final):
    if final:
        (q_ref, i_ref, f_ref, lb_ref, s0_ref, of_ref, gate_ref, ng_ref, o_ref, sfin_ref, s_scr) = refs
    else:
        (q_ref, i_ref, f_ref, lb_ref, s0_ref, o_ref, sfin_ref, s_scr) = refs
    cs = HG_CHUNK
    bsz, n_heads, _, dk = s_scr.shape

    @pl.when(pl.program_id(0) == 0)
    def _():
        s_scr[...] = s0_ref[...]

    row = lax.broadcasted_iota(I32, (cs, cs), 0)
    col = lax.broadcasted_iota(I32, (cs, cs), 1)
    tri = (col >= row) if reverse else (col <= row)
    tri_b = tri.astype(BF16)
    end_row = 0 if reverse else cs - 1
    mid_row = cs // 2 if reverse else cs // 2 - 1

    def chunk_body(bi, ci):
        c = (n_chunks - 1 - ci) if reverse else ci
        rows = slice(c * cs, (c + 1) * cs)
        lb = lb_ref[...]
        f = lb + (1.0 - lb) * jax.nn.sigmoid(f_ref[bi, rows, :].astype(F32))
        lf_hi, lf_lo = _split_bf16(jnp.log(f))
        b_all = _dot(tri_b, lf_hi) + _dot(tri_b, lf_lo)
        k_all = 1.0 - f
        q_all = _silu(q_ref[bi, rows, :].astype(F32))
        for h in range(n_heads):
            sl = slice(h * dk, (h + 1) * dk)
            b = b_all[:, sl]
            q = q_all[:, sl]
            k = k_all[:, sl]
            v = i_ref[bi, rows, sl]
            b_end = b[end_row:end_row + 1]
            b_mid = b[mid_row:mid_row + 1]
            qd = (q * jnp.exp(b - b_mid)).astype(BF16)
            kd = (k * jnp.exp(b_mid - b)).astype(BF16)
            att = jnp.where(tri, _dot_nt(qd, kd), 0.0).astype(BF16)
            st = s_scr[bi, h]
            qe = (q * jnp.exp(b)).astype(BF16)
            o = _dot(att, v) + _dot_nt(qe, st.astype(BF16))
            ke = (k * jnp.exp(b_end - b)).astype(BF16)
            s_scr[bi, h] = st * jnp.exp(b_end) + _dot_tn(v, ke)
            if final:
                o = o + of_ref[bi, rows, sl].astype(F32)
                o = o * lax.rsqrt(jnp.mean(o * o, axis=-1, keepdims=True) + NORM_EPS) * ng_ref[...]
                o = o * _silu(gate_ref[bi, rows, sl].astype(F32))
            o_ref[bi, rows, sl] = o.astype(o_ref.dtype)

    for ci in range(n_chunks):
        for bi in range(bsz):
            chunk_body(bi, ci)
    sfin_ref[...] = s_scr[...]


def hgrn_scan(p, cols, lb, s0, seq, tb, *, reverse, o_fwd=None, gate_col=None, norm_g=None):
    bsz, n_heads, dv, dk = s0.shape
    width = n_heads * dk
    nt = seq // tb
    final = o_fwd is not None
    p3 = p.reshape(bsz, seq, p.shape[1])
    tmap = (lambda t: nt - 1 - t) if reverse else (lambda t: t)
    colspec = lambda cb: pl.BlockSpec((bsz, tb, width), lambda t: (0, tmap(t), cb))
    state = pl.BlockSpec((bsz, n_heads, dv, dk), lambda t: (0, 0, 0, 0))
    in_specs = [colspec(cols[0]), colspec(cols[1]), colspec(cols[2]),
                pl.BlockSpec((1, width), lambda t: (0, 0)), state]
    args = [p3, p3, p3, lb.reshape(1, width), s0]
    if final:
        in_specs += [colspec(0), colspec(gate_col), pl.BlockSpec((1, dk), lambda t: (0, 0))]
        args += [o_fwd.reshape(bsz, seq, width), p3, norm_g.reshape(1, dk)]
    o, s_fin = pl.pallas_call(
        functools.partial(_hgrn_kernel, reverse=reverse, n_chunks=tb // HG_CHUNK, final=final),
        out_shape=(jax.ShapeDtypeStruct((bsz, seq, width), BF16),
                   jax.ShapeDtypeStruct((bsz, n_heads, dv, dk), F32)),
        grid=(nt,),
        in_specs=in_specs,
        out_specs=(colspec(0), state),
        scratch_shapes=[pltpu.VMEM((bsz, n_heads, dv, dk), F32)],
        compiler_params=_cparams("arbitrary"),
        name="hgrn_bwd" if reverse else "hgrn_fwd",
    )(*args)
    return o.reshape(bsz * seq, width), s_fin


def _shortconv_kernel(p_ref, prev_ref, next_ref, w_ref, b_ref, o_ref, *, tiles_per_batch):
    i = pl.program_id(0)
    ti = i % tiles_per_batch
    p = p_ref[...].astype(F32)
    tm = p.shape[0]
    row = lax.broadcasted_iota(I32, (tm, 1), 0)
    prev_row = jnp.where(ti == 0, 0.0, prev_ref[HALO_ROWS - 1:HALO_ROWS, :].astype(F32))
    next_row = jnp.where(ti == tiles_per_batch - 1, 0.0, next_ref[0:1, :].astype(F32))
    p_prev = jnp.where(row == 0, prev_row, pltpu.roll(p, 1, axis=0))
    p_next = jnp.where(row == tm - 1, next_row, pltpu.roll(p, tm - 1, axis=0))
    u = w_ref[0:1, :] * p_prev + w_ref[1:2, :] * p + w_ref[2:3, :] * p_next + b_ref[...]
    o_ref[...] = u.astype(o_ref.dtype)


def short_conv(p, width, conv_w, conv_b, seq, tm):
    t = p.shape[0]
    nt = t // tm
    tiles_per_batch = seq // tm
    sub = tm // HALO_ROWS
    return pl.pallas_call(
        functools.partial(_shortconv_kernel, tiles_per_batch=tiles_per_batch),
        out_shape=jax.ShapeDtypeStruct((t, width), BF16),
        grid=(nt,),
        in_specs=[pl.BlockSpec((tm, width), lambda i: (i, 0)),
                  pl.BlockSpec((HALO_ROWS, width), lambda i: (jnp.maximum(i * sub - 1, 0), 0)),
                  pl.BlockSpec((HALO_ROWS, width), lambda i: (jnp.minimum((i + 1) * sub, t // HALO_ROWS - 1), 0)),
                  pl.BlockSpec((HY_SHORT, width), lambda i: (0, 0)),
                  pl.BlockSpec((1, width), lambda i: (0, 0))],
        out_specs=pl.BlockSpec((tm, width), lambda i: (i, 0)),
        compiler_params=_cparams("arbitrary"),
        name="short_conv",
    )(p, p, p, conv_w, conv_b.reshape(1, width))


def _filter_kernel(band_ref, w1t_ref, w1c_ref, w1s_ref, b1_ref, w2_ref, b2_ref, w3_ref, b3_ref,
                   w4f_ref, w4b_ref, fr_ref, delta_ref, k_ref, s_ref, *, seq):
    step = pl.program_id(0)
    gb, q, ncol = k_ref.shape
    half = q // 2
    width = delta_ref.shape[1]
    nrow = gb * q
    nf = gb * half

    def positions(shape, axis):
        r = lax.broadcasted_iota(I32, shape, axis)
        is_bwd = r >= nf
        rr = jnp.where(is_bwd, r - nf, r)
        j = lax.shift_right_logical(rr, int(math.log2(half)))
        a = (rr & (half - 1)) + jnp.where(is_bwd, half, 0)
        n = (a * DFT_P + step * gb + j).astype(F32)
        t = jnp.where(is_bwd, 2.0 * seq - n, n)
        return n, t, t / float(max(seq - 1, 1))

    _, t_l, tn_l = positions((1, nrow), 1)
    ang = (2.0 * math.pi / seq) * t_l * band_ref[...]
    fr = fr_ref[...]
    pre = (w1t_ref[...] * tn_l + _dot_hi(w1c_ref[...], jnp.cos(ang)) - _dot_hi(w1s_ref[...], jnp.sin(ang))
           + b1_ref[...])
    act = jnp.sin(fr * pre)
    act = jnp.sin(fr * (_dot_hi(w2_ref[...], act) + b2_ref[...]))
    act = jnp.sin(fr * (_dot_hi(w3_ref[...], act) + b3_ref[...])).astype(BF16)
    n_s, _, tn_s = positions((nrow, 1), 0)
    delta = jnp.concatenate([delta_ref[...]] * (ncol // width), axis=1)
    hf = _dot_tn(act[:, :nf], w4f_ref[...]) * jnp.exp(-tn_s[:nf] * delta)
    hb = _dot_tn(act[:, nf:], w4b_ref[...]) * jnp.exp(-tn_s[nf:] * delta)
    hb = jnp.where(n_s[nf:] == float(seq), 0.0, hb)
    k_ref[:, :half, :] = hf.reshape(gb, half, ncol).astype(k_ref.dtype)
    k_ref[:, half:, :] = hb.reshape(gb, half, ncol).astype(k_ref.dtype)
    tot = jnp.sum(jnp.abs(hf), axis=0, keepdims=True) + jnp.sum(jnp.abs(hb), axis=0, keepdims=True)

    @pl.when(step == 0)
    def _():
        s_ref[...] = jnp.zeros_like(s_ref)

    s_ref[...] += tot


def hyena_filter_taps(seq, w1, b1, w2, b2, w3, b3, w4, freq, width):
    emb = w1.shape[0]
    hid = w1.shape[1]
    bands = (emb - 1) // 2
    q = 2 * seq // DFT_P
    ncol = HY_ORDER * width
    band = np.linspace(1e-4, bands - 1, bands, dtype=np.float32).reshape(bands, 1)
    min_decay = math.log(HY_DECAY_TARGET) / HY_SLOW_PCT
    max_decay = math.log(HY_DECAY_TARGET) / HY_FAST_PCT
    delta = np.abs(np.linspace(min_decay, max_decay, width, dtype=np.float32)).reshape(1, width)
    w1t = w1.astype(F32).T
    col = lambda v: v.reshape(hid, 1).astype(F32)
    w4r = w4.astype(BF16).reshape(hid, HY_ORDER, 2, width)
    w4f = w4r[:, :, 0, :].reshape(hid, ncol)
    w4b = w4r[:, :, 1, :].reshape(hid, ncol)
    gb = FILTER_GROUP
    const = lambda shape: pl.BlockSpec(shape, lambda i: tuple(0 for _ in shape))
    return pl.pallas_call(
        functools.partial(_filter_kernel, seq=seq),
        out_shape=(jax.ShapeDtypeStruct((DFT_P, q, ncol), BF16),
                   jax.ShapeDtypeStruct((1, ncol), F32)),
        grid=(DFT_P // gb,),
        in_specs=[const((bands, 1)), const((hid, 1)), const((hid, bands)), const((hid, bands)), const((hid, 1)),
                  const((hid, hid)), const((hid, 1)), const((hid, hid)), const((hid, 1)),
                  const((hid, ncol)), const((hid, ncol)), const((hid, 1)), const((1, width))],
        out_specs=(pl.BlockSpec((gb, q, ncol), lambda i: (i, 0, 0)),
                   pl.BlockSpec((1, ncol), lambda i: (0, 0))),
        compiler_params=_cparams("arbitrary"),
        name="hyena_filter",
    )(jnp.asarray(band), w1t[:, 0:1], w1t[:, 1:1 + bands], w1t[:, 1 + bands:1 + 2 * bands], col(b1),
      w2.astype(F32).T, col(b2), w3.astype(F32).T, col(b3), w4f, w4b, col(freq), jnp.asarray(delta))


def _dft_tables(seq):
    p = DFT_P
    n_fft = 2 * seq
    q = n_fft // p
    qh = q // 2
    ka = jnp.arange(q, dtype=I32)
    a = jnp.arange(q, dtype=I32)
    b = jnp.arange(p, dtype=I32)
    nn = a[None, :] * p + b[:, None]
    ph = (ka[None, :, None] * nn[:, None, :]) % n_fft
    ang = ph.astype(F32) * (2.0 * math.pi / n_fft)
    mr, mi = jnp.cos(ang), -jnp.sin(ang)
    m1c = jnp.concatenate([jnp.concatenate([mr[:, :, :qh], -mi[:, :, :qh]], axis=2),
                           jnp.concatenate([mi[:, :, :qh], mr[:, :, :qh]], axis=2)], axis=1)
    m1r = jnp.concatenate([mr, mi], axis=1)
    gr = jnp.swapaxes(mr[:, :, :qh], 1, 2) / n_fft
    gi = -jnp.swapaxes(mi[:, :, :qh], 1, 2) / n_fft
    m4 = jnp.concatenate([jnp.concatenate([gr, -gi], axis=2), jnp.concatenate([gi, gr], axis=2)], axis=1)
    kb = np.arange(p)
    ang2 = 2.0 * np.pi * ((kb[:, None] * kb[None, :]) % p) / p
    fr, fi = np.cos(ang2), -np.sin(ang2)
    m2 = np.block([[fr, -fi], [fi, fr]]).astype(np.float32)
    m3 = np.block([[fr, fi], [-fi, fr]]).astype(np.float32)
    return (m1c.astype(BF16), m1r.astype(BF16), jnp.asarray(m2, BF16), jnp.asarray(m3, BF16), m4.astype(BF16))


def _bmm_kernel(w_ref, x_ref, o_ref, *, shared_w):
    for j in range(x_ref.shape[0]):
        w = w_ref[...] if shared_w else w_ref[j]
        o_ref[j] = _dot(w, x_ref[j]).astype(o_ref.dtype)


def batched_left_matmul(w, x, col_block, ncols, name, gb):
    g, k = x.shape[0], x.shape[1]
    shared = w.ndim == 2
    m = w.shape[-2]
    wspec = (pl.BlockSpec((m, k), lambda i: (0, 0)) if shared
             else pl.BlockSpec((gb, m, k), lambda i: (i, 0, 0)))
    return pl.pallas_call(
        functools.partial(_bmm_kernel, shared_w=shared),
        out_shape=jax.ShapeDtypeStruct((g, m, ncols), BF16),
        grid=(g // gb,),
        in_specs=[wspec, pl.BlockSpec((gb, k, ncols), lambda i: (i, 0, col_block))],
        out_specs=pl.BlockSpec((gb, m, ncols), lambda i: (i, 0, 0)),
        compiler_params=_cparams("arbitrary"),
        name=name,
    )(w, x)


def _dft_mid_kernel(m2_ref, m3_ref, x_ref, k_ref, o_ref):
    half = x_ref.shape[1] // 2
    for j in range(x_ref.shape[0]):
        xf = _dot(m2_ref[...], x_ref[j])
        kf = k_ref[j].astype(F32)
        xr, xi = xf[:half], xf[half:]
        kr, ki = kf[:half], kf[half:]
        z = jnp.concatenate([xr * kr - xi * ki, xr * ki + xi * kr], axis=0).astype(BF16)
        o_ref[j] = _dot(m3_ref[...], z).astype(o_ref.dtype)


def dft_mid(m2, m3, x, kspec, kcol, ncols):
    g, r = x.shape[0], x.shape[1]
    gb = DFT_GROUP
    return pl.pallas_call(
        _dft_mid_kernel,
        out_shape=jax.ShapeDtypeStruct((g, r, ncols), BF16),
        grid=(g // gb,),
        in_specs=[pl.BlockSpec((r, r), lambda i: (0, 0)),
                  pl.BlockSpec((r, r), lambda i: (0, 0)),
                  pl.BlockSpec((gb, r, ncols), lambda i: (i, 0, 0)),
                  pl.BlockSpec((gb, r, ncols), lambda i: (i, 0, kcol))],
        out_specs=pl.BlockSpec((gb, r, ncols), lambda i: (i, 0, 0)),
        compiler_params=_cparams("arbitrary"),
        name="dft_mid",
    )(m2, m3, x, kspec)


def _dft_out_kernel(m4_ref, y_ref, inv_ref, skip_ref, v_ref, mul_ref, o_ref):
    for j in range(y_ref.shape[0]):
        conv = _dot(m4_ref[j], y_ref[j]) * inv_ref[...] + v_ref[j].astype(F32) * skip_ref[...]
        o_ref[j] = (mul_ref[j].astype(F32) * conv).astype(o_ref.dtype)


def dft_out(m4, y, inv_l1, skip, u, v_col, mul, mul_col, ncols):
    g, r = y.shape[0], y.shape[1]
    rows = m4.shape[1]
    gb = DFT_GROUP
    return pl.pallas_call(
        _dft_out_kernel,
        out_shape=jax.ShapeDtypeStruct((g, rows, ncols), BF16),
        grid=(g // gb,),
        in_specs=[pl.BlockSpec((gb, rows, r), lambda i: (i, 0, 0)),
                  pl.BlockSpec((gb, r, ncols), lambda i: (i, 0, 0)),
                  pl.BlockSpec((1, ncols), lambda i: (0, 0)),
                  pl.BlockSpec((1, ncols), lambda i: (0, 0)),
                  pl.BlockSpec((gb, rows, ncols), lambda i: (i, 0, v_col)),
                  pl.BlockSpec((gb, rows, ncols), lambda i: (i, 0, mul_col))],
        out_specs=pl.BlockSpec((gb, rows, ncols), lambda i: (i, 0, 0)),
        compiler_params=_cparams("arbitrary"),
        name="dft_out",
    )(m4, y, inv_l1, skip, u, mul)


def _swap_ab(x):
    g1, r, c = x.shape
    g2 = r // 2
    return x.reshape(g1, 2, g2, c).transpose(2, 1, 0, 3).reshape(g2, 2 * g1, c)


def hyena_branch(u_nat, bsz, seq, width, taps, l1, skip):
    p = DFT_P
    qh = seq // p
    m1c, m1r, m2, m3, m4 = _dft_tables(seq)
    ncol = HY_ORDER * width
    ks1 = batched_left_matmul(m1r, taps, 0, ncol, "dft_k1", FILTER_GROUP)
    kspec = batched_left_matmul(m2, _swap_ab(ks1), 0, ncol, "dft_k2", FILTER_GROUP)
    inv_l1 = 1.0 / l1
    u = u_nat.reshape(bsz, qh, p, 3 * width).transpose(2, 0, 1, 3).reshape(p, bsz * qh, 3 * width)
    z = None
    for order in range(HY_ORDER):
        src, src_col = (u, 0) if order == 0 else (z, 0)
        s1 = batched_left_matmul(m1c, src, src_col, width, "dft_s1", DFT_GROUP)
        mid = dft_mid(m2, m3, _swap_ab(s1), kspec, order, width)
        z = dft_out(m4, _swap_ab(mid), inv_l1[:, order * width:(order + 1) * width],
                    skip[order].reshape(1, width).astype(F32), src, src_col, u, order + 1, width)
    return z.reshape(p, bsz, qh, width).transpose(1, 2, 0, 3).reshape(bsz * seq, width)


def _pack_pairs(x):
    w = x.shape[1] // 2
    u = lax.bitcast_convert_type(x, U32)
    r = (u + U32(0x7FFF) + ((u >> 16) & U32(1))) >> 16
    return r[:, :w] | (r[:, w:] << 16)


def _unpack_pairs(p):
    lo = lax.bitcast_convert_type(p << 16, F32)
    hi = lax.bitcast_convert_type(p & U32(0xFFFF0000), F32)
    return jnp.concatenate([lo, hi], axis=1)


def _merge_kernel(x_ref, er_ref, ec_ref, yhy_ref, yhg_ref, ghy_ref, ghg_ref, why_ref, whg_ref, wo_ref,
                  g1_ref, n2_ref, sh2_ref, sc2_ref, g2_ref, rwh_ref, rwl_ref, sgu_ref, sd_ref,
                  xres_ref, h2p_ref, lg_ref):
    x = x_ref[...]
    rows, gw, d = x.shape
    half = d // 2
    xp = jnp.concatenate([x[:, :, :half] + er_ref[...], x[:, :, half:] + ec_ref[...]], axis=-1)
    xp = xp.reshape(rows * gw, d)
    m = (jax.nn.sigmoid(ghy_ref[...].astype(F32)) * _dot(yhy_ref[...], why_ref[...])
         + jax.nn.sigmoid(ghg_ref[...].astype(F32)) * _dot(yhg_ref[...], whg_ref[...]))
    x1 = xp + g1_ref[...] * _dot(m.astype(BF16), wo_ref[...])
    ms = jnp.mean(x1 * x1, axis=-1, keepdims=True)
    h2 = x1 * lax.rsqrt(ms + NORM_EPS) * n2_ref[...] * (1.0 + sc2_ref[...]) + sh2_ref[...]
    h_hi, h_lo = _split_bf16(h2)
    lg_ref[...] = _dot_nt(rwh_ref[...], h_hi) + (_dot_nt(rwl_ref[...], h_hi) + _dot_nt(rwh_ref[...], h_lo))
    gu = _dot(h_hi, sgu_ref[...])
    fs = gu.shape[1] // 2
    shared = _dot((_silu(gu[:, :fs]) * gu[:, fs:]).astype(BF16), sd_ref[...])
    xres_ref[...] = x1 + g2_ref[...] * shared
    h2p_ref[...] = _pack_pairs(h2)


def merge_stage(x, emb_r, emb_c, y_hy, y_hg, p, gate_cols, w_hy_out, w_hg_out, w_out, g1, norm2_g,
                sh2, sc2, g2, router_wt, sh_gate_up, sh_down, tm):
    b, s, d = x.shape
    rows_per_batch = s // GRID_W
    rt = tm // GRID_W
    tiles_per_batch = rows_per_batch // rt
    x3 = x.reshape(b * rows_per_batch, GRID_W, d)
    wb = y_hy.shape[1]
    ne = router_wt.shape[0]
    fs2 = sh_gate_up.shape[1]
    rw_hi, rw_lo = _split_bf16(router_wt)
    tok = lambda cb, w: pl.BlockSpec((tm, w), lambda i: (i, cb))
    const = lambda shape: pl.BlockSpec(shape, lambda i: tuple(0 for _ in shape))
    per_b = pl.BlockSpec((None, 1, d), lambda i: (i // tiles_per_batch, 0, 0))
    return pl.pallas_call(
        _merge_kernel,
        out_shape=(jax.ShapeDtypeStruct((b * s, d), F32),
                   jax.ShapeDtypeStruct((b * s, d // 2), U32),
                   jax.ShapeDtypeStruct((ne, b * s), F32)),
        grid=(b * tiles_per_batch,),
        in_specs=[pl.BlockSpec((rt, GRID_W, d), lambda i: (i, 0, 0)),
                  pl.BlockSpec((rt, 1, d // 2), lambda i: (i % tiles_per_batch, 0, 0)),
                  const((GRID_W, d // 2)),
                  tok(0, wb), tok(0, wb), tok(gate_cols[0], d), tok(gate_cols[1], d),
                  const((wb, d)), const((wb, d)), const((d, d)),
                  per_b, const((1, d)), per_b, per_b, per_b,
                  const((ne, d)), const((ne, d)), const((d, fs2)), const((fs2 // 2, d))],
        out_specs=(pl.BlockSpec((tm, d), lambda i: (i, 0)),
                   pl.BlockSpec((tm, d // 2), lambda i: (i, 0)),
                   pl.BlockSpec((ne, tm), lambda i: (0, i))),
        compiler_params=_cparams("arbitrary"),
        name="merge",
    )(x3, emb_r, emb_c, y_hy, y_hg, p, p, w_hy_out, w_hg_out, w_out, g1, norm2_g.reshape(1, d),
      sh2, sc2, g2, rw_hi, rw_lo, sh_gate_up, sh_down)


def _route_kernel(lg_ref, bias_ref, eidx_ref, wsel_ref, rank_ref, cnt_ref, carry):
    ne, tr = lg_ref.shape
    gsz = ne // N_GROUPS
    neg = -jnp.inf

    @pl.when(pl.program_id(0) == 0)
    def _():
        carry[...] = jnp.zeros_like(carry)

    scores = jax.nn.sigmoid(lg_ref[...])
    biased = scores + bias_ref[...]
    riota = lax.broadcasted_iota(I32, (gsz, tr), 0).astype(F32)
    gs = []
    for g in range(N_GROUPS):
        vg = biased[g * gsz:(g + 1) * gsz]
        m1 = jnp.max(vg, axis=0, keepdims=True)
        i1 = jnp.min(jnp.where(vg == m1, riota, float(gsz)), axis=0, keepdims=True)
        m2 = jnp.max(jnp.where(riota == i1, neg, vg), axis=0, keepdims=True)
        gs.append(m1 + m2)
    cur = jnp.concatenate(gs, axis=0)
    giota = lax.broadcasted_iota(I32, (N_GROUPS, tr), 0).astype(F32)
    gsel = jnp.zeros((N_GROUPS, tr), F32)
    for _ in range(TOPK_GROUPS):
        m = jnp.max(cur, axis=0, keepdims=True)
        idx = jnp.min(jnp.where(cur == m, giota, float(N_GROUPS)), axis=0, keepdims=True)
        hit = giota == idx
        gsel = jnp.where(hit, 1.0, gsel)
        cur = jnp.where(hit, neg, cur)
    cur = jnp.concatenate([jnp.where(gsel[g:g + 1] > 0.0, biased[g * gsz:(g + 1) * gsz], neg)
                           for g in range(N_GROUPS)], axis=0)
    eiota = lax.broadcasted_iota(I32, (ne, tr), 0).astype(F32)
    chosen = jnp.zeros((ne, tr), F32)
    idxs, ws = [], []
    for _ in range(TOP_K):
        m = jnp.max(cur, axis=0, keepdims=True)
        idx = jnp.min(jnp.where(cur == m, eiota, float(ne)), axis=0, keepdims=True)
        hit = eiota == idx
        idxs.append(idx)
        ws.append(jnp.sum(jnp.where(hit, scores, 0.0), axis=0, keepdims=True))
        chosen = jnp.where(hit, 1.0, chosen)
        cur = jnp.where(hit, neg, cur)
    w = jnp.concatenate(ws, axis=0)
    wsel_ref[...] = w / jnp.sum(w, axis=0, keepdims=True) * ROUTED_SCALE
    eidx_ref[...] = jnp.concatenate(idxs, axis=0).astype(I32)
    srow = lax.broadcasted_iota(I32, (tr, tr), 0)
    scol = lax.broadcasted_iota(I32, (tr, tr), 1)
    before = (srow < scol).astype(BF16)
    base = carry[...] + _dot(chosen.astype(BF16), before)
    ranks = [jnp.sum(jnp.where(eiota == idx, base, 0.0), axis=0, keepdims=True) for idx in idxs]
    rank_ref[...] = jnp.concatenate(ranks, axis=0).astype(I32)
    carry[...] += jnp.sum(chosen, axis=1, keepdims=True)
    cnt_ref[...] = carry[...]


def route(logits_t, router_bias, tr):
    ne, t = logits_t.shape
    return pl.pallas_call(
        _route_kernel,
        out_shape=(jax.ShapeDtypeStruct((TOP_K, t), I32),
                   jax.ShapeDtypeStruct((TOP_K, t), F32),
                   jax.ShapeDtypeStruct((TOP_K, t), I32),
                   jax.ShapeDtypeStruct((ne, 1), F32)),
        grid=(t // tr,),
        in_specs=[pl.BlockSpec((ne, tr), lambda i: (0, i)),
                  pl.BlockSpec((ne, 1), lambda i: (0, 0))],
        out_specs=(pl.BlockSpec((TOP_K, tr), lambda i: (0, i)),
                   pl.BlockSpec((TOP_K, tr), lambda i: (0, i)),
                   pl.BlockSpec((TOP_K, tr), lambda i: (0, i)),
                   pl.BlockSpec((ne, 1), lambda i: (0, 0))),
        scratch_shapes=[pltpu.VMEM((ne, 1), F32)],
        compiler_params=_cparams("arbitrary"),
        name="route",
    )(logits_t, router_bias.reshape(ne, 1).astype(F32))


def _dest_kernel(cnt_ref, eidx_ref, rank_ref, dest_ref, be_ref, nv_ref, nb_ref, start_scr):
    ne = cnt_ref.shape[0]
    tr = eidx_ref.shape[1]

    @pl.when(pl.program_id(0) == 0)
    def _():
        cnt = jnp.broadcast_to(cnt_ref[...], (ne, LANES))
        padded = jnp.floor((cnt + float(MOE_ROWS - 1)) / float(MOE_ROWS)) * float(MOE_ROWS)
        r = lax.broadcasted_iota(I32, (ne, ne), 0)
        c = lax.broadcasted_iota(I32, (ne, ne), 1)
        start = _dot_hi((c < r).astype(F32), padded)
        start_scr[...] = start
        end = start[:, 0:1] + padded[:, 0:1]
        used = start[:, 0:1] + cnt[:, 0:1]
        nbl = be_ref.shape[1]
        blk_row = (lax.broadcasted_iota(I32, (1, nbl), 1) * MOE_ROWS).astype(F32)
        total = jnp.max(end, axis=0, keepdims=True)
        last_row = total - float(MOE_ROWS)
        blk_row_c = jnp.minimum(blk_row, last_row)
        e_of = jnp.sum((end <= blk_row_c).astype(F32), axis=0, keepdims=True)
        e_of = jnp.minimum(e_of, float(ne - 1))
        eio = lax.broadcasted_iota(I32, (ne, nbl), 0).astype(F32)
        used_e = jnp.sum(jnp.where(eio == e_of, used, 0.0), axis=0, keepdims=True)
        valid = jnp.clip(used_e - blk_row_c, 0.0, float(MOE_ROWS))
        be_ref[...] = e_of.astype(I32)
        nv_ref[...] = jnp.where(blk_row <= last_row, valid, 0.0).astype(I32)
        nb_ref[...] = jnp.broadcast_to(total / float(MOE_ROWS), nb_ref.shape).astype(I32)

    eiota = lax.broadcasted_iota(I32, (ne, tr), 0)
    start_col = start_scr[:, 0:1]
    rows = []
    for k in range(TOP_K):
        hit = eiota == eidx_ref[k:k + 1, :]
        rows.append(jnp.sum(jnp.where(hit, start_col, 0.0), axis=0, keepdims=True))
    dest_ref[...] = jnp.concatenate(rows, axis=0).astype(I32) + rank_ref[...]


def dispatch_plan(counts, eidx, rank, tr, n_blocks):
    ne = counts.shape[0]
    t = eidx.shape[1]
    nbl = pl.cdiv(n_blocks, LANES) * LANES
    return pl.pallas_call(
        _dest_kernel,
        out_shape=(jax.ShapeDtypeStruct((TOP_K, t), I32),
                   jax.ShapeDtypeStruct((1, nbl), I32),
                   jax.ShapeDtypeStruct((1, nbl), I32),
                   jax.ShapeDtypeStruct((1, LANES), I32)),
        grid=(t // tr,),
        in_specs=[pl.BlockSpec((ne, 1), lambda i: (0, 0)),
                  pl.BlockSpec((TOP_K, tr), lambda i: (0, i)),
                  pl.BlockSpec((TOP_K, tr), lambda i: (0, i))],
        out_specs=(pl.BlockSpec((TOP_K, tr), lambda i: (0, i)),
                   pl.BlockSpec((1, nbl), lambda i: (0, 0)),
                   pl.BlockSpec((1, nbl), lambda i: (0, 0)),
                   pl.BlockSpec((1, LANES), lambda i: (0, 0))),
        scratch_shapes=[pltpu.VMEM((ne, LANES), F32)],
        compiler_params=_cparams("arbitrary"),
        name="dispatch_plan",
    )(counts, eidx, rank)


def _tile_dest(dest, tm):
    k, t = dest.shape
    return dest.reshape(k, t // tm, tm).transpose(1, 0, 2).reshape(t // tm, 1, k * tm)


def _scatter_kernel(dest_ref, h_ref, xs_ref, sem):
    tm = h_ref.shape[0]

    def row_copy(t, d):
        return pltpu.make_async_copy(h_ref.at[pl.ds(t, 1)], xs_ref.at[pl.ds(d, 1)], sem)

    def issue(t, carry):
        for k in range(TOP_K):
            row_copy(t, dest_ref[0, k * tm + t]).start(priority=k % 2)
        return carry

    lax.fori_loop(0, tm, issue, 0, unroll=2)

    def drain(t, carry):
        for k in range(TOP_K):
            row_copy(0, 0).wait()
        return carry

    lax.fori_loop(0, tm, drain, 0)


def scatter_rows(dest, h2p, n_rows, tm):
    t, w = h2p.shape
    return pl.pallas_call(
        _scatter_kernel,
        out_shape=jax.ShapeDtypeStruct((n_rows, w), U32),
        grid=(t // tm,),
        in_specs=[pl.BlockSpec((None, 1, TOP_K * tm), lambda i: (i, 0, 0), memory_space=pltpu.SMEM),
                  pl.BlockSpec((tm, w), lambda i: (i, 0))],
        out_specs=pl.BlockSpec(memory_space=pl.ANY),
        scratch_shapes=[pltpu.SemaphoreType.DMA(())],
        compiler_params=_cparams("arbitrary"),
        name="scatter_rows",
    )(dest, h2p)


def _gmm_kernel(be_ref, nv_ref, nb_ref, first_ref, slot_ref, nxt_ref, xs_ref, wg_hbm, wu_hbm, wd_hbm, y_ref,
                wg_buf, wu_buf, wd_buf, sem):
    def weight_copies(e, s):
        return (pltpu.make_async_copy(wg_hbm.at[e], wg_buf.at[s], sem.at[s]),
                pltpu.make_async_copy(wu_hbm.at[e], wu_buf.at[s], sem.at[s]),
                pltpu.make_async_copy(wd_hbm.at[e], wd_buf.at[s], sem.at[s]))

    def one_block(j, rows):
        s = slot_ref[j]

        @pl.when(j == 0)
        def _():
            for c in weight_copies(be_ref[0], 0):
                c.start()

        @pl.when(first_ref[j] == 1)
        def _():
            for c in weight_copies(be_ref[j], s):
                c.wait()

            @pl.when(nxt_ref[j] >= 0)
            def _():
                for c in weight_copies(nxt_ref[j], 1 - s):
                    c.start()

        x = _unpack_pairs(xs_ref[rows, :])
        row = lax.broadcasted_iota(I32, (x.shape[0], 1), 0)
        x = jnp.where(row < nv_ref[j], x, 0.0)
        hmid = _silu(_dot(x, wg_buf[s])) * _dot(x, wu_buf[s])
        y_ref[rows, :] = _pack_pairs(_dot(hmid, wd_buf[s]))

    for sb in range(MOE_STEP_BLOCKS):
        j = pl.program_id(0) * MOE_STEP_BLOCKS + sb
        pl.when(j < nb_ref[0])(functools.partial(one_block, j, slice(sb * MOE_ROWS, (sb + 1) * MOE_ROWS)))


def grouped_mlp(block_e, block_valid, n_used, xs, w_gate, w_up, w_down, n_blocks):
    ne, d, f = w_gate.shape
    w = xs.shape[1]
    jj = jnp.arange(block_e.shape[0], dtype=I32)
    active = jj < n_used[0]
    first = (active & ((jj == 0) | (block_e != jnp.roll(block_e, 1)))).astype(I32)
    run = jnp.cumsum(first) - 1
    slot = (run % 2).astype(I32)
    nbl = block_e.shape[0]
    run_expert = jnp.full((nbl + 1,), -1, I32).at[jnp.where(first == 1, run, nbl)].set(
        jnp.where(first == 1, block_e, -1))
    nxt = run_expert[jnp.minimum(run + 1, nbl)]
    step_rows = MOE_STEP_BLOCKS * MOE_ROWS
    last = lambda g, nb: jnp.minimum(g, (nb[0] - 1) // MOE_STEP_BLOCKS)
    grid_spec = pltpu.PrefetchScalarGridSpec(
        num_scalar_prefetch=6,
        grid=(n_blocks // MOE_STEP_BLOCKS,),
        in_specs=[pl.BlockSpec((step_rows, w), lambda g, be, nv, nb, fi, sl, nx: (last(g, nb), 0)),
                  pl.BlockSpec(memory_space=pl.ANY),
                  pl.BlockSpec(memory_space=pl.ANY),
                  pl.BlockSpec(memory_space=pl.ANY)],
        out_specs=pl.BlockSpec((step_rows, w), lambda g, be, nv, nb, fi, sl, nx: (last(g, nb), 0)),
        scratch_shapes=[pltpu.VMEM((2, d, f), F32), pltpu.VMEM((2, d, f), F32), pltpu.VMEM((2, f, d), F32),
                        pltpu.SemaphoreType.DMA((2,))],
    )
    return pl.pallas_call(
        _gmm_kernel,
        out_shape=jax.ShapeDtypeStruct(xs.shape, U32),
        grid_spec=grid_spec,
        compiler_params=_cparams("arbitrary"),
        name="grouped_mlp",
    )(block_e, block_valid, n_used, first, slot, nxt, xs, w_gate, w_up, w_down)


def _combine_kernel(dcur_ref, dnext_ref, xres_ref, wt_ref, g2_ref, fg_ref, ys_ref, o_ref, buf, sem):
    tm = xres_ref.shape[0]
    i = pl.program_id(0)
    slot = i % 2

    def row_copy(s, k, t, d):
        return pltpu.make_async_copy(ys_ref.at[pl.ds(d, 1)], buf.at[s, k, pl.ds(t, 1)], sem.at[s])

    def issue_tile(d_ref, s):
        def issue(t, carry):
            for k in range(TOP_K):
                row_copy(s, k, t, d_ref[0, k * tm + t]).start(priority=k % 2)
            return carry

        lax.fori_loop(0, tm, issue, 0, unroll=2)

    @pl.when(i == 0)
    def _():
        issue_tile(dcur_ref, 0)

    @pl.when(i + 1 < pl.num_programs(0))
    def _():
        issue_tile(dnext_ref, 1 - slot)

    def drain(t, carry):
        for k in range(TOP_K):
            row_copy(slot, 0, 0, 0).wait()
        return carry

    lax.fori_loop(0, tm, drain, 0)

    wt = wt_ref[...]
    routed = jnp.zeros(xres_ref.shape, F32)
    for k in range(TOP_K):
        routed = routed + wt[:, k:k + 1] * _unpack_pairs(buf[slot, k])
    x2 = xres_ref[...] + g2_ref[...] * routed
    ms = jnp.mean(x2 * x2, axis=-1, keepdims=True)
    o_ref[...] = x2 * lax.rsqrt(ms + NORM_EPS) * fg_ref[...]


def combine(dest, xres, wsel_t, g2, final_g, ys, seq, tm):
    t, d = xres.shape
    tiles_per_batch = seq // tm
    nt = t // tm
    return pl.pallas_call(
        _combine_kernel,
        out_shape=jax.ShapeDtypeStruct((t, d), F32),
        grid=(nt,),
        in_specs=[pl.BlockSpec((None, 1, TOP_K * tm), lambda i: (i, 0, 0), memory_space=pltpu.SMEM),
                  pl.BlockSpec((None, 1, TOP_K * tm), lambda i: (jnp.minimum(i + 1, nt - 1), 0, 0),
                               memory_space=pltpu.SMEM),
                  pl.BlockSpec((tm, d), lambda i: (i, 0)),
                  pl.BlockSpec((tm, TOP_K), lambda i: (i, 0)),
                  pl.BlockSpec((None, 1, d), lambda i: (i // tiles_per_batch, 0, 0)),
                  pl.BlockSpec((1, d), lambda i: (0, 0)),
                  pl.BlockSpec(memory_space=pl.ANY)],
        out_specs=pl.BlockSpec((tm, d), lambda i: (i, 0)),
        scratch_shapes=[pltpu.VMEM((2, TOP_K, tm, d // 2), U32), pltpu.SemaphoreType.DMA((2,))],
        compiler_params=_cparams("arbitrary"),
        name="combine",
    )(dest, dest, xres, wsel_t, g2, final_g.reshape(1, d), ys)


def _pos_tables(rows, cols, dim):
    quarter = dim // 4
    omega = 1.0 / (POS_BASE ** (np.arange(quarter, dtype=np.float32) / quarter))
    ang_r = np.arange(rows, dtype=np.float32)[:, None] * omega
    ang_c = np.arange(cols, dtype=np.float32)[:, None] * omega
    emb_r = np.concatenate([np.sin(ang_r), np.cos(ang_r)], axis=-1).astype(np.float32)
    emb_c = np.concatenate([np.sin(ang_c), np.cos(ang_c)], axis=-1).astype(np.float32)
    return jnp.asarray(emb_r.reshape(rows, 1, dim // 2)), jnp.asarray(emb_c)


def kernel(x, c, ctx, c_ctx, norm1_g, norm2_g, ada_w, ada_b, w_in, hy_conv_w, hy_conv_b, hy_f_w1, hy_f_b1, hy_f_w2, hy_f_b2, hy_f_w3, hy_f_b3, hy_f_w4, hy_f_freq, hy_skip, hg_lb_logits, hg_norm_g, w_hy_out, w_hg_out, w_out, router_w, router_bias, exp_w_gate, exp_w_up, exp_w_down, sh_w_gate, sh_w_up, sh_w_down, final_g):
    bsz, seq, d = x.shape
    n_ctx = ctx.shape[1]
    hy_w = w_hy_out.shape[1]
    hg_w = w_hg_out.shape[1]
    dk = hg_norm_g.shape[1]
    n_heads = hg_w // dk
    ne = router_w.shape[2]
    l = 0

    c_rows = jnp.zeros((SUBLANES, d), F32).at[:bsz].set(c).at[bsz].set(c_ctx)
    mods = ada_vectors(c_rows, ada_w[l], ada_b[l])
    sh1, sc1, g1, sh2, sc2, g2 = [mods[:bsz, j * d:(j + 1) * d].reshape(bsz, 1, d) for j in range(N_ADA)]
    csh1 = jnp.broadcast_to(mods[bsz, 0:d].reshape(1, 1, d), (bsz, 1, d))
    csc1 = jnp.broadcast_to(mods[bsz, d:2 * d].reshape(1, 1, d), (bsz, 1, d))

    emb_r, emb_c = _pos_tables(seq // GRID_W, GRID_W, d)
    w_in_b = w_in[l].astype(BF16)
    hy_proj = 3 * hy_w
    p = in_projection(x, emb_r, emb_c, norm1_g[l], sh1, sc1, w_in_b, TOKEN_TILE)
    hg_cols = slice(hy_proj, hy_proj + 5 * hg_w)
    zero_r = jnp.zeros((n_ctx // GRID_W, 1, d // 2), F32)
    zero_c = jnp.zeros((GRID_W, d // 2), F32)
    pc = in_projection(ctx, zero_r, zero_c, norm1_g[l], csh1, csc1, w_in_b[:, hg_cols], n_ctx)

    lbs = jnp.cumsum(jax.nn.softmax(hg_lb_logits.astype(F32), axis=0), axis=0)
    lb_f, lb_b = lbs[l, 0], lbs[l, 1]
    zero_state = jnp.zeros((bsz, n_heads, dk, dk), F32)
    base = hy_proj // hg_w
    _, st_f = hgrn_scan(pc, (0, 1, 2), lb_f, zero_state, n_ctx, n_ctx, reverse=False)
    _, st_b = hgrn_scan(pc, (0, 1, 3), lb_b, zero_state, n_ctx, n_ctx, reverse=True)
    o_f, _ = hgrn_scan(p, (base, base + 1, base + 2), lb_f, st_f, seq, HG_TIME_BLOCK, reverse=False)
    y_hg, _ = hgrn_scan(p, (base, base + 1, base + 3), lb_b, st_b, seq, HG_TIME_BLOCK, reverse=True,
                        o_fwd=o_f, gate_col=base + 4, norm_g=hg_norm_g[l])

    u = short_conv(p, hy_proj, hy_conv_w[l], hy_conv_b[l], seq, TOKEN_TILE)
    taps, l1 = hyena_filter_taps(seq, hy_f_w1[l], hy_f_b1[l], hy_f_w2[l], hy_f_b2[l], hy_f_w3[l], hy_f_b3[l],
                                 hy_f_w4[l], hy_f_freq[l], hy_w)
    y_hy = hyena_branch(u, bsz, seq, hy_w, taps, l1, hy_skip[l])

    gate_base = (hy_proj + 5 * hg_w) // d
    sh_gu = jnp.concatenate([sh_w_gate[l], sh_w_up[l]], axis=1).astype(BF16)
    xres, h2p, logits_t = merge_stage(
        x, emb_r, emb_c, y_hy, y_hg, p, (gate_base, gate_base + 1),
        w_hy_out[l].astype(BF16), w_hg_out[l].astype(BF16), w_out[l].astype(BF16), g1, norm2_g[l],
        sh2, sc2, g2, router_w[l].T.astype(F32), sh_gu, sh_w_down[l].astype(BF16), TOKEN_TILE)

    t = bsz * seq
    eidx, wsel, rank, counts = route(logits_t, router_bias[l], TOKEN_TILE)
    n_rows = t * TOP_K + ne * (MOE_ROWS - 1)
    n_blocks = pl.cdiv(pl.cdiv(n_rows, MOE_ROWS), MOE_STEP_BLOCKS) * MOE_STEP_BLOCKS
    dest, block_e, block_valid, n_used = dispatch_plan(counts, eidx, rank, TOKEN_TILE, n_blocks)

    xs = scatter_rows(_tile_dest(dest, SCATTER_TILE), h2p, n_blocks * MOE_ROWS, SCATTER_TILE)
    ys = grouped_mlp(block_e.reshape(-1), block_valid.reshape(-1), n_used.reshape(-1)[:1], xs,
                     exp_w_gate[l], exp_w_up[l], exp_w_down[l], n_blocks)
    out = combine(_tile_dest(dest, GATHER_TILE), xres, wsel.T, g2, final_g, ys, seq, GATHER_TILE)
    return out.reshape(bsz, seq, d)
```

```python
import functools
import math

import numpy as np
import jax
import jax.numpy as jnp
from jax import lax
from jax.experimental import pallas as pl
from jax.experimental.pallas import tpu as pltpu
from jax.experimental.pallas import tpu_sc as plsc

F32 = jnp.float32
BF16 = jnp.bfloat16
U32 = jnp.uint32
I32 = jnp.int32
HIGHEST = lax.Precision.HIGHEST

GRID_W = 64
POS_BASE = 10000.0
NORM_EPS = 1e-6
N_ADA = 6
HY_ORDER = 2
HY_SHORT = 3
HY_DECAY_TARGET = 1e-2
HY_FAST_PCT = 0.3
HY_SLOW_PCT = 1.5
HG_HEADS = 4
HG_CHUNK = 64
N_GROUPS = 8
TOPK_GROUPS = 4
TOP_K = 8
ROUTED_SCALE = 2.5

LANES = 128
SUBLANES = 8
VMEM_LIMIT = 56 * 1024 * 1024

TOKEN_TILE = 512
HG_TIME_BLOCK = 512
HALO_ROWS = 16
DFT_P = 128
FILTER_GROUP = 8
DFT_GROUP = 16
MOE_ROWS = 256
MOE_STEP_BLOCKS = 4
SC_ROWS = 128


def _cparams(*sem):
    return pltpu.CompilerParams(dimension_semantics=sem, vmem_limit_bytes=VMEM_LIMIT)


def _dot(a, b):
    return jnp.dot(a, b, preferred_element_type=F32)


def _dot_hi(a, b):
    return jnp.dot(a, b, preferred_element_type=F32, precision=HIGHEST)


def _dot_nt(a, b):
    return lax.dot_general(a, b, (((1,), (1,)), ((), ())), preferred_element_type=F32)


def _dot_tn(a, b):
    return lax.dot_general(a, b, (((0,), (0,)), ((), ())), preferred_element_type=F32)


def _silu(x):
    return x * jax.nn.sigmoid(x)


def _split_bf16(x):
    hi = x.astype(BF16)
    return hi, (x - hi.astype(F32)).astype(BF16)


def _ada_kernel(c_ref, w_ref, b_ref, o_ref):
    o_ref[...] = _dot_hi(_silu(c_ref[...]), w_ref[...]) + b_ref[...]


def ada_vectors(c_rows, ada_w, ada_b):
    r, d = c_rows.shape
    n = ada_w.shape[1]
    bn = 1024
    return pl.pallas_call(
        _ada_kernel,
        out_shape=jax.ShapeDtypeStruct((r, n), F32),
        grid=(n // bn,),
        in_specs=[pl.BlockSpec((r, d), lambda j: (0, 0)),
                  pl.BlockSpec((d, bn), lambda j: (0, j)),
                  pl.BlockSpec((1, bn), lambda j: (0, j))],
        out_specs=pl.BlockSpec((r, bn), lambda j: (0, j)),
        compiler_params=_cparams("arbitrary"),
        name="ada_vectors",
    )(c_rows, ada_w, ada_b.reshape(1, n))


def _inproj_kernel(x_ref, er_ref, ec_ref, g_ref, sh_ref, sc_ref, w_ref, o_ref, *, col_chunk):
    x = x_ref[...]
    rows, gw, d = x.shape
    half = d // 2
    xp = jnp.concatenate([x[:, :, :half] + er_ref[...], x[:, :, half:] + ec_ref[...]], axis=-1)
    xp = xp.reshape(rows * gw, d)
    ms = jnp.mean(xp * xp, axis=-1, keepdims=True)
    y = xp * lax.rsqrt(ms + NORM_EPS) * g_ref[...]
    h = (y * (1.0 + sc_ref[...]) + sh_ref[...]).astype(BF16)
    n = o_ref.shape[1]
    for j in range(n // col_chunk):
        sl = slice(j * col_chunk, (j + 1) * col_chunk)
        o_ref[:, sl] = _dot(h, w_ref[:, sl]).astype(o_ref.dtype)


def in_projection(x, emb_r, emb_c, norm_g, shift, scale, w_bf16, tm):
    b, s, d = x.shape
    n = w_bf16.shape[1]
    rows_per_batch = s // GRID_W
    rt = tm // GRID_W
    tiles_per_batch = rows_per_batch // rt
    x3 = x.reshape(b * rows_per_batch, GRID_W, d)
    col_chunk = 512
    return pl.pallas_call(
        functools.partial(_inproj_kernel, col_chunk=col_chunk),
        out_shape=jax.ShapeDtypeStruct((b * s, n), BF16),
        grid=(b * tiles_per_batch,),
        in_specs=[pl.BlockSpec((rt, GRID_W, d), lambda i: (i, 0, 0)),
                  pl.BlockSpec((rt, 1, d // 2), lambda i: (i % tiles_per_batch, 0, 0)),
                  pl.BlockSpec((GRID_W, d // 2), lambda i: (0, 0)),
                  pl.BlockSpec((1, d), lambda i: (0, 0)),
                  pl.BlockSpec((None, 1, d), lambda i: (i // tiles_per_batch, 0, 0)),
                  pl.BlockSpec((None, 1, d), lambda i: (i // tiles_per_batch, 0, 0)),
                  pl.BlockSpec((d, n), lambda i: (0, 0))],
        out_specs=pl.BlockSpec((tm, n), lambda i: (i, 0)),
        compiler_params=_cparams("arbitrary"),
        name="in_projection",
    )(x3, emb_r, emb_c, norm_g.reshape(1, d), shift, scale, w_bf16)


def _hgrn_kernel(*refs, reverse, n_chunks, final):
    if final:
        (q_ref, i_ref, f_ref, lb_ref, s0_ref, of_ref, gate_ref, ng_ref, o_ref, sfin_ref, s_scr) = refs
    else:
        (q_ref, i_ref, f_ref, lb_ref, s0_ref, o_ref, sfin_ref, s_scr) = refs
    cs = HG_CHUNK
    bsz, n_heads, _, dk = s_scr.shape

    @pl.when(pl.program_id(0) == 0)
    def _():
        s_scr[...] = s0_ref[...]

    row = lax.broadcasted_iota(I32, (cs, cs), 0)
    col = lax.broadcasted_iota(I32, (cs, cs), 1)
    tri = (col >= row) if reverse else (col <= row)
    tri_b = tri.astype(BF16)
    end_row = 0 if reverse else cs - 1
    mid_row = cs // 2 if reverse else cs // 2 - 1

    def chunk_body(bi, ci):
        c = (n_chunks - 1 - ci) if reverse else ci
        rows = slice(c * cs, (c + 1) * cs)
        lb = lb_ref[...]
        f = lb + (1.0 - lb) * jax.nn.sigmoid(f_ref[bi, rows, :].astype(F32))
        lf_hi, lf_lo = _split_bf16(jnp.log(f))
        b_all = _dot(tri_b, lf_hi) + _dot(tri_b, lf_lo)
        k_all = 1.0 - f
        q_all = _silu(q_ref[bi, rows, :].astype(F32))
        for h in range(n_heads):
            sl = slice(h * dk, (h + 1) * dk)
            b = b_all[:, sl]
            q = q_all[:, sl]
            k = k_all[:, sl]
            v = i_ref[bi, rows, sl]
            b_end = b[end_row:end_row + 1]
            b_mid = b[mid_row:mid_row + 1]
            qd = (q * jnp.exp(b - b_mid)).astype(BF16)
            kd = (k * jnp.exp(b_mid - b)).astype(BF16)
            att = jnp.where(tri, _dot_nt(qd, kd), 0.0).astype(BF16)
            st = s_scr[bi, h]
            qe = (q * jnp.exp(b)).astype(BF16)
            o = _dot(att, v) + _dot_nt(qe, st.astype(BF16))
            ke = (k * jnp.exp(b_end - b)).astype(BF16)
            s_scr[bi, h] = st * jnp.exp(b_end) + _dot_tn(v, ke)
            if final:
                o = o + of_ref[bi, rows, sl].astype(F32)
                o = o * lax.rsqrt(jnp.mean(o * o, axis=-1, keepdims=True) + NORM_EPS) * ng_ref[...]
                o = o * _silu(gate_ref[bi, rows, sl].astype(F32))
            o_ref[bi, rows, sl] = o.astype(o_ref.dtype)

    for ci in range(n_chunks):
        for bi in range(bsz):
            chunk_body(bi, ci)
    sfin_ref[...] = s_scr[...]


def hgrn_scan(p, cols, lb, s0, seq, tb, *, reverse, o_fwd=None, gate_col=None, norm_g=None):
    bsz, n_heads, dv, dk = s0.shape
    width = n_heads * dk
    nt = seq // tb
    final = o_fwd is not None
    p3 = p.reshape(bsz, seq, p.shape[1])
    tmap = (lambda t: nt - 1 - t) if reverse else (lambda t: t)
    colspec = lambda cb: pl.BlockSpec((bsz, tb, width), lambda t: (0, tmap(t), cb))
    state = pl.BlockSpec((bsz, n_heads, dv, dk), lambda t: (0, 0, 0, 0))
    in_specs = [colspec(cols[0]), colspec(cols[1]), colspec(cols[2]),
                pl.BlockSpec((1, width), lambda t: (0, 0)), state]
    args = [p3, p3, p3, lb.reshape(1, width), s0]
    if final:
        in_specs += [colspec(0), colspec(gate_col), pl.BlockSpec((1, dk), lambda t: (0, 0))]
        args += [o_fwd.reshape(bsz, seq, width), p3, norm_g.reshape(1, dk)]
    o, s_fin = pl.pallas_call(
        functools.partial(_hgrn_kernel, reverse=reverse, n_chunks=tb // HG_CHUNK, final=final),
        out_shape=(jax.ShapeDtypeStruct((bsz, seq, width), BF16),
                   jax.ShapeDtypeStruct((bsz, n_heads, dv, dk), F32)),
        grid=(nt,),
        in_specs=in_specs,
        out_specs=(colspec(0), state),
        scratch_shapes=[pltpu.VMEM((bsz, n_heads, dv, dk), F32)],
        compiler_params=_cparams("arbitrary"),
        name="hgrn_bwd" if reverse else "hgrn_fwd",
    )(*args)
    return o.reshape(bsz * seq, width), s_fin


def _shortconv_kernel(p_ref, prev_ref, next_ref, w_ref, b_ref, o_ref, *, tiles_per_batch):
    i = pl.program_id(0)
    ti = i % tiles_per_batch
    p = p_ref[...].astype(F32)
    tm = p.shape[0]
    row = lax.broadcasted_iota(I32, (tm, 1), 0)
    prev_row = jnp.where(ti == 0, 0.0, prev_ref[HALO_ROWS - 1:HALO_ROWS, :].astype(F32))
    next_row = jnp.where(ti == tiles_per_batch - 1, 0.0, next_ref[0:1, :].astype(F32))
    p_prev = jnp.where(row == 0, prev_row, pltpu.roll(p, 1, axis=0))
    p_next = jnp.where(row == tm - 1, next_row, pltpu.roll(p, tm - 1, axis=0))
    u = w_ref[0:1, :] * p_prev + w_ref[1:2, :] * p + w_ref[2:3, :] * p_next + b_ref[...]
    o_ref[...] = u.astype(o_ref.dtype)


def short_conv(p, width, conv_w, conv_b, seq, tm):
    t = p.shape[0]
    nt = t // tm
    tiles_per_batch = seq // tm
    sub = tm // HALO_ROWS
    return pl.pallas_call(
        functools.partial(_shortconv_kernel, tiles_per_batch=tiles_per_batch),
        out_shape=jax.ShapeDtypeStruct((t, width), BF16),
        grid=(nt,),
        in_specs=[pl.BlockSpec((tm, width), lambda i: (i, 0)),
                  pl.BlockSpec((HALO_ROWS, width), lambda i: (jnp.maximum(i * sub - 1, 0), 0)),
                  pl.BlockSpec((HALO_ROWS, width), lambda i: (jnp.minimum((i + 1) * sub, t // HALO_ROWS - 1), 0)),
                  pl.BlockSpec((HY_SHORT, width), lambda i: (0, 0)),
                  pl.BlockSpec((1, width), lambda i: (0, 0))],
        out_specs=pl.BlockSpec((tm, width), lambda i: (i, 0)),
        compiler_params=_cparams("arbitrary"),
        name="short_conv",
    )(p, p, p, conv_w, conv_b.reshape(1, width))


def _filter_kernel(band_ref, w1t_ref, w1c_ref, w1s_ref, b1_ref, w2_ref, b2_ref, w3_ref, b3_ref,
                   w4f_ref, w4b_ref, fr_ref, delta_ref, k_ref, s_ref, *, seq):
    step = pl.program_id(0)
    gb, q, ncol = k_ref.shape
    half = q // 2
    width = delta_ref.shape[1]
    nrow = gb * q
    nf = gb * half

    def positions(shape, axis):
        r = lax.broadcasted_iota(I32, shape, axis)
        is_bwd = r >= nf
        rr = jnp.where(is_bwd, r - nf, r)
        j = lax.shift_right_logical(rr, int(math.log2(half)))
        a = (rr & (half - 1)) + jnp.where(is_bwd, half, 0)
        n = (a * DFT_P + step * gb + j).astype(F32)
        t = jnp.where(is_bwd, 2.0 * seq - n, n)
        return n, t, t / float(max(seq - 1, 1))

    _, t_l, tn_l = positions((1, nrow), 1)
    ang = (2.0 * math.pi / seq) * t_l * band_ref[...]
    fr = fr_ref[...]
    pre = (w1t_ref[...] * tn_l + _dot_hi(w1c_ref[...], jnp.cos(ang)) - _dot_hi(w1s_ref[...], jnp.sin(ang))
           + b1_ref[...])
    act = jnp.sin(fr * pre)
    act = jnp.sin(fr * (_dot_hi(w2_ref[...], act) + b2_ref[...]))
    act = jnp.sin(fr * (_dot_hi(w3_ref[...], act) + b3_ref[...])).astype(BF16)
    n_s, _, tn_s = positions((nrow, 1), 0)
    delta = jnp.concatenate([delta_ref[...]] * (ncol // width), axis=1)
    hf = _dot_tn(act[:, :nf], w4f_ref[...]) * jnp.exp(-tn_s[:nf] * delta)
    hb = _dot_tn(act[:, nf:], w4b_ref[...]) * jnp.exp(-tn_s[nf:] * delta)
    hb = jnp.where(n_s[nf:] == float(seq), 0.0, hb)
    k_ref[:, :half, :] = hf.reshape(gb, half, ncol).astype(k_ref.dtype)
    k_ref[:, half:, :] = hb.reshape(gb, half, ncol).astype(k_ref.dtype)
    tot = jnp.sum(jnp.abs(hf), axis=0, keepdims=True) + jnp.sum(jnp.abs(hb), axis=0, keepdims=True)

    @pl.when(step == 0)
    def _():
        s_ref[...] = jnp.zeros_like(s_ref)

    s_ref[...] += tot


def hyena_filter_taps(seq, w1, b1, w2, b2, w3, b3, w4, freq, width):
    emb = w1.shape[0]
    hid = w1.shape[1]
    bands = (emb - 1) // 2
    q = 2 * seq // DFT_P
    ncol = HY_ORDER * width
    band = np.linspace(1e-4, bands - 1, bands, dtype=np.float32).reshape(bands, 1)
    min_decay = math.log(HY_DECAY_TARGET) / HY_SLOW_PCT
    max_decay = math.log(HY_DECAY_TARGET) / HY_FAST_PCT
    delta = np.abs(np.linspace(min_decay, max_decay, width, dtype=np.float32)).reshape(1, width)
    w1t = w1.astype(F32).T
    col = lambda v: v.reshape(hid, 1).astype(F32)
    w4r = w4.astype(BF16).reshape(hid, HY_ORDER, 2, width)
    w4f = w4r[:, :, 0, :].reshape(hid, ncol)
    w4b = w4r[:, :, 1, :].reshape(hid, ncol)
    gb = FILTER_GROUP
    const = lambda shape: pl.BlockSpec(shape, lambda i: tuple(0 for _ in shape))
    return pl.pallas_call(
        functools.partial(_filter_kernel, seq=seq),
        out_shape=(jax.ShapeDtypeStruct((DFT_P, q, ncol), BF16),
                   jax.ShapeDtypeStruct((1, ncol), F32)),
        grid=(DFT_P // gb,),
        in_specs=[const((bands, 1)), const((hid, 1)), const((hid, bands)), const((hid, bands)), const((hid, 1)),
                  const((hid, hid)), const((hid, 1)), const((hid, hid)), const((hid, 1)),
                  const((hid, ncol)), const((hid, ncol)), const((hid, 1)), const((1, width))],
        out_specs=(pl.BlockSpec((gb, q, ncol), lambda i: (i, 0, 0)),
                   pl.BlockSpec((1, ncol), lambda i: (0, 0))),
        compiler_params=_cparams("arbitrary"),
        name="hyena_filter",
    )(jnp.asarray(band), w1t[:, 0:1], w1t[:, 1:1 + bands], w1t[:, 1 + bands:1 + 2 * bands], col(b1),
      w2.astype(F32).T, col(b2), w3.astype(F32).T, col(b3), w4f, w4b, col(freq), jnp.asarray(delta))


def _dft_tables(seq):
    p = DFT_P
    n_fft = 2 * seq
    q = n_fft // p
    qh = q // 2
    ka = jnp.arange(q, dtype=I32)
    a = jnp.arange(q, dtype=I32)
    b = jnp.arange(p, dtype=I32)
    nn = a[None, :] * p + b[:, None]
    ph = (ka[None, :, None] * nn[:, None, :]) % n_fft
    ang = ph.astype(F32) * (2.0 * math.pi / n_fft)
    mr, mi = jnp.cos(ang), -jnp.sin(ang)
    m1c = jnp.concatenate([jnp.concatenate([mr[:, :, :qh], -mi[:, :, :qh]], axis=2),
                           jnp.concatenate([mi[:, :, :qh], mr[:, :, :qh]], axis=2)], axis=1)
    m1r = jnp.concatenate([mr, mi], axis=1)
    gr = jnp.swapaxes(mr[:, :, :qh], 1, 2) / n_fft
    gi = -jnp.swapaxes(mi[:, :, :qh], 1, 2) / n_fft
    m4 = jnp.concatenate([jnp.concatenate([gr, -gi], axis=2), jnp.concatenate([gi, gr], axis=2)], axis=1)
    kb = np.arange(p)
    ang2 = 2.0 * np.pi * ((kb[:, None] * kb[None, :]) % p) / p
    fr, fi = np.cos(ang2), -np.sin(ang2)
    m2 = np.block([[fr, -fi], [fi, fr]]).astype(np.float32)
    m3 = np.block([[fr, fi], [-fi, fr]]).astype(np.float32)
    return (m1c.astype(BF16), m1r.astype(BF16), jnp.asarray(m2, BF16), jnp.asarray(m3, BF16), m4.astype(BF16))


def _bmm_kernel(w_ref, x_ref, o_ref, *, shared_w):
    for j in range(x_ref.shape[0]):
        w = w_ref[...] if shared_w else w_ref[j]
        o_ref[j] = _dot(w, x_ref[j]).astype(o_ref.dtype)


def batched_left_matmul(w, x, col_block, ncols, name, gb):
    g, k = x.shape[0], x.shape[1]
    shared = w.ndim == 2
    m = w.shape[-2]
    wspec = (pl.BlockSpec((m, k), lambda i: (0, 0)) if shared
             else pl.BlockSpec((gb, m, k), lambda i: (i, 0, 0)))
    return pl.pallas_call(
        functools.partial(_bmm_kernel, shared_w=shared),
        out_shape=jax.ShapeDtypeStruct((g, m, ncols), BF16),
        grid=(g // gb,),
        in_specs=[wspec, pl.BlockSpec((gb, k, ncols), lambda i: (i, 0, col_block))],
        out_specs=pl.BlockSpec((gb, m, ncols), lambda i: (i, 0, 0)),
        compiler_params=_cparams("arbitrary"),
        name=name,
    )(w, x)


def _dft_mid_kernel(m2_ref, m3_ref, x_ref, k_ref, o_ref):
    half = x_ref.shape[1] // 2
    for j in range(x_ref.shape[0]):
        xf = _dot(m2_ref[...], x_ref[j])
        kf = k_ref[j].astype(F32)
        xr, xi = xf[:half], xf[half:]
        kr, ki = kf[:half], kf[half:]
        z = jnp.concatenate([xr * kr - xi * ki, xr * ki + xi * kr], axis=0).astype(BF16)
        o_ref[j] = _dot(m3_ref[...], z).astype(o_ref.dtype)


def dft_mid(m2, m3, x, kspec, kcol, ncols):
    g, r = x.shape[0], x.shape[1]
    gb = DFT_GROUP
    return pl.pallas_call(
        _dft_mid_kernel,
        out_shape=jax.ShapeDtypeStruct((g, r, ncols), BF16),
        grid=(g // gb,),
        in_specs=[pl.BlockSpec((r, r), lambda i: (0, 0)),
                  pl.BlockSpec((r, r), lambda i: (0, 0)),
                  pl.BlockSpec((gb, r, ncols), lambda i: (i, 0, 0)),
                  pl.BlockSpec((gb, r, ncols), lambda i: (i, 0, kcol))],
        out_specs=pl.BlockSpec((gb, r, ncols), lambda i: (i, 0, 0)),
        compiler_params=_cparams("arbitrary"),
        name="dft_mid",
    )(m2, m3, x, kspec)


def _dft_out_kernel(m4_ref, y_ref, inv_ref, skip_ref, v_ref, mul_ref, o_ref):
    for j in range(y_ref.shape[0]):
        conv = _dot(m4_ref[j], y_ref[j]) * inv_ref[...] + v_ref[j].astype(F32) * skip_ref[...]
        o_ref[j] = (mul_ref[j].astype(F32) * conv).astype(o_ref.dtype)


def dft_out(m4, y, inv_l1, skip, u, v_col, mul, mul_col, ncols):
    g, r = y.shape[0], y.shape[1]
    rows = m4.shape[1]
    gb = DFT_GROUP
    return pl.pallas_call(
        _dft_out_kernel,
        out_shape=jax.ShapeDtypeStruct((g, rows, ncols), BF16),
        grid=(g // gb,),
        in_specs=[pl.BlockSpec((gb, rows, r), lambda i: (i, 0, 0)),
                  pl.BlockSpec((gb, r, ncols), lambda i: (i, 0, 0)),
                  pl.BlockSpec((1, ncols), lambda i: (0, 0)),
                  pl.BlockSpec((1, ncols), lambda i: (0, 0)),
                  pl.BlockSpec((gb, rows, ncols), lambda i: (i, 0, v_col)),
                  pl.BlockSpec((gb, rows, ncols), lambda i: (i, 0, mul_col))],
        out_specs=pl.BlockSpec((gb, rows, ncols), lambda i: (i, 0, 0)),
        compiler_params=_cparams("arbitrary"),
        name="dft_out",
    )(m4, y, inv_l1, skip, u, mul)


def _swap_ab(x):
    g1, r, c = x.shape
    g2 = r // 2
    return x.reshape(g1, 2, g2, c).transpose(2, 1, 0, 3).reshape(g2, 2 * g1, c)


def hyena_branch(u_nat, bsz, seq, width, taps, l1, skip):
    p = DFT_P
    qh = seq // p
    m1c, m1r, m2, m3, m4 = _dft_tables(seq)
    ncol = HY_ORDER * width
    ks1 = batched_left_matmul(m1r, taps, 0, ncol, "dft_k1", FILTER_GROUP)
    kspec = batched_left_matmul(m2, _swap_ab(ks1), 0, ncol, "dft_k2", FILTER_GROUP)
    inv_l1 = 1.0 / l1
    u = u_nat.reshape(bsz, qh, p, 3 * width).transpose(2, 0, 1, 3).reshape(p, bsz * qh, 3 * width)
    z = None
    for order in range(HY_ORDER):
        src, src_col = (u, 0) if order == 0 else (z, 0)
        s1 = batched_left_matmul(m1c, src, src_col, width, "dft_s1", DFT_GROUP)
        mid = dft_mid(m2, m3, _swap_ab(s1), kspec, order, width)
        z = dft_out(m4, _swap_ab(mid), inv_l1[:, order * width:(order + 1) * width],
                    skip[order].reshape(1, width).astype(F32), src, src_col, u, order + 1, width)
    return z.reshape(p, bsz, qh, width).transpose(1, 2, 0, 3).reshape(bsz * seq, width)


def _pack_pairs(x):
    w = x.shape[1] // 2
    u = lax.bitcast_convert_type(x, U32)
    r = (u + U32(0x7FFF) + ((u >> 16) & U32(1))) >> 16
    return r[:, :w] | (r[:, w:] << 16)


def _unpack_pairs(p):
    lo = lax.bitcast_convert_type(p << 16, F32)
    hi = lax.bitcast_convert_type(p & U32(0xFFFF0000), F32)
    return jnp.concatenate([lo, hi], axis=1)


def _merge_kernel(x_ref, er_ref, ec_ref, yhy_ref, yhg_ref, ghy_ref, ghg_ref, why_ref, whg_ref, wo_ref,
                  g1_ref, n2_ref, sh2_ref, sc2_ref, g2_ref, rwh_ref, rwl_ref, sgu_ref, sd_ref,
                  xres_ref, h2p_ref, lg_ref):
    x = x_ref[...]
    rows, gw, d = x.shape
    half = d // 2
    xp = jnp.concatenate([x[:, :, :half] + er_ref[...], x[:, :, half:] + ec_ref[...]], axis=-1)
    xp = xp.reshape(rows * gw, d)
    m = (jax.nn.sigmoid(ghy_ref[...].astype(F32)) * _dot(yhy_ref[...], why_ref[...])
         + jax.nn.sigmoid(ghg_ref[...].astype(F32)) * _dot(yhg_ref[...], whg_ref[...]))
    x1 = xp + g1_ref[...] * _dot(m.astype(BF16), wo_ref[...])
    ms = jnp.mean(x1 * x1, axis=-1, keepdims=True)
    h2 = x1 * lax.rsqrt(ms + NORM_EPS) * n2_ref[...] * (1.0 + sc2_ref[...]) + sh2_ref[...]
    h_hi, h_lo = _split_bf16(h2)
    lg_ref[...] = _dot_nt(rwh_ref[...], h_hi) + (_dot_nt(rwl_ref[...], h_hi) + _dot_nt(rwh_ref[...], h_lo))
    gu = _dot(h_hi, sgu_ref[...])
    fs = gu.shape[1] // 2
    shared = _dot((_silu(gu[:, :fs]) * gu[:, fs:]).astype(BF16), sd_ref[...])
    xres_ref[...] = x1 + g2_ref[...] * shared
    h2p_ref[...] = _pack_pairs(h2)


def merge_stage(x, emb_r, emb_c, y_hy, y_hg, p, gate_cols, w_hy_out, w_hg_out, w_out, g1, norm2_g,
                sh2, sc2, g2, router_wt, sh_gate_up, sh_down, tm):
    b, s, d = x.shape
    rows_per_batch = s // GRID_W
    rt = tm // GRID_W
    tiles_per_batch = rows_per_batch // rt
    x3 = x.reshape(b * rows_per_batch, GRID_W, d)
    wb = y_hy.shape[1]
    ne = router_wt.shape[0]
    fs2 = sh_gate_up.shape[1]
    rw_hi, rw_lo = _split_bf16(router_wt)
    tok = lambda cb, w: pl.BlockSpec((tm, w), lambda i: (i, cb))
    const = lambda shape: pl.BlockSpec(shape, lambda i: tuple(0 for _ in shape))
    per_b = pl.BlockSpec((None, 1, d), lambda i: (i // tiles_per_batch, 0, 0))
    return pl.pallas_call(
        _merge_kernel,
        out_shape=(jax.ShapeDtypeStruct((b * s, d), F32),
                   jax.ShapeDtypeStruct((b * s, d // 2), U32),
                   jax.ShapeDtypeStruct((ne, b * s), F32)),
        grid=(b * tiles_per_batch,),
        in_specs=[pl.BlockSpec((rt, GRID_W, d), lambda i: (i, 0, 0)),
                  pl.BlockSpec((rt, 1, d // 2), lambda i: (i % tiles_per_batch, 0, 0)),
                  const((GRID_W, d // 2)),
                  tok(0, wb), tok(0, wb), tok(gate_cols[0], d), tok(gate_cols[1], d),
                  const((wb, d)), const((wb, d)), const((d, d)),
                  per_b, const((1, d)), per_b, per_b, per_b,
                  const((ne, d)), const((ne, d)), const((d, fs2)), const((fs2 // 2, d))],
        out_specs=(pl.BlockSpec((tm, d), lambda i: (i, 0)),
                   pl.BlockSpec((tm, d // 2), lambda i: (i, 0)),
                   pl.BlockSpec((ne, tm), lambda i: (0, i))),
        compiler_params=_cparams("arbitrary"),
        name="merge",
    )(x3, emb_r, emb_c, y_hy, y_hg, p, p, w_hy_out, w_hg_out, w_out, g1, norm2_g.reshape(1, d),
      sh2, sc2, g2, rw_hi, rw_lo, sh_gate_up, sh_down)


def _route_kernel(lg_ref, bias_ref, eidx_ref, wsel_ref, rank_ref, cnt_ref, carry):
    ne, tr = lg_ref.shape
    gsz = ne // N_GROUPS
    neg = -jnp.inf

    @pl.when(pl.program_id(0) == 0)
    def _():
        carry[...] = jnp.zeros_like(carry)

    scores = jax.nn.sigmoid(lg_ref[...])
    biased = scores + bias_ref[...]
    riota = lax.broadcasted_iota(I32, (gsz, tr), 0).astype(F32)
    gs = []
    for g in range(N_GROUPS):
        vg = biased[g * gsz:(g + 1) * gsz]
        m1 = jnp.max(vg, axis=0, keepdims=True)
        i1 = jnp.min(jnp.where(vg == m1, riota, float(gsz)), axis=0, keepdims=True)
        m2 = jnp.max(jnp.where(riota == i1, neg, vg), axis=0, keepdims=True)
        gs.append(m1 + m2)
    cur = jnp.concatenate(gs, axis=0)
    giota = lax.broadcasted_iota(I32, (N_GROUPS, tr), 0).astype(F32)
    gsel = jnp.zeros((N_GROUPS, tr), F32)
    for _ in range(TOPK_GROUPS):
        m = jnp.max(cur, axis=0, keepdims=True)
        idx = jnp.min(jnp.where(cur == m, giota, float(N_GROUPS)), axis=0, keepdims=True)
        hit = giota == idx
        gsel = jnp.where(hit, 1.0, gsel)
        cur = jnp.where(hit, neg, cur)
    cur = jnp.concatenate([jnp.where(gsel[g:g + 1] > 0.0, biased[g * gsz:(g + 1) * gsz], neg)
                           for g in range(N_GROUPS)], axis=0)
    eiota = lax.broadcasted_iota(I32, (ne, tr), 0).astype(F32)
    chosen = jnp.zeros((ne, tr), F32)
    idxs, ws = [], []
    for _ in range(TOP_K):
        m = jnp.max(cur, axis=0, keepdims=True)
        idx = jnp.min(jnp.where(cur == m, eiota, float(ne)), axis=0, keepdims=True)
        hit = eiota == idx
        idxs.append(idx)
        ws.append(jnp.sum(jnp.where(hit, scores, 0.0), axis=0, keepdims=True))
        chosen = jnp.where(hit, 1.0, chosen)
        cur = jnp.where(hit, neg, cur)
    w = jnp.concatenate(ws, axis=0)
    wsel_ref[...] = w / jnp.sum(w, axis=0, keepdims=True) * ROUTED_SCALE
    eidx_ref[...] = jnp.concatenate(idxs, axis=0).astype(I32)
    srow = lax.broadcasted_iota(I32, (tr, tr), 0)
    scol = lax.broadcasted_iota(I32, (tr, tr), 1)
    before = (srow < scol).astype(BF16)
    base = carry[...] + _dot(chosen.astype(BF16), before)
    ranks = [jnp.sum(jnp.where(eiota == idx, base, 0.0), axis=0, keepdims=True) for idx in idxs]
    rank_ref[...] = jnp.concatenate(ranks, axis=0).astype(I32)
    carry[...] += jnp.sum(chosen, axis=1, keepdims=True)
    cnt_ref[...] = carry[...]


def route(logits_t, router_bias, tr):
    ne, t = logits_t.shape
    return pl.pallas_call(
        _route_kernel,
        out_shape=(jax.ShapeDtypeStruct((TOP_K, t), I32),
                   jax.ShapeDtypeStruct((TOP_K, t), F32),
                   jax.ShapeDtypeStruct((TOP_K, t), I32),
                   jax.ShapeDtypeStruct((ne, 1), F32)),
        grid=(t // tr,),
        in_specs=[pl.BlockSpec((ne, tr), lambda i: (0, i)),
                  pl.BlockSpec((ne, 1), lambda i: (0, 0))],
        out_specs=(pl.BlockSpec((TOP_K, tr), lambda i: (0, i)),
                   pl.BlockSpec((TOP_K, tr), lambda i: (0, i)),
                   pl.BlockSpec((TOP_K, tr), lambda i: (0, i)),
                   pl.BlockSpec((ne, 1), lambda i: (0, 0))),
        scratch_shapes=[pltpu.VMEM((ne, 1), F32)],
        compiler_params=_cparams("arbitrary"),
        name="route",
    )(logits_t, router_bias.reshape(ne, 1).astype(F32))


def _dest_kernel(cnt_ref, eidx_ref, rank_ref, dest_ref, be_ref, nv_ref, nb_ref, start_scr):
    ne = cnt_ref.shape[0]
    tr = eidx_ref.shape[1]

    @pl.when(pl.program_id(0) == 0)
    def _():
        cnt = jnp.broadcast_to(cnt_ref[...], (ne, LANES))
        padded = jnp.floor((cnt + float(MOE_ROWS - 1)) / float(MOE_ROWS)) * float(MOE_ROWS)
        r = lax.broadcasted_iota(I32, (ne, ne), 0)
        c = lax.broadcasted_iota(I32, (ne, ne), 1)
        start = _dot_hi((c < r).astype(F32), padded)
        start_scr[...] = start
        end = start[:, 0:1] + padded[:, 0:1]
        used = start[:, 0:1] + cnt[:, 0:1]
        nbl = be_ref.shape[1]
        blk_row = (lax.broadcasted_iota(I32, (1, nbl), 1) * MOE_ROWS).astype(F32)
        total = jnp.max(end, axis=0, keepdims=True)
        last_row = total - float(MOE_ROWS)
        blk_row_c = jnp.minimum(blk_row, last_row)
        e_of = jnp.sum((end <= blk_row_c).astype(F32), axis=0, keepdims=True)
        e_of = jnp.minimum(e_of, float(ne - 1))
        eio = lax.broadcasted_iota(I32, (ne, nbl), 0).astype(F32)
        used_e = jnp.sum(jnp.where(eio == e_of, used, 0.0), axis=0, keepdims=True)
        valid = jnp.clip(used_e - blk_row_c, 0.0, float(MOE_ROWS))
        be_ref[...] = e_of.astype(I32)
        nv_ref[...] = jnp.where(blk_row <= last_row, valid, 0.0).astype(I32)
        nb_ref[...] = jnp.broadcast_to(total / float(MOE_ROWS), nb_ref.shape).astype(I32)

    eiota = lax.broadcasted_iota(I32, (ne, tr), 0)
    start_col = start_scr[:, 0:1]
    rows = []
    for k in range(TOP_K):
        hit = eiota == eidx_ref[k:k + 1, :]
        rows.append(jnp.sum(jnp.where(hit, start_col, 0.0), axis=0, keepdims=True))
    dest_ref[...] = jnp.concatenate(rows, axis=0).astype(I32) + rank_ref[...]


def dispatch_plan(counts, eidx, rank, tr, n_blocks):
    ne = counts.shape[0]
    t = eidx.shape[1]
    nbl = pl.cdiv(n_blocks, LANES) * LANES
    return pl.pallas_call(
        _dest_kernel,
        out_shape=(jax.ShapeDtypeStruct((TOP_K, t), I32),
                   jax.ShapeDtypeStruct((1, nbl), I32),
                   jax.ShapeDtypeStruct((1, nbl), I32),
                   jax.ShapeDtypeStruct((1, LANES), I32)),
        grid=(t // tr,),
        in_specs=[pl.BlockSpec((ne, 1), lambda i: (0, 0)),
                  pl.BlockSpec((TOP_K, tr), lambda i: (0, i)),
                  pl.BlockSpec((TOP_K, tr), lambda i: (0, i))],
        out_specs=(pl.BlockSpec((TOP_K, tr), lambda i: (0, i)),
                   pl.BlockSpec((1, nbl), lambda i: (0, 0)),
                   pl.BlockSpec((1, nbl), lambda i: (0, 0)),
                   pl.BlockSpec((1, LANES), lambda i: (0, 0))),
        scratch_shapes=[pltpu.VMEM((ne, LANES), F32)],
        compiler_params=_cparams("arbitrary"),
        name="dispatch_plan",
    )(counts, eidx, rank)


def _sc_workers():
    info = plsc.get_sparse_core_info()
    return info.num_cores, info.num_cores * info.num_subcores


def scatter_rows(dest_flat, h2p, n_rows):
    t, w = h2p.shape
    n_cores, n_workers = _sc_workers()
    per_worker = t // n_workers
    mesh = plsc.VectorSubcoreMesh(core_axis_name="c", subcore_axis_name="s")

    @functools.partial(
        pl.kernel, mesh=mesh, out_type=jax.ShapeDtypeStruct((n_rows, w), U32),
        scratch_types=[pltpu.VMEM((TOP_K, SC_ROWS), I32), pltpu.VMEM((SC_ROWS, w), U32), pltpu.SemaphoreType.DMA])
    def body(h_hbm, dest_hbm, xs_hbm, idx_v, rows_v, sem):
        base = (lax.axis_index("s") * n_cores + lax.axis_index("c")) * per_worker

        @pl.loop(0, per_worker // SC_ROWS)
        def _(ci):
            off = pl.multiple_of(base + ci * SC_ROWS, SC_ROWS)
            pltpu.sync_copy(h_hbm.at[pl.ds(off, SC_ROWS)], rows_v)
            for k in range(TOP_K):
                pltpu.sync_copy(dest_hbm.at[pl.ds(k * t + off, SC_ROWS)], idx_v.at[k])
            copies = [pltpu.async_copy(rows_v, xs_hbm.at[idx_v.at[k]], sem) for k in range(TOP_K)]
            for c in copies:
                c.wait()

    return body(h2p, dest_flat)


def gather_rows(idx_flat, table):
    n = idx_flat.shape[0]
    w = table.shape[1]
    n_cores, n_workers = _sc_workers()
    per_worker = n // n_workers
    mesh = plsc.VectorSubcoreMesh(core_axis_name="c", subcore_axis_name="s")

    @functools.partial(
        pl.kernel, mesh=mesh, out_type=jax.ShapeDtypeStruct((n, w), table.dtype),
        scratch_types=[pltpu.VMEM((SC_ROWS,), I32), pltpu.VMEM((SC_ROWS, w), table.dtype), pltpu.SemaphoreType.DMA])
    def body(table_hbm, idx_hbm, out_hbm, idx_v, rows_v, sem):
        base = (lax.axis_index("s") * n_cores + lax.axis_index("c")) * per_worker

        @pl.loop(0, per_worker // SC_ROWS)
        def _(ci):
            off = pl.multiple_of(base + ci * SC_ROWS, SC_ROWS)
            pltpu.sync_copy(idx_hbm.at[pl.ds(off, SC_ROWS)], idx_v)
            pltpu.async_copy(table_hbm.at[idx_v], rows_v, sem).wait()
            pltpu.sync_copy(rows_v, out_hbm.at[pl.ds(off, SC_ROWS)])

    return body(table, idx_flat)


def _gmm_kernel(be_ref, nv_ref, nb_ref, first_ref, slot_ref, nxt_ref, xs_ref, wg_hbm, wu_hbm, wd_hbm, y_ref,
                wg_buf, wu_buf, wd_buf, sem):
    def weight_copies(e, s):
        return (pltpu.make_async_copy(wg_hbm.at[e], wg_buf.at[s], sem.at[s]),
                pltpu.make_async_copy(wu_hbm.at[e], wu_buf.at[s], sem.at[s]),
                pltpu.make_async_copy(wd_hbm.at[e], wd_buf.at[s], sem.at[s]))

    def one_block(j, rows):
        s = slot_ref[j]

        @pl.when(j == 0)
        def _():
            for c in weight_copies(be_ref[0], 0):
                c.start()

        @pl.when(first_ref[j] == 1)
        def _():
            for c in weight_copies(be_ref[j], s):
                c.wait()

            @pl.when(nxt_ref[j] >= 0)
            def _():
                for c in weight_copies(nxt_ref[j], 1 - s):
                    c.start()

        x = _unpack_pairs(xs_ref[rows, :])
        row = lax.broadcasted_iota(I32, (x.shape[0], 1), 0)
        x = jnp.where(row < nv_ref[j], x, 0.0)
        hmid = _silu(_dot(x, wg_buf[s])) * _dot(x, wu_buf[s])
        y_ref[rows, :] = _pack_pairs(_dot(hmid, wd_buf[s]))

    for sb in range(MOE_STEP_BLOCKS):
        j = pl.program_id(0) * MOE_STEP_BLOCKS + sb
        pl.when(j < nb_ref[0])(functools.partial(one_block, j, slice(sb * MOE_ROWS, (sb + 1) * MOE_ROWS)))


def grouped_mlp(block_e, block_valid, n_used, xs, w_gate, w_up, w_down, n_blocks):
    ne, d, f = w_gate.shape
    w = xs.shape[1]
    jj = jnp.arange(block_e.shape[0], dtype=I32)
    active = jj < n_used[0]
    first = (active & ((jj == 0) | (block_e != jnp.roll(block_e, 1)))).astype(I32)
    run = jnp.cumsum(first) - 1
    slot = (run % 2).astype(I32)
    nbl = block_e.shape[0]
    run_expert = jnp.full((nbl + 1,), -1, I32).at[jnp.where(first == 1, run, nbl)].set(
        jnp.where(first == 1, block_e, -1))
    nxt = run_expert[jnp.minimum(run + 1, nbl)]
    step_rows = MOE_STEP_BLOCKS * MOE_ROWS
    last = lambda g, nb: jnp.minimum(g, (nb[0] - 1) // MOE_STEP_BLOCKS)
    grid_spec = pltpu.PrefetchScalarGridSpec(
        num_scalar_prefetch=6,
        grid=(n_blocks // MOE_STEP_BLOCKS,),
        in_specs=[pl.BlockSpec((step_rows, w), lambda g, be, nv, nb, fi, sl, nx: (last(g, nb), 0)),
                  pl.BlockSpec(memory_space=pl.ANY),
                  pl.BlockSpec(memory_space=pl.ANY),
                  pl.BlockSpec(memory_space=pl.ANY)],
        out_specs=pl.BlockSpec((step_rows, w), lambda g, be, nv, nb, fi, sl, nx: (last(g, nb), 0)),
        scratch_shapes=[pltpu.VMEM((2, d, f), F32), pltpu.VMEM((2, d, f), F32), pltpu.VMEM((2, f, d), F32),
                        pltpu.SemaphoreType.DMA((2,))],
    )
    return pl.pallas_call(
        _gmm_kernel,
        out_shape=jax.ShapeDtypeStruct(xs.shape, U32),
        grid_spec=grid_spec,
        compiler_params=_cparams("arbitrary"),
        name="grouped_mlp",
    )(block_e, block_valid, n_used, first, slot, nxt, xs, w_gate, w_up, w_down)


def _combine_kernel(y_ref, xres_ref, wt_ref, g2_ref, fg_ref, o_ref):
    wt = wt_ref[...]
    routed = jnp.zeros(xres_ref.shape, F32)
    for k in range(TOP_K):
        routed = routed + wt[:, k:k + 1] * _unpack_pairs(y_ref[k])
    x2 = xres_ref[...] + g2_ref[...] * routed
    ms = jnp.mean(x2 * x2, axis=-1, keepdims=True)
    o_ref[...] = x2 * lax.rsqrt(ms + NORM_EPS) * fg_ref[...]


def combine(y_tok, xres, wsel_t, g2, final_g, seq, tm):
    t, d = xres.shape
    tiles_per_batch = seq // tm
    return pl.pallas_call(
        _combine_kernel,
        out_shape=jax.ShapeDtypeStruct((t, d), F32),
        grid=(t // tm,),
        in_specs=[pl.BlockSpec((TOP_K, tm, d // 2), lambda i: (0, i, 0)),
                  pl.BlockSpec((tm, d), lambda i: (i, 0)),
                  pl.BlockSpec((tm, TOP_K), lambda i: (i, 0)),
                  pl.BlockSpec((None, 1, d), lambda i: (i // tiles_per_batch, 0, 0)),
                  pl.BlockSpec((1, d), lambda i: (0, 0))],
        out_specs=pl.BlockSpec((tm, d), lambda i: (i, 0)),
        compiler_params=_cparams("arbitrary"),
        name="combine",
    )(y_tok, xres, wsel_t, g2, final_g.reshape(1, d))


def _pos_tables(rows, cols, dim):
    quarter = dim // 4
    omega = 1.0 / (POS_BASE ** (np.arange(quarter, dtype=np.float32) / quarter))
    ang_r = np.arange(rows, dtype=np.float32)[:, None] * omega
    ang_c = np.arange(cols, dtype=np.float32)[:, None] * omega
    emb_r = np.concatenate([np.sin(ang_r), np.cos(ang_r)], axis=-1).astype(np.float32)
    emb_c = np.concatenate([np.sin(ang_c), np.cos(ang_c)], axis=-1).astype(np.float32)
    return jnp.asarray(emb_r.reshape(rows, 1, dim // 2)), jnp.asarray(emb_c)


def kernel(x, c, ctx, c_ctx, norm1_g, norm2_g, ada_w, ada_b, w_in, hy_conv_w, hy_conv_b, hy_f_w1, hy_f_b1, hy_f_w2, hy_f_b2, hy_f_w3, hy_f_b3, hy_f_w4, hy_f_freq, hy_skip, hg_lb_logits, hg_norm_g, w_hy_out, w_hg_out, w_out, router_w, router_bias, exp_w_gate, exp_w_up, exp_w_down, sh_w_gate, sh_w_up, sh_w_down, final_g):
    bsz, seq, d = x.shape
    n_ctx = ctx.shape[1]
    hy_w = w_hy_out.shape[1]
    hg_w = w_hg_out.shape[1]
    dk = hg_norm_g.shape[1]
    n_heads = hg_w // dk
    ne = router_w.shape[2]
    l = 0

    c_rows = jnp.zeros((SUBLANES, d), F32).at[:bsz].set(c).at[bsz].set(c_ctx)
    mods = ada_vectors(c_rows, ada_w[l], ada_b[l])
    sh1, sc1, g1, sh2, sc2, g2 = [mods[:bsz, j * d:(j + 1) * d].reshape(bsz, 1, d) for j in range(N_ADA)]
    csh1 = jnp.broadcast_to(mods[bsz, 0:d].reshape(1, 1, d), (bsz, 1, d))
    csc1 = jnp.broadcast_to(mods[bsz, d:2 * d].reshape(1, 1, d), (bsz, 1, d))

    emb_r, emb_c = _pos_tables(seq // GRID_W, GRID_W, d)
    w_in_b = w_in[l].astype(BF16)
    hy_proj = 3 * hy_w
    p = in_projection(x, emb_r, emb_c, norm1_g[l], sh1, sc1, w_in_b, TOKEN_TILE)
    hg_cols = slice(hy_proj, hy_proj + 5 * hg_w)
    zero_r = jnp.zeros((n_ctx // GRID_W, 1, d // 2), F32)
    zero_c = jnp.zeros((GRID_W, d // 2), F32)
    pc = in_projection(ctx, zero_r, zero_c, norm1_g[l], csh1, csc1, w_in_b[:, hg_cols], n_ctx)

    lbs = jnp.cumsum(jax.nn.softmax(hg_lb_logits.astype(F32), axis=0), axis=0)
    lb_f, lb_b = lbs[l, 0], lbs[l, 1]
    zero_state = jnp.zeros((bsz, n_heads, dk, dk), F32)
    base = hy_proj // hg_w
    _, st_f = hgrn_scan(pc, (0, 1, 2), lb_f, zero_state, n_ctx, n_ctx, reverse=False)
    _, st_b = hgrn_scan(pc, (0, 1, 3), lb_b, zero_state, n_ctx, n_ctx, reverse=True)
    o_f, _ = hgrn_scan(p, (base, base + 1, base + 2), lb_f, st_f, seq, HG_TIME_BLOCK, reverse=False)
    y_hg, _ = hgrn_scan(p, (base, base + 1, base + 3), lb_b, st_b, seq, HG_TIME_BLOCK, reverse=True,
                        o_fwd=o_f, gate_col=base + 4, norm_g=hg_norm_g[l])

    u = short_conv(p, hy_proj, hy_conv_w[l], hy_conv_b[l], seq, TOKEN_TILE)
    taps, l1 = hyena_filter_taps(seq, hy_f_w1[l], hy_f_b1[l], hy_f_w2[l], hy_f_b2[l], hy_f_w3[l], hy_f_b3[l],
                                 hy_f_w4[l], hy_f_freq[l], hy_w)
    y_hy = hyena_branch(u, bsz, seq, hy_w, taps, l1, hy_skip[l])

    gate_base = (hy_proj + 5 * hg_w) // d
    sh_gu = jnp.concatenate([sh_w_gate[l], sh_w_up[l]], axis=1).astype(BF16)
    xres, h2p, logits_t = merge_stage(
        x, emb_r, emb_c, y_hy, y_hg, p, (gate_base, gate_base + 1),
        w_hy_out[l].astype(BF16), w_hg_out[l].astype(BF16), w_out[l].astype(BF16), g1, norm2_g[l],
        sh2, sc2, g2, router_w[l].T.astype(F32), sh_gu, sh_w_down[l].astype(BF16), TOKEN_TILE)

    t = bsz * seq
    eidx, wsel, rank, counts = route(logits_t, router_bias[l], TOKEN_TILE)
    n_rows = t * TOP_K + ne * (MOE_ROWS - 1)
    n_blocks = pl.cdiv(pl.cdiv(n_rows, MOE_ROWS), MOE_STEP_BLOCKS) * MOE_STEP_BLOCKS
    dest, block_e, block_valid, n_used = dispatch_plan(counts, eidx, rank, TOKEN_TILE, n_blocks)

    dest_flat = dest.reshape(-1)
    xs = scatter_rows(dest_flat, h2p, n_blocks * MOE_ROWS)
    ys = grouped_mlp(block_e.reshape(-1), block_valid.reshape(-1), n_used.reshape(-1)[:1], xs,
                     exp_w_gate[l], exp_w_up[l], exp_w_down[l], n_blocks)
    y_tok = gather_rows(dest_flat, ys).reshape(TOP_K, t, d // 2)
    out = combine(y_tok, xres, wsel.T, g2, final_g, seq, TOKEN_TILE)
    return out.reshape(bsz, seq, d)
```

```python
import functools
import math

import numpy as np
import jax
import jax.numpy as jnp
from jax import lax
from jax.experimental import pallas as pl
from jax.experimental.pallas import tpu as pltpu
from jax.experimental.pallas import tpu_sc as plsc

F32 = jnp.float32
BF16 = jnp.bfloat16
U32 = jnp.uint32
I32 = jnp.int32
HIGHEST = lax.Precision.HIGHEST

GRID_W = 64
POS_BASE = 10000.0
NORM_EPS = 1e-6
N_ADA = 6
HY_ORDER = 2
HY_SHORT = 3
HY_DECAY_TARGET = 1e-2
HY_FAST_PCT = 0.3
HY_SLOW_PCT = 1.5
HG_HEADS = 4
HG_CHUNK = 64
N_GROUPS = 8
TOPK_GROUPS = 4
TOP_K = 8
ROUTED_SCALE = 2.5

LANES = 128
SUBLANES = 8
VMEM_LIMIT = 56 * 1024 * 1024

TOKEN_TILE = 512
HG_TIME_BLOCK = 512
HALO_ROWS = 16
CONV_TILE = 2048
DFT_P = 128
FILTER_GROUP = 8
DFT_GROUP = 16
MOE_ROWS = 256
MOE_STEP_BLOCKS = 4
SC_ROWS = 128


def _cparams(*sem):
    return pltpu.CompilerParams(dimension_semantics=sem, vmem_limit_bytes=VMEM_LIMIT)


def _dot(a, b):
    return jnp.dot(a, b, preferred_element_type=F32)


def _dot_hi(a, b):
    return jnp.dot(a, b, preferred_element_type=F32, precision=HIGHEST)


def _dot_nt(a, b):
    return lax.dot_general(a, b, (((1,), (1,)), ((), ())), preferred_element_type=F32)


def _dot_tn(a, b):
    return lax.dot_general(a, b, (((0,), (0,)), ((), ())), preferred_element_type=F32)


def _silu(x):
    return x * jax.nn.sigmoid(x)


def _split_bf16(x):
    hi = x.astype(BF16)
    return hi, (x - hi.astype(F32)).astype(BF16)


def _ada_kernel(c_ref, w_ref, b_ref, o_ref):
    o_ref[...] = _dot_hi(_silu(c_ref[...]), w_ref[...]) + b_ref[...]


def ada_vectors(c_rows, ada_w, ada_b):
    r, d = c_rows.shape
    n = ada_w.shape[1]
    bn = 1024
    return pl.pallas_call(
        _ada_kernel,
        out_shape=jax.ShapeDtypeStruct((r, n), F32),
        grid=(n // bn,),
        in_specs=[pl.BlockSpec((r, d), lambda j: (0, 0)),
                  pl.BlockSpec((d, bn), lambda j: (0, j)),
                  pl.BlockSpec((1, bn), lambda j: (0, j))],
        out_specs=pl.BlockSpec((r, bn), lambda j: (0, j)),
        compiler_params=_cparams("arbitrary"),
        name="ada_vectors",
    )(c_rows, ada_w, ada_b.reshape(1, n))


def _inproj_kernel(x_ref, er_ref, ec_ref, g_ref, sh_ref, sc_ref, w_ref, o_ref, *, col_chunk):
    x = x_ref[...]
    rows, gw, d = x.shape
    half = d // 2
    xp = jnp.concatenate([x[:, :, :half] + er_ref[...], x[:, :, half:] + ec_ref[...]], axis=-1)
    xp = xp.reshape(rows * gw, d)
    ms = jnp.mean(xp * xp, axis=-1, keepdims=True)
    y = xp * lax.rsqrt(ms + NORM_EPS) * g_ref[...]
    h = (y * (1.0 + sc_ref[...]) + sh_ref[...]).astype(BF16)
    n = o_ref.shape[1]
    for j in range(n // col_chunk):
        sl = slice(j * col_chunk, (j + 1) * col_chunk)
        o_ref[:, sl] = _dot(h, w_ref[:, sl]).astype(o_ref.dtype)


def in_projection(x, emb_r, emb_c, norm_g, shift, scale, w_bf16, tm):
    b, s, d = x.shape
    n = w_bf16.shape[1]
    rows_per_batch = s // GRID_W
    rt = tm // GRID_W
    tiles_per_batch = rows_per_batch // rt
    x3 = x.reshape(b * rows_per_batch, GRID_W, d)
    col_chunk = 512
    return pl.pallas_call(
        functools.partial(_inproj_kernel, col_chunk=col_chunk),
        out_shape=jax.ShapeDtypeStruct((b * s, n), BF16),
        grid=(b * tiles_per_batch,),
        in_specs=[pl.BlockSpec((rt, GRID_W, d), lambda i: (i, 0, 0)),
                  pl.BlockSpec((rt, 1, d // 2), lambda i: (i % tiles_per_batch, 0, 0)),
                  pl.BlockSpec((GRID_W, d // 2), lambda i: (0, 0)),
                  pl.BlockSpec((1, d), lambda i: (0, 0)),
                  pl.BlockSpec((None, 1, d), lambda i: (i // tiles_per_batch, 0, 0)),
                  pl.BlockSpec((None, 1, d), lambda i: (i // tiles_per_batch, 0, 0)),
                  pl.BlockSpec((d, n), lambda i: (0, 0))],
        out_specs=pl.BlockSpec((tm, n), lambda i: (i, 0)),
        compiler_params=_cparams("arbitrary"),
        name="in_projection",
    )(x3, emb_r, emb_c, norm_g.reshape(1, d), shift, scale, w_bf16)


def _hgrn_kernel(*refs, reverse, n_chunks, final):
    if final:
        (q_ref, i_ref, f_ref, lb_ref, s0_ref, of_ref, gate_ref, ng_ref, o_ref, sfin_ref, s_scr) = refs
    else:
        (q_ref, i_ref, f_ref, lb_ref, s0_ref, o_ref, sfin_ref, s_scr) = refs
    cs = HG_CHUNK
    bsz, n_heads, _, dk = s_scr.shape

    @pl.when(pl.program_id(0) == 0)
    def _():
        s_scr[...] = s0_ref[...]

    row = lax.broadcasted_iota(I32, (cs, cs), 0)
    col = lax.broadcasted_iota(I32, (cs, cs), 1)
    tri = (col >= row) if reverse else (col <= row)
    tri_b = tri.astype(BF16)
    end_row = 0 if reverse else cs - 1
    mid_row = cs // 2 if reverse else cs // 2 - 1

    def chunk_body(bi, ci):
        c = (n_chunks - 1 - ci) if reverse else ci
        rows = slice(c * cs, (c + 1) * cs)
        lb = lb_ref[...]
        f = lb + (1.0 - lb) * jax.nn.sigmoid(f_ref[bi, rows, :].astype(F32))
        lf_hi, lf_lo = _split_bf16(jnp.log(f))
        b_all = _dot(tri_b, lf_hi) + _dot(tri_b, lf_lo)
        k_all = 1.0 - f
        q_all = _silu(q_ref[bi, rows, :].astype(F32))
        for h in range(n_heads):
            sl = slice(h * dk, (h + 1) * dk)
            b = b_all[:, sl]
            q = q_all[:, sl]
            k = k_all[:, sl]
            v = i_ref[bi, rows, sl]
            b_end = b[end_row:end_row + 1]
            b_mid = b[mid_row:mid_row + 1]
            qd = (q * jnp.exp(b - b_mid)).astype(BF16)
            kd = (k * jnp.exp(b_mid - b)).astype(BF16)
            att = jnp.where(tri, _dot_nt(qd, kd), 0.0).astype(BF16)
            st = s_scr[bi, h]
            qe = (q * jnp.exp(b)).astype(BF16)
            o = _dot(att, v) + _dot_nt(qe, st.astype(BF16))
            ke = (k * jnp.exp(b_end - b)).astype(BF16)
            s_scr[bi, h] = st * jnp.exp(b_end) + _dot_tn(v, ke)
            if final:
                o = o + of_ref[bi, rows, sl].astype(F32)
                o = o * lax.rsqrt(jnp.mean(o * o, axis=-1, keepdims=True) + NORM_EPS) * ng_ref[...]
                o = o * _silu(gate_ref[bi, rows, sl].astype(F32))
            o_ref[bi, rows, sl] = o.astype(o_ref.dtype)

    for ci in range(n_chunks):
        for bi in range(bsz):
            chunk_body(bi, ci)
    sfin_ref[...] = s_scr[...]


def hgrn_scan(p, cols, lb, s0, seq, tb, *, reverse, o_fwd=None, gate_col=None, norm_g=None):
    bsz, n_heads, dv, dk = s0.shape
    width = n_heads * dk
    nt = seq // tb
    final = o_fwd is not None
    p3 = p.reshape(bsz, seq, p.shape[1])
    tmap = (lambda t: nt - 1 - t) if reverse else (lambda t: t)
    colspec = lambda cb: pl.BlockSpec((bsz, tb, width), lambda t: (0, tmap(t), cb))
    state = pl.BlockSpec((bsz, n_heads, dv, dk), lambda t: (0, 0, 0, 0))
    in_specs = [colspec(cols[0]), colspec(cols[1]), colspec(cols[2]),
                pl.BlockSpec((1, width), lambda t: (0, 0)), state]
    args = [p3, p3, p3, lb.reshape(1, width), s0]
    if final:
        in_specs += [colspec(0), colspec(gate_col), pl.BlockSpec((1, dk), lambda t: (0, 0))]
        args += [o_fwd.reshape(bsz, seq, width), p3, norm_g.reshape(1, dk)]
    o, s_fin = pl.pallas_call(
        functools.partial(_hgrn_kernel, reverse=reverse, n_chunks=tb // HG_CHUNK, final=final),
        out_shape=(jax.ShapeDtypeStruct((bsz, seq, width), BF16),
                   jax.ShapeDtypeStruct((bsz, n_heads, dv, dk), F32)),
        grid=(nt,),
        in_specs=in_specs,
        out_specs=(colspec(0), state),
        scratch_shapes=[pltpu.VMEM((bsz, n_heads, dv, dk), F32)],
        compiler_params=_cparams("arbitrary"),
        name="hgrn_bwd" if reverse else "hgrn_fwd",
    )(*args)
    return o.reshape(bsz * seq, width), s_fin


def _shortconv_kernel(p_ref, prev_ref, next_ref, w_ref, b_ref, o_ref, *, tiles_per_batch):
    i = pl.program_id(0)
    ti = i % tiles_per_batch
    p = p_ref[...].astype(F32)
    tm = p.shape[0]
    row = lax.broadcasted_iota(I32, (tm, 1), 0)
    prev_row = jnp.where(ti == 0, 0.0, prev_ref[HALO_ROWS - 1:HALO_ROWS, :].astype(F32))
    next_row = jnp.where(ti == tiles_per_batch - 1, 0.0, next_ref[0:1, :].astype(F32))
    p_prev = jnp.where(row == 0, prev_row, pltpu.roll(p, 1, axis=0))
    p_next = jnp.where(row == tm - 1, next_row, pltpu.roll(p, tm - 1, axis=0))
    u = w_ref[0:1, :] * p_prev + w_ref[1:2, :] * p + w_ref[2:3, :] * p_next + b_ref[...]
    o_ref[...] = jnp.swapaxes(u.reshape(tm // DFT_P, DFT_P, u.shape[1]), 0, 1).astype(o_ref.dtype)


def short_conv(p, width, conv_w, conv_b, seq, tm, cw):
    t = p.shape[0]
    nt = t // tm
    tiles_per_batch = seq // tm
    sub = tm // HALO_ROWS
    ta = tm // DFT_P
    return pl.pallas_call(
        functools.partial(_shortconv_kernel, tiles_per_batch=tiles_per_batch),
        out_shape=jax.ShapeDtypeStruct((DFT_P, t // DFT_P, width), BF16),
        grid=(nt, width // cw),
        in_specs=[pl.BlockSpec((tm, cw), lambda i, j: (i, j)),
                  pl.BlockSpec((HALO_ROWS, cw), lambda i, j: (jnp.maximum(i * sub - 1, 0), j)),
                  pl.BlockSpec((HALO_ROWS, cw), lambda i, j: (jnp.minimum((i + 1) * sub, t // HALO_ROWS - 1), j)),
                  pl.BlockSpec((HY_SHORT, cw), lambda i, j: (0, j)),
                  pl.BlockSpec((1, cw), lambda i, j: (0, j))],
        out_specs=pl.BlockSpec((DFT_P, ta, cw), lambda i, j: (0, i, j)),
        compiler_params=_cparams("arbitrary", "arbitrary"),
        name="short_conv",
    )(p, p, p, conv_w, conv_b.reshape(1, width))


def _filter_kernel(band_ref, w1t_ref, w1c_ref, w1s_ref, b1_ref, w2_ref, b2_ref, w3_ref, b3_ref,
                   w4f_ref, w4b_ref, fr_ref, delta_ref, k_ref, s_ref, *, seq):
    step = pl.program_id(0)
    gb, q, ncol = k_ref.shape
    half = q // 2
    width = delta_ref.shape[1]
    nrow = gb * q
    nf = gb * half

    def positions(shape, axis):
        r = lax.broadcasted_iota(I32, shape, axis)
        is_bwd = r >= nf
        rr = jnp.where(is_bwd, r - nf, r)
        j = lax.shift_right_logical(rr, int(math.log2(half)))
        a = (rr & (half - 1)) + jnp.where(is_bwd, half, 0)
        n = (a * DFT_P + step * gb + j).astype(F32)
        t = jnp.where(is_bwd, 2.0 * seq - n, n)
        return n, t, t / float(max(seq - 1, 1))

    _, t_l, tn_l = positions((1, nrow), 1)
    ang = (2.0 * math.pi / seq) * t_l * band_ref[...]
    fr = fr_ref[...]
    pre = (w1t_ref[...] * tn_l + _dot_hi(w1c_ref[...], jnp.cos(ang)) - _dot_hi(w1s_ref[...], jnp.sin(ang))
           + b1_ref[...])
    act = jnp.sin(fr * pre)
    act = jnp.sin(fr * (_dot_hi(w2_ref[...], act) + b2_ref[...]))
    act = jnp.sin(fr * (_dot_hi(w3_ref[...], act) + b3_ref[...])).astype(BF16)
    n_s, _, tn_s = positions((nrow, 1), 0)
    delta = jnp.concatenate([delta_ref[...]] * (ncol // width), axis=1)
    hf = _dot_tn(act[:, :nf], w4f_ref[...]) * jnp.exp(-tn_s[:nf] * delta)
    hb = _dot_tn(act[:, nf:], w4b_ref[...]) * jnp.exp(-tn_s[nf:] * delta)
    hb = jnp.where(n_s[nf:] == float(seq), 0.0, hb)
    k_ref[:, :half, :] = hf.reshape(gb, half, ncol).astype(k_ref.dtype)
    k_ref[:, half:, :] = hb.reshape(gb, half, ncol).astype(k_ref.dtype)
    tot = jnp.sum(jnp.abs(hf), axis=0, keepdims=True) + jnp.sum(jnp.abs(hb), axis=0, keepdims=True)

    @pl.when(step == 0)
    def _():
        s_ref[...] = jnp.zeros_like(s_ref)

    s_ref[...] += tot


def hyena_filter_taps(seq, w1, b1, w2, b2, w3, b3, w4, freq, width):
    emb = w1.shape[0]
    hid = w1.shape[1]
    bands = (emb - 1) // 2
    q = 2 * seq // DFT_P
    ncol = HY_ORDER * width
    band = np.linspace(1e-4, bands - 1, bands, dtype=np.float32).reshape(bands, 1)
    min_decay = math.log(HY_DECAY_TARGET) / HY_SLOW_PCT
    max_decay = math.log(HY_DECAY_TARGET) / HY_FAST_PCT
    delta = np.abs(np.linspace(min_decay, max_decay, width, dtype=np.float32)).reshape(1, width)
    w1t = w1.astype(F32).T
    col = lambda v: v.reshape(hid, 1).astype(F32)
    w4r = w4.astype(BF16).reshape(hid, HY_ORDER, 2, width)
    w4f = w4r[:, :, 0, :].reshape(hid, ncol)
    w4b = w4r[:, :, 1, :].reshape(hid, ncol)
    gb = FILTER_GROUP
    const = lambda shape: pl.BlockSpec(shape, lambda i: tuple(0 for _ in shape))
    return pl.pallas_call(
        functools.partial(_filter_kernel, seq=seq),
        out_shape=(jax.ShapeDtypeStruct((DFT_P, q, ncol), BF16),
                   jax.ShapeDtypeStruct((1, ncol), F32)),
        grid=(DFT_P // gb,),
        in_specs=[const((bands, 1)), const((hid, 1)), const((hid, bands)), const((hid, bands)), const((hid, 1)),
                  const((hid, hid)), const((hid, 1)), const((hid, hid)), const((hid, 1)),
                  const((hid, ncol)), const((hid, ncol)), const((hid, 1)), const((1, width))],
        out_specs=(pl.BlockSpec((gb, q, ncol), lambda i: (i, 0, 0)),
                   pl.BlockSpec((1, ncol), lambda i: (0, 0))),
        compiler_params=_cparams("arbitrary"),
        name="hyena_filter",
    )(jnp.asarray(band), w1t[:, 0:1], w1t[:, 1:1 + bands], w1t[:, 1 + bands:1 + 2 * bands], col(b1),
      w2.astype(F32).T, col(b2), w3.astype(F32).T, col(b3), w4f, w4b, col(freq), jnp.asarray(delta))


def _dft_tables(seq):
    p = DFT_P
    n_fft = 2 * seq
    q = n_fft // p
    qh = q // 2
    ka = jnp.arange(q, dtype=I32)
    a = jnp.arange(q, dtype=I32)
    b = jnp.arange(p, dtype=I32)
    nn = a[None, :] * p + b[:, None]
    ph = (ka[None, :, None] * nn[:, None, :]) % n_fft
    ang = ph.astype(F32) * (2.0 * math.pi / n_fft)
    mr, mi = jnp.cos(ang), -jnp.sin(ang)
    m1c = jnp.concatenate([jnp.concatenate([mr[:, :, :qh], -mi[:, :, :qh]], axis=2),
                           jnp.concatenate([mi[:, :, :qh], mr[:, :, :qh]], axis=2)], axis=1)
    m1r = jnp.concatenate([mr, mi], axis=1)
    gr = jnp.swapaxes(mr[:, :, :qh], 1, 2) / n_fft
    gi = -jnp.swapaxes(mi[:, :, :qh], 1, 2) / n_fft
    m4 = jnp.concatenate([jnp.concatenate([gr, -gi], axis=2), jnp.concatenate([gi, gr], axis=2)], axis=1)
    kb = np.arange(p)
    ang2 = 2.0 * np.pi * ((kb[:, None] * kb[None, :]) % p) / p
    fr, fi = np.cos(ang2), -np.sin(ang2)
    m2 = np.block([[fr, -fi], [fi, fr]]).astype(np.float32)
    m3 = np.block([[fr, fi], [-fi, fr]]).astype(np.float32)
    return (m1c.astype(BF16), m1r.astype(BF16), jnp.asarray(m2, BF16), jnp.asarray(m3, BF16), m4.astype(BF16))


def _bmm_kernel(w_ref, x_ref, o_ref, *, shared_w):
    for j in range(x_ref.shape[0]):
        w = w_ref[...] if shared_w else w_ref[j]
        o_ref[j] = _dot(w, x_ref[j]).astype(o_ref.dtype)


def batched_left_matmul(w, x, col_block, ncols, name, gb):
    g, k = x.shape[0], x.shape[1]
    shared = w.ndim == 2
    m = w.shape[-2]
    wspec = (pl.BlockSpec((m, k), lambda i: (0, 0)) if shared
             else pl.BlockSpec((gb, m, k), lambda i: (i, 0, 0)))
    return pl.pallas_call(
        functools.partial(_bmm_kernel, shared_w=shared),
        out_shape=jax.ShapeDtypeStruct((g, m, ncols), BF16),
        grid=(g // gb,),
        in_specs=[wspec, pl.BlockSpec((gb, k, ncols), lambda i: (i, 0, col_block))],
        out_specs=pl.BlockSpec((gb, m, ncols), lambda i: (i, 0, 0)),
        compiler_params=_cparams("arbitrary"),
        name=name,
    )(w, x)


def _dft_mid_kernel(m2_ref, m3_ref, x_ref, k_ref, o_ref):
    half = x_ref.shape[1] // 2
    for j in range(x_ref.shape[0]):
        xf = _dot(m2_ref[...], x_ref[j])
        kf = _dot(m2_ref[...], k_ref[j])
        xr, xi = xf[:half], xf[half:]
        kr, ki = kf[:half], kf[half:]
        z = jnp.concatenate([xr * kr - xi * ki, xr * ki + xi * kr], axis=0).astype(BF16)
        o_ref[j] = _dot(m3_ref[...], z).astype(o_ref.dtype)


def dft_mid(m2, m3, x, kspec, kcol, ncols):
    g, r = x.shape[0], x.shape[1]
    gb = DFT_GROUP
    return pl.pallas_call(
        _dft_mid_kernel,
        out_shape=jax.ShapeDtypeStruct((g, r, ncols), BF16),
        grid=(g // gb,),
        in_specs=[pl.BlockSpec((r, r), lambda i: (0, 0)),
                  pl.BlockSpec((r, r), lambda i: (0, 0)),
                  pl.BlockSpec((gb, r, ncols), lambda i: (i, 0, 0)),
                  pl.BlockSpec((gb, r, ncols), lambda i: (i, 0, kcol))],
        out_specs=pl.BlockSpec((gb, r, ncols), lambda i: (i, 0, 0)),
        compiler_params=_cparams("arbitrary"),
        name="dft_mid",
    )(m2, m3, x, kspec)


def _dft_out_kernel(m4_ref, y_ref, inv_ref, skip_ref, v_ref, mul_ref, o_ref, *, token_order):
    res = []
    for j in range(y_ref.shape[0]):
        conv = _dot(m4_ref[j], y_ref[j]) * inv_ref[...] + v_ref[j].astype(F32) * skip_ref[...]
        res.append(mul_ref[j].astype(F32) * conv)
    if token_order:
        o_ref[...] = jnp.swapaxes(jnp.stack(res, axis=0), 0, 1).astype(o_ref.dtype)
    else:
        for j, r in enumerate(res):
            o_ref[j] = r.astype(o_ref.dtype)


def dft_out(m4, y, inv_l1, skip, u, v_col, mul, mul_col, ncols, token_order):
    g, r = y.shape[0], y.shape[1]
    rows = m4.shape[1]
    gb = DFT_GROUP
    out_shape, out_spec = (((rows, g, ncols), pl.BlockSpec((rows, gb, ncols), lambda i: (0, i, 0))) if token_order
                           else ((g, rows, ncols), pl.BlockSpec((gb, rows, ncols), lambda i: (i, 0, 0))))
    return pl.pallas_call(
        functools.partial(_dft_out_kernel, token_order=token_order),
        out_shape=jax.ShapeDtypeStruct(out_shape, BF16),
        grid=(g // gb,),
        in_specs=[pl.BlockSpec((gb, rows, r), lambda i: (i, 0, 0)),
                  pl.BlockSpec((gb, r, ncols), lambda i: (i, 0, 0)),
                  pl.BlockSpec((1, ncols), lambda i: (0, 0)),
                  pl.BlockSpec((1, ncols), lambda i: (0, 0)),
                  pl.BlockSpec((gb, rows, ncols), lambda i: (i, 0, v_col)),
                  pl.BlockSpec((gb, rows, ncols), lambda i: (i, 0, mul_col))],
        out_specs=out_spec,
        compiler_params=_cparams("arbitrary"),
        name="dft_out",
    )(m4, y, inv_l1, skip, u, mul)


def _swap_ab(x):
    g1, r, c = x.shape
    g2 = r // 2
    return x.reshape(g1, 2, g2, c).transpose(2, 1, 0, 3).reshape(g2, 2 * g1, c)


def hyena_branch(u, bsz, seq, width, taps, l1, skip):
    m1c, m1r, m2, m3, m4 = _dft_tables(seq)
    ncol = HY_ORDER * width
    ks1 = _swap_ab(batched_left_matmul(m1r, taps, 0, ncol, "dft_k1", FILTER_GROUP))
    inv_l1 = 1.0 / l1
    z = None
    for order in range(HY_ORDER):
        src, src_col = (u, 0) if order == 0 else (z, 0)
        s1 = batched_left_matmul(m1c, src, src_col, width, "dft_s1", DFT_GROUP)
        mid = dft_mid(m2, m3, _swap_ab(s1), ks1, order, width)
        z = dft_out(m4, _swap_ab(mid), inv_l1[:, order * width:(order + 1) * width],
                    skip[order].reshape(1, width).astype(F32), src, src_col, u, order + 1, width,
                    token_order=order == HY_ORDER - 1)
    return z.reshape(bsz * seq, width)


def _pack_pairs(x):
    w = x.shape[1] // 2
    u = lax.bitcast_convert_type(x, U32)
    r = (u + U32(0x7FFF) + ((u >> 16) & U32(1))) >> 16
    return r[:, :w] | (r[:, w:] << 16)


def _unpack_pairs(p):
    lo = lax.bitcast_convert_type(p << 16, F32)
    hi = lax.bitcast_convert_type(p & U32(0xFFFF0000), F32)
    return jnp.concatenate([lo, hi], axis=1)


def _merge_kernel(x_ref, er_ref, ec_ref, yhy_ref, yhg_ref, ghy_ref, ghg_ref, why_ref, whg_ref, wo_ref,
                  g1_ref, n2_ref, sh2_ref, sc2_ref, g2_ref, rwh_ref, rwl_ref, sgu_ref, sd_ref,
                  xres_ref, h2p_ref, lg_ref):
    x = x_ref[...]
    rows, gw, d = x.shape
    half = d // 2
    xp = jnp.concatenate([x[:, :, :half] + er_ref[...], x[:, :, half:] + ec_ref[...]], axis=-1)
    xp = xp.reshape(rows * gw, d)
    m = (jax.nn.sigmoid(ghy_ref[...].astype(F32)) * _dot(yhy_ref[...], why_ref[...])
         + jax.nn.sigmoid(ghg_ref[...].astype(F32)) * _dot(yhg_ref[...], whg_ref[...]))
    x1 = xp + g1_ref[...] * _dot(m.astype(BF16), wo_ref[...])
    ms = jnp.mean(x1 * x1, axis=-1, keepdims=True)
    h2 = x1 * lax.rsqrt(ms + NORM_EPS) * n2_ref[...] * (1.0 + sc2_ref[...]) + sh2_ref[...]
    h_hi, h_lo = _split_bf16(h2)
    lg_ref[...] = _dot_nt(rwh_ref[...], h_hi) + (_dot_nt(rwl_ref[...], h_hi) + _dot_nt(rwh_ref[...], h_lo))
    gu = _dot(h_hi, sgu_ref[...])
    fs = gu.shape[1] // 2
    shared = _dot((_silu(gu[:, :fs]) * gu[:, fs:]).astype(BF16), sd_ref[...])
    xres_ref[...] = x1 + g2_ref[...] * shared
    h2p_ref[...] = _pack_pairs(h2)


def merge_stage(x, emb_r, emb_c, y_hy, y_hg, p, gate_cols, w_hy_out, w_hg_out, w_out, g1, norm2_g,
                sh2, sc2, g2, router_wt, sh_gate_up, sh_down, tm):
    b, s, d = x.shape
    rows_per_batch = s // GRID_W
    rt = tm // GRID_W
    tiles_per_batch = rows_per_batch // rt
    x3 = x.reshape(b * rows_per_batch, GRID_W, d)
    wb = y_hy.shape[1]
    ne = router_wt.shape[0]
    fs2 = sh_gate_up.shape[1]
    rw_hi, rw_lo = _split_bf16(router_wt)
    tok = lambda cb, w: pl.BlockSpec((tm, w), lambda i: (i, cb))
    const = lambda shape: pl.BlockSpec(shape, lambda i: tuple(0 for _ in shape))
    per_b = pl.BlockSpec((None, 1, d), lambda i: (i // tiles_per_batch, 0, 0))
    return pl.pallas_call(
        _merge_kernel,
        out_shape=(jax.ShapeDtypeStruct((b * s, d), F32),
                   jax.ShapeDtypeStruct((b * s, d // 2), U32),
                   jax.ShapeDtypeStruct((ne, b * s), F32)),
        grid=(b * tiles_per_batch,),
        in_specs=[pl.BlockSpec((rt, GRID_W, d), lambda i: (i, 0, 0)),
                  pl.BlockSpec((rt, 1, d // 2), lambda i: (i % tiles_per_batch, 0, 0)),
                  const((GRID_W, d // 2)),
                  tok(0, wb), tok(0, wb), tok(gate_cols[0], d), tok(gate_cols[1], d),
                  const((wb, d)), const((wb, d)), const((d, d)),
                  per_b, const((1, d)), per_b, per_b, per_b,
                  const((ne, d)), const((ne, d)), const((d, fs2)), const((fs2 // 2, d))],
        out_specs=(pl.BlockSpec((tm, d), lambda i: (i, 0)),
                   pl.BlockSpec((tm, d // 2), lambda i: (i, 0)),
                   pl.BlockSpec((ne, tm), lambda i: (0, i))),
        compiler_params=_cparams("arbitrary"),
        name="merge",
    )(x3, emb_r, emb_c, y_hy, y_hg, p, p, w_hy_out, w_hg_out, w_out, g1, norm2_g.reshape(1, d),
      sh2, sc2, g2, rw_hi, rw_lo, sh_gate_up, sh_down)


def _route_kernel(lg_ref, bias_ref, eidx_ref, wsel_ref, rank_ref, cnt_ref, carry):
    ne, tr = lg_ref.shape
    gsz = ne // N_GROUPS
    neg = -jnp.inf

    @pl.when(pl.program_id(0) == 0)
    def _():
        carry[...] = jnp.zeros_like(carry)

    scores = jax.nn.sigmoid(lg_ref[...])
    biased = scores + bias_ref[...]
    riota = lax.broadcasted_iota(I32, (gsz, tr), 0).astype(F32)
    gs = []
    for g in range(N_GROUPS):
        vg = biased[g * gsz:(g + 1) * gsz]
        m1 = jnp.max(vg, axis=0, keepdims=True)
        i1 = jnp.min(jnp.where(vg == m1, riota, float(gsz)), axis=0, keepdims=True)
        m2 = jnp.max(jnp.where(riota == i1, neg, vg), axis=0, keepdims=True)
        gs.append(m1 + m2)
    cur = jnp.concatenate(gs, axis=0)
    giota = lax.broadcasted_iota(I32, (N_GROUPS, tr), 0).astype(F32)
    gsel = jnp.zeros((N_GROUPS, tr), F32)
    for _ in range(TOPK_GROUPS):
        m = jnp.max(cur, axis=0, keepdims=True)
        idx = jnp.min(jnp.where(cur == m, giota, float(N_GROUPS)), axis=0, keepdims=True)
        hit = giota == idx
        gsel = jnp.where(hit, 1.0, gsel)
        cur = jnp.where(hit, neg, cur)
    cur = jnp.concatenate([jnp.where(gsel[g:g + 1] > 0.0, biased[g * gsz:(g + 1) * gsz], neg)
                           for g in range(N_GROUPS)], axis=0)
    eiota = lax.broadcasted_iota(I32, (ne, tr), 0).astype(F32)
    chosen = jnp.zeros((ne, tr), F32)
    idxs, ws = [], []
    for _ in range(TOP_K):
        m = jnp.max(cur, axis=0, keepdims=True)
        idx = jnp.min(jnp.where(cur == m, eiota, float(ne)), axis=0, keepdims=True)
        hit = eiota == idx
        idxs.append(idx)
        ws.append(jnp.sum(jnp.where(hit, scores, 0.0), axis=0, keepdims=True))
        chosen = jnp.where(hit, 1.0, chosen)
        cur = jnp.where(hit, neg, cur)
    w = jnp.concatenate(ws, axis=0)
    wsel_ref[...] = w / jnp.sum(w, axis=0, keepdims=True) * ROUTED_SCALE
    eidx_ref[...] = jnp.concatenate(idxs, axis=0).astype(I32)
    srow = lax.broadcasted_iota(I32, (tr, tr), 0)
    scol = lax.broadcasted_iota(I32, (tr, tr), 1)
    before = (srow < scol).astype(BF16)
    base = carry[...] + _dot(chosen.astype(BF16), before)
    ranks = [jnp.sum(jnp.where(eiota == idx, base, 0.0), axis=0, keepdims=True) for idx in idxs]
    rank_ref[...] = jnp.concatenate(ranks, axis=0).astype(I32)
    carry[...] += jnp.sum(chosen, axis=1, keepdims=True)
    cnt_ref[...] = carry[...]


def route(logits_t, router_bias, tr):
    ne, t = logits_t.shape
    return pl.pallas_call(
        _route_kernel,
        out_shape=(jax.ShapeDtypeStruct((TOP_K, t), I32),
                   jax.ShapeDtypeStruct((TOP_K, t), F32),
                   jax.ShapeDtypeStruct((TOP_K, t), I32),
                   jax.ShapeDtypeStruct((ne, 1), F32)),
        grid=(t // tr,),
        in_specs=[pl.BlockSpec((ne, tr), lambda i: (0, i)),
                  pl.BlockSpec((ne, 1), lambda i: (0, 0))],
        out_specs=(pl.BlockSpec((TOP_K, tr), lambda i: (0, i)),
                   pl.BlockSpec((TOP_K, tr), lambda i: (0, i)),
                   pl.BlockSpec((TOP_K, tr), lambda i: (0, i)),
                   pl.BlockSpec((ne, 1), lambda i: (0, 0))),
        scratch_shapes=[pltpu.VMEM((ne, 1), F32)],
        compiler_params=_cparams("arbitrary"),
        name="route",
    )(logits_t, router_bias.reshape(ne, 1).astype(F32))


def _dest_kernel(cnt_ref, eidx_ref, rank_ref, dest_ref, be_ref, nv_ref, nb_ref, start_scr):
    ne = cnt_ref.shape[0]
    tr = eidx_ref.shape[1]

    @pl.when(pl.program_id(0) == 0)
    def _():
        cnt = jnp.broadcast_to(cnt_ref[...], (ne, LANES))
        padded = jnp.floor((cnt + float(MOE_ROWS - 1)) / float(MOE_ROWS)) * float(MOE_ROWS)
        r = lax.broadcasted_iota(I32, (ne, ne), 0)
        c = lax.broadcasted_iota(I32, (ne, ne), 1)
        start = _dot_hi((c < r).astype(F32), padded)
        start_scr[...] = start
        end = start[:, 0:1] + padded[:, 0:1]
        used = start[:, 0:1] + cnt[:, 0:1]
        nbl = be_ref.shape[1]
        blk_row = (lax.broadcasted_iota(I32, (1, nbl), 1) * MOE_ROWS).astype(F32)
        total = jnp.max(end, axis=0, keepdims=True)
        last_row = total - float(MOE_ROWS)
        blk_row_c = jnp.minimum(blk_row, last_row)
        e_of = jnp.sum((end <= blk_row_c).astype(F32), axis=0, keepdims=True)
        e_of = jnp.minimum(e_of, float(ne - 1))
        eio = lax.broadcasted_iota(I32, (ne, nbl), 0).astype(F32)
        used_e = jnp.sum(jnp.where(eio == e_of, used, 0.0), axis=0, keepdims=True)
        valid = jnp.clip(used_e - blk_row_c, 0.0, float(MOE_ROWS))
        be_ref[...] = e_of.astype(I32)
        nv_ref[...] = jnp.where(blk_row <= last_row, valid, 0.0).astype(I32)
        nb_ref[...] = jnp.broadcast_to(total / float(MOE_ROWS), nb_ref.shape).astype(I32)

    eiota = lax.broadcasted_iota(I32, (ne, tr), 0)
    start_col = start_scr[:, 0:1]
    rows = []
    for k in range(TOP_K):
        hit = eiota == eidx_ref[k:k + 1, :]
        rows.append(jnp.sum(jnp.where(hit, start_col, 0.0), axis=0, keepdims=True))
    dest_ref[...] = jnp.concatenate(rows, axis=0).astype(I32) + rank_ref[...]


def dispatch_plan(counts, eidx, rank, tr, n_blocks):
    ne = counts.shape[0]
    t = eidx.shape[1]
    nbl = pl.cdiv(n_blocks, LANES) * LANES
    return pl.pallas_call(
        _dest_kernel,
        out_shape=(jax.ShapeDtypeStruct((TOP_K, t), I32),
                   jax.ShapeDtypeStruct((1, nbl), I32),
                   jax.ShapeDtypeStruct((1, nbl), I32),
                   jax.ShapeDtypeStruct((1, LANES), I32)),
        grid=(t // tr,),
        in_specs=[pl.BlockSpec((ne, 1), lambda i: (0, 0)),
                  pl.BlockSpec((TOP_K, tr), lambda i: (0, i)),
                  pl.BlockSpec((TOP_K, tr), lambda i: (0, i))],
        out_specs=(pl.BlockSpec((TOP_K, tr), lambda i: (0, i)),
                   pl.BlockSpec((1, nbl), lambda i: (0, 0)),
                   pl.BlockSpec((1, nbl), lambda i: (0, 0)),
                   pl.BlockSpec((1, LANES), lambda i: (0, 0))),
        scratch_shapes=[pltpu.VMEM((ne, LANES), F32)],
        compiler_params=_cparams("arbitrary"),
        name="dispatch_plan",
    )(counts, eidx, rank)


def _sc_workers():
    info = plsc.get_sparse_core_info()
    return info.num_cores, info.num_cores * info.num_subcores


def scatter_rows(dest_flat, h2p, n_rows):
    t, w = h2p.shape
    n_cores, n_workers = _sc_workers()
    per_worker = t // n_workers
    mesh = plsc.VectorSubcoreMesh(core_axis_name="c", subcore_axis_name="s")

    @functools.partial(
        pl.kernel, mesh=mesh, out_type=jax.ShapeDtypeStruct((n_rows, w), U32),
        scratch_types=[pltpu.VMEM((TOP_K, SC_ROWS), I32), pltpu.VMEM((SC_ROWS, w), U32), pltpu.SemaphoreType.DMA])
    def body(h_hbm, dest_hbm, xs_hbm, idx_v, rows_v, sem):
        base = (lax.axis_index("s") * n_cores + lax.axis_index("c")) * per_worker

        @pl.loop(0, per_worker // SC_ROWS)
        def _(ci):
            off = pl.multiple_of(base + ci * SC_ROWS, SC_ROWS)
            pltpu.sync_copy(h_hbm.at[pl.ds(off, SC_ROWS)], rows_v)
            for k in range(TOP_K):
                pltpu.sync_copy(dest_hbm.at[pl.ds(k * t + off, SC_ROWS)], idx_v.at[k])
            copies = [pltpu.async_copy(rows_v, xs_hbm.at[idx_v.at[k]], sem) for k in range(TOP_K)]
            for c in copies:
                c.wait()

    return body(h2p, dest_flat)


def gather_rows(idx_flat, table):
    n = idx_flat.shape[0]
    w = table.shape[1]
    n_cores, n_workers = _sc_workers()
    per_worker = n // n_workers
    mesh = plsc.VectorSubcoreMesh(core_axis_name="c", subcore_axis_name="s")

    @functools.partial(
        pl.kernel, mesh=mesh, out_type=jax.ShapeDtypeStruct((n, w), table.dtype),
        scratch_types=[pltpu.VMEM((SC_ROWS,), I32), pltpu.VMEM((SC_ROWS, w), table.dtype), pltpu.SemaphoreType.DMA])
    def body(table_hbm, idx_hbm, out_hbm, idx_v, rows_v, sem):
        base = (lax.axis_index("s") * n_cores + lax.axis_index("c")) * per_worker

        @pl.loop(0, per_worker // SC_ROWS)
        def _(ci):
            off = pl.multiple_of(base + ci * SC_ROWS, SC_ROWS)
            pltpu.sync_copy(idx_hbm.at[pl.ds(off, SC_ROWS)], idx_v)
            pltpu.async_copy(table_hbm.at[idx_v], rows_v, sem).wait()
            pltpu.sync_copy(rows_v, out_hbm.at[pl.ds(off, SC_ROWS)])

    return body(table, idx_flat)


def _gmm_kernel(be_ref, nv_ref, nb_ref, first_ref, slot_ref, nxt_ref, xs_ref, wg_hbm, wu_hbm, wd_hbm, y_ref,
                wg_buf, wu_buf, wd_buf, sem):
    def weight_copies(e, s):
        return (pltpu.make_async_copy(wg_hbm.at[e], wg_buf.at[s], sem.at[s]),
                pltpu.make_async_copy(wu_hbm.at[e], wu_buf.at[s], sem.at[s]),
                pltpu.make_async_copy(wd_hbm.at[e], wd_buf.at[s], sem.at[s]))

    def one_block(j, rows):
        s = slot_ref[j]

        @pl.when(j == 0)
        def _():
            for c in weight_copies(be_ref[0], 0):
                c.start()

        @pl.when(first_ref[j] == 1)
        def _():
            for c in weight_copies(be_ref[j], s):
                c.wait()

            @pl.when(nxt_ref[j] >= 0)
            def _():
                for c in weight_copies(nxt_ref[j], 1 - s):
                    c.start()

        x = _unpack_pairs(xs_ref[rows, :])
        row = lax.broadcasted_iota(I32, (x.shape[0], 1), 0)
        x = jnp.where(row < nv_ref[j], x, 0.0)
        hmid = _silu(_dot(x, wg_buf[s])) * _dot(x, wu_buf[s])
        y_ref[rows, :] = _pack_pairs(_dot(hmid, wd_buf[s]))

    for sb in range(MOE_STEP_BLOCKS):
        j = pl.program_id(0) * MOE_STEP_BLOCKS + sb
        pl.when(j < nb_ref[0])(functools.partial(one_block, j, slice(sb * MOE_ROWS, (sb + 1) * MOE_ROWS)))


def grouped_mlp(block_e, block_valid, n_used, xs, w_gate, w_up, w_down, n_blocks):
    ne, d, f = w_gate.shape
    w = xs.shape[1]
    jj = jnp.arange(block_e.shape[0], dtype=I32)
    active = jj < n_used[0]
    first = (active & ((jj == 0) | (block_e != jnp.roll(block_e, 1)))).astype(I32)
    run = jnp.cumsum(first) - 1
    slot = (run % 2).astype(I32)
    nbl = block_e.shape[0]
    run_expert = jnp.full((nbl + 1,), -1, I32).at[jnp.where(first == 1, run, nbl)].set(
        jnp.where(first == 1, block_e, -1))
    nxt = run_expert[jnp.minimum(run + 1, nbl)]
    step_rows = MOE_STEP_BLOCKS * MOE_ROWS
    last = lambda g, nb: jnp.minimum(g, (nb[0] - 1) // MOE_STEP_BLOCKS)
    grid_spec = pltpu.PrefetchScalarGridSpec(
        num_scalar_prefetch=6,
        grid=(n_blocks // MOE_STEP_BLOCKS,),
        in_specs=[pl.BlockSpec((step_rows, w), lambda g, be, nv, nb, fi, sl, nx: (last(g, nb), 0)),
                  pl.BlockSpec(memory_space=pl.ANY),
                  pl.BlockSpec(memory_space=pl.ANY),
                  pl.BlockSpec(memory_space=pl.ANY)],
        out_specs=pl.BlockSpec((step_rows, w), lambda g, be, nv, nb, fi, sl, nx: (last(g, nb), 0)),
        scratch_shapes=[pltpu.VMEM((2, d, f), F32), pltpu.VMEM((2, d, f), F32), pltpu.VMEM((2, f, d), F32),
                        pltpu.SemaphoreType.DMA((2,))],
    )
    return pl.pallas_call(
        _gmm_kernel,
        out_shape=jax.ShapeDtypeStruct(xs.shape, U32),
        grid_spec=grid_spec,
        compiler_params=_cparams("arbitrary"),
        name="grouped_mlp",
    )(block_e, block_valid, n_used, first, slot, nxt, xs, w_gate, w_up, w_down)


def _combine_kernel(y_ref, xres_ref, wt_ref, g2_ref, fg_ref, o_ref):
    wt = wt_ref[...]
    routed = jnp.zeros(xres_ref.shape, F32)
    for k in range(TOP_K):
        routed = routed + wt[:, k:k + 1] * _unpack_pairs(y_ref[k])
    x2 = xres_ref[...] + g2_ref[...] * routed
    ms = jnp.mean(x2 * x2, axis=-1, keepdims=True)
    o_ref[...] = x2 * lax.rsqrt(ms + NORM_EPS) * fg_ref[...]


def combine(y_tok, xres, wsel_t, g2, final_g, seq, tm):
    t, d = xres.shape
    tiles_per_batch = seq // tm
    return pl.pallas_call(
        _combine_kernel,
        out_shape=jax.ShapeDtypeStruct((t, d), F32),
        grid=(t // tm,),
        in_specs=[pl.BlockSpec((TOP_K, tm, d // 2), lambda i: (0, i, 0)),
                  pl.BlockSpec((tm, d), lambda i: (i, 0)),
                  pl.BlockSpec((tm, TOP_K), lambda i: (i, 0)),
                  pl.BlockSpec((None, 1, d), lambda i: (i // tiles_per_batch, 0, 0)),
                  pl.BlockSpec((1, d), lambda i: (0, 0))],
        out_specs=pl.BlockSpec((tm, d), lambda i: (i, 0)),
        compiler_params=_cparams("arbitrary"),
        name="combine",
    )(y_tok, xres, wsel_t, g2, final_g.reshape(1, d))


def _pos_tables(rows, cols, dim):
    quarter = dim // 4
    omega = 1.0 / (POS_BASE ** (np.arange(quarter, dtype=np.float32) / quarter))
    ang_r = np.arange(rows, dtype=np.float32)[:, None] * omega
    ang_c = np.arange(cols, dtype=np.float32)[:, None] * omega
    emb_r = np.concatenate([np.sin(ang_r), np.cos(ang_r)], axis=-1).astype(np.float32)
    emb_c = np.concatenate([np.sin(ang_c), np.cos(ang_c)], axis=-1).astype(np.float32)
    return jnp.asarray(emb_r.reshape(rows, 1, dim // 2)), jnp.asarray(emb_c)


def kernel(x, c, ctx, c_ctx, norm1_g, norm2_g, ada_w, ada_b, w_in, hy_conv_w, hy_conv_b, hy_f_w1, hy_f_b1, hy_f_w2, hy_f_b2, hy_f_w3, hy_f_b3, hy_f_w4, hy_f_freq, hy_skip, hg_lb_logits, hg_norm_g, w_hy_out, w_hg_out, w_out, router_w, router_bias, exp_w_gate, exp_w_up, exp_w_down, sh_w_gate, sh_w_up, sh_w_down, final_g):
    bsz, seq, d = x.shape
    n_ctx = ctx.shape[1]
    hy_w = w_hy_out.shape[1]
    hg_w = w_hg_out.shape[1]
    dk = hg_norm_g.shape[1]
    n_heads = hg_w // dk
    ne = router_w.shape[2]
    l = 0

    c_rows = jnp.zeros((SUBLANES, d), F32).at[:bsz].set(c).at[bsz].set(c_ctx)
    mods = ada_vectors(c_rows, ada_w[l], ada_b[l])
    sh1, sc1, g1, sh2, sc2, g2 = [mods[:bsz, j * d:(j + 1) * d].reshape(bsz, 1, d) for j in range(N_ADA)]
    csh1 = jnp.broadcast_to(mods[bsz, 0:d].reshape(1, 1, d), (bsz, 1, d))
    csc1 = jnp.broadcast_to(mods[bsz, d:2 * d].reshape(1, 1, d), (bsz, 1, d))

    emb_r, emb_c = _pos_tables(seq // GRID_W, GRID_W, d)
    w_in_b = w_in[l].astype(BF16)
    hy_proj = 3 * hy_w
    p = in_projection(x, emb_r, emb_c, norm1_g[l], sh1, sc1, w_in_b, TOKEN_TILE)
    hg_cols = slice(hy_proj, hy_proj + 5 * hg_w)
    zero_r = jnp.zeros((n_ctx // GRID_W, 1, d // 2), F32)
    zero_c = jnp.zeros((GRID_W, d // 2), F32)
    pc = in_projection(ctx, zero_r, zero_c, norm1_g[l], csh1, csc1, w_in_b[:, hg_cols], n_ctx)

    lbs = jnp.cumsum(jax.nn.softmax(hg_lb_logits.astype(F32), axis=0), axis=0)
    lb_f, lb_b = lbs[l, 0], lbs[l, 1]
    zero_state = jnp.zeros((bsz, n_heads, dk, dk), F32)
    base = hy_proj // hg_w
    _, st_f = hgrn_scan(pc, (0, 1, 2), lb_f, zero_state, n_ctx, n_ctx, reverse=False)
    _, st_b = hgrn_scan(pc, (0, 1, 3), lb_b, zero_state, n_ctx, n_ctx, reverse=True)
    o_f, _ = hgrn_scan(p, (base, base + 1, base + 2), lb_f, st_f, seq, HG_TIME_BLOCK, reverse=False)
    y_hg, _ = hgrn_scan(p, (base, base + 1, base + 3), lb_b, st_b, seq, HG_TIME_BLOCK, reverse=True,
                        o_fwd=o_f, gate_col=base + 4, norm_g=hg_norm_g[l])

    u = short_conv(p, hy_proj, hy_conv_w[l], hy_conv_b[l], seq, CONV_TILE, hy_w)
    taps, l1 = hyena_filter_taps(seq, hy_f_w1[l], hy_f_b1[l], hy_f_w2[l], hy_f_b2[l], hy_f_w3[l], hy_f_b3[l],
                                 hy_f_w4[l], hy_f_freq[l], hy_w)
    y_hy = hyena_branch(u, bsz, seq, hy_w, taps, l1, hy_skip[l])

    gate_base = (hy_proj + 5 * hg_w) // d
    sh_gu = jnp.concatenate([sh_w_gate[l], sh_w_up[l]], axis=1).astype(BF16)
    xres, h2p, logits_t = merge_stage(
        x, emb_r, emb_c, y_hy, y_hg, p, (gate_base, gate_base + 1),
        w_hy_out[l].astype(BF16), w_hg_out[l].astype(BF16), w_out[l].astype(BF16), g1, norm2_g[l],
        sh2, sc2, g2, router_w[l].T.astype(F32), sh_gu, sh_w_down[l].astype(BF16), TOKEN_TILE)

    t = bsz * seq
    eidx, wsel, rank, counts = route(logits_t, router_bias[l], TOKEN_TILE)
    n_rows = t * TOP_K + ne * (MOE_ROWS - 1)
    n_blocks = pl.cdiv(pl.cdiv(n_rows, MOE_ROWS), MOE_STEP_BLOCKS) * MOE_STEP_BLOCKS
    dest, block_e, block_valid, n_used = dispatch_plan(counts, eidx, rank, TOKEN_TILE, n_blocks)

    dest_flat = dest.reshape(-1)
    xs = scatter_rows(dest_flat, h2p, n_blocks * MOE_ROWS)
    ys = grouped_mlp(block_e.reshape(-1), block_valid.reshape(-1), n_used.reshape(-1)[:1], xs,
                     exp_w_gate[l], exp_w_up[l], exp_w_down[l], n_blocks)
    y_tok = gather_rows(dest_flat, ys).reshape(TOP_K, t, d // 2)
    out = combine(y_tok, xres, wsel.T, g2, final_g, seq, TOKEN_TILE)
    return out.reshape(bsz, seq, d)
```

```python
import functools
import math

import numpy as np
import jax
import jax.numpy as jnp
from jax import lax
from jax.experimental import pallas as pl
from jax.experimental.pallas import tpu as pltpu
from jax.experimental.pallas import tpu_sc as plsc

F32 = jnp.float32
BF16 = jnp.bfloat16
U32 = jnp.uint32
I32 = jnp.int32
HIGHEST = lax.Precision.HIGHEST

GRID_W = 64
POS_BASE = 10000.0
NORM_EPS = 1e-6
N_ADA = 6
HY_ORDER = 2
HY_SHORT = 3
HY_DECAY_TARGET = 1e-2
HY_FAST_PCT = 0.3
HY_SLOW_PCT = 1.5
HG_HEADS = 4
HG_CHUNK = 64
N_GROUPS = 8
TOPK_GROUPS = 4
TOP_K = 8
ROUTED_SCALE = 2.5

LANES = 128
SUBLANES = 8
VMEM_LIMIT = 56 * 1024 * 1024

TOKEN_TILE = 512
HG_TIME_BLOCK = 512
HALO_ROWS = 16
CONV_TILE = 2048
DFT_P = 128
FILTER_GROUP = 8
DFT_GROUP = 16
MOE_ROWS = 256
MOE_STEP_BLOCKS = 4
SC_ROWS = 128
GATHER_SPLIT = 2


def _cparams(*sem):
    return pltpu.CompilerParams(dimension_semantics=sem, vmem_limit_bytes=VMEM_LIMIT)


def _dot(a, b):
    return jnp.dot(a, b, preferred_element_type=F32)


def _dot_hi(a, b):
    return jnp.dot(a, b, preferred_element_type=F32, precision=HIGHEST)


def _dot_nt(a, b):
    return lax.dot_general(a, b, (((1,), (1,)), ((), ())), preferred_element_type=F32)


def _dot_tn(a, b):
    return lax.dot_general(a, b, (((0,), (0,)), ((), ())), preferred_element_type=F32)


def _silu(x):
    return x * jax.nn.sigmoid(x)


def _split_bf16(x):
    hi = x.astype(BF16)
    return hi, (x - hi.astype(F32)).astype(BF16)


def _ada_kernel(c_ref, w_ref, b_ref, o_ref):
    o_ref[...] = _dot_hi(_silu(c_ref[...]), w_ref[...]) + b_ref[...]


def ada_vectors(c_rows, ada_w, ada_b):
    r, d = c_rows.shape
    n = ada_w.shape[1]
    bn = 1024
    return pl.pallas_call(
        _ada_kernel,
        out_shape=jax.ShapeDtypeStruct((r, n), F32),
        grid=(n // bn,),
        in_specs=[pl.BlockSpec((r, d), lambda j: (0, 0)),
                  pl.BlockSpec((d, bn), lambda j: (0, j)),
                  pl.BlockSpec((1, bn), lambda j: (0, j))],
        out_specs=pl.BlockSpec((r, bn), lambda j: (0, j)),
        compiler_params=_cparams("arbitrary"),
        name="ada_vectors",
    )(c_rows, ada_w, ada_b.reshape(1, n))


def _inproj_kernel(x_ref, er_ref, ec_ref, g_ref, sh_ref, sc_ref, w_ref, o_ref, *, col_chunk):
    x = x_ref[...]
    rows, gw, d = x.shape
    half = d // 2
    xp = jnp.concatenate([x[:, :, :half] + er_ref[...], x[:, :, half:] + ec_ref[...]], axis=-1)
    xp = xp.reshape(rows * gw, d)
    ms = jnp.mean(xp * xp, axis=-1, keepdims=True)
    y = xp * lax.rsqrt(ms + NORM_EPS) * g_ref[...]
    h = (y * (1.0 + sc_ref[...]) + sh_ref[...]).astype(BF16)
    n = o_ref.shape[1]
    for j in range(n // col_chunk):
        sl = slice(j * col_chunk, (j + 1) * col_chunk)
        o_ref[:, sl] = _dot(h, w_ref[:, sl]).astype(o_ref.dtype)


def in_projection(x, emb_r, emb_c, norm_g, shift, scale, w_bf16, tm):
    b, s, d = x.shape
    n = w_bf16.shape[1]
    rows_per_batch = s // GRID_W
    rt = tm // GRID_W
    tiles_per_batch = rows_per_batch // rt
    x3 = x.reshape(b * rows_per_batch, GRID_W, d)
    col_chunk = 512
    return pl.pallas_call(
        functools.partial(_inproj_kernel, col_chunk=col_chunk),
        out_shape=jax.ShapeDtypeStruct((b * s, n), BF16),
        grid=(b * tiles_per_batch,),
        in_specs=[pl.BlockSpec((rt, GRID_W, d), lambda i: (i, 0, 0)),
                  pl.BlockSpec((rt, 1, d // 2), lambda i: (i % tiles_per_batch, 0, 0)),
                  pl.BlockSpec((GRID_W, d // 2), lambda i: (0, 0)),
                  pl.BlockSpec((1, d), lambda i: (0, 0)),
                  pl.BlockSpec((None, 1, d), lambda i: (i // tiles_per_batch, 0, 0)),
                  pl.BlockSpec((None, 1, d), lambda i: (i // tiles_per_batch, 0, 0)),
                  pl.BlockSpec((d, n), lambda i: (0, 0))],
        out_specs=pl.BlockSpec((tm, n), lambda i: (i, 0)),
        compiler_params=_cparams("arbitrary"),
        name="in_projection",
    )(x3, emb_r, emb_c, norm_g.reshape(1, d), shift, scale, w_bf16)


def _hgrn_kernel(*refs, reverse, n_chunks, final):
    if final:
        (q_ref, i_ref, f_ref, lb_ref, s0_ref, of_ref, gate_ref, ng_ref, o_ref, sfin_ref, s_scr) = refs
    else:
        (q_ref, i_ref, f_ref, lb_ref, s0_ref, o_ref, sfin_ref, s_scr) = refs
    cs = HG_CHUNK
    bsz, n_heads, _, dk = s_scr.shape

    @pl.when(pl.program_id(0) == 0)
    def _():
        s_scr[...] = s0_ref[...]

    row = lax.broadcasted_iota(I32, (cs, cs), 0)
    col = lax.broadcasted_iota(I32, (cs, cs), 1)
    tri = (col >= row) if reverse else (col <= row)
    tri_b = tri.astype(BF16)
    end_row = 0 if reverse else cs - 1
    mid_row = cs // 2 if reverse else cs // 2 - 1

    def chunk_body(bi, ci):
        c = (n_chunks - 1 - ci) if reverse else ci
        rows = slice(c * cs, (c + 1) * cs)
        lb = lb_ref[...]
        f = lb + (1.0 - lb) * jax.nn.sigmoid(f_ref[bi, rows, :].astype(F32))
        lf_hi, lf_lo = _split_bf16(jnp.log(f))
        b_all = _dot(tri_b, lf_hi) + _dot(tri_b, lf_lo)
        k_all = 1.0 - f
        q_all = _silu(q_ref[bi, rows, :].astype(F32))
        for h in range(n_heads):
            sl = slice(h * dk, (h + 1) * dk)
            b = b_all[:, sl]
            q = q_all[:, sl]
            k = k_all[:, sl]
            v = i_ref[bi, rows, sl]
            b_end = b[end_row:end_row + 1]
            b_mid = b[mid_row:mid_row + 1]
            qd = (q * jnp.exp(b - b_mid)).astype(BF16)
            kd = (k * jnp.exp(b_mid - b)).astype(BF16)
            att = jnp.where(tri, _dot_nt(qd, kd), 0.0).astype(BF16)
            st = s_scr[bi, h]
            qe = (q * jnp.exp(b)).astype(BF16)
            o = _dot(att, v) + _dot_nt(qe, st.astype(BF16))
            ke = (k * jnp.exp(b_end - b)).astype(BF16)
            s_scr[bi, h] = st * jnp.exp(b_end) + _dot_tn(v, ke)
            if final:
                o = o + of_ref[bi, rows, sl].astype(F32)
                o = o * lax.rsqrt(jnp.mean(o * o, axis=-1, keepdims=True) + NORM_EPS) * ng_ref[...]
                o = o * _silu(gate_ref[bi, rows, sl].astype(F32))
            o_ref[bi, rows, sl] = o.astype(o_ref.dtype)

    for ci in range(n_chunks):
        for bi in range(bsz):
            chunk_body(bi, ci)
    sfin_ref[...] = s_scr[...]


def hgrn_scan(p, cols, lb, s0, seq, tb, *, reverse, o_fwd=None, gate_col=None, norm_g=None):
    bsz, n_heads, dv, dk = s0.shape
    width = n_heads * dk
    nt = seq // tb
    final = o_fwd is not None
    p3 = p.reshape(bsz, seq, p.shape[1])
    tmap = (lambda t: nt - 1 - t) if reverse else (lambda t: t)
    colspec = lambda cb: pl.BlockSpec((bsz, tb, width), lambda t: (0, tmap(t), cb))
    state = pl.BlockSpec((bsz, n_heads, dv, dk), lambda t: (0, 0, 0, 0))
    in_specs = [colspec(cols[0]), colspec(cols[1]), colspec(cols[2]),
                pl.BlockSpec((1, width), lambda t: (0, 0)), state]
    args = [p3, p3, p3, lb.reshape(1, width), s0]
    if final:
        in_specs += [colspec(0), colspec(gate_col), pl.BlockSpec((1, dk), lambda t: (0, 0))]
        args += [o_fwd.reshape(bsz, seq, width), p3, norm_g.reshape(1, dk)]
    o, s_fin = pl.pallas_call(
        functools.partial(_hgrn_kernel, reverse=reverse, n_chunks=tb // HG_CHUNK, final=final),
        out_shape=(jax.ShapeDtypeStruct((bsz, seq, width), BF16),
                   jax.ShapeDtypeStruct((bsz, n_heads, dv, dk), F32)),
        grid=(nt,),
        in_specs=in_specs,
        out_specs=(colspec(0), state),
        scratch_shapes=[pltpu.VMEM((bsz, n_heads, dv, dk), F32)],
        compiler_params=_cparams("arbitrary"),
        name="hgrn_bwd" if reverse else "hgrn_fwd",
    )(*args)
    return o.reshape(bsz * seq, width), s_fin


def _shortconv_kernel(p_ref, prev_ref, next_ref, w_ref, b_ref, o_ref, *, tiles_per_batch):
    i = pl.program_id(0)
    ti = i % tiles_per_batch
    p = p_ref[...].astype(F32)
    tm = p.shape[0]
    row = lax.broadcasted_iota(I32, (tm, 1), 0)
    prev_row = jnp.where(ti == 0, 0.0, prev_ref[HALO_ROWS - 1:HALO_ROWS, :].astype(F32))
    next_row = jnp.where(ti == tiles_per_batch - 1, 0.0, next_ref[0:1, :].astype(F32))
    p_prev = jnp.where(row == 0, prev_row, pltpu.roll(p, 1, axis=0))
    p_next = jnp.where(row == tm - 1, next_row, pltpu.roll(p, tm - 1, axis=0))
    u = w_ref[0:1, :] * p_prev + w_ref[1:2, :] * p + w_ref[2:3, :] * p_next + b_ref[...]
    o_ref[...] = jnp.swapaxes(u.reshape(tm // DFT_P, DFT_P, u.shape[1]), 0, 1).astype(o_ref.dtype)


def short_conv(p, width, conv_w, conv_b, seq, tm, cw):
    t = p.shape[0]
    nt = t // tm
    tiles_per_batch = seq // tm
    sub = tm // HALO_ROWS
    ta = tm // DFT_P
    return pl.pallas_call(
        functools.partial(_shortconv_kernel, tiles_per_batch=tiles_per_batch),
        out_shape=jax.ShapeDtypeStruct((DFT_P, t // DFT_P, width), BF16),
        grid=(nt, width // cw),
        in_specs=[pl.BlockSpec((tm, cw), lambda i, j: (i, j)),
                  pl.BlockSpec((HALO_ROWS, cw), lambda i, j: (jnp.maximum(i * sub - 1, 0), j)),
                  pl.BlockSpec((HALO_ROWS, cw), lambda i, j: (jnp.minimum((i + 1) * sub, t // HALO_ROWS - 1), j)),
                  pl.BlockSpec((HY_SHORT, cw), lambda i, j: (0, j)),
                  pl.BlockSpec((1, cw), lambda i, j: (0, j))],
        out_specs=pl.BlockSpec((DFT_P, ta, cw), lambda i, j: (0, i, j)),
        compiler_params=_cparams("arbitrary", "arbitrary"),
        name="short_conv",
    )(p, p, p, conv_w, conv_b.reshape(1, width))


def _filter_kernel(band_ref, w1t_ref, w1c_ref, w1s_ref, b1_ref, w2_ref, b2_ref, w3_ref, b3_ref,
                   w4f_ref, w4b_ref, fr_ref, delta_ref, k_ref, s_ref, *, seq):
    step = pl.program_id(0)
    gb, q, ncol = k_ref.shape
    half = q // 2
    width = delta_ref.shape[1]
    nrow = gb * q
    nf = gb * half

    def positions(shape, axis):
        r = lax.broadcasted_iota(I32, shape, axis)
        is_bwd = r >= nf
        rr = jnp.where(is_bwd, r - nf, r)
        j = lax.shift_right_logical(rr, int(math.log2(half)))
        a = (rr & (half - 1)) + jnp.where(is_bwd, half, 0)
        n = (a * DFT_P + step * gb + j).astype(F32)
        t = jnp.where(is_bwd, 2.0 * seq - n, n)
        return n, t, t / float(max(seq - 1, 1))

    _, t_l, tn_l = positions((1, nrow), 1)
    ang = (2.0 * math.pi / seq) * t_l * band_ref[...]
    fr = fr_ref[...]
    pre = (w1t_ref[...] * tn_l + _dot_hi(w1c_ref[...], jnp.cos(ang)) - _dot_hi(w1s_ref[...], jnp.sin(ang))
           + b1_ref[...])
    act = jnp.sin(fr * pre)
    act = jnp.sin(fr * (_dot_hi(w2_ref[...], act) + b2_ref[...]))
    act = jnp.sin(fr * (_dot_hi(w3_ref[...], act) + b3_ref[...])).astype(BF16)
    n_s, _, tn_s = positions((nrow, 1), 0)
    delta = jnp.concatenate([delta_ref[...]] * (ncol // width), axis=1)
    hf = _dot_tn(act[:, :nf], w4f_ref[...]) * jnp.exp(-tn_s[:nf] * delta)
    hb = _dot_tn(act[:, nf:], w4b_ref[...]) * jnp.exp(-tn_s[nf:] * delta)
    hb = jnp.where(n_s[nf:] == float(seq), 0.0, hb)
    k_ref[:, :half, :] = hf.reshape(gb, half, ncol).astype(k_ref.dtype)
    k_ref[:, half:, :] = hb.reshape(gb, half, ncol).astype(k_ref.dtype)
    tot = jnp.sum(jnp.abs(hf), axis=0, keepdims=True) + jnp.sum(jnp.abs(hb), axis=0, keepdims=True)

    @pl.when(step == 0)
    def _():
        s_ref[...] = jnp.zeros_like(s_ref)

    s_ref[...] += tot


def hyena_filter_taps(seq, w1, b1, w2, b2, w3, b3, w4, freq, width):
    emb = w1.shape[0]
    hid = w1.shape[1]
    bands = (emb - 1) // 2
    q = 2 * seq // DFT_P
    ncol = HY_ORDER * width
    band = np.linspace(1e-4, bands - 1, bands, dtype=np.float32).reshape(bands, 1)
    min_decay = math.log(HY_DECAY_TARGET) / HY_SLOW_PCT
    max_decay = math.log(HY_DECAY_TARGET) / HY_FAST_PCT
    delta = np.abs(np.linspace(min_decay, max_decay, width, dtype=np.float32)).reshape(1, width)
    w1t = w1.astype(F32).T
    col = lambda v: v.reshape(hid, 1).astype(F32)
    w4r = w4.astype(BF16).reshape(hid, HY_ORDER, 2, width)
    w4f = w4r[:, :, 0, :].reshape(hid, ncol)
    w4b = w4r[:, :, 1, :].reshape(hid, ncol)
    gb = FILTER_GROUP
    const = lambda shape: pl.BlockSpec(shape, lambda i: tuple(0 for _ in shape))
    return pl.pallas_call(
        functools.partial(_filter_kernel, seq=seq),
        out_shape=(jax.ShapeDtypeStruct((DFT_P, q, ncol), BF16),
                   jax.ShapeDtypeStruct((1, ncol), F32)),
        grid=(DFT_P // gb,),
        in_specs=[const((bands, 1)), const((hid, 1)), const((hid, bands)), const((hid, bands)), const((hid, 1)),
                  const((hid, hid)), const((hid, 1)), const((hid, hid)), const((hid, 1)),
                  const((hid, ncol)), const((hid, ncol)), const((hid, 1)), const((1, width))],
        out_specs=(pl.BlockSpec((gb, q, ncol), lambda i: (i, 0, 0)),
                   pl.BlockSpec((1, ncol), lambda i: (0, 0))),
        compiler_params=_cparams("arbitrary"),
        name="hyena_filter",
    )(jnp.asarray(band), w1t[:, 0:1], w1t[:, 1:1 + bands], w1t[:, 1 + bands:1 + 2 * bands], col(b1),
      w2.astype(F32).T, col(b2), w3.astype(F32).T, col(b3), w4f, w4b, col(freq), jnp.asarray(delta))


def _dft_tables(seq):
    p = DFT_P
    n_fft = 2 * seq
    q = n_fft // p
    qh = q // 2
    ka = jnp.arange(q, dtype=I32)
    a = jnp.arange(q, dtype=I32)
    b = jnp.arange(p, dtype=I32)
    nn = a[None, :] * p + b[:, None]
    ph = (ka[None, :, None] * nn[:, None, :]) % n_fft
    ang = ph.astype(F32) * (2.0 * math.pi / n_fft)
    mr, mi = jnp.cos(ang), -jnp.sin(ang)
    m1c = jnp.concatenate([jnp.concatenate([mr[:, :, :qh], -mi[:, :, :qh]], axis=2),
                           jnp.concatenate([mi[:, :, :qh], mr[:, :, :qh]], axis=2)], axis=1)
    m1r = jnp.concatenate([mr, mi], axis=1)
    gr = jnp.swapaxes(mr[:, :, :qh], 1, 2) / n_fft
    gi = -jnp.swapaxes(mi[:, :, :qh], 1, 2) / n_fft
    m4 = jnp.concatenate([jnp.concatenate([gr, -gi], axis=2), jnp.concatenate([gi, gr], axis=2)], axis=1)
    kb = np.arange(p)
    ang2 = 2.0 * np.pi * ((kb[:, None] * kb[None, :]) % p) / p
    fr, fi = np.cos(ang2), -np.sin(ang2)
    m2 = np.block([[fr, -fi], [fi, fr]]).astype(np.float32)
    m3 = np.block([[fr, fi], [-fi, fr]]).astype(np.float32)
    return (m1c.astype(BF16), m1r.astype(BF16), jnp.asarray(m2, BF16), jnp.asarray(m3, BF16), m4.astype(BF16))


def _bmm_kernel(w_ref, x_ref, o_ref, *, shared_w):
    for j in range(x_ref.shape[0]):
        w = w_ref[...] if shared_w else w_ref[j]
        o_ref[j] = _dot(w, x_ref[j]).astype(o_ref.dtype)


def batched_left_matmul(w, x, col_block, ncols, name, gb):
    g, k = x.shape[0], x.shape[1]
    shared = w.ndim == 2
    m = w.shape[-2]
    wspec = (pl.BlockSpec((m, k), lambda i: (0, 0)) if shared
             else pl.BlockSpec((gb, m, k), lambda i: (i, 0, 0)))
    return pl.pallas_call(
        functools.partial(_bmm_kernel, shared_w=shared),
        out_shape=jax.ShapeDtypeStruct((g, m, ncols), BF16),
        grid=(g // gb,),
        in_specs=[wspec, pl.BlockSpec((gb, k, ncols), lambda i: (i, 0, col_block))],
        out_specs=pl.BlockSpec((gb, m, ncols), lambda i: (i, 0, 0)),
        compiler_params=_cparams("arbitrary"),
        name=name,
    )(w, x)


def _dft_mid_kernel(m2_ref, m3_ref, x_ref, k_ref, o_ref):
    half = x_ref.shape[1] // 2
    for j in range(x_ref.shape[0]):
        xf = _dot(m2_ref[...], x_ref[j])
        kf = _dot(m2_ref[...], k_ref[j])
        xr, xi = xf[:half], xf[half:]
        kr, ki = kf[:half], kf[half:]
        z = jnp.concatenate([xr * kr - xi * ki, xr * ki + xi * kr], axis=0).astype(BF16)
        o_ref[j] = _dot(m3_ref[...], z).astype(o_ref.dtype)


def dft_mid(m2, m3, x, kspec, kcol, ncols):
    g, r = x.shape[0], x.shape[1]
    gb = DFT_GROUP
    return pl.pallas_call(
        _dft_mid_kernel,
        out_shape=jax.ShapeDtypeStruct((g, r, ncols), BF16),
        grid=(g // gb,),
        in_specs=[pl.BlockSpec((r, r), lambda i: (0, 0)),
                  pl.BlockSpec((r, r), lambda i: (0, 0)),
                  pl.BlockSpec((gb, r, ncols), lambda i: (i, 0, 0)),
                  pl.BlockSpec((gb, r, ncols), lambda i: (i, 0, kcol))],
        out_specs=pl.BlockSpec((gb, r, ncols), lambda i: (i, 0, 0)),
        compiler_params=_cparams("arbitrary"),
        name="dft_mid",
    )(m2, m3, x, kspec)


def _dft_out_kernel(m4_ref, y_ref, inv_ref, skip_ref, v_ref, mul_ref, o_ref, *, token_order):
    res = []
    for j in range(y_ref.shape[0]):
        conv = _dot(m4_ref[j], y_ref[j]) * inv_ref[...] + v_ref[j].astype(F32) * skip_ref[...]
        res.append(mul_ref[j].astype(F32) * conv)
    if token_order:
        o_ref[...] = jnp.swapaxes(jnp.stack(res, axis=0), 0, 1).astype(o_ref.dtype)
    else:
        for j, r in enumerate(res):
            o_ref[j] = r.astype(o_ref.dtype)


def dft_out(m4, y, inv_l1, skip, u, v_col, mul, mul_col, ncols, token_order):
    g, r = y.shape[0], y.shape[1]
    rows = m4.shape[1]
    gb = DFT_GROUP
    out_shape, out_spec = (((rows, g, ncols), pl.BlockSpec((rows, gb, ncols), lambda i: (0, i, 0))) if token_order
                           else ((g, rows, ncols), pl.BlockSpec((gb, rows, ncols), lambda i: (i, 0, 0))))
    return pl.pallas_call(
        functools.partial(_dft_out_kernel, token_order=token_order),
        out_shape=jax.ShapeDtypeStruct(out_shape, BF16),
        grid=(g // gb,),
        in_specs=[pl.BlockSpec((gb, rows, r), lambda i: (i, 0, 0)),
                  pl.BlockSpec((gb, r, ncols), lambda i: (i, 0, 0)),
                  pl.BlockSpec((1, ncols), lambda i: (0, 0)),
                  pl.BlockSpec((1, ncols), lambda i: (0, 0)),
                  pl.BlockSpec((gb, rows, ncols), lambda i: (i, 0, v_col)),
                  pl.BlockSpec((gb, rows, ncols), lambda i: (i, 0, mul_col))],
        out_specs=out_spec,
        compiler_params=_cparams("arbitrary"),
        name="dft_out",
    )(m4, y, inv_l1, skip, u, mul)


def _swap_ab(x):
    g1, r, c = x.shape
    g2 = r // 2
    return x.reshape(g1, 2, g2, c).transpose(2, 1, 0, 3).reshape(g2, 2 * g1, c)


def hyena_branch(u, bsz, seq, width, taps, l1, skip):
    m1c, m1r, m2, m3, m4 = _dft_tables(seq)
    ncol = HY_ORDER * width
    ks1 = _swap_ab(batched_left_matmul(m1r, taps, 0, ncol, "dft_k1", FILTER_GROUP))
    inv_l1 = 1.0 / l1
    z = None
    for order in range(HY_ORDER):
        src, src_col = (u, 0) if order == 0 else (z, 0)
        s1 = batched_left_matmul(m1c, src, src_col, width, "dft_s1", DFT_GROUP)
        mid = dft_mid(m2, m3, _swap_ab(s1), ks1, order, width)
        z = dft_out(m4, _swap_ab(mid), inv_l1[:, order * width:(order + 1) * width],
                    skip[order].reshape(1, width).astype(F32), src, src_col, u, order + 1, width,
                    token_order=order == HY_ORDER - 1)
    return z.reshape(bsz * seq, width)


def _pack_pairs(x):
    w = x.shape[1] // 2
    u = lax.bitcast_convert_type(x, U32)
    r = (u + U32(0x7FFF) + ((u >> 16) & U32(1))) >> 16
    return r[:, :w] | (r[:, w:] << 16)


def _unpack_pairs(p):
    lo = lax.bitcast_convert_type(p << 16, F32)
    hi = lax.bitcast_convert_type(p & U32(0xFFFF0000), F32)
    return jnp.concatenate([lo, hi], axis=1)


def _merge_kernel(x_ref, er_ref, ec_ref, yhy_ref, yhg_ref, ghy_ref, ghg_ref, why_ref, whg_ref, wo_ref,
                  g1_ref, n2_ref, sh2_ref, sc2_ref, g2_ref, rwh_ref, rwl_ref, sgu_ref, sd_ref,
                  xres_ref, h2p_ref, lg_ref):
    x = x_ref[...]
    rows, gw, d = x.shape
    half = d // 2
    xp = jnp.concatenate([x[:, :, :half] + er_ref[...], x[:, :, half:] + ec_ref[...]], axis=-1)
    xp = xp.reshape(rows * gw, d)
    m = (jax.nn.sigmoid(ghy_ref[...].astype(F32)) * _dot(yhy_ref[...], why_ref[...])
         + jax.nn.sigmoid(ghg_ref[...].astype(F32)) * _dot(yhg_ref[...], whg_ref[...]))
    x1 = xp + g1_ref[...] * _dot(m.astype(BF16), wo_ref[...])
    ms = jnp.mean(x1 * x1, axis=-1, keepdims=True)
    h2 = x1 * lax.rsqrt(ms + NORM_EPS) * n2_ref[...] * (1.0 + sc2_ref[...]) + sh2_ref[...]
    h_hi, h_lo = _split_bf16(h2)
    lg_ref[...] = _dot_nt(rwh_ref[...], h_hi) + (_dot_nt(rwl_ref[...], h_hi) + _dot_nt(rwh_ref[...], h_lo))
    gu = _dot(h_hi, sgu_ref[...])
    fs = gu.shape[1] // 2
    shared = _dot((_silu(gu[:, :fs]) * gu[:, fs:]).astype(BF16), sd_ref[...])
    xres_ref[...] = x1 + g2_ref[...] * shared
    h2p_ref[...] = _pack_pairs(h2)


def merge_stage(x, emb_r, emb_c, y_hy, y_hg, p, gate_cols, w_hy_out, w_hg_out, w_out, g1, norm2_g,
                sh2, sc2, g2, router_wt, sh_gate_up, sh_down, tm):
    b, s, d = x.shape
    rows_per_batch = s // GRID_W
    rt = tm // GRID_W
    tiles_per_batch = rows_per_batch // rt
    x3 = x.reshape(b * rows_per_batch, GRID_W, d)
    wb = y_hy.shape[1]
    ne = router_wt.shape[0]
    fs2 = sh_gate_up.shape[1]
    rw_hi, rw_lo = _split_bf16(router_wt)
    tok = lambda cb, w: pl.BlockSpec((tm, w), lambda i: (i, cb))
    const = lambda shape: pl.BlockSpec(shape, lambda i: tuple(0 for _ in shape))
    per_b = pl.BlockSpec((None, 1, d), lambda i: (i // tiles_per_batch, 0, 0))
    return pl.pallas_call(
        _merge_kernel,
        out_shape=(jax.ShapeDtypeStruct((b * s, d), F32),
                   jax.ShapeDtypeStruct((b * s, d // 2), U32),
                   jax.ShapeDtypeStruct((ne, b * s), F32)),
        grid=(b * tiles_per_batch,),
        in_specs=[pl.BlockSpec((rt, GRID_W, d), lambda i: (i, 0, 0)),
                  pl.BlockSpec((rt, 1, d // 2), lambda i: (i % tiles_per_batch, 0, 0)),
                  const((GRID_W, d // 2)),
                  tok(0, wb), tok(0, wb), tok(gate_cols[0], d), tok(gate_cols[1], d),
                  const((wb, d)), const((wb, d)), const((d, d)),
                  per_b, const((1, d)), per_b, per_b, per_b,
                  const((ne, d)), const((ne, d)), const((d, fs2)), const((fs2 // 2, d))],
        out_specs=(pl.BlockSpec((tm, d), lambda i: (i, 0)),
                   pl.BlockSpec((tm, d // 2), lambda i: (i, 0)),
                   pl.BlockSpec((ne, tm), lambda i: (0, i))),
        compiler_params=_cparams("arbitrary"),
        name="merge",
    )(x3, emb_r, emb_c, y_hy, y_hg, p, p, w_hy_out, w_hg_out, w_out, g1, norm2_g.reshape(1, d),
      sh2, sc2, g2, rw_hi, rw_lo, sh_gate_up, sh_down)


def _route_kernel(lg_ref, bias_ref, eidx_ref, wsel_ref, rank_ref, cnt_ref, carry):
    ne, tr = lg_ref.shape
    gsz = ne // N_GROUPS
    neg = -jnp.inf

    @pl.when(pl.program_id(0) == 0)
    def _():
        carry[...] = jnp.zeros_like(carry)

    scores = jax.nn.sigmoid(lg_ref[...])
    biased = scores + bias_ref[...]
    riota = lax.broadcasted_iota(I32, (gsz, tr), 0).astype(F32)
    gs = []
    for g in range(N_GROUPS):
        vg = biased[g * gsz:(g + 1) * gsz]
        m1 = jnp.max(vg, axis=0, keepdims=True)
        i1 = jnp.min(jnp.where(vg == m1, riota, float(gsz)), axis=0, keepdims=True)
        m2 = jnp.max(jnp.where(riota == i1, neg, vg), axis=0, keepdims=True)
        gs.append(m1 + m2)
    cur = jnp.concatenate(gs, axis=0)
    giota = lax.broadcasted_iota(I32, (N_GROUPS, tr), 0).astype(F32)
    gsel = jnp.zeros((N_GROUPS, tr), F32)
    for _ in range(TOPK_GROUPS):
        m = jnp.max(cur, axis=0, keepdims=True)
        idx = jnp.min(jnp.where(cur == m, giota, float(N_GROUPS)), axis=0, keepdims=True)
        hit = giota == idx
        gsel = jnp.where(hit, 1.0, gsel)
        cur = jnp.where(hit, neg, cur)
    cur = jnp.concatenate([jnp.where(gsel[g:g + 1] > 0.0, biased[g * gsz:(g + 1) * gsz], neg)
                           for g in range(N_GROUPS)], axis=0)
    eiota = lax.broadcasted_iota(I32, (ne, tr), 0).astype(F32)
    chosen = jnp.zeros((ne, tr), F32)
    idxs, ws = [], []
    for _ in range(TOP_K):
        m = jnp.max(cur, axis=0, keepdims=True)
        idx = jnp.min(jnp.where(cur == m, eiota, float(ne)), axis=0, keepdims=True)
        hit = eiota == idx
        idxs.append(idx)
        ws.append(jnp.sum(jnp.where(hit, scores, 0.0), axis=0, keepdims=True))
        chosen = jnp.where(hit, 1.0, chosen)
        cur = jnp.where(hit, neg, cur)
    w = jnp.concatenate(ws, axis=0)
    wsel_ref[...] = w / jnp.sum(w, axis=0, keepdims=True) * ROUTED_SCALE
    eidx_ref[...] = jnp.concatenate(idxs, axis=0).astype(I32)
    srow = lax.broadcasted_iota(I32, (tr, tr), 0)
    scol = lax.broadcasted_iota(I32, (tr, tr), 1)
    before = (srow < scol).astype(BF16)
    base = carry[...] + _dot(chosen.astype(BF16), before)
    ranks = [jnp.sum(jnp.where(eiota == idx, base, 0.0), axis=0, keepdims=True) for idx in idxs]
    rank_ref[...] = jnp.concatenate(ranks, axis=0).astype(I32)
    carry[...] += jnp.sum(chosen, axis=1, keepdims=True)
    cnt_ref[...] = carry[...]


def route(logits_t, router_bias, tr):
    ne, t = logits_t.shape
    return pl.pallas_call(
        _route_kernel,
        out_shape=(jax.ShapeDtypeStruct((TOP_K, t), I32),
                   jax.ShapeDtypeStruct((TOP_K, t), F32),
                   jax.ShapeDtypeStruct((TOP_K, t), I32),
                   jax.ShapeDtypeStruct((ne, 1), F32)),
        grid=(t // tr,),
        in_specs=[pl.BlockSpec((ne, tr), lambda i: (0, i)),
                  pl.BlockSpec((ne, 1), lambda i: (0, 0))],
        out_specs=(pl.BlockSpec((TOP_K, tr), lambda i: (0, i)),
                   pl.BlockSpec((TOP_K, tr), lambda i: (0, i)),
                   pl.BlockSpec((TOP_K, tr), lambda i: (0, i)),
                   pl.BlockSpec((ne, 1), lambda i: (0, 0))),
        scratch_shapes=[pltpu.VMEM((ne, 1), F32)],
        compiler_params=_cparams("arbitrary"),
        name="route",
    )(logits_t, router_bias.reshape(ne, 1).astype(F32))


def _dest_kernel(cnt_ref, eidx_ref, rank_ref, dest_ref, be_ref, nv_ref, nb_ref, start_scr):
    ne = cnt_ref.shape[0]
    tr = eidx_ref.shape[1]

    @pl.when(pl.program_id(0) == 0)
    def _():
        cnt = jnp.broadcast_to(cnt_ref[...], (ne, LANES))
        padded = jnp.floor((cnt + float(MOE_ROWS - 1)) / float(MOE_ROWS)) * float(MOE_ROWS)
        r = lax.broadcasted_iota(I32, (ne, ne), 0)
        c = lax.broadcasted_iota(I32, (ne, ne), 1)
        start = _dot_hi((c < r).astype(F32), padded)
        start_scr[...] = start
        end = start[:, 0:1] + padded[:, 0:1]
        used = start[:, 0:1] + cnt[:, 0:1]
        nbl = be_ref.shape[1]
        blk_row = (lax.broadcasted_iota(I32, (1, nbl), 1) * MOE_ROWS).astype(F32)
        total = jnp.max(end, axis=0, keepdims=True)
        last_row = total - float(MOE_ROWS)
        blk_row_c = jnp.minimum(blk_row, last_row)
        e_of = jnp.sum((end <= blk_row_c).astype(F32), axis=0, keepdims=True)
        e_of = jnp.minimum(e_of, float(ne - 1))
        eio = lax.broadcasted_iota(I32, (ne, nbl), 0).astype(F32)
        used_e = jnp.sum(jnp.where(eio == e_of, used, 0.0), axis=0, keepdims=True)
        valid = jnp.clip(used_e - blk_row_c, 0.0, float(MOE_ROWS))
        be_ref[...] = e_of.astype(I32)
        nv_ref[...] = jnp.where(blk_row <= last_row, valid, 0.0).astype(I32)
        nb_ref[...] = jnp.broadcast_to(total / float(MOE_ROWS), nb_ref.shape).astype(I32)

    eiota = lax.broadcasted_iota(I32, (ne, tr), 0)
    start_col = start_scr[:, 0:1]
    rows = []
    for k in range(TOP_K):
        hit = eiota == eidx_ref[k:k + 1, :]
        rows.append(jnp.sum(jnp.where(hit, start_col, 0.0), axis=0, keepdims=True))
    dest_ref[...] = jnp.concatenate(rows, axis=0).astype(I32) + rank_ref[...]


def dispatch_plan(counts, eidx, rank, tr, n_blocks):
    ne = counts.shape[0]
    t = eidx.shape[1]
    nbl = pl.cdiv(n_blocks, LANES) * LANES
    return pl.pallas_call(
        _dest_kernel,
        out_shape=(jax.ShapeDtypeStruct((TOP_K, t), I32),
                   jax.ShapeDtypeStruct((1, nbl), I32),
                   jax.ShapeDtypeStruct((1, nbl), I32),
                   jax.ShapeDtypeStruct((1, LANES), I32)),
        grid=(t // tr,),
        in_specs=[pl.BlockSpec((ne, 1), lambda i: (0, 0)),
                  pl.BlockSpec((TOP_K, tr), lambda i: (0, i)),
                  pl.BlockSpec((TOP_K, tr), lambda i: (0, i))],
        out_specs=(pl.BlockSpec((TOP_K, tr), lambda i: (0, i)),
                   pl.BlockSpec((1, nbl), lambda i: (0, 0)),
                   pl.BlockSpec((1, nbl), lambda i: (0, 0)),
                   pl.BlockSpec((1, LANES), lambda i: (0, 0))),
        scratch_shapes=[pltpu.VMEM((ne, LANES), F32)],
        compiler_params=_cparams("arbitrary"),
        name="dispatch_plan",
    )(counts, eidx, rank)


def _sc_workers():
    info = plsc.get_sparse_core_info()
    return info.num_cores, info.num_cores * info.num_subcores


def scatter_rows(dest_flat, h2p, n_rows):
    t, w = h2p.shape
    n_cores, n_workers = _sc_workers()
    per_worker = t // n_workers
    mesh = plsc.VectorSubcoreMesh(core_axis_name="c", subcore_axis_name="s")

    @functools.partial(
        pl.kernel, mesh=mesh, out_type=jax.ShapeDtypeStruct((n_rows, w), U32),
        scratch_types=[pltpu.VMEM((TOP_K, SC_ROWS), I32), pltpu.VMEM((SC_ROWS, w), U32), pltpu.SemaphoreType.DMA])
    def body(h_hbm, dest_hbm, xs_hbm, idx_v, rows_v, sem):
        base = (lax.axis_index("s") * n_cores + lax.axis_index("c")) * per_worker

        @pl.loop(0, per_worker // SC_ROWS)
        def _(ci):
            off = pl.multiple_of(base + ci * SC_ROWS, SC_ROWS)
            pltpu.sync_copy(h_hbm.at[pl.ds(off, SC_ROWS)], rows_v)
            for k in range(TOP_K):
                pltpu.sync_copy(dest_hbm.at[pl.ds(k * t + off, SC_ROWS)], idx_v.at[k])
            copies = [pltpu.async_copy(rows_v, xs_hbm.at[idx_v.at[k]], sem) for k in range(TOP_K)]
            for c in copies:
                c.wait()

    return body(h2p, dest_flat)


def gather_rows(idx_flat, table):
    n = idx_flat.shape[0]
    w = table.shape[1]
    n_cores, n_workers = _sc_workers()
    per_worker = n // n_workers
    mesh = plsc.VectorSubcoreMesh(core_axis_name="c", subcore_axis_name="s")

    ch = SC_ROWS // 2
    n_chunks = per_worker // ch

    @functools.partial(
        pl.kernel, mesh=mesh, out_type=jax.ShapeDtypeStruct((n, w), table.dtype),
        scratch_types=[pltpu.VMEM((2, ch), I32), pltpu.VMEM((2, ch, w), table.dtype), pltpu.SemaphoreType.DMA((2,))])
    def body(table_hbm, idx_hbm, out_hbm, idx_v, rows_v, sem):
        base = (lax.axis_index("s") * n_cores + lax.axis_index("c")) * per_worker

        def read(b):
            return pltpu.make_async_copy(table_hbm.at[idx_v.at[b]], rows_v.at[b], sem.at[b])

        def start(c, b):
            off = pl.multiple_of(base + c * ch, ch)
            pltpu.sync_copy(idx_hbm.at[pl.ds(off, ch)], idx_v.at[b])
            read(b).start()

        def finish(c, b):
            read(b).wait()
            pltpu.sync_copy(rows_v.at[b], out_hbm.at[pl.ds(pl.multiple_of(base + c * ch, ch), ch)])

        start(0, 0)

        @pl.loop(0, n_chunks, step=2)
        def _(c):
            start(c + 1, 1)
            finish(c, 0)

            @pl.when(c + 2 < n_chunks)
            def _():
                start(c + 2, 0)

            finish(c + 1, 1)

    return body(table, idx_flat)


def _gmm_kernel(be_ref, nv_ref, nb_ref, first_ref, slot_ref, nxt_ref, xs_ref, wg_hbm, wu_hbm, wd_hbm, y_ref,
                wg_buf, wu_buf, wd_buf, sem):
    def weight_copies(e, s):
        return (pltpu.make_async_copy(wg_hbm.at[e], wg_buf.at[s], sem.at[s]),
                pltpu.make_async_copy(wu_hbm.at[e], wu_buf.at[s], sem.at[s]),
                pltpu.make_async_copy(wd_hbm.at[e], wd_buf.at[s], sem.at[s]))

    def one_block(j, rows):
        s = slot_ref[j]

        @pl.when(j == 0)
        def _():
            for c in weight_copies(be_ref[0], 0):
                c.start()

        @pl.when(first_ref[j] == 1)
        def _():
            for c in weight_copies(be_ref[j], s):
                c.wait()

            @pl.when(nxt_ref[j] >= 0)
            def _():
                for c in weight_copies(nxt_ref[j], 1 - s):
                    c.start()

        x = _unpack_pairs(xs_ref[rows, :])
        row = lax.broadcasted_iota(I32, (x.shape[0], 1), 0)
        x = jnp.where(row < nv_ref[j], x, 0.0)
        hmid = _silu(_dot(x, wg_buf[s])) * _dot(x, wu_buf[s])
        y_ref[rows, :] = _pack_pairs(_dot(hmid, wd_buf[s]))

    for sb in range(MOE_STEP_BLOCKS):
        j = pl.program_id(0) * MOE_STEP_BLOCKS + sb
        pl.when(j < nb_ref[0])(functools.partial(one_block, j, slice(sb * MOE_ROWS, (sb + 1) * MOE_ROWS)))


def grouped_mlp(block_e, block_valid, n_used, xs, w_gate, w_up, w_down, n_blocks):
    ne, d, f = w_gate.shape
    w = xs.shape[1]
    jj = jnp.arange(block_e.shape[0], dtype=I32)
    active = jj < n_used[0]
    first = (active & ((jj == 0) | (block_e != jnp.roll(block_e, 1)))).astype(I32)
    run = jnp.cumsum(first) - 1
    slot = (run % 2).astype(I32)
    nbl = block_e.shape[0]
    run_expert = jnp.full((nbl + 1,), -1, I32).at[jnp.where(first == 1, run, nbl)].set(
        jnp.where(first == 1, block_e, -1))
    nxt = run_expert[jnp.minimum(run + 1, nbl)]
    step_rows = MOE_STEP_BLOCKS * MOE_ROWS
    last = lambda g, nb: jnp.minimum(g, (nb[0] - 1) // MOE_STEP_BLOCKS)
    grid_spec = pltpu.PrefetchScalarGridSpec(
        num_scalar_prefetch=6,
        grid=(n_blocks // MOE_STEP_BLOCKS,),
        in_specs=[pl.BlockSpec((step_rows, w), lambda g, be, nv, nb, fi, sl, nx: (last(g, nb), 0)),
                  pl.BlockSpec(memory_space=pl.ANY),
                  pl.BlockSpec(memory_space=pl.ANY),
                  pl.BlockSpec(memory_space=pl.ANY)],
        out_specs=pl.BlockSpec((step_rows, w), lambda g, be, nv, nb, fi, sl, nx: (last(g, nb), 0)),
        scratch_shapes=[pltpu.VMEM((2, d, f), F32), pltpu.VMEM((2, d, f), F32), pltpu.VMEM((2, f, d), F32),
                        pltpu.SemaphoreType.DMA((2,))],
    )
    return pl.pallas_call(
        _gmm_kernel,
        out_shape=jax.ShapeDtypeStruct(xs.shape, U32),
        grid_spec=grid_spec,
        compiler_params=_cparams("arbitrary"),
        name="grouped_mlp",
    )(block_e, block_valid, n_used, first, slot, nxt, xs, w_gate, w_up, w_down)


def _combine_kernel(y_ref, xres_ref, wt_ref, g2_ref, fg_ref, o_ref):
    wt = wt_ref[...]
    routed = jnp.zeros(xres_ref.shape, F32)
    for k in range(TOP_K):
        routed = routed + wt[:, k:k + 1] * _unpack_pairs(y_ref[k])
    x2 = xres_ref[...] + g2_ref[...] * routed
    ms = jnp.mean(x2 * x2, axis=-1, keepdims=True)
    o_ref[...] = x2 * lax.rsqrt(ms + NORM_EPS) * fg_ref[...]


def _combine_into_kernel(prev_ref, *refs):
    del prev_ref
    _combine_kernel(*refs)


def combine(y_tok, xres, wsel_t, g2, final_g, seq, tm, first_tile, prev_out):
    t, d = xres.shape
    tiles_per_batch = seq // tm
    tile = lambda i: i + first_tile
    in_specs = [pl.BlockSpec((TOP_K, tm, d // 2), lambda i: (0, i, 0)),
                pl.BlockSpec((tm, d), lambda i: (tile(i), 0)),
                pl.BlockSpec((tm, TOP_K), lambda i: (tile(i), 0)),
                pl.BlockSpec((None, 1, d), lambda i: (tile(i) // tiles_per_batch, 0, 0)),
                pl.BlockSpec((1, d), lambda i: (0, 0))]
    args = [y_tok, xres, wsel_t, g2, final_g.reshape(1, d)]
    body, aliases = _combine_kernel, {}
    if prev_out is not None:
        body, aliases = _combine_into_kernel, {0: 0}
        in_specs = [pl.BlockSpec(memory_space=pl.ANY)] + in_specs
        args = [prev_out] + args
    return pl.pallas_call(
        body,
        out_shape=jax.ShapeDtypeStruct((t, d), F32),
        grid=(y_tok.shape[1] // tm,),
        in_specs=in_specs,
        out_specs=pl.BlockSpec((tm, d), lambda i: (tile(i), 0)),
        input_output_aliases=aliases,
        compiler_params=_cparams("arbitrary"),
        name="combine",
    )(*args)


def _pos_tables(rows, cols, dim):
    quarter = dim // 4
    omega = 1.0 / (POS_BASE ** (np.arange(quarter, dtype=np.float32) / quarter))
    ang_r = np.arange(rows, dtype=np.float32)[:, None] * omega
    ang_c = np.arange(cols, dtype=np.float32)[:, None] * omega
    emb_r = np.concatenate([np.sin(ang_r), np.cos(ang_r)], axis=-1).astype(np.float32)
    emb_c = np.concatenate([np.sin(ang_c), np.cos(ang_c)], axis=-1).astype(np.float32)
    return jnp.asarray(emb_r.reshape(rows, 1, dim // 2)), jnp.asarray(emb_c)


def kernel(x, c, ctx, c_ctx, norm1_g, norm2_g, ada_w, ada_b, w_in, hy_conv_w, hy_conv_b, hy_f_w1, hy_f_b1, hy_f_w2, hy_f_b2, hy_f_w3, hy_f_b3, hy_f_w4, hy_f_freq, hy_skip, hg_lb_logits, hg_norm_g, w_hy_out, w_hg_out, w_out, router_w, router_bias, exp_w_gate, exp_w_up, exp_w_down, sh_w_gate, sh_w_up, sh_w_down, final_g):
    bsz, seq, d = x.shape
    n_ctx = ctx.shape[1]
    hy_w = w_hy_out.shape[1]
    hg_w = w_hg_out.shape[1]
    dk = hg_norm_g.shape[1]
    n_heads = hg_w // dk
    ne = router_w.shape[2]
    l = 0

    c_rows = jnp.zeros((SUBLANES, d), F32).at[:bsz].set(c).at[bsz].set(c_ctx)
    mods = ada_vectors(c_rows, ada_w[l], ada_b[l])
    sh1, sc1, g1, sh2, sc2, g2 = [mods[:bsz, j * d:(j + 1) * d].reshape(bsz, 1, d) for j in range(N_ADA)]
    csh1 = jnp.broadcast_to(mods[bsz, 0:d].reshape(1, 1, d), (bsz, 1, d))
    csc1 = jnp.broadcast_to(mods[bsz, d:2 * d].reshape(1, 1, d), (bsz, 1, d))

    emb_r, emb_c = _pos_tables(seq // GRID_W, GRID_W, d)
    w_in_b = w_in[l].astype(BF16)
    hy_proj = 3 * hy_w
    p = in_projection(x, emb_r, emb_c, norm1_g[l], sh1, sc1, w_in_b, TOKEN_TILE)
    hg_cols = slice(hy_proj, hy_proj + 5 * hg_w)
    zero_r = jnp.zeros((n_ctx // GRID_W, 1, d // 2), F32)
    zero_c = jnp.zeros((GRID_W, d // 2), F32)
    pc = in_projection(ctx, zero_r, zero_c, norm1_g[l], csh1, csc1, w_in_b[:, hg_cols], n_ctx)

    lbs = jnp.cumsum(jax.nn.softmax(hg_lb_logits.astype(F32), axis=0), axis=0)
    lb_f, lb_b = lbs[l, 0], lbs[l, 1]
    zero_state = jnp.zeros((bsz, n_heads, dk, dk), F32)
    base = hy_proj // hg_w
    _, st_f = hgrn_scan(pc, (0, 1, 2), lb_f, zero_state, n_ctx, n_ctx, reverse=False)
    _, st_b = hgrn_scan(pc, (0, 1, 3), lb_b, zero_state, n_ctx, n_ctx, reverse=True)
    o_f, _ = hgrn_scan(p, (base, base + 1, base + 2), lb_f, st_f, seq, HG_TIME_BLOCK, reverse=False)
    y_hg, _ = hgrn_scan(p, (base, base + 1, base + 3), lb_b, st_b, seq, HG_TIME_BLOCK, reverse=True,
                        o_fwd=o_f, gate_col=base + 4, norm_g=hg_norm_g[l])

    u = short_conv(p, hy_proj, hy_conv_w[l], hy_conv_b[l], seq, CONV_TILE, hy_w)
    taps, l1 = hyena_filter_taps(seq, hy_f_w1[l], hy_f_b1[l], hy_f_w2[l], hy_f_b2[l], hy_f_w3[l], hy_f_b3[l],
                                 hy_f_w4[l], hy_f_freq[l], hy_w)
    y_hy = hyena_branch(u, bsz, seq, hy_w, taps, l1, hy_skip[l])

    gate_base = (hy_proj + 5 * hg_w) // d
    sh_gu = jnp.concatenate([sh_w_gate[l], sh_w_up[l]], axis=1).astype(BF16)
    xres, h2p, logits_t = merge_stage(
        x, emb_r, emb_c, y_hy, y_hg, p, (gate_base, gate_base + 1),
        w_hy_out[l].astype(BF16), w_hg_out[l].astype(BF16), w_out[l].astype(BF16), g1, norm2_g[l],
        sh2, sc2, g2, router_w[l].T.astype(F32), sh_gu, sh_w_down[l].astype(BF16), TOKEN_TILE)

    t = bsz * seq
    eidx, wsel, rank, counts = route(logits_t, router_bias[l], TOKEN_TILE)
    n_rows = t * TOP_K + ne * (MOE_ROWS - 1)
    n_blocks = pl.cdiv(pl.cdiv(n_rows, MOE_ROWS), MOE_STEP_BLOCKS) * MOE_STEP_BLOCKS
    dest, block_e, block_valid, n_used = dispatch_plan(counts, eidx, rank, TOKEN_TILE, n_blocks)

    dest_flat = dest.reshape(-1)
    xs = scatter_rows(dest_flat, h2p, n_blocks * MOE_ROWS)
    ys = grouped_mlp(block_e.reshape(-1), block_valid.reshape(-1), n_used.reshape(-1)[:1], xs,
                     exp_w_gate[l], exp_w_up[l], exp_w_down[l], n_blocks)
    wsel_t = wsel.T
    out = None
    for h in range(GATHER_SPLIT):
        lo = h * (t // GATHER_SPLIT)
        rng = dest[:, lo:lo + t // GATHER_SPLIT]
        y_tok = gather_rows(rng.reshape(-1), ys).reshape(TOP_K, t // GATHER_SPLIT, d // 2)
        out = combine(y_tok, xres, wsel_t, g2, final_g, seq, TOKEN_TILE, lo // TOKEN_TILE, out)
    return out.reshape(bsz, seq, d)
```

```python
import functools
import math

import numpy as np
import jax
import jax.numpy as jnp
from jax import lax
from jax.experimental import pallas as pl
from jax.experimental.pallas import tpu as pltpu
from jax.experimental.pallas import tpu_sc as plsc

F32 = jnp.float32
BF16 = jnp.bfloat16
U32 = jnp.uint32
I32 = jnp.int32
HIGHEST = lax.Precision.HIGHEST

GRID_W = 64
POS_BASE = 10000.0
NORM_EPS = 1e-6
N_ADA = 6
HY_ORDER = 2
HY_SHORT = 3
HY_DECAY_TARGET = 1e-2
HY_FAST_PCT = 0.3
HY_SLOW_PCT = 1.5
HG_HEADS = 4
HG_CHUNK = 64
N_GROUPS = 8
TOPK_GROUPS = 4
TOP_K = 8
ROUTED_SCALE = 2.5

LANES = 128
SUBLANES = 8
VMEM_LIMIT = 56 * 1024 * 1024

TOKEN_TILE = 512
HG_TIME_BLOCK = 512
HALO_ROWS = 16
CONV_TILE = 2048
DFT_P = 128
FILTER_GROUP = 8
DFT_GROUP = 16
MOE_ROWS = 256
MOE_STEP_BLOCKS = 4
WEIGHT_SLOTS = 3
SC_ROWS = 128
GATHER_SPLIT = 2


def _cparams(*sem):
    return pltpu.CompilerParams(dimension_semantics=sem, vmem_limit_bytes=VMEM_LIMIT)


def _dot(a, b):
    return jnp.dot(a, b, preferred_element_type=F32)


def _dot_hi(a, b):
    return jnp.dot(a, b, preferred_element_type=F32, precision=HIGHEST)


def _dot_nt(a, b):
    return lax.dot_general(a, b, (((1,), (1,)), ((), ())), preferred_element_type=F32)


def _dot_tn(a, b):
    return lax.dot_general(a, b, (((0,), (0,)), ((), ())), preferred_element_type=F32)


def _silu(x):
    return x * jax.nn.sigmoid(x)


def _split_bf16(x):
    hi = x.astype(BF16)
    return hi, (x - hi.astype(F32)).astype(BF16)


def _ada_kernel(c_ref, w_ref, b_ref, o_ref):
    o_ref[...] = _dot_hi(_silu(c_ref[...]), w_ref[...]) + b_ref[...]


def ada_vectors(c_rows, ada_w, ada_b):
    r, d = c_rows.shape
    n = ada_w.shape[1]
    bn = 1024
    return pl.pallas_call(
        _ada_kernel,
        out_shape=jax.ShapeDtypeStruct((r, n), F32),
        grid=(n // bn,),
        in_specs=[pl.BlockSpec((r, d), lambda j: (0, 0)),
                  pl.BlockSpec((d, bn), lambda j: (0, j)),
                  pl.BlockSpec((1, bn), lambda j: (0, j))],
        out_specs=pl.BlockSpec((r, bn), lambda j: (0, j)),
        compiler_params=_cparams("arbitrary"),
        name="ada_vectors",
    )(c_rows, ada_w, ada_b.reshape(1, n))


def _inproj_kernel(x_ref, er_ref, ec_ref, g_ref, sh_ref, sc_ref, w_ref, o_ref, *, col_chunk):
    x = x_ref[...]
    rows, gw, d = x.shape
    half = d // 2
    xp = jnp.concatenate([x[:, :, :half] + er_ref[...], x[:, :, half:] + ec_ref[...]], axis=-1)
    xp = xp.reshape(rows * gw, d)
    ms = jnp.mean(xp * xp, axis=-1, keepdims=True)
    y = xp * lax.rsqrt(ms + NORM_EPS) * g_ref[...]
    h = (y * (1.0 + sc_ref[...]) + sh_ref[...]).astype(BF16)
    n = o_ref.shape[1]
    for j in range(n // col_chunk):
        sl = slice(j * col_chunk, (j + 1) * col_chunk)
        o_ref[:, sl] = _dot(h, w_ref[:, sl]).astype(o_ref.dtype)


def in_projection(x, emb_r, emb_c, norm_g, shift, scale, w_bf16, tm):
    b, s, d = x.shape
    n = w_bf16.shape[1]
    rows_per_batch = s // GRID_W
    rt = tm // GRID_W
    tiles_per_batch = rows_per_batch // rt
    x3 = x.reshape(b * rows_per_batch, GRID_W, d)
    col_chunk = 512
    return pl.pallas_call(
        functools.partial(_inproj_kernel, col_chunk=col_chunk),
        out_shape=jax.ShapeDtypeStruct((b * s, n), BF16),
        grid=(b * tiles_per_batch,),
        in_specs=[pl.BlockSpec((rt, GRID_W, d), lambda i: (i, 0, 0)),
                  pl.BlockSpec((rt, 1, d // 2), lambda i: (i % tiles_per_batch, 0, 0)),
                  pl.BlockSpec((GRID_W, d // 2), lambda i: (0, 0)),
                  pl.BlockSpec((1, d), lambda i: (0, 0)),
                  pl.BlockSpec((None, 1, d), lambda i: (i // tiles_per_batch, 0, 0)),
                  pl.BlockSpec((None, 1, d), lambda i: (i // tiles_per_batch, 0, 0)),
                  pl.BlockSpec((d, n), lambda i: (0, 0))],
        out_specs=pl.BlockSpec((tm, n), lambda i: (i, 0)),
        compiler_params=_cparams("arbitrary"),
        name="in_projection",
    )(x3, emb_r, emb_c, norm_g.reshape(1, d), shift, scale, w_bf16)


def _hgrn_kernel(*refs, reverse, n_chunks, final):
    if final:
        (q_ref, i_ref, f_ref, lb_ref, s0_ref, of_ref, gate_ref, ng_ref, o_ref, sfin_ref, s_scr) = refs
    else:
        (q_ref, i_ref, f_ref, lb_ref, s0_ref, o_ref, sfin_ref, s_scr) = refs
    cs = HG_CHUNK
    bsz, n_heads, _, dk = s_scr.shape

    @pl.when(pl.program_id(0) == 0)
    def _():
        s_scr[...] = s0_ref[...]

    row = lax.broadcasted_iota(I32, (cs, cs), 0)
    col = lax.broadcasted_iota(I32, (cs, cs), 1)
    tri = (col >= row) if reverse else (col <= row)
    tri_b = tri.astype(BF16)
    end_row = 0 if reverse else cs - 1
    mid_row = cs // 2 if reverse else cs // 2 - 1

    def chunk_body(bi, ci):
        c = (n_chunks - 1 - ci) if reverse else ci
        rows = slice(c * cs, (c + 1) * cs)
        lb = lb_ref[...]
        f = lb + (1.0 - lb) * jax.nn.sigmoid(f_ref[bi, rows, :].astype(F32))
        lf_hi, lf_lo = _split_bf16(jnp.log(f))
        b_all = _dot(tri_b, lf_hi) + _dot(tri_b, lf_lo)
        k_all = 1.0 - f
        q_all = _silu(q_ref[bi, rows, :].astype(F32))
        for h in range(n_heads):
            sl = slice(h * dk, (h + 1) * dk)
            b = b_all[:, sl]
            q = q_all[:, sl]
            k = k_all[:, sl]
            v = i_ref[bi, rows, sl]
            b_end = b[end_row:end_row + 1]
            b_mid = b[mid_row:mid_row + 1]
            qd = (q * jnp.exp(b - b_mid)).astype(BF16)
            kd = (k * jnp.exp(b_mid - b)).astype(BF16)
            att = jnp.where(tri, _dot_nt(qd, kd), 0.0).astype(BF16)
            st = s_scr[bi, h]
            qe = (q * jnp.exp(b)).astype(BF16)
            o = _dot(att, v) + _dot_nt(qe, st.astype(BF16))
            ke = (k * jnp.exp(b_end - b)).astype(BF16)
            s_scr[bi, h] = st * jnp.exp(b_end) + _dot_tn(v, ke)
            if final:
                o = o + of_ref[bi, rows, sl].astype(F32)
                o = o * lax.rsqrt(jnp.mean(o * o, axis=-1, keepdims=True) + NORM_EPS) * ng_ref[...]
                o = o * _silu(gate_ref[bi, rows, sl].astype(F32))
            o_ref[bi, rows, sl] = o.astype(o_ref.dtype)

    for ci in range(n_chunks):
        for bi in range(bsz):
            chunk_body(bi, ci)
    sfin_ref[...] = s_scr[...]


def hgrn_scan(p, cols, lb, s0, seq, tb, *, reverse, o_fwd=None, gate_col=None, norm_g=None):
    bsz, n_heads, dv, dk = s0.shape
    width = n_heads * dk
    nt = seq // tb
    final = o_fwd is not None
    p3 = p.reshape(bsz, seq, p.shape[1])
    tmap = (lambda t: nt - 1 - t) if reverse else (lambda t: t)
    colspec = lambda cb: pl.BlockSpec((bsz, tb, width), lambda t: (0, tmap(t), cb))
    state = pl.BlockSpec((bsz, n_heads, dv, dk), lambda t: (0, 0, 0, 0))
    in_specs = [colspec(cols[0]), colspec(cols[1]), colspec(cols[2]),
                pl.BlockSpec((1, width), lambda t: (0, 0)), state]
    args = [p3, p3, p3, lb.reshape(1, width), s0]
    if final:
        in_specs += [colspec(0), colspec(gate_col), pl.BlockSpec((1, dk), lambda t: (0, 0))]
        args += [o_fwd.reshape(bsz, seq, width), p3, norm_g.reshape(1, dk)]
    o, s_fin = pl.pallas_call(
        functools.partial(_hgrn_kernel, reverse=reverse, n_chunks=tb // HG_CHUNK, final=final),
        out_shape=(jax.ShapeDtypeStruct((bsz, seq, width), BF16),
                   jax.ShapeDtypeStruct((bsz, n_heads, dv, dk), F32)),
        grid=(nt,),
        in_specs=in_specs,
        out_specs=(colspec(0), state),
        scratch_shapes=[pltpu.VMEM((bsz, n_heads, dv, dk), F32)],
        compiler_params=_cparams("arbitrary"),
        name="hgrn_bwd" if reverse else "hgrn_fwd",
    )(*args)
    return o.reshape(bsz * seq, width), s_fin


def _shortconv_kernel(p_ref, prev_ref, next_ref, w_ref, b_ref, o_ref, *, tiles_per_batch):
    i = pl.program_id(0)
    ti = i % tiles_per_batch
    p = p_ref[...].astype(F32)
    tm = p.shape[0]
    row = lax.broadcasted_iota(I32, (tm, 1), 0)
    prev_row = jnp.where(ti == 0, 0.0, prev_ref[HALO_ROWS - 1:HALO_ROWS, :].astype(F32))
    next_row = jnp.where(ti == tiles_per_batch - 1, 0.0, next_ref[0:1, :].astype(F32))
    p_prev = jnp.where(row == 0, prev_row, pltpu.roll(p, 1, axis=0))
    p_next = jnp.where(row == tm - 1, next_row, pltpu.roll(p, tm - 1, axis=0))
    u = w_ref[0:1, :] * p_prev + w_ref[1:2, :] * p + w_ref[2:3, :] * p_next + b_ref[...]
    o_ref[...] = jnp.swapaxes(u.reshape(tm // DFT_P, DFT_P, u.shape[1]), 0, 1).astype(o_ref.dtype)


def short_conv(p, width, conv_w, conv_b, seq, tm, cw):
    t = p.shape[0]
    nt = t // tm
    tiles_per_batch = seq // tm
    sub = tm // HALO_ROWS
    ta = tm // DFT_P
    return pl.pallas_call(
        functools.partial(_shortconv_kernel, tiles_per_batch=tiles_per_batch),
        out_shape=jax.ShapeDtypeStruct((DFT_P, t // DFT_P, width), BF16),
        grid=(nt, width // cw),
        in_specs=[pl.BlockSpec((tm, cw), lambda i, j: (i, j)),
                  pl.BlockSpec((HALO_ROWS, cw), lambda i, j: (jnp.maximum(i * sub - 1, 0), j)),
                  pl.BlockSpec((HALO_ROWS, cw), lambda i, j: (jnp.minimum((i + 1) * sub, t // HALO_ROWS - 1), j)),
                  pl.BlockSpec((HY_SHORT, cw), lambda i, j: (0, j)),
                  pl.BlockSpec((1, cw), lambda i, j: (0, j))],
        out_specs=pl.BlockSpec((DFT_P, ta, cw), lambda i, j: (0, i, j)),
        compiler_params=_cparams("arbitrary", "arbitrary"),
        name="short_conv",
    )(p, p, p, conv_w, conv_b.reshape(1, width))


def _filter_kernel(band_ref, w1t_ref, w1c_ref, w1s_ref, b1_ref, w2_ref, b2_ref, w3_ref, b3_ref,
                   w4f_ref, w4b_ref, fr_ref, delta_ref, k_ref, s_ref, *, seq):
    step = pl.program_id(0)
    gb, q, ncol = k_ref.shape
    half = q // 2
    width = delta_ref.shape[1]
    nrow = gb * q
    nf = gb * half

    def positions(shape, axis):
        r = lax.broadcasted_iota(I32, shape, axis)
        is_bwd = r >= nf
        rr = jnp.where(is_bwd, r - nf, r)
        j = lax.shift_right_logical(rr, int(math.log2(half)))
        a = (rr & (half - 1)) + jnp.where(is_bwd, half, 0)
        n = (a * DFT_P + step * gb + j).astype(F32)
        t = jnp.where(is_bwd, 2.0 * seq - n, n)
        return n, t, t / float(max(seq - 1, 1))

    _, t_l, tn_l = positions((1, nrow), 1)
    ang = (2.0 * math.pi / seq) * t_l * band_ref[...]
    fr = fr_ref[...]
    pre = (w1t_ref[...] * tn_l + _dot_hi(w1c_ref[...], jnp.cos(ang)) - _dot_hi(w1s_ref[...], jnp.sin(ang))
           + b1_ref[...])
    act = jnp.sin(fr * pre)
    act = jnp.sin(fr * (_dot_hi(w2_ref[...], act) + b2_ref[...]))
    act = jnp.sin(fr * (_dot_hi(w3_ref[...], act) + b3_ref[...])).astype(BF16)
    n_s, _, tn_s = positions((nrow, 1), 0)
    delta = jnp.concatenate([delta_ref[...]] * (ncol // width), axis=1)
    hf = _dot_tn(act[:, :nf], w4f_ref[...]) * jnp.exp(-tn_s[:nf] * delta)
    hb = _dot_tn(act[:, nf:], w4b_ref[...]) * jnp.exp(-tn_s[nf:] * delta)
    hb = jnp.where(n_s[nf:] == float(seq), 0.0, hb)
    k_ref[:, :half, :] = hf.reshape(gb, half, ncol).astype(k_ref.dtype)
    k_ref[:, half:, :] = hb.reshape(gb, half, ncol).astype(k_ref.dtype)
    tot = jnp.sum(jnp.abs(hf), axis=0, keepdims=True) + jnp.sum(jnp.abs(hb), axis=0, keepdims=True)

    @pl.when(step == 0)
    def _():
        s_ref[...] = jnp.zeros_like(s_ref)

    s_ref[...] += tot


def hyena_filter_taps(seq, w1, b1, w2, b2, w3, b3, w4, freq, width):
    emb = w1.shape[0]
    hid = w1.shape[1]
    bands = (emb - 1) // 2
    q = 2 * seq // DFT_P
    ncol = HY_ORDER * width
    band = np.linspace(1e-4, bands - 1, bands, dtype=np.float32).reshape(bands, 1)
    min_decay = math.log(HY_DECAY_TARGET) / HY_SLOW_PCT
    max_decay = math.log(HY_DECAY_TARGET) / HY_FAST_PCT
    delta = np.abs(np.linspace(min_decay, max_decay, width, dtype=np.float32)).reshape(1, width)
    w1t = w1.astype(F32).T
    col = lambda v: v.reshape(hid, 1).astype(F32)
    w4r = w4.astype(BF16).reshape(hid, HY_ORDER, 2, width)
    w4f = w4r[:, :, 0, :].reshape(hid, ncol)
    w4b = w4r[:, :, 1, :].reshape(hid, ncol)
    gb = FILTER_GROUP
    const = lambda shape: pl.BlockSpec(shape, lambda i: tuple(0 for _ in shape))
    return pl.pallas_call(
        functools.partial(_filter_kernel, seq=seq),
        out_shape=(jax.ShapeDtypeStruct((DFT_P, q, ncol), BF16),
                   jax.ShapeDtypeStruct((1, ncol), F32)),
        grid=(DFT_P // gb,),
        in_specs=[const((bands, 1)), const((hid, 1)), const((hid, bands)), const((hid, bands)), const((hid, 1)),
                  const((hid, hid)), const((hid, 1)), const((hid, hid)), const((hid, 1)),
                  const((hid, ncol)), const((hid, ncol)), const((hid, 1)), const((1, width))],
        out_specs=(pl.BlockSpec((gb, q, ncol), lambda i: (i, 0, 0)),
                   pl.BlockSpec((1, ncol), lambda i: (0, 0))),
        compiler_params=_cparams("arbitrary"),
        name="hyena_filter",
    )(jnp.asarray(band), w1t[:, 0:1], w1t[:, 1:1 + bands], w1t[:, 1 + bands:1 + 2 * bands], col(b1),
      w2.astype(F32).T, col(b2), w3.astype(F32).T, col(b3), w4f, w4b, col(freq), jnp.asarray(delta))


def _dft_tables(seq):
    p = DFT_P
    n_fft = 2 * seq
    q = n_fft // p
    qh = q // 2
    ka = jnp.arange(q, dtype=I32)
    a = jnp.arange(q, dtype=I32)
    b = jnp.arange(p, dtype=I32)
    nn = a[None, :] * p + b[:, None]
    ph = (ka[None, :, None] * nn[:, None, :]) % n_fft
    ang = ph.astype(F32) * (2.0 * math.pi / n_fft)
    mr, mi = jnp.cos(ang), -jnp.sin(ang)
    m1c = jnp.concatenate([jnp.concatenate([mr[:, :, :qh], -mi[:, :, :qh]], axis=2),
                           jnp.concatenate([mi[:, :, :qh], mr[:, :, :qh]], axis=2)], axis=1)
    m1r = jnp.concatenate([mr, mi], axis=1)
    gr = jnp.swapaxes(mr[:, :, :qh], 1, 2) / n_fft
    gi = -jnp.swapaxes(mi[:, :, :qh], 1, 2) / n_fft
    m4 = jnp.concatenate([jnp.concatenate([gr, -gi], axis=2), jnp.concatenate([gi, gr], axis=2)], axis=1)
    kb = np.arange(p)
    ang2 = 2.0 * np.pi * ((kb[:, None] * kb[None, :]) % p) / p
    fr, fi = np.cos(ang2), -np.sin(ang2)
    m2 = np.block([[fr, -fi], [fi, fr]]).astype(np.float32)
    m3 = np.block([[fr, fi], [-fi, fr]]).astype(np.float32)
    return (m1c.astype(BF16), m1r.astype(BF16), jnp.asarray(m2, BF16), jnp.asarray(m3, BF16), m4.astype(BF16))


def _bmm_kernel(w_ref, x_ref, o_ref, *, shared_w):
    for j in range(x_ref.shape[0]):
        w = w_ref[...] if shared_w else w_ref[j]
        o_ref[j] = _dot(w, x_ref[j]).astype(o_ref.dtype)


def batched_left_matmul(w, x, col_block, ncols, name, gb):
    g, k = x.shape[0], x.shape[1]
    shared = w.ndim == 2
    m = w.shape[-2]
    wspec = (pl.BlockSpec((m, k), lambda i: (0, 0)) if shared
             else pl.BlockSpec((gb, m, k), lambda i: (i, 0, 0)))
    return pl.pallas_call(
        functools.partial(_bmm_kernel, shared_w=shared),
        out_shape=jax.ShapeDtypeStruct((g, m, ncols), BF16),
        grid=(g // gb,),
        in_specs=[wspec, pl.BlockSpec((gb, k, ncols), lambda i: (i, 0, col_block))],
        out_specs=pl.BlockSpec((gb, m, ncols), lambda i: (i, 0, 0)),
        compiler_params=_cparams("arbitrary"),
        name=name,
    )(w, x)


def _dft_mid_kernel(m2_ref, m3_ref, x_ref, k_ref, o_ref):
    half = x_ref.shape[1] // 2
    for j in range(x_ref.shape[0]):
        xf = _dot(m2_ref[...], x_ref[j])
        kf = _dot(m2_ref[...], k_ref[j])
        xr, xi = xf[:half], xf[half:]
        kr, ki = kf[:half], kf[half:]
        z = jnp.concatenate([xr * kr - xi * ki, xr * ki + xi * kr], axis=0).astype(BF16)
        o_ref[j] = _dot(m3_ref[...], z).astype(o_ref.dtype)


def dft_mid(m2, m3, x, kspec, kcol, ncols):
    g, r = x.shape[0], x.shape[1]
    gb = DFT_GROUP
    return pl.pallas_call(
        _dft_mid_kernel,
        out_shape=jax.ShapeDtypeStruct((g, r, ncols), BF16),
        grid=(g // gb,),
        in_specs=[pl.BlockSpec((r, r), lambda i: (0, 0)),
                  pl.BlockSpec((r, r), lambda i: (0, 0)),
                  pl.BlockSpec((gb, r, ncols), lambda i: (i, 0, 0)),
                  pl.BlockSpec((gb, r, ncols), lambda i: (i, 0, kcol))],
        out_specs=pl.BlockSpec((gb, r, ncols), lambda i: (i, 0, 0)),
        compiler_params=_cparams("arbitrary"),
        name="dft_mid",
    )(m2, m3, x, kspec)


def _dft_out_kernel(m4_ref, y_ref, inv_ref, skip_ref, v_ref, mul_ref, o_ref, *, token_order):
    res = []
    for j in range(y_ref.shape[0]):
        conv = _dot(m4_ref[j], y_ref[j]) * inv_ref[...] + v_ref[j].astype(F32) * skip_ref[...]
        res.append(mul_ref[j].astype(F32) * conv)
    if token_order:
        o_ref[...] = jnp.swapaxes(jnp.stack(res, axis=0), 0, 1).astype(o_ref.dtype)
    else:
        for j, r in enumerate(res):
            o_ref[j] = r.astype(o_ref.dtype)


def dft_out(m4, y, inv_l1, skip, u, v_col, mul, mul_col, ncols, token_order):
    g, r = y.shape[0], y.shape[1]
    rows = m4.shape[1]
    gb = DFT_GROUP
    out_shape, out_spec = (((rows, g, ncols), pl.BlockSpec((rows, gb, ncols), lambda i: (0, i, 0))) if token_order
                           else ((g, rows, ncols), pl.BlockSpec((gb, rows, ncols), lambda i: (i, 0, 0))))
    return pl.pallas_call(
        functools.partial(_dft_out_kernel, token_order=token_order),
        out_shape=jax.ShapeDtypeStruct(out_shape, BF16),
        grid=(g // gb,),
        in_specs=[pl.BlockSpec((gb, rows, r), lambda i: (i, 0, 0)),
                  pl.BlockSpec((gb, r, ncols), lambda i: (i, 0, 0)),
                  pl.BlockSpec((1, ncols), lambda i: (0, 0)),
                  pl.BlockSpec((1, ncols), lambda i: (0, 0)),
                  pl.BlockSpec((gb, rows, ncols), lambda i: (i, 0, v_col)),
                  pl.BlockSpec((gb, rows, ncols), lambda i: (i, 0, mul_col))],
        out_specs=out_spec,
        compiler_params=_cparams("arbitrary"),
        name="dft_out",
    )(m4, y, inv_l1, skip, u, mul)


def _swap_ab(x):
    g1, r, c = x.shape
    g2 = r // 2
    return x.reshape(g1, 2, g2, c).transpose(2, 1, 0, 3).reshape(g2, 2 * g1, c)


def hyena_branch(u, bsz, seq, width, taps, l1, skip):
    m1c, m1r, m2, m3, m4 = _dft_tables(seq)
    ncol = HY_ORDER * width
    ks1 = _swap_ab(batched_left_matmul(m1r, taps, 0, ncol, "dft_k1", FILTER_GROUP))
    inv_l1 = 1.0 / l1
    z = None
    for order in range(HY_ORDER):
        src, src_col = (u, 0) if order == 0 else (z, 0)
        s1 = batched_left_matmul(m1c, src, src_col, width, "dft_s1", DFT_GROUP)
        mid = dft_mid(m2, m3, _swap_ab(s1), ks1, order, width)
        z = dft_out(m4, _swap_ab(mid), inv_l1[:, order * width:(order + 1) * width],
                    skip[order].reshape(1, width).astype(F32), src, src_col, u, order + 1, width,
                    token_order=order == HY_ORDER - 1)
    return z.reshape(bsz * seq, width)


def _pack_pairs(x):
    w = x.shape[1] // 2
    u = lax.bitcast_convert_type(x, U32)
    r = (u + U32(0x7FFF) + ((u >> 16) & U32(1))) >> 16
    return r[:, :w] | (r[:, w:] << 16)


def _unpack_pairs(p):
    lo = lax.bitcast_convert_type(p << 16, F32)
    hi = lax.bitcast_convert_type(p & U32(0xFFFF0000), F32)
    return jnp.concatenate([lo, hi], axis=1)


def _pack_pairs_native(x):
    w = x.shape[1] // 2
    return lax.bitcast_convert_type(pltpu.pack_elementwise([x[:, :w], x[:, w:]], packed_dtype=BF16), U32)


def _unpack_pairs_native(p):
    pi = lax.bitcast_convert_type(p, I32)
    halves = [pltpu.unpack_elementwise(pi, index=i, packed_dtype=BF16, unpacked_dtype=F32) for i in range(2)]
    return jnp.concatenate(halves, axis=1)


def _merge_kernel(x_ref, er_ref, ec_ref, yhy_ref, yhg_ref, ghy_ref, ghg_ref, why_ref, whg_ref, wo_ref,
                  g1_ref, n2_ref, sh2_ref, sc2_ref, g2_ref, rwh_ref, rwl_ref, sgu_ref, sd_ref,
                  xres_ref, h2p_ref, lg_ref):
    x = x_ref[...]
    rows, gw, d = x.shape
    half = d // 2
    xp = jnp.concatenate([x[:, :, :half] + er_ref[...], x[:, :, half:] + ec_ref[...]], axis=-1)
    xp = xp.reshape(rows * gw, d)
    m = (jax.nn.sigmoid(ghy_ref[...].astype(F32)) * _dot(yhy_ref[...], why_ref[...])
         + jax.nn.sigmoid(ghg_ref[...].astype(F32)) * _dot(yhg_ref[...], whg_ref[...]))
    x1 = xp + g1_ref[...] * _dot(m.astype(BF16), wo_ref[...])
    ms = jnp.mean(x1 * x1, axis=-1, keepdims=True)
    h2 = x1 * lax.rsqrt(ms + NORM_EPS) * n2_ref[...] * (1.0 + sc2_ref[...]) + sh2_ref[...]
    h_hi, h_lo = _split_bf16(h2)
    lg_ref[...] = _dot_nt(rwh_ref[...], h_hi) + (_dot_nt(rwl_ref[...], h_hi) + _dot_nt(rwh_ref[...], h_lo))
    gu = _dot(h_hi, sgu_ref[...])
    fs = gu.shape[1] // 2
    shared = _dot((_silu(gu[:, :fs]) * gu[:, fs:]).astype(BF16), sd_ref[...])
    xres_ref[...] = x1 + g2_ref[...] * shared
    h2p_ref[...] = _pack_pairs(h2)


def merge_stage(x, emb_r, emb_c, y_hy, y_hg, p, gate_cols, w_hy_out, w_hg_out, w_out, g1, norm2_g,
                sh2, sc2, g2, router_wt, sh_gate_up, sh_down, tm):
    b, s, d = x.shape
    rows_per_batch = s // GRID_W
    rt = tm // GRID_W
    tiles_per_batch = rows_per_batch // rt
    x3 = x.reshape(b * rows_per_batch, GRID_W, d)
    wb = y_hy.shape[1]
    ne = router_wt.shape[0]
    fs2 = sh_gate_up.shape[1]
    rw_hi, rw_lo = _split_bf16(router_wt)
    tok = lambda cb, w: pl.BlockSpec((tm, w), lambda i: (i, cb))
    const = lambda shape: pl.BlockSpec(shape, lambda i: tuple(0 for _ in shape))
    per_b = pl.BlockSpec((None, 1, d), lambda i: (i // tiles_per_batch, 0, 0))
    return pl.pallas_call(
        _merge_kernel,
        out_shape=(jax.ShapeDtypeStruct((b * s, d), F32),
                   jax.ShapeDtypeStruct((b * s, d // 2), U32),
                   jax.ShapeDtypeStruct((ne, b * s), F32)),
        grid=(b * tiles_per_batch,),
        in_specs=[pl.BlockSpec((rt, GRID_W, d), lambda i: (i, 0, 0)),
                  pl.BlockSpec((rt, 1, d // 2), lambda i: (i % tiles_per_batch, 0, 0)),
                  const((GRID_W, d // 2)),
                  tok(0, wb), tok(0, wb), tok(gate_cols[0], d), tok(gate_cols[1], d),
                  const((wb, d)), const((wb, d)), const((d, d)),
                  per_b, const((1, d)), per_b, per_b, per_b,
                  const((ne, d)), const((ne, d)), const((d, fs2)), const((fs2 // 2, d))],
        out_specs=(pl.BlockSpec((tm, d), lambda i: (i, 0)),
                   pl.BlockSpec((tm, d // 2), lambda i: (i, 0)),
                   pl.BlockSpec((ne, tm), lambda i: (0, i))),
        compiler_params=_cparams("arbitrary"),
        name="merge",
    )(x3, emb_r, emb_c, y_hy, y_hg, p, p, w_hy_out, w_hg_out, w_out, g1, norm2_g.reshape(1, d),
      sh2, sc2, g2, rw_hi, rw_lo, sh_gate_up, sh_down)


def _route_kernel(lg_ref, bias_ref, eidx_ref, wsel_ref, rank_ref, cnt_ref, carry):
    ne, tr = lg_ref.shape
    gsz = ne // N_GROUPS
    neg = -jnp.inf

    @pl.when(pl.program_id(0) == 0)
    def _():
        carry[...] = jnp.zeros_like(carry)

    scores = jax.nn.sigmoid(lg_ref[...])
    biased = scores + bias_ref[...]
    riota = lax.broadcasted_iota(I32, (gsz, tr), 0).astype(F32)
    gs = []
    for g in range(N_GROUPS):
        vg = biased[g * gsz:(g + 1) * gsz]
        m1 = jnp.max(vg, axis=0, keepdims=True)
        i1 = jnp.min(jnp.where(vg == m1, riota, float(gsz)), axis=0, keepdims=True)
        m2 = jnp.max(jnp.where(riota == i1, neg, vg), axis=0, keepdims=True)
        gs.append(m1 + m2)
    cur = jnp.concatenate(gs, axis=0)
    giota = lax.broadcasted_iota(I32, (N_GROUPS, tr), 0).astype(F32)
    gsel = jnp.zeros((N_GROUPS, tr), F32)
    for _ in range(TOPK_GROUPS):
        m = jnp.max(cur, axis=0, keepdims=True)
        idx = jnp.min(jnp.where(cur == m, giota, float(N_GROUPS)), axis=0, keepdims=True)
        hit = giota == idx
        gsel = jnp.where(hit, 1.0, gsel)
        cur = jnp.where(hit, neg, cur)
    cur = jnp.concatenate([jnp.where(gsel[g:g + 1] > 0.0, biased[g * gsz:(g + 1) * gsz], neg)
                           for g in range(N_GROUPS)], axis=0)
    eiota = lax.broadcasted_iota(I32, (ne, tr), 0).astype(F32)
    chosen = jnp.zeros((ne, tr), F32)
    idxs, ws = [], []
    for _ in range(TOP_K):
        m = jnp.max(cur, axis=0, keepdims=True)
        idx = jnp.min(jnp.where(cur == m, eiota, float(ne)), axis=0, keepdims=True)
        hit = eiota == idx
        idxs.append(idx)
        ws.append(jnp.sum(jnp.where(hit, scores, 0.0), axis=0, keepdims=True))
        chosen = jnp.where(hit, 1.0, chosen)
        cur = jnp.where(hit, neg, cur)
    w = jnp.concatenate(ws, axis=0)
    wsel_ref[...] = w / jnp.sum(w, axis=0, keepdims=True) * ROUTED_SCALE
    eidx_ref[...] = jnp.concatenate(idxs, axis=0).astype(I32)
    srow = lax.broadcasted_iota(I32, (tr, tr), 0)
    scol = lax.broadcasted_iota(I32, (tr, tr), 1)
    before = (srow < scol).astype(BF16)
    base = carry[...] + _dot(chosen.astype(BF16), before)
    ranks = [jnp.sum(jnp.where(eiota == idx, base, 0.0), axis=0, keepdims=True) for idx in idxs]
    rank_ref[...] = jnp.concatenate(ranks, axis=0).astype(I32)
    carry[...] += jnp.sum(chosen, axis=1, keepdims=True)
    cnt_ref[...] = carry[...]


def route(logits_t, router_bias, tr):
    ne, t = logits_t.shape
    return pl.pallas_call(
        _route_kernel,
        out_shape=(jax.ShapeDtypeStruct((TOP_K, t), I32),
                   jax.ShapeDtypeStruct((TOP_K, t), F32),
                   jax.ShapeDtypeStruct((TOP_K, t), I32),
                   jax.ShapeDtypeStruct((ne, 1), F32)),
        grid=(t // tr,),
        in_specs=[pl.BlockSpec((ne, tr), lambda i: (0, i)),
                  pl.BlockSpec((ne, 1), lambda i: (0, 0))],
        out_specs=(pl.BlockSpec((TOP_K, tr), lambda i: (0, i)),
                   pl.BlockSpec((TOP_K, tr), lambda i: (0, i)),
                   pl.BlockSpec((TOP_K, tr), lambda i: (0, i)),
                   pl.BlockSpec((ne, 1), lambda i: (0, 0))),
        scratch_shapes=[pltpu.VMEM((ne, 1), F32)],
        compiler_params=_cparams("arbitrary"),
        name="route",
    )(logits_t, router_bias.reshape(ne, 1).astype(F32))


def _dest_kernel(cnt_ref, eidx_ref, rank_ref, dest_ref, be_ref, nv_ref, nb_ref, start_scr):
    ne = cnt_ref.shape[0]
    tr = eidx_ref.shape[1]

    @pl.when(pl.program_id(0) == 0)
    def _():
        cnt = jnp.broadcast_to(cnt_ref[...], (ne, LANES))
        padded = jnp.floor((cnt + float(MOE_ROWS - 1)) / float(MOE_ROWS)) * float(MOE_ROWS)
        r = lax.broadcasted_iota(I32, (ne, ne), 0)
        c = lax.broadcasted_iota(I32, (ne, ne), 1)
        start = _dot_hi((c < r).astype(F32), padded)
        start_scr[...] = start
        end = start[:, 0:1] + padded[:, 0:1]
        used = start[:, 0:1] + cnt[:, 0:1]
        nbl = be_ref.shape[1]
        blk_row = (lax.broadcasted_iota(I32, (1, nbl), 1) * MOE_ROWS).astype(F32)
        total = jnp.max(end, axis=0, keepdims=True)
        last_row = total - float(MOE_ROWS)
        blk_row_c = jnp.minimum(blk_row, last_row)
        e_of = jnp.sum((end <= blk_row_c).astype(F32), axis=0, keepdims=True)
        e_of = jnp.minimum(e_of, float(ne - 1))
        eio = lax.broadcasted_iota(I32, (ne, nbl), 0).astype(F32)
        used_e = jnp.sum(jnp.where(eio == e_of, used, 0.0), axis=0, keepdims=True)
        valid = jnp.clip(used_e - blk_row_c, 0.0, float(MOE_ROWS))
        be_ref[...] = e_of.astype(I32)
        nv_ref[...] = jnp.where(blk_row <= last_row, valid, 0.0).astype(I32)
        nb_ref[...] = jnp.broadcast_to(total / float(MOE_ROWS), nb_ref.shape).astype(I32)

    eiota = lax.broadcasted_iota(I32, (ne, tr), 0)
    start_col = start_scr[:, 0:1]
    rows = []
    for k in range(TOP_K):
        hit = eiota == eidx_ref[k:k + 1, :]
        rows.append(jnp.sum(jnp.where(hit, start_col, 0.0), axis=0, keepdims=True))
    dest_ref[...] = jnp.concatenate(rows, axis=0).astype(I32) + rank_ref[...]


def dispatch_plan(counts, eidx, rank, tr, n_blocks):
    ne = counts.shape[0]
    t = eidx.shape[1]
    nbl = pl.cdiv(n_blocks, LANES) * LANES
    return pl.pallas_call(
        _dest_kernel,
        out_shape=(jax.ShapeDtypeStruct((TOP_K, t), I32),
                   jax.ShapeDtypeStruct((1, nbl), I32),
                   jax.ShapeDtypeStruct((1, nbl), I32),
                   jax.ShapeDtypeStruct((1, LANES), I32)),
        grid=(t // tr,),
        in_specs=[pl.BlockSpec((ne, 1), lambda i: (0, 0)),
                  pl.BlockSpec((TOP_K, tr), lambda i: (0, i)),
                  pl.BlockSpec((TOP_K, tr), lambda i: (0, i))],
        out_specs=(pl.BlockSpec((TOP_K, tr), lambda i: (0, i)),
                   pl.BlockSpec((1, nbl), lambda i: (0, 0)),
                   pl.BlockSpec((1, nbl), lambda i: (0, 0)),
                   pl.BlockSpec((1, LANES), lambda i: (0, 0))),
        scratch_shapes=[pltpu.VMEM((ne, LANES), F32)],
        compiler_params=_cparams("arbitrary"),
        name="dispatch_plan",
    )(counts, eidx, rank)


def _sc_workers():
    info = plsc.get_sparse_core_info()
    return info.num_cores, info.num_cores * info.num_subcores


def scatter_rows(dest_flat, h2p, n_rows):
    t, w = h2p.shape
    n_cores, n_workers = _sc_workers()
    per_worker = t // n_workers
    mesh = plsc.VectorSubcoreMesh(core_axis_name="c", subcore_axis_name="s")

    @functools.partial(
        pl.kernel, mesh=mesh, out_type=jax.ShapeDtypeStruct((n_rows, w), U32),
        scratch_types=[pltpu.VMEM((TOP_K, SC_ROWS), I32), pltpu.VMEM((SC_ROWS, w), U32), pltpu.SemaphoreType.DMA])
    def body(h_hbm, dest_hbm, xs_hbm, idx_v, rows_v, sem):
        base = (lax.axis_index("s") * n_cores + lax.axis_index("c")) * per_worker

        @pl.loop(0, per_worker // SC_ROWS)
        def _(ci):
            off = pl.multiple_of(base + ci * SC_ROWS, SC_ROWS)
            pltpu.sync_copy(h_hbm.at[pl.ds(off, SC_ROWS)], rows_v)
            for k in range(TOP_K):
                pltpu.sync_copy(dest_hbm.at[pl.ds(k * t + off, SC_ROWS)], idx_v.at[k])
            copies = [pltpu.async_copy(rows_v, xs_hbm.at[idx_v.at[k]], sem) for k in range(TOP_K)]
            for c in copies:
                c.wait()

    return body(h2p, dest_flat)


def gather_rows(idx_flat, table):
    n = idx_flat.shape[0]
    w = table.shape[1]
    n_cores, n_workers = _sc_workers()
    per_worker = n // n_workers
    mesh = plsc.VectorSubcoreMesh(core_axis_name="c", subcore_axis_name="s")

    ch = SC_ROWS // 2
    n_chunks = per_worker // ch

    @functools.partial(
        pl.kernel, mesh=mesh, out_type=jax.ShapeDtypeStruct((n, w), table.dtype),
        scratch_types=[pltpu.VMEM((2, ch), I32), pltpu.VMEM((2, ch, w), table.dtype), pltpu.SemaphoreType.DMA((2,))])
    def body(table_hbm, idx_hbm, out_hbm, idx_v, rows_v, sem):
        base = (lax.axis_index("s") * n_cores + lax.axis_index("c")) * per_worker

        def read(b):
            return pltpu.make_async_copy(table_hbm.at[idx_v.at[b]], rows_v.at[b], sem.at[b])

        def start(c, b):
            off = pl.multiple_of(base + c * ch, ch)
            pltpu.sync_copy(idx_hbm.at[pl.ds(off, ch)], idx_v.at[b])
            read(b).start()

        def finish(c, b):
            read(b).wait()
            pltpu.sync_copy(rows_v.at[b], out_hbm.at[pl.ds(pl.multiple_of(base + c * ch, ch), ch)])

        start(0, 0)

        @pl.loop(0, n_chunks, step=2)
        def _(c):
            start(c + 1, 1)
            finish(c, 0)

            @pl.when(c + 2 < n_chunks)
            def _():
                start(c + 2, 0)

            finish(c + 1, 1)

    return body(table, idx_flat)


def _gmm_kernel(be_ref, nv_ref, nb_ref, first_ref, slot_ref, nxt1_ref, nxt2_ref, xs_ref, wg_hbm, wu_hbm, wd_hbm,
                y_ref, wg_buf, wu_buf, wd_buf, sem):
    def weight_copies(e, s):
        return (pltpu.make_async_copy(wg_hbm.at[e], wg_buf.at[s], sem.at[s]),
                pltpu.make_async_copy(wu_hbm.at[e], wu_buf.at[s], sem.at[s]),
                pltpu.make_async_copy(wd_hbm.at[e], wd_buf.at[s], sem.at[s]))

    def one_block(j, rows):
        s = slot_ref[j]

        def fetch(e, slot):
            @pl.when(e >= 0)
            def _():
                for c in weight_copies(e, slot):
                    c.start()

        @pl.when(j == 0)
        def _():
            fetch(be_ref[0], 0)
            fetch(nxt1_ref[0], 1)

        @pl.when(first_ref[j] == 1)
        def _():
            for c in weight_copies(be_ref[j], s):
                c.wait()
            fetch(nxt2_ref[j], (s + WEIGHT_SLOTS - 1) % WEIGHT_SLOTS)

        x = _unpack_pairs_native(xs_ref[rows, :])
        row = lax.broadcasted_iota(I32, (x.shape[0], 1), 0)
        x = jnp.where(row < nv_ref[j], x, 0.0)
        hmid = _silu(_dot(x, wg_buf[s])) * _dot(x, wu_buf[s])
        y_ref[rows, :] = _pack_pairs_native(_dot(hmid, wd_buf[s]))

    for sb in range(MOE_STEP_BLOCKS):
        j = pl.program_id(0) * MOE_STEP_BLOCKS + sb
        pl.when(j < nb_ref[0])(functools.partial(one_block, j, slice(sb * MOE_ROWS, (sb + 1) * MOE_ROWS)))


def grouped_mlp(block_e, block_valid, n_used, xs, w_gate, w_up, w_down, n_blocks):
    ne, d, f = w_gate.shape
    w = xs.shape[1]
    jj = jnp.arange(block_e.shape[0], dtype=I32)
    active = jj < n_used[0]
    first = (active & ((jj == 0) | (block_e != jnp.roll(block_e, 1)))).astype(I32)
    run = jnp.cumsum(first) - 1
    slot = (run % WEIGHT_SLOTS).astype(I32)
    nbl = block_e.shape[0]
    run_expert = jnp.full((nbl + WEIGHT_SLOTS,), -1, I32).at[jnp.where(first == 1, run, nbl)].set(
        jnp.where(first == 1, block_e, -1))
    nxt1 = run_expert[jnp.minimum(run + 1, nbl)]
    nxt2 = run_expert[jnp.minimum(run + 2, nbl + 1)]
    step_rows = MOE_STEP_BLOCKS * MOE_ROWS
    last = lambda g, nb: jnp.minimum(g, (nb[0] - 1) // MOE_STEP_BLOCKS)
    row_block = pl.BlockSpec((step_rows, w), lambda g, be, nv, nb, fi, sl, n1, n2: (last(g, nb), 0))
    grid_spec = pltpu.PrefetchScalarGridSpec(
        num_scalar_prefetch=7,
        grid=(n_blocks // MOE_STEP_BLOCKS,),
        in_specs=[row_block,
                  pl.BlockSpec(memory_space=pl.ANY),
                  pl.BlockSpec(memory_space=pl.ANY),
                  pl.BlockSpec(memory_space=pl.ANY)],
        out_specs=row_block,
        scratch_shapes=[pltpu.VMEM((WEIGHT_SLOTS, d, f), F32), pltpu.VMEM((WEIGHT_SLOTS, d, f), F32),
                        pltpu.VMEM((WEIGHT_SLOTS, f, d), F32), pltpu.SemaphoreType.DMA((WEIGHT_SLOTS,))],
    )
    return pl.pallas_call(
        _gmm_kernel,
        out_shape=jax.ShapeDtypeStruct(xs.shape, U32),
        grid_spec=grid_spec,
        compiler_params=_cparams("arbitrary"),
        name="grouped_mlp",
    )(block_e, block_valid, n_used, first, slot, nxt1, nxt2, xs, w_gate, w_up, w_down)


def _combine_kernel(y_ref, xres_ref, wt_ref, g2_ref, fg_ref, o_ref):
    wt = wt_ref[...]
    routed = jnp.zeros(xres_ref.shape, F32)
    for k in range(TOP_K):
        routed = routed + wt[:, k:k + 1] * _unpack_pairs_native(y_ref[k])
    x2 = xres_ref[...] + g2_ref[...] * routed
    ms = jnp.mean(x2 * x2, axis=-1, keepdims=True)
    o_ref[...] = x2 * lax.rsqrt(ms + NORM_EPS) * fg_ref[...]


def _combine_into_kernel(prev_ref, *refs):
    del prev_ref
    _combine_kernel(*refs)


def combine(y_tok, xres, wsel_t, g2, final_g, seq, tm, first_tile, prev_out):
    t, d = xres.shape
    tiles_per_batch = seq // tm
    tile = lambda i: i + first_tile
    in_specs = [pl.BlockSpec((TOP_K, tm, d // 2), lambda i: (0, i, 0)),
                pl.BlockSpec((tm, d), lambda i: (tile(i), 0)),
                pl.BlockSpec((tm, TOP_K), lambda i: (tile(i), 0)),
                pl.BlockSpec((None, 1, d), lambda i: (tile(i) // tiles_per_batch, 0, 0)),
                pl.BlockSpec((1, d), lambda i: (0, 0))]
    args = [y_tok, xres, wsel_t, g2, final_g.reshape(1, d)]
    body, aliases = _combine_kernel, {}
    if prev_out is not None:
        body, aliases = _combine_into_kernel, {0: 0}
        in_specs = [pl.BlockSpec(memory_space=pl.ANY)] + in_specs
        args = [prev_out] + args
    return pl.pallas_call(
        body,
        out_shape=jax.ShapeDtypeStruct((t, d), F32),
        grid=(y_tok.shape[1] // tm,),
        in_specs=in_specs,
        out_specs=pl.BlockSpec((tm, d), lambda i: (tile(i), 0)),
        input_output_aliases=aliases,
        compiler_params=_cparams("arbitrary"),
        name="combine",
    )(*args)


def _pos_tables(rows, cols, dim):
    quarter = dim // 4
    omega = 1.0 / (POS_BASE ** (np.arange(quarter, dtype=np.float32) / quarter))
    ang_r = np.arange(rows, dtype=np.float32)[:, None] * omega
    ang_c = np.arange(cols, dtype=np.float32)[:, None] * omega
    emb_r = np.concatenate([np.sin(ang_r), np.cos(ang_r)], axis=-1).astype(np.float32)
    emb_c = np.concatenate([np.sin(ang_c), np.cos(ang_c)], axis=-1).astype(np.float32)
    return jnp.asarray(emb_r.reshape(rows, 1, dim // 2)), jnp.asarray(emb_c)


def kernel(x, c, ctx, c_ctx, norm1_g, norm2_g, ada_w, ada_b, w_in, hy_conv_w, hy_conv_b, hy_f_w1, hy_f_b1, hy_f_w2, hy_f_b2, hy_f_w3, hy_f_b3, hy_f_w4, hy_f_freq, hy_skip, hg_lb_logits, hg_norm_g, w_hy_out, w_hg_out, w_out, router_w, router_bias, exp_w_gate, exp_w_up, exp_w_down, sh_w_gate, sh_w_up, sh_w_down, final_g):
    bsz, seq, d = x.shape
    n_ctx = ctx.shape[1]
    hy_w = w_hy_out.shape[1]
    hg_w = w_hg_out.shape[1]
    dk = hg_norm_g.shape[1]
    n_heads = hg_w // dk
    ne = router_w.shape[2]
    l = 0

    c_rows = jnp.zeros((SUBLANES, d), F32).at[:bsz].set(c).at[bsz].set(c_ctx)
    mods = ada_vectors(c_rows, ada_w[l], ada_b[l])
    sh1, sc1, g1, sh2, sc2, g2 = [mods[:bsz, j * d:(j + 1) * d].reshape(bsz, 1, d) for j in range(N_ADA)]
    csh1 = jnp.broadcast_to(mods[bsz, 0:d].reshape(1, 1, d), (bsz, 1, d))
    csc1 = jnp.broadcast_to(mods[bsz, d:2 * d].reshape(1, 1, d), (bsz, 1, d))

    emb_r, emb_c = _pos_tables(seq // GRID_W, GRID_W, d)
    w_in_b = w_in[l].astype(BF16)
    hy_proj = 3 * hy_w
    p = in_projection(x, emb_r, emb_c, norm1_g[l], sh1, sc1, w_in_b, TOKEN_TILE)
    hg_cols = slice(hy_proj, hy_proj + 5 * hg_w)
    zero_r = jnp.zeros((n_ctx // GRID_W, 1, d // 2), F32)
    zero_c = jnp.zeros((GRID_W, d // 2), F32)
    pc = in_projection(ctx, zero_r, zero_c, norm1_g[l], csh1, csc1, w_in_b[:, hg_cols], n_ctx)

    lbs = jnp.cumsum(jax.nn.softmax(hg_lb_logits.astype(F32), axis=0), axis=0)
    lb_f, lb_b = lbs[l, 0], lbs[l, 1]
    zero_state = jnp.zeros((bsz, n_heads, dk, dk), F32)
    base = hy_proj // hg_w
    _, st_f = hgrn_scan(pc, (0, 1, 2), lb_f, zero_state, n_ctx, n_ctx, reverse=False)
    _, st_b = hgrn_scan(pc, (0, 1, 3), lb_b, zero_state, n_ctx, n_ctx, reverse=True)
    o_f, _ = hgrn_scan(p, (base, base + 1, base + 2), lb_f, st_f, seq, HG_TIME_BLOCK, reverse=False)
    y_hg, _ = hgrn_scan(p, (base, base + 1, base + 3), lb_b, st_b, seq, HG_TIME_BLOCK, reverse=True,
                        o_fwd=o_f, gate_col=base + 4, norm_g=hg_norm_g[l])

    u = short_conv(p, hy_proj, hy_conv_w[l], hy_conv_b[l], seq, CONV_TILE, hy_w)
    taps, l1 = hyena_filter_taps(seq, hy_f_w1[l], hy_f_b1[l], hy_f_w2[l], hy_f_b2[l], hy_f_w3[l], hy_f_b3[l],
                                 hy_f_w4[l], hy_f_freq[l], hy_w)
    y_hy = hyena_branch(u, bsz, seq, hy_w, taps, l1, hy_skip[l])

    gate_base = (hy_proj + 5 * hg_w) // d
    sh_gu = jnp.concatenate([sh_w_gate[l], sh_w_up[l]], axis=1).astype(BF16)
    xres, h2p, logits_t = merge_stage(
        x, emb_r, emb_c, y_hy, y_hg, p, (gate_base, gate_base + 1),
        w_hy_out[l].astype(BF16), w_hg_out[l].astype(BF16), w_out[l].astype(BF16), g1, norm2_g[l],
        sh2, sc2, g2, router_w[l].T.astype(F32), sh_gu, sh_w_down[l].astype(BF16), TOKEN_TILE)

    t = bsz * seq
    eidx, wsel, rank, counts = route(logits_t, router_bias[l], TOKEN_TILE)
    n_rows = t * TOP_K + ne * (MOE_ROWS - 1)
    n_blocks = pl.cdiv(pl.cdiv(n_rows, MOE_ROWS), MOE_STEP_BLOCKS) * MOE_STEP_BLOCKS
    dest, block_e, block_valid, n_used = dispatch_plan(counts, eidx, rank, TOKEN_TILE, n_blocks)

    dest_flat = dest.reshape(-1)
    xs = scatter_rows(dest_flat, h2p, n_blocks * MOE_ROWS)
    ys = grouped_mlp(block_e.reshape(-1), block_valid.reshape(-1), n_used.reshape(-1)[:1], xs,
                     exp_w_gate[l], exp_w_up[l], exp_w_down[l], n_blocks)
    wsel_t = wsel.T
    out = None
    for h in range(GATHER_SPLIT):
        lo = h * (t // GATHER_SPLIT)
        rng = dest[:, lo:lo + t // GATHER_SPLIT]
        y_tok = gather_rows(rng.reshape(-1), ys).reshape(TOP_K, t // GATHER_SPLIT, d // 2)
        out = combine(y_tok, xres, wsel_t, g2, final_g, seq, TOKEN_TILE, lo // TOKEN_TILE, out)
    return out.reshape(bsz, seq, d)
```

```python
import functools
import math

import numpy as np
import jax
import jax.numpy as jnp
from jax import lax
from jax.experimental import pallas as pl
from jax.experimental.pallas import tpu as pltpu
from jax.experimental.pallas import tpu_sc as plsc

F32 = jnp.float32
BF16 = jnp.bfloat16
U32 = jnp.uint32
I32 = jnp.int32
HIGHEST = lax.Precision.HIGHEST

GRID_W = 64
POS_BASE = 10000.0
NORM_EPS = 1e-6
N_ADA = 6
HY_ORDER = 2
HY_SHORT = 3
HY_DECAY_TARGET = 1e-2
HY_FAST_PCT = 0.3
HY_SLOW_PCT = 1.5
HG_HEADS = 4
HG_CHUNK = 64
N_GROUPS = 8
TOPK_GROUPS = 4
TOP_K = 8
ROUTED_SCALE = 2.5

LANES = 128
SUBLANES = 8
VMEM_LIMIT = 56 * 1024 * 1024

TOKEN_TILE = 512
HG_TIME_BLOCK = 512
HALO_ROWS = 16
CONV_TILE = 2048
DFT_P = 128
FILTER_GROUP = 8
DFT_GROUP = 16
MOE_ROWS = 256
MOE_STEP_BLOCKS = 4
WEIGHT_SLOTS = 3
SC_ROWS = 128
GATHER_SPLIT = 4


def _cparams(*sem):
    return pltpu.CompilerParams(dimension_semantics=sem, vmem_limit_bytes=VMEM_LIMIT)


def _dot(a, b):
    return jnp.dot(a, b, preferred_element_type=F32)


def _dot_hi(a, b):
    return jnp.dot(a, b, preferred_element_type=F32, precision=HIGHEST)


def _dot_nt(a, b):
    return lax.dot_general(a, b, (((1,), (1,)), ((), ())), preferred_element_type=F32)


def _dot_tn(a, b):
    return lax.dot_general(a, b, (((0,), (0,)), ((), ())), preferred_element_type=F32)


def _silu(x):
    return x * jax.nn.sigmoid(x)


def _split_bf16(x):
    hi = x.astype(BF16)
    return hi, (x - hi.astype(F32)).astype(BF16)


def _ada_kernel(c_ref, w_ref, b_ref, o_ref):
    o_ref[...] = _dot_hi(_silu(c_ref[...]), w_ref[...]) + b_ref[...]


def ada_vectors(c_rows, ada_w, ada_b):
    r, d = c_rows.shape
    n = ada_w.shape[1]
    bn = 1024
    return pl.pallas_call(
        _ada_kernel,
        out_shape=jax.ShapeDtypeStruct((r, n), F32),
        grid=(n // bn,),
        in_specs=[pl.BlockSpec((r, d), lambda j: (0, 0)),
                  pl.BlockSpec((d, bn), lambda j: (0, j)),
                  pl.BlockSpec((1, bn), lambda j: (0, j))],
        out_specs=pl.BlockSpec((r, bn), lambda j: (0, j)),
        compiler_params=_cparams("arbitrary"),
        name="ada_vectors",
    )(c_rows, ada_w, ada_b.reshape(1, n))


def _inproj_kernel(x_ref, er_ref, ec_ref, g_ref, sh_ref, sc_ref, w_ref, o_ref, *, col_chunk):
    x = x_ref[...]
    rows, gw, d = x.shape
    half = d // 2
    xp = jnp.concatenate([x[:, :, :half] + er_ref[...], x[:, :, half:] + ec_ref[...]], axis=-1)
    xp = xp.reshape(rows * gw, d)
    ms = jnp.mean(xp * xp, axis=-1, keepdims=True)
    y = xp * lax.rsqrt(ms + NORM_EPS) * g_ref[...]
    h = (y * (1.0 + sc_ref[...]) + sh_ref[...]).astype(BF16)
    n = o_ref.shape[1]
    for j in range(n // col_chunk):
        sl = slice(j * col_chunk, (j + 1) * col_chunk)
        o_ref[:, sl] = _dot(h, w_ref[:, sl]).astype(o_ref.dtype)


def in_projection(x, emb_r, emb_c, norm_g, shift, scale, w_bf16, tm):
    b, s, d = x.shape
    n = w_bf16.shape[1]
    rows_per_batch = s // GRID_W
    rt = tm // GRID_W
    tiles_per_batch = rows_per_batch // rt
    x3 = x.reshape(b * rows_per_batch, GRID_W, d)
    col_chunk = 512
    return pl.pallas_call(
        functools.partial(_inproj_kernel, col_chunk=col_chunk),
        out_shape=jax.ShapeDtypeStruct((b * s, n), BF16),
        grid=(b * tiles_per_batch,),
        in_specs=[pl.BlockSpec((rt, GRID_W, d), lambda i: (i, 0, 0)),
                  pl.BlockSpec((rt, 1, d // 2), lambda i: (i % tiles_per_batch, 0, 0)),
                  pl.BlockSpec((GRID_W, d // 2), lambda i: (0, 0)),
                  pl.BlockSpec((1, d), lambda i: (0, 0)),
                  pl.BlockSpec((None, 1, d), lambda i: (i // tiles_per_batch, 0, 0)),
                  pl.BlockSpec((None, 1, d), lambda i: (i // tiles_per_batch, 0, 0)),
                  pl.BlockSpec((d, n), lambda i: (0, 0))],
        out_specs=pl.BlockSpec((tm, n), lambda i: (i, 0)),
        compiler_params=_cparams("arbitrary"),
        name="in_projection",
    )(x3, emb_r, emb_c, norm_g.reshape(1, d), shift, scale, w_bf16)


def _hgrn_kernel(*refs, reverse, n_chunks, final):
    if final:
        (q_ref, i_ref, f_ref, lb_ref, s0_ref, of_ref, gate_ref, ng_ref, o_ref, sfin_ref, s_scr) = refs
    else:
        (q_ref, i_ref, f_ref, lb_ref, s0_ref, o_ref, sfin_ref, s_scr) = refs
    cs = HG_CHUNK
    bsz, n_heads, _, dk = s_scr.shape

    @pl.when(pl.program_id(0) == 0)
    def _():
        s_scr[...] = s0_ref[...]

    row = lax.broadcasted_iota(I32, (cs, cs), 0)
    col = lax.broadcasted_iota(I32, (cs, cs), 1)
    tri = (col >= row) if reverse else (col <= row)
    tri_b = tri.astype(BF16)
    end_row = 0 if reverse else cs - 1
    mid_row = cs // 2 if reverse else cs // 2 - 1

    def chunk_body(bi, ci):
        c = (n_chunks - 1 - ci) if reverse else ci
        rows = slice(c * cs, (c + 1) * cs)
        lb = lb_ref[...]
        f = lb + (1.0 - lb) * jax.nn.sigmoid(f_ref[bi, rows, :].astype(F32))
        lf_hi, lf_lo = _split_bf16(jnp.log(f))
        b_all = _dot(tri_b, lf_hi) + _dot(tri_b, lf_lo)
        k_all = 1.0 - f
        q_all = _silu(q_ref[bi, rows, :].astype(F32))
        for h in range(n_heads):
            sl = slice(h * dk, (h + 1) * dk)
            b = b_all[:, sl]
            q = q_all[:, sl]
            k = k_all[:, sl]
            v = i_ref[bi, rows, sl]
            b_end = b[end_row:end_row + 1]
            b_mid = b[mid_row:mid_row + 1]
            qd = (q * jnp.exp(b - b_mid)).astype(BF16)
            kd = (k * jnp.exp(b_mid - b)).astype(BF16)
            att = jnp.where(tri, _dot_nt(qd, kd), 0.0).astype(BF16)
            st = s_scr[bi, h]
            qe = (q * jnp.exp(b)).astype(BF16)
            o = _dot(att, v) + _dot_nt(qe, st.astype(BF16))
            ke = (k * jnp.exp(b_end - b)).astype(BF16)
            s_scr[bi, h] = st * jnp.exp(b_end) + _dot_tn(v, ke)
            if final:
                o = o + of_ref[bi, rows, sl].astype(F32)
                o = o * lax.rsqrt(jnp.mean(o * o, axis=-1, keepdims=True) + NORM_EPS) * ng_ref[...]
                o = o * _silu(gate_ref[bi, rows, sl].astype(F32))
            o_ref[bi, rows, sl] = o.astype(o_ref.dtype)

    for ci in range(n_chunks):
        for bi in range(bsz):
            chunk_body(bi, ci)
    sfin_ref[...] = s_scr[...]


def hgrn_scan(p, cols, lb, s0, seq, tb, *, reverse, o_fwd=None, gate_col=None, norm_g=None):
    bsz, n_heads, dv, dk = s0.shape
    width = n_heads * dk
    nt = seq // tb
    final = o_fwd is not None
    p3 = p.reshape(bsz, seq, p.shape[1])
    tmap = (lambda t: nt - 1 - t) if reverse else (lambda t: t)
    colspec = lambda cb: pl.BlockSpec((bsz, tb, width), lambda t: (0, tmap(t), cb))
    state = pl.BlockSpec((bsz, n_heads, dv, dk), lambda t: (0, 0, 0, 0))
    in_specs = [colspec(cols[0]), colspec(cols[1]), colspec(cols[2]),
                pl.BlockSpec((1, width), lambda t: (0, 0)), state]
    args = [p3, p3, p3, lb.reshape(1, width), s0]
    if final:
        in_specs += [colspec(0), colspec(gate_col), pl.BlockSpec((1, dk), lambda t: (0, 0))]
        args += [o_fwd.reshape(bsz, seq, width), p3, norm_g.reshape(1, dk)]
    o, s_fin = pl.pallas_call(
        functools.partial(_hgrn_kernel, reverse=reverse, n_chunks=tb // HG_CHUNK, final=final),
        out_shape=(jax.ShapeDtypeStruct((bsz, seq, width), BF16),
                   jax.ShapeDtypeStruct((bsz, n_heads, dv, dk), F32)),
        grid=(nt,),
        in_specs=in_specs,
        out_specs=(colspec(0), state),
        scratch_shapes=[pltpu.VMEM((bsz, n_heads, dv, dk), F32)],
        compiler_params=_cparams("arbitrary"),
        name="hgrn_bwd" if reverse else "hgrn_fwd",
    )(*args)
    return o.reshape(bsz * seq, width), s_fin


def _shortconv_kernel(p_ref, prev_ref, next_ref, w_ref, b_ref, o_ref, *, tiles_per_batch):
    i = pl.program_id(0)
    ti = i % tiles_per_batch
    p = p_ref[...].astype(F32)
    tm = p.shape[0]
    row = lax.broadcasted_iota(I32, (tm, 1), 0)
    prev_row = jnp.where(ti == 0, 0.0, prev_ref[HALO_ROWS - 1:HALO_ROWS, :].astype(F32))
    next_row = jnp.where(ti == tiles_per_batch - 1, 0.0, next_ref[0:1, :].astype(F32))
    p_prev = jnp.where(row == 0, prev_row, pltpu.roll(p, 1, axis=0))
    p_next = jnp.where(row == tm - 1, next_row, pltpu.roll(p, tm - 1, axis=0))
    u = w_ref[0:1, :] * p_prev + w_ref[1:2, :] * p + w_ref[2:3, :] * p_next + b_ref[...]
    o_ref[...] = jnp.swapaxes(u.reshape(tm // DFT_P, DFT_P, u.shape[1]), 0, 1).astype(o_ref.dtype)


def short_conv(p, width, conv_w, conv_b, seq, tm, cw):
    t = p.shape[0]
    nt = t // tm
    tiles_per_batch = seq // tm
    sub = tm // HALO_ROWS
    ta = tm // DFT_P
    return pl.pallas_call(
        functools.partial(_shortconv_kernel, tiles_per_batch=tiles_per_batch),
        out_shape=jax.ShapeDtypeStruct((DFT_P, t // DFT_P, width), BF16),
        grid=(nt, width // cw),
        in_specs=[pl.BlockSpec((tm, cw), lambda i, j: (i, j)),
                  pl.BlockSpec((HALO_ROWS, cw), lambda i, j: (jnp.maximum(i * sub - 1, 0), j)),
                  pl.BlockSpec((HALO_ROWS, cw), lambda i, j: (jnp.minimum((i + 1) * sub, t // HALO_ROWS - 1), j)),
                  pl.BlockSpec((HY_SHORT, cw), lambda i, j: (0, j)),
                  pl.BlockSpec((1, cw), lambda i, j: (0, j))],
        out_specs=pl.BlockSpec((DFT_P, ta, cw), lambda i, j: (0, i, j)),
        compiler_params=_cparams("arbitrary", "arbitrary"),
        name="short_conv",
    )(p, p, p, conv_w, conv_b.reshape(1, width))


def _filter_kernel(band_ref, w1t_ref, w1c_ref, w1s_ref, b1_ref, w2_ref, b2_ref, w3_ref, b3_ref,
                   w4f_ref, w4b_ref, fr_ref, delta_ref, k_ref, s_ref, *, seq):
    step = pl.program_id(0)
    gb, q, ncol = k_ref.shape
    half = q // 2
    width = delta_ref.shape[1]
    nrow = gb * q
    nf = gb * half

    def positions(shape, axis):
        r = lax.broadcasted_iota(I32, shape, axis)
        is_bwd = r >= nf
        rr = jnp.where(is_bwd, r - nf, r)
        j = lax.shift_right_logical(rr, int(math.log2(half)))
        a = (rr & (half - 1)) + jnp.where(is_bwd, half, 0)
        n = (a * DFT_P + step * gb + j).astype(F32)
        t = jnp.where(is_bwd, 2.0 * seq - n, n)
        return n, t, t / float(max(seq - 1, 1))

    _, t_l, tn_l = positions((1, nrow), 1)
    ang = (2.0 * math.pi / seq) * t_l * band_ref[...]
    fr = fr_ref[...]
    pre = (w1t_ref[...] * tn_l + _dot_hi(w1c_ref[...], jnp.cos(ang)) - _dot_hi(w1s_ref[...], jnp.sin(ang))
           + b1_ref[...])
    act = jnp.sin(fr * pre)
    act = jnp.sin(fr * (_dot_hi(w2_ref[...], act) + b2_ref[...]))
    act = jnp.sin(fr * (_dot_hi(w3_ref[...], act) + b3_ref[...])).astype(BF16)
    n_s, _, tn_s = positions((nrow, 1), 0)
    delta = jnp.concatenate([delta_ref[...]] * (ncol // width), axis=1)
    hf = _dot_tn(act[:, :nf], w4f_ref[...]) * jnp.exp(-tn_s[:nf] * delta)
    hb = _dot_tn(act[:, nf:], w4b_ref[...]) * jnp.exp(-tn_s[nf:] * delta)
    hb = jnp.where(n_s[nf:] == float(seq), 0.0, hb)
    k_ref[:, :half, :] = hf.reshape(gb, half, ncol).astype(k_ref.dtype)
    k_ref[:, half:, :] = hb.reshape(gb, half, ncol).astype(k_ref.dtype)
    tot = jnp.sum(jnp.abs(hf), axis=0, keepdims=True) + jnp.sum(jnp.abs(hb), axis=0, keepdims=True)

    @pl.when(step == 0)
    def _():
        s_ref[...] = jnp.zeros_like(s_ref)

    s_ref[...] += tot


def hyena_filter_taps(seq, w1, b1, w2, b2, w3, b3, w4, freq, width):
    emb = w1.shape[0]
    hid = w1.shape[1]
    bands = (emb - 1) // 2
    q = 2 * seq // DFT_P
    ncol = HY_ORDER * width
    band = np.linspace(1e-4, bands - 1, bands, dtype=np.float32).reshape(bands, 1)
    min_decay = math.log(HY_DECAY_TARGET) / HY_SLOW_PCT
    max_decay = math.log(HY_DECAY_TARGET) / HY_FAST_PCT
    delta = np.abs(np.linspace(min_decay, max_decay, width, dtype=np.float32)).reshape(1, width)
    w1t = w1.astype(F32).T
    col = lambda v: v.reshape(hid, 1).astype(F32)
    w4r = w4.astype(BF16).reshape(hid, HY_ORDER, 2, width)
    w4f = w4r[:, :, 0, :].reshape(hid, ncol)
    w4b = w4r[:, :, 1, :].reshape(hid, ncol)
    gb = FILTER_GROUP
    const = lambda shape: pl.BlockSpec(shape, lambda i: tuple(0 for _ in shape))
    return pl.pallas_call(
        functools.partial(_filter_kernel, seq=seq),
        out_shape=(jax.ShapeDtypeStruct((DFT_P, q, ncol), BF16),
                   jax.ShapeDtypeStruct((1, ncol), F32)),
        grid=(DFT_P // gb,),
        in_specs=[const((bands, 1)), const((hid, 1)), const((hid, bands)), const((hid, bands)), const((hid, 1)),
                  const((hid, hid)), const((hid, 1)), const((hid, hid)), const((hid, 1)),
                  const((hid, ncol)), const((hid, ncol)), const((hid, 1)), const((1, width))],
        out_specs=(pl.BlockSpec((gb, q, ncol), lambda i: (i, 0, 0)),
                   pl.BlockSpec((1, ncol), lambda i: (0, 0))),
        compiler_params=_cparams("arbitrary"),
        name="hyena_filter",
    )(jnp.asarray(band), w1t[:, 0:1], w1t[:, 1:1 + bands], w1t[:, 1 + bands:1 + 2 * bands], col(b1),
      w2.astype(F32).T, col(b2), w3.astype(F32).T, col(b3), w4f, w4b, col(freq), jnp.asarray(delta))


def _dft_tables(seq):
    p = DFT_P
    n_fft = 2 * seq
    q = n_fft // p
    qh = q // 2
    ka = np.arange(q)
    nn = np.arange(q)[None, :] * p + np.arange(p)[:, None]
    ang = ((ka[None, :, None] * nn[:, None, :]) % n_fft) * (2.0 * np.pi / n_fft)
    mr, mi = np.cos(ang), -np.sin(ang)
    m1c = np.concatenate([np.concatenate([mr[:, :, :qh], -mi[:, :, :qh]], axis=2),
                          np.concatenate([mi[:, :, :qh], mr[:, :, :qh]], axis=2)], axis=1)
    m1r = np.concatenate([mr, mi], axis=1)
    gr = np.swapaxes(mr[:, :, :qh], 1, 2) / n_fft
    gi = -np.swapaxes(mi[:, :, :qh], 1, 2) / n_fft
    m4 = np.concatenate([np.concatenate([gr, -gi], axis=2), np.concatenate([gi, gr], axis=2)], axis=1)
    kb = np.arange(p)
    ang2 = 2.0 * np.pi * ((kb[:, None] * kb[None, :]) % p) / p
    fr, fi = np.cos(ang2), -np.sin(ang2)
    m2 = np.block([[fr, -fi], [fi, fr]])
    m3 = np.block([[fr, fi], [-fi, fr]])
    return tuple(jnp.asarray(m.astype(np.float32).astype(BF16)) for m in (m1c, m1r, m2, m3, m4))


def _bmm_kernel(w_ref, x_ref, o_ref, *, shared_w):
    for j in range(x_ref.shape[0]):
        w = w_ref[...] if shared_w else w_ref[j]
        o_ref[j] = _dot(w, x_ref[j]).astype(o_ref.dtype)


def batched_left_matmul(w, x, col_block, ncols, name, gb):
    g, k = x.shape[0], x.shape[1]
    shared = w.ndim == 2
    m = w.shape[-2]
    wspec = (pl.BlockSpec((m, k), lambda i: (0, 0)) if shared
             else pl.BlockSpec((gb, m, k), lambda i: (i, 0, 0)))
    return pl.pallas_call(
        functools.partial(_bmm_kernel, shared_w=shared),
        out_shape=jax.ShapeDtypeStruct((g, m, ncols), BF16),
        grid=(g // gb,),
        in_specs=[wspec, pl.BlockSpec((gb, k, ncols), lambda i: (i, 0, col_block))],
        out_specs=pl.BlockSpec((gb, m, ncols), lambda i: (i, 0, 0)),
        compiler_params=_cparams("arbitrary"),
        name=name,
    )(w, x)


def _dft_mid_kernel(m2_ref, m3_ref, x_ref, k_ref, o_ref):
    half = x_ref.shape[1] // 2
    for j in range(x_ref.shape[0]):
        xf = _dot(m2_ref[...], x_ref[j])
        kf = _dot(m2_ref[...], k_ref[j])
        xr, xi = xf[:half], xf[half:]
        kr, ki = kf[:half], kf[half:]
        z = jnp.concatenate([xr * kr - xi * ki, xr * ki + xi * kr], axis=0).astype(BF16)
        o_ref[j] = _dot(m3_ref[...], z).astype(o_ref.dtype)


def dft_mid(m2, m3, x, kspec, kcol, ncols):
    g, r = x.shape[0], x.shape[1]
    gb = DFT_GROUP
    return pl.pallas_call(
        _dft_mid_kernel,
        out_shape=jax.ShapeDtypeStruct((g, r, ncols), BF16),
        grid=(g // gb,),
        in_specs=[pl.BlockSpec((r, r), lambda i: (0, 0)),
                  pl.BlockSpec((r, r), lambda i: (0, 0)),
                  pl.BlockSpec((gb, r, ncols), lambda i: (i, 0, 0)),
                  pl.BlockSpec((gb, r, ncols), lambda i: (i, 0, kcol))],
        out_specs=pl.BlockSpec((gb, r, ncols), lambda i: (i, 0, 0)),
        compiler_params=_cparams("arbitrary"),
        name="dft_mid",
    )(m2, m3, x, kspec)


def _dft_out_kernel(m4_ref, y_ref, inv_ref, skip_ref, v_ref, mul_ref, o_ref, *, token_order):
    res = []
    for j in range(y_ref.shape[0]):
        conv = _dot(m4_ref[j], y_ref[j]) * inv_ref[...] + v_ref[j].astype(F32) * skip_ref[...]
        res.append(mul_ref[j].astype(F32) * conv)
    if token_order:
        o_ref[...] = jnp.swapaxes(jnp.stack(res, axis=0), 0, 1).astype(o_ref.dtype)
    else:
        for j, r in enumerate(res):
            o_ref[j] = r.astype(o_ref.dtype)


def dft_out(m4, y, inv_l1, skip, u, v_col, mul, mul_col, ncols, token_order):
    g, r = y.shape[0], y.shape[1]
    rows = m4.shape[1]
    gb = DFT_GROUP
    out_shape, out_spec = (((rows, g, ncols), pl.BlockSpec((rows, gb, ncols), lambda i: (0, i, 0))) if token_order
                           else ((g, rows, ncols), pl.BlockSpec((gb, rows, ncols), lambda i: (i, 0, 0))))
    return pl.pallas_call(
        functools.partial(_dft_out_kernel, token_order=token_order),
        out_shape=jax.ShapeDtypeStruct(out_shape, BF16),
        grid=(g // gb,),
        in_specs=[pl.BlockSpec((gb, rows, r), lambda i: (i, 0, 0)),
                  pl.BlockSpec((gb, r, ncols), lambda i: (i, 0, 0)),
                  pl.BlockSpec((1, ncols), lambda i: (0, 0)),
                  pl.BlockSpec((1, ncols), lambda i: (0, 0)),
                  pl.BlockSpec((gb, rows, ncols), lambda i: (i, 0, v_col)),
                  pl.BlockSpec((gb, rows, ncols), lambda i: (i, 0, mul_col))],
        out_specs=out_spec,
        compiler_params=_cparams("arbitrary"),
        name="dft_out",
    )(m4, y, inv_l1, skip, u, mul)


def _swap_ab(x):
    g1, r, c = x.shape
    g2 = r // 2
    return x.reshape(g1, 2, g2, c).transpose(2, 1, 0, 3).reshape(g2, 2 * g1, c)


def hyena_branch(u, bsz, seq, width, taps, l1, skip):
    m1c, m1r, m2, m3, m4 = _dft_tables(seq)
    ncol = HY_ORDER * width
    ks1 = _swap_ab(batched_left_matmul(m1r, taps, 0, ncol, "dft_k1", FILTER_GROUP))
    inv_l1 = 1.0 / l1
    z = None
    for order in range(HY_ORDER):
        src, src_col = (u, 0) if order == 0 else (z, 0)
        s1 = batched_left_matmul(m1c, src, src_col, width, "dft_s1", DFT_GROUP)
        mid = dft_mid(m2, m3, _swap_ab(s1), ks1, order, width)
        z = dft_out(m4, _swap_ab(mid), inv_l1[:, order * width:(order + 1) * width],
                    skip[order].reshape(1, width).astype(F32), src, src_col, u, order + 1, width,
                    token_order=order == HY_ORDER - 1)
    return z.reshape(bsz * seq, width)


def _pack_pairs(x):
    w = x.shape[1] // 2
    u = lax.bitcast_convert_type(x, U32)
    r = (u + U32(0x7FFF) + ((u >> 16) & U32(1))) >> 16
    return r[:, :w] | (r[:, w:] << 16)


def _unpack_pairs(p):
    lo = lax.bitcast_convert_type(p << 16, F32)
    hi = lax.bitcast_convert_type(p & U32(0xFFFF0000), F32)
    return jnp.concatenate([lo, hi], axis=1)


def _pack_pairs_native(x):
    w = x.shape[1] // 2
    return lax.bitcast_convert_type(pltpu.pack_elementwise([x[:, :w], x[:, w:]], packed_dtype=BF16), U32)


def _unpack_pairs_native(p):
    pi = lax.bitcast_convert_type(p, I32)
    halves = [pltpu.unpack_elementwise(pi, index=i, packed_dtype=BF16, unpacked_dtype=F32) for i in range(2)]
    return jnp.concatenate(halves, axis=1)


def _merge_kernel(x_ref, er_ref, ec_ref, yhy_ref, yhg_ref, ghy_ref, ghg_ref, why_ref, whg_ref, wo_ref,
                  g1_ref, n2_ref, sh2_ref, sc2_ref, g2_ref, rwh_ref, rwl_ref, sgu_ref, sd_ref,
                  xres_ref, h2p_ref, lg_ref):
    x = x_ref[...]
    rows, gw, d = x.shape
    half = d // 2
    xp = jnp.concatenate([x[:, :, :half] + er_ref[...], x[:, :, half:] + ec_ref[...]], axis=-1)
    xp = xp.reshape(rows * gw, d)
    m = (jax.nn.sigmoid(ghy_ref[...].astype(F32)) * _dot(yhy_ref[...], why_ref[...])
         + jax.nn.sigmoid(ghg_ref[...].astype(F32)) * _dot(yhg_ref[...], whg_ref[...]))
    x1 = xp + g1_ref[...] * _dot(m.astype(BF16), wo_ref[...])
    ms = jnp.mean(x1 * x1, axis=-1, keepdims=True)
    h2 = x1 * lax.rsqrt(ms + NORM_EPS) * n2_ref[...] * (1.0 + sc2_ref[...]) + sh2_ref[...]
    h_hi, h_lo = _split_bf16(h2)
    lg_ref[...] = _dot_nt(rwh_ref[...], h_hi) + (_dot_nt(rwl_ref[...], h_hi) + _dot_nt(rwh_ref[...], h_lo))
    gu = _dot(h_hi, sgu_ref[...])
    fs = gu.shape[1] // 2
    shared = _dot((_silu(gu[:, :fs]) * gu[:, fs:]).astype(BF16), sd_ref[...])
    xres_ref[...] = x1 + g2_ref[...] * shared
    h2p_ref[...] = _pack_pairs(h2)


def merge_stage(x, emb_r, emb_c, y_hy, y_hg, p, gate_cols, w_hy_out, w_hg_out, w_out, g1, norm2_g,
                sh2, sc2, g2, router_wt, sh_gate_up, sh_down, tm):
    b, s, d = x.shape
    rows_per_batch = s // GRID_W
    rt = tm // GRID_W
    tiles_per_batch = rows_per_batch // rt
    x3 = x.reshape(b * rows_per_batch, GRID_W, d)
    wb = y_hy.shape[1]
    ne = router_wt.shape[0]
    fs2 = sh_gate_up.shape[1]
    rw_hi, rw_lo = _split_bf16(router_wt)
    tok = lambda cb, w: pl.BlockSpec((tm, w), lambda i: (i, cb))
    const = lambda shape: pl.BlockSpec(shape, lambda i: tuple(0 for _ in shape))
    per_b = pl.BlockSpec((None, 1, d), lambda i: (i // tiles_per_batch, 0, 0))
    return pl.pallas_call(
        _merge_kernel,
        out_shape=(jax.ShapeDtypeStruct((b * s, d), F32),
                   jax.ShapeDtypeStruct((b * s, d // 2), U32),
                   jax.ShapeDtypeStruct((ne, b * s), F32)),
        grid=(b * tiles_per_batch,),
        in_specs=[pl.BlockSpec((rt, GRID_W, d), lambda i: (i, 0, 0)),
                  pl.BlockSpec((rt, 1, d // 2), lambda i: (i % tiles_per_batch, 0, 0)),
                  const((GRID_W, d // 2)),
                  tok(0, wb), tok(0, wb), tok(gate_cols[0], d), tok(gate_cols[1], d),
                  const((wb, d)), const((wb, d)), const((d, d)),
                  per_b, const((1, d)), per_b, per_b, per_b,
                  const((ne, d)), const((ne, d)), const((d, fs2)), const((fs2 // 2, d))],
        out_specs=(pl.BlockSpec((tm, d), lambda i: (i, 0)),
                   pl.BlockSpec((tm, d // 2), lambda i: (i, 0)),
                   pl.BlockSpec((ne, tm), lambda i: (0, i))),
        compiler_params=_cparams("arbitrary"),
        name="merge",
    )(x3, emb_r, emb_c, y_hy, y_hg, p, p, w_hy_out, w_hg_out, w_out, g1, norm2_g.reshape(1, d),
      sh2, sc2, g2, rw_hi, rw_lo, sh_gate_up, sh_down)


def _route_kernel(lg_ref, bias_ref, eidx_ref, wsel_ref, rank_ref, cnt_ref, carry):
    ne, tr = lg_ref.shape
    gsz = ne // N_GROUPS
    neg = -jnp.inf

    @pl.when(pl.program_id(0) == 0)
    def _():
        carry[...] = jnp.zeros_like(carry)

    scores = jax.nn.sigmoid(lg_ref[...])
    biased = scores + bias_ref[...]
    riota = lax.broadcasted_iota(I32, (gsz, tr), 0).astype(F32)
    gs = []
    for g in range(N_GROUPS):
        vg = biased[g * gsz:(g + 1) * gsz]
        m1 = jnp.max(vg, axis=0, keepdims=True)
        i1 = jnp.min(jnp.where(vg == m1, riota, float(gsz)), axis=0, keepdims=True)
        m2 = jnp.max(jnp.where(riota == i1, neg, vg), axis=0, keepdims=True)
        gs.append(m1 + m2)
    cur = jnp.concatenate(gs, axis=0)
    giota = lax.broadcasted_iota(I32, (N_GROUPS, tr), 0).astype(F32)
    gsel = jnp.zeros((N_GROUPS, tr), F32)
    for _ in range(TOPK_GROUPS):
        m = jnp.max(cur, axis=0, keepdims=True)
        idx = jnp.min(jnp.where(cur == m, giota, float(N_GROUPS)), axis=0, keepdims=True)
        hit = giota == idx
        gsel = jnp.where(hit, 1.0, gsel)
        cur = jnp.where(hit, neg, cur)
    cur = jnp.concatenate([jnp.where(gsel[g:g + 1] > 0.0, biased[g * gsz:(g + 1) * gsz], neg)
                           for g in range(N_GROUPS)], axis=0)
    eiota = lax.broadcasted_iota(I32, (ne, tr), 0).astype(F32)
    chosen = jnp.zeros((ne, tr), F32)
    idxs, ws = [], []
    for _ in range(TOP_K):
        m = jnp.max(cur, axis=0, keepdims=True)
        idx = jnp.min(jnp.where(cur == m, eiota, float(ne)), axis=0, keepdims=True)
        hit = eiota == idx
        idxs.append(idx)
        ws.append(jnp.sum(jnp.where(hit, scores, 0.0), axis=0, keepdims=True))
        chosen = jnp.where(hit, 1.0, chosen)
        cur = jnp.where(hit, neg, cur)
    w = jnp.concatenate(ws, axis=0)
    wsel_ref[...] = w / jnp.sum(w, axis=0, keepdims=True) * ROUTED_SCALE
    eidx_ref[...] = jnp.concatenate(idxs, axis=0).astype(I32)
    srow = lax.broadcasted_iota(I32, (tr, tr), 0)
    scol = lax.broadcasted_iota(I32, (tr, tr), 1)
    before = (srow < scol).astype(BF16)
    base = carry[...] + _dot(chosen.astype(BF16), before)
    ranks = [jnp.sum(jnp.where(eiota == idx, base, 0.0), axis=0, keepdims=True) for idx in idxs]
    rank_ref[...] = jnp.concatenate(ranks, axis=0).astype(I32)
    carry[...] += jnp.sum(chosen, axis=1, keepdims=True)
    cnt_ref[...] = carry[...]


def route(logits_t, router_bias, tr):
    ne, t = logits_t.shape
    return pl.pallas_call(
        _route_kernel,
        out_shape=(jax.ShapeDtypeStruct((TOP_K, t), I32),
                   jax.ShapeDtypeStruct((TOP_K, t), F32),
                   jax.ShapeDtypeStruct((TOP_K, t), I32),
                   jax.ShapeDtypeStruct((ne, 1), F32)),
        grid=(t // tr,),
        in_specs=[pl.BlockSpec((ne, tr), lambda i: (0, i)),
                  pl.BlockSpec((ne, 1), lambda i: (0, 0))],
        out_specs=(pl.BlockSpec((TOP_K, tr), lambda i: (0, i)),
                   pl.BlockSpec((TOP_K, tr), lambda i: (0, i)),
                   pl.BlockSpec((TOP_K, tr), lambda i: (0, i)),
                   pl.BlockSpec((ne, 1), lambda i: (0, 0))),
        scratch_shapes=[pltpu.VMEM((ne, 1), F32)],
        compiler_params=_cparams("arbitrary"),
        name="route",
    )(logits_t, router_bias.reshape(ne, 1).astype(F32))


def _dest_kernel(cnt_ref, eidx_ref, rank_ref, dest_ref, be_ref, nv_ref, nb_ref, start_scr):
    ne = cnt_ref.shape[0]
    tr = eidx_ref.shape[1]

    @pl.when(pl.program_id(0) == 0)
    def _():
        cnt = jnp.broadcast_to(cnt_ref[...], (ne, LANES))
        padded = jnp.floor((cnt + float(MOE_ROWS - 1)) / float(MOE_ROWS)) * float(MOE_ROWS)
        r = lax.broadcasted_iota(I32, (ne, ne), 0)
        c = lax.broadcasted_iota(I32, (ne, ne), 1)
        start = _dot_hi((c < r).astype(F32), padded)
        start_scr[...] = start
        end = start[:, 0:1] + padded[:, 0:1]
        used = start[:, 0:1] + cnt[:, 0:1]
        nbl = be_ref.shape[1]
        blk_row = (lax.broadcasted_iota(I32, (1, nbl), 1) * MOE_ROWS).astype(F32)
        total = jnp.max(end, axis=0, keepdims=True)
        last_row = total - float(MOE_ROWS)
        blk_row_c = jnp.minimum(blk_row, last_row)
        e_of = jnp.sum((end <= blk_row_c).astype(F32), axis=0, keepdims=True)
        e_of = jnp.minimum(e_of, float(ne - 1))
        eio = lax.broadcasted_iota(I32, (ne, nbl), 0).astype(F32)
        used_e = jnp.sum(jnp.where(eio == e_of, used, 0.0), axis=0, keepdims=True)
        valid = jnp.clip(used_e - blk_row_c, 0.0, float(MOE_ROWS))
        be_ref[...] = e_of.astype(I32)
        nv_ref[...] = jnp.where(blk_row <= last_row, valid, 0.0).astype(I32)
        nb_ref[...] = jnp.broadcast_to(total / float(MOE_ROWS), nb_ref.shape).astype(I32)

    eiota = lax.broadcasted_iota(I32, (ne, tr), 0)
    start_col = start_scr[:, 0:1]
    rows = []
    for k in range(TOP_K):
        hit = eiota == eidx_ref[k:k + 1, :]
        rows.append(jnp.sum(jnp.where(hit, start_col, 0.0), axis=0, keepdims=True))
    dest_ref[...] = jnp.concatenate(rows, axis=0).astype(I32) + rank_ref[...]


def dispatch_plan(counts, eidx, rank, tr, n_blocks):
    ne = counts.shape[0]
    t = eidx.shape[1]
    nbl = pl.cdiv(n_blocks, LANES) * LANES
    return pl.pallas_call(
        _dest_kernel,
        out_shape=(jax.ShapeDtypeStruct((TOP_K, t), I32),
                   jax.ShapeDtypeStruct((1, nbl), I32),
                   jax.ShapeDtypeStruct((1, nbl), I32),
                   jax.ShapeDtypeStruct((1, LANES), I32)),
        grid=(t // tr,),
        in_specs=[pl.BlockSpec((ne, 1), lambda i: (0, 0)),
                  pl.BlockSpec((TOP_K, tr), lambda i: (0, i)),
                  pl.BlockSpec((TOP_K, tr), lambda i: (0, i))],
        out_specs=(pl.BlockSpec((TOP_K, tr), lambda i: (0, i)),
                   pl.BlockSpec((1, nbl), lambda i: (0, 0)),
                   pl.BlockSpec((1, nbl), lambda i: (0, 0)),
                   pl.BlockSpec((1, LANES), lambda i: (0, 0))),
        scratch_shapes=[pltpu.VMEM((ne, LANES), F32)],
        compiler_params=_cparams("arbitrary"),
        name="dispatch_plan",
    )(counts, eidx, rank)


def _sc_workers():
    info = plsc.get_sparse_core_info()
    return info.num_cores, info.num_cores * info.num_subcores


def scatter_rows(dest_flat, h2p, n_rows):
    t, w = h2p.shape
    n_cores, n_workers = _sc_workers()
    per_worker = t // n_workers
    mesh = plsc.VectorSubcoreMesh(core_axis_name="c", subcore_axis_name="s")

    @functools.partial(
        pl.kernel, mesh=mesh, out_type=jax.ShapeDtypeStruct((n_rows, w), U32),
        scratch_types=[pltpu.VMEM((TOP_K, SC_ROWS), I32), pltpu.VMEM((SC_ROWS, w), U32), pltpu.SemaphoreType.DMA])
    def body(h_hbm, dest_hbm, xs_hbm, idx_v, rows_v, sem):
        base = (lax.axis_index("s") * n_cores + lax.axis_index("c")) * per_worker

        @pl.loop(0, per_worker // SC_ROWS)
        def _(ci):
            off = pl.multiple_of(base + ci * SC_ROWS, SC_ROWS)
            pltpu.sync_copy(h_hbm.at[pl.ds(off, SC_ROWS)], rows_v)
            for k in range(TOP_K):
                pltpu.sync_copy(dest_hbm.at[pl.ds(k * t + off, SC_ROWS)], idx_v.at[k])
            copies = [pltpu.async_copy(rows_v, xs_hbm.at[idx_v.at[k]], sem) for k in range(TOP_K)]
            for c in copies:
                c.wait()

    return body(h2p, dest_flat)


def gather_rows(idx_flat, table):
    n = idx_flat.shape[0]
    w = table.shape[1]
    n_cores, n_workers = _sc_workers()
    per_worker = n // n_workers
    mesh = plsc.VectorSubcoreMesh(core_axis_name="c", subcore_axis_name="s")

    ch = SC_ROWS // 2
    n_chunks = per_worker // ch

    @functools.partial(
        pl.kernel, mesh=mesh, out_type=jax.ShapeDtypeStruct((n, w), table.dtype),
        scratch_types=[pltpu.VMEM((2, ch), I32), pltpu.VMEM((2, ch, w), table.dtype), pltpu.SemaphoreType.DMA((2,))])
    def body(table_hbm, idx_hbm, out_hbm, idx_v, rows_v, sem):
        base = (lax.axis_index("s") * n_cores + lax.axis_index("c")) * per_worker

        def read(b):
            return pltpu.make_async_copy(table_hbm.at[idx_v.at[b]], rows_v.at[b], sem.at[b])

        def start(c, b):
            off = pl.multiple_of(base + c * ch, ch)
            pltpu.sync_copy(idx_hbm.at[pl.ds(off, ch)], idx_v.at[b])
            read(b).start()

        def finish(c, b):
            read(b).wait()
            pltpu.sync_copy(rows_v.at[b], out_hbm.at[pl.ds(pl.multiple_of(base + c * ch, ch), ch)])

        start(0, 0)

        @pl.loop(0, n_chunks, step=2)
        def _(c):
            start(c + 1, 1)
            finish(c, 0)

            @pl.when(c + 2 < n_chunks)
            def _():
                start(c + 2, 0)

            finish(c + 1, 1)

    return body(table, idx_flat)


def _gmm_kernel(be_ref, nv_ref, nb_ref, first_ref, slot_ref, nxt1_ref, nxt2_ref, xs_ref, wg_hbm, wu_hbm, wd_hbm,
                y_ref, wg_buf, wu_buf, wd_buf, sem):
    def weight_copies(e, s):
        return (pltpu.make_async_copy(wg_hbm.at[e], wg_buf.at[s], sem.at[s]),
                pltpu.make_async_copy(wu_hbm.at[e], wu_buf.at[s], sem.at[s]),
                pltpu.make_async_copy(wd_hbm.at[e], wd_buf.at[s], sem.at[s]))

    def one_block(j, rows):
        s = slot_ref[j]

        def fetch(e, slot):
            @pl.when(e >= 0)
            def _():
                for c in weight_copies(e, slot):
                    c.start()

        @pl.when(j == 0)
        def _():
            fetch(be_ref[0], 0)
            fetch(nxt1_ref[0], 1)

        @pl.when(first_ref[j] == 1)
        def _():
            for c in weight_copies(be_ref[j], s):
                c.wait()
            fetch(nxt2_ref[j], (s + WEIGHT_SLOTS - 1) % WEIGHT_SLOTS)

        x = _unpack_pairs_native(xs_ref[rows, :])
        row = lax.broadcasted_iota(I32, (x.shape[0], 1), 0)
        x = jnp.where(row < nv_ref[j], x, 0.0)
        hmid = _silu(_dot(x, wg_buf[s])) * _dot(x, wu_buf[s])
        y_ref[rows, :] = _pack_pairs_native(_dot(hmid, wd_buf[s]))

    for sb in range(MOE_STEP_BLOCKS):
        j = pl.program_id(0) * MOE_STEP_BLOCKS + sb
        pl.when(j < nb_ref[0])(functools.partial(one_block, j, slice(sb * MOE_ROWS, (sb + 1) * MOE_ROWS)))


def grouped_mlp(block_e, block_valid, n_used, xs, w_gate, w_up, w_down, n_blocks):
    ne, d, f = w_gate.shape
    w = xs.shape[1]
    jj = jnp.arange(block_e.shape[0], dtype=I32)
    active = jj < n_used[0]
    first = (active & ((jj == 0) | (block_e != jnp.roll(block_e, 1)))).astype(I32)
    run = jnp.cumsum(first) - 1
    slot = (run % WEIGHT_SLOTS).astype(I32)
    nbl = block_e.shape[0]
    run_expert = jnp.full((nbl + WEIGHT_SLOTS,), -1, I32).at[jnp.where(first == 1, run, nbl)].set(
        jnp.where(first == 1, block_e, -1))
    nxt1 = run_expert[jnp.minimum(run + 1, nbl)]
    nxt2 = run_expert[jnp.minimum(run + 2, nbl + 1)]
    step_rows = MOE_STEP_BLOCKS * MOE_ROWS
    last = lambda g, nb: jnp.minimum(g, (nb[0] - 1) // MOE_STEP_BLOCKS)
    row_block = pl.BlockSpec((step_rows, w), lambda g, be, nv, nb, fi, sl, n1, n2: (last(g, nb), 0))
    grid_spec = pltpu.PrefetchScalarGridSpec(
        num_scalar_prefetch=7,
        grid=(n_blocks // MOE_STEP_BLOCKS,),
        in_specs=[row_block,
                  pl.BlockSpec(memory_space=pl.ANY),
                  pl.BlockSpec(memory_space=pl.ANY),
                  pl.BlockSpec(memory_space=pl.ANY)],
        out_specs=row_block,
        scratch_shapes=[pltpu.VMEM((WEIGHT_SLOTS, d, f), F32), pltpu.VMEM((WEIGHT_SLOTS, d, f), F32),
                        pltpu.VMEM((WEIGHT_SLOTS, f, d), F32), pltpu.SemaphoreType.DMA((WEIGHT_SLOTS,))],
    )
    return pl.pallas_call(
        _gmm_kernel,
        out_shape=jax.ShapeDtypeStruct(xs.shape, U32),
        grid_spec=grid_spec,
        compiler_params=_cparams("arbitrary"),
        name="grouped_mlp",
    )(block_e, block_valid, n_used, first, slot, nxt1, nxt2, xs, w_gate, w_up, w_down)


def _combine_kernel(y_ref, xres_ref, wt_ref, g2_ref, fg_ref, o_ref):
    wt = wt_ref[...]
    routed = jnp.zeros(xres_ref.shape, F32)
    for k in range(TOP_K):
        routed = routed + wt[:, k:k + 1] * _unpack_pairs_native(y_ref[k])
    x2 = xres_ref[...] + g2_ref[...] * routed
    ms = jnp.mean(x2 * x2, axis=-1, keepdims=True)
    o_ref[...] = x2 * lax.rsqrt(ms + NORM_EPS) * fg_ref[...]


def _combine_into_kernel(prev_ref, *refs):
    del prev_ref
    _combine_kernel(*refs)


def combine(y_tok, xres, wsel_t, g2, final_g, seq, tm, first_tile, prev_out):
    t, d = xres.shape
    tiles_per_batch = seq // tm
    tile = lambda i: i + first_tile
    in_specs = [pl.BlockSpec((TOP_K, tm, d // 2), lambda i: (0, i, 0)),
                pl.BlockSpec((tm, d), lambda i: (tile(i), 0)),
                pl.BlockSpec((tm, TOP_K), lambda i: (tile(i), 0)),
                pl.BlockSpec((None, 1, d), lambda i: (tile(i) // tiles_per_batch, 0, 0)),
                pl.BlockSpec((1, d), lambda i: (0, 0))]
    args = [y_tok, xres, wsel_t, g2, final_g.reshape(1, d)]
    body, aliases = _combine_kernel, {}
    if prev_out is not None:
        body, aliases = _combine_into_kernel, {0: 0}
        in_specs = [pl.BlockSpec(memory_space=pl.ANY)] + in_specs
        args = [prev_out] + args
    return pl.pallas_call(
        body,
        out_shape=jax.ShapeDtypeStruct((t, d), F32),
        grid=(y_tok.shape[1] // tm,),
        in_specs=in_specs,
        out_specs=pl.BlockSpec((tm, d), lambda i: (tile(i), 0)),
        input_output_aliases=aliases,
        compiler_params=_cparams("arbitrary"),
        name="combine",
    )(*args)


def _pos_tables(rows, cols, dim):
    quarter = dim // 4
    omega = 1.0 / (POS_BASE ** (np.arange(quarter, dtype=np.float32) / quarter))
    ang_r = np.arange(rows, dtype=np.float32)[:, None] * omega
    ang_c = np.arange(cols, dtype=np.float32)[:, None] * omega
    emb_r = np.concatenate([np.sin(ang_r), np.cos(ang_r)], axis=-1).astype(np.float32)
    emb_c = np.concatenate([np.sin(ang_c), np.cos(ang_c)], axis=-1).astype(np.float32)
    return jnp.asarray(emb_r.reshape(rows, 1, dim // 2)), jnp.asarray(emb_c)


def kernel(x, c, ctx, c_ctx, norm1_g, norm2_g, ada_w, ada_b, w_in, hy_conv_w, hy_conv_b, hy_f_w1, hy_f_b1, hy_f_w2, hy_f_b2, hy_f_w3, hy_f_b3, hy_f_w4, hy_f_freq, hy_skip, hg_lb_logits, hg_norm_g, w_hy_out, w_hg_out, w_out, router_w, router_bias, exp_w_gate, exp_w_up, exp_w_down, sh_w_gate, sh_w_up, sh_w_down, final_g):
    bsz, seq, d = x.shape
    n_ctx = ctx.shape[1]
    hy_w = w_hy_out.shape[1]
    hg_w = w_hg_out.shape[1]
    dk = hg_norm_g.shape[1]
    n_heads = hg_w // dk
    ne = router_w.shape[2]
    l = 0

    c_rows = jnp.zeros((SUBLANES, d), F32).at[:bsz].set(c).at[bsz].set(c_ctx)
    mods = ada_vectors(c_rows, ada_w[l], ada_b[l])
    sh1, sc1, g1, sh2, sc2, g2 = [mods[:bsz, j * d:(j + 1) * d].reshape(bsz, 1, d) for j in range(N_ADA)]
    csh1 = jnp.broadcast_to(mods[bsz, 0:d].reshape(1, 1, d), (bsz, 1, d))
    csc1 = jnp.broadcast_to(mods[bsz, d:2 * d].reshape(1, 1, d), (bsz, 1, d))

    emb_r, emb_c = _pos_tables(seq // GRID_W, GRID_W, d)
    w_in_b = w_in[l].astype(BF16)
    hy_proj = 3 * hy_w
    p = in_projection(x, emb_r, emb_c, norm1_g[l], sh1, sc1, w_in_b, TOKEN_TILE)
    hg_cols = slice(hy_proj, hy_proj + 5 * hg_w)
    zero_r = jnp.zeros((n_ctx // GRID_W, 1, d // 2), F32)
    zero_c = jnp.zeros((GRID_W, d // 2), F32)
    pc = in_projection(ctx, zero_r, zero_c, norm1_g[l], csh1, csc1, w_in_b[:, hg_cols], n_ctx)

    lbs = jnp.cumsum(jax.nn.softmax(hg_lb_logits.astype(F32), axis=0), axis=0)
    lb_f, lb_b = lbs[l, 0], lbs[l, 1]
    zero_state = jnp.zeros((bsz, n_heads, dk, dk), F32)
    base = hy_proj // hg_w
    _, st_f = hgrn_scan(pc, (0, 1, 2), lb_f, zero_state, n_ctx, n_ctx, reverse=False)
    _, st_b = hgrn_scan(pc, (0, 1, 3), lb_b, zero_state, n_ctx, n_ctx, reverse=True)
    o_f, _ = hgrn_scan(p, (base, base + 1, base + 2), lb_f, st_f, seq, HG_TIME_BLOCK, reverse=False)
    y_hg, _ = hgrn_scan(p, (base, base + 1, base + 3), lb_b, st_b, seq, HG_TIME_BLOCK, reverse=True,
                        o_fwd=o_f, gate_col=base + 4, norm_g=hg_norm_g[l])

    u = short_conv(p, hy_proj, hy_conv_w[l], hy_conv_b[l], seq, CONV_TILE, hy_w)
    taps, l1 = hyena_filter_taps(seq, hy_f_w1[l], hy_f_b1[l], hy_f_w2[l], hy_f_b2[l], hy_f_w3[l], hy_f_b3[l],
                                 hy_f_w4[l], hy_f_freq[l], hy_w)
    y_hy = hyena_branch(u, bsz, seq, hy_w, taps, l1, hy_skip[l])

    gate_base = (hy_proj + 5 * hg_w) // d
    sh_gu = jnp.concatenate([sh_w_gate[l], sh_w_up[l]], axis=1).astype(BF16)
    xres, h2p, logits_t = merge_stage(
        x, emb_r, emb_c, y_hy, y_hg, p, (gate_base, gate_base + 1),
        w_hy_out[l].astype(BF16), w_hg_out[l].astype(BF16), w_out[l].astype(BF16), g1, norm2_g[l],
        sh2, sc2, g2, router_w[l].T.astype(F32), sh_gu, sh_w_down[l].astype(BF16), TOKEN_TILE)

    t = bsz * seq
    eidx, wsel, rank, counts = route(logits_t, router_bias[l], TOKEN_TILE)
    n_rows = t * TOP_K + ne * (MOE_ROWS - 1)
    n_blocks = pl.cdiv(pl.cdiv(n_rows, MOE_ROWS), MOE_STEP_BLOCKS) * MOE_STEP_BLOCKS
    dest, block_e, block_valid, n_used = dispatch_plan(counts, eidx, rank, TOKEN_TILE, n_blocks)

    dest_flat = dest.reshape(-1)
    xs = scatter_rows(dest_flat, h2p, n_blocks * MOE_ROWS)
    ys = grouped_mlp(block_e.reshape(-1), block_valid.reshape(-1), n_used.reshape(-1)[:1], xs,
                     exp_w_gate[l], exp_w_up[l], exp_w_down[l], n_blocks)
    wsel_t = wsel.T
    out = None
    for h in range(GATHER_SPLIT):
        lo = h * (t // GATHER_SPLIT)
        rng = dest[:, lo:lo + t // GATHER_SPLIT]
        y_tok = gather_rows(rng.reshape(-1), ys).reshape(TOP_K, t // GATHER_SPLIT, d // 2)
        out = combine(y_tok, xres, wsel_t, g2, final_g, seq, TOKEN_TILE, lo // TOKEN_TILE, out)
    return out.reshape(bsz, seq, d)
```

```python
import functools
import math

import numpy as np
import jax
import jax.numpy as jnp
from jax import lax
from jax.experimental import pallas as pl
from jax.experimental.pallas import tpu as pltpu
from jax.experimental.pallas import tpu_sc as plsc

F32 = jnp.float32
BF16 = jnp.bfloat16
U32 = jnp.uint32
I32 = jnp.int32
HIGHEST = lax.Precision.HIGHEST

GRID_W = 64
POS_BASE = 10000.0
NORM_EPS = 1e-6
N_ADA = 6
HY_ORDER = 2
HY_SHORT = 3
HY_DECAY_TARGET = 1e-2
HY_FAST_PCT = 0.3
HY_SLOW_PCT = 1.5
HG_HEADS = 4
HG_CHUNK = 64
N_GROUPS = 8
TOPK_GROUPS = 4
TOP_K = 8
ROUTED_SCALE = 2.5

LANES = 128
SUBLANES = 8
VMEM_LIMIT = 56 * 1024 * 1024

TOKEN_TILE = 512
HG_TIME_BLOCK = 512
HALO_ROWS = 16
CONV_TILE = 2048
DFT_P = 128
FILTER_GROUP = 8
DFT_GROUP = 16
MOE_ROWS = 256
MOE_STEP_BLOCKS = 4
WEIGHT_AHEAD = 3
WEIGHT_SLOTS = WEIGHT_AHEAD + MOE_STEP_BLOCKS
SC_ROWS = 128
GATHER_SPLIT = 4


def _cparams(*sem):
    return pltpu.CompilerParams(dimension_semantics=sem, vmem_limit_bytes=VMEM_LIMIT)


def _dot(a, b):
    return jnp.dot(a, b, preferred_element_type=F32)


def _dot_hi(a, b):
    return jnp.dot(a, b, preferred_element_type=F32, precision=HIGHEST)


def _dot_nt(a, b):
    return lax.dot_general(a, b, (((1,), (1,)), ((), ())), preferred_element_type=F32)


def _dot_tn(a, b):
    return lax.dot_general(a, b, (((0,), (0,)), ((), ())), preferred_element_type=F32)


def _silu(x):
    return x * jax.nn.sigmoid(x)


def _split_bf16(x):
    hi = x.astype(BF16)
    return hi, (x - hi.astype(F32)).astype(BF16)


def _ada_kernel(c_ref, w_ref, b_ref, o_ref):
    o_ref[...] = _dot_hi(_silu(c_ref[...]), w_ref[...]) + b_ref[...]


def ada_vectors(c_rows, ada_w, ada_b):
    r, d = c_rows.shape
    n = ada_w.shape[1]
    bn = 1024
    return pl.pallas_call(
        _ada_kernel,
        out_shape=jax.ShapeDtypeStruct((r, n), F32),
        grid=(n // bn,),
        in_specs=[pl.BlockSpec((r, d), lambda j: (0, 0)),
                  pl.BlockSpec((d, bn), lambda j: (0, j)),
                  pl.BlockSpec((1, bn), lambda j: (0, j))],
        out_specs=pl.BlockSpec((r, bn), lambda j: (0, j)),
        compiler_params=_cparams("arbitrary"),
        name="ada_vectors",
    )(c_rows, ada_w, ada_b.reshape(1, n))


def _inproj_kernel(x_ref, er_ref, ec_ref, g_ref, sh_ref, sc_ref, w_ref, o_ref, *, col_chunk):
    x = x_ref[...]
    rows, gw, d = x.shape
    half = d // 2
    xp = jnp.concatenate([x[:, :, :half] + er_ref[...], x[:, :, half:] + ec_ref[...]], axis=-1)
    xp = xp.reshape(rows * gw, d)
    ms = jnp.mean(xp * xp, axis=-1, keepdims=True)
    y = xp * lax.rsqrt(ms + NORM_EPS) * g_ref[...]
    h = (y * (1.0 + sc_ref[...]) + sh_ref[...]).astype(BF16)
    n = o_ref.shape[1]
    for j in range(n // col_chunk):
        sl = slice(j * col_chunk, (j + 1) * col_chunk)
        o_ref[:, sl] = _dot(h, w_ref[:, sl]).astype(o_ref.dtype)


def in_projection(x, emb_r, emb_c, norm_g, shift, scale, w_bf16, tm):
    b, s, d = x.shape
    n = w_bf16.shape[1]
    rows_per_batch = s // GRID_W
    rt = tm // GRID_W
    tiles_per_batch = rows_per_batch // rt
    x3 = x.reshape(b * rows_per_batch, GRID_W, d)
    col_chunk = 512
    return pl.pallas_call(
        functools.partial(_inproj_kernel, col_chunk=col_chunk),
        out_shape=jax.ShapeDtypeStruct((b * s, n), BF16),
        grid=(b * tiles_per_batch,),
        in_specs=[pl.BlockSpec((rt, GRID_W, d), lambda i: (i, 0, 0)),
                  pl.BlockSpec((rt, 1, d // 2), lambda i: (i % tiles_per_batch, 0, 0)),
                  pl.BlockSpec((GRID_W, d // 2), lambda i: (0, 0)),
                  pl.BlockSpec((1, d), lambda i: (0, 0)),
                  pl.BlockSpec((None, 1, d), lambda i: (i // tiles_per_batch, 0, 0)),
                  pl.BlockSpec((None, 1, d), lambda i: (i // tiles_per_batch, 0, 0)),
                  pl.BlockSpec((d, n), lambda i: (0, 0))],
        out_specs=pl.BlockSpec((tm, n), lambda i: (i, 0)),
        compiler_params=_cparams("arbitrary"),
        name="in_projection",
    )(x3, emb_r, emb_c, norm_g.reshape(1, d), shift, scale, w_bf16)


def _hgrn_kernel(*refs, reverse, n_chunks, final):
    if final:
        (q_ref, i_ref, f_ref, lb_ref, s0_ref, of_ref, gate_ref, ng_ref, o_ref, sfin_ref, s_scr) = refs
    else:
        (q_ref, i_ref, f_ref, lb_ref, s0_ref, o_ref, sfin_ref, s_scr) = refs
    cs = HG_CHUNK
    bsz, n_heads, _, dk = s_scr.shape

    @pl.when(pl.program_id(0) == 0)
    def _():
        s_scr[...] = s0_ref[...]

    row = lax.broadcasted_iota(I32, (cs, cs), 0)
    col = lax.broadcasted_iota(I32, (cs, cs), 1)
    tri = (col >= row) if reverse else (col <= row)
    tri_b = tri.astype(BF16)
    end_row = 0 if reverse else cs - 1
    mid_row = cs // 2 if reverse else cs // 2 - 1

    def chunk_body(bi, ci):
        c = (n_chunks - 1 - ci) if reverse else ci
        rows = slice(c * cs, (c + 1) * cs)
        lb = lb_ref[...]
        f = lb + (1.0 - lb) * jax.nn.sigmoid(f_ref[bi, rows, :].astype(F32))
        lf_hi, lf_lo = _split_bf16(jnp.log(f))
        b_all = _dot(tri_b, lf_hi) + _dot(tri_b, lf_lo)
        k_all = 1.0 - f
        q_all = _silu(q_ref[bi, rows, :].astype(F32))
        for h in range(n_heads):
            sl = slice(h * dk, (h + 1) * dk)
            b = b_all[:, sl]
            q = q_all[:, sl]
            k = k_all[:, sl]
            v = i_ref[bi, rows, sl]
            b_end = b[end_row:end_row + 1]
            b_mid = b[mid_row:mid_row + 1]
            qd = (q * jnp.exp(b - b_mid)).astype(BF16)
            kd = (k * jnp.exp(b_mid - b)).astype(BF16)
            att = jnp.where(tri, _dot_nt(qd, kd), 0.0).astype(BF16)
            st = s_scr[bi, h]
            qe = (q * jnp.exp(b)).astype(BF16)
            o = _dot(att, v) + _dot_nt(qe, st.astype(BF16))
            ke = (k * jnp.exp(b_end - b)).astype(BF16)
            s_scr[bi, h] = st * jnp.exp(b_end) + _dot_tn(v, ke)
            if final:
                o = o + of_ref[bi, rows, sl].astype(F32)
                o = o * lax.rsqrt(jnp.mean(o * o, axis=-1, keepdims=True) + NORM_EPS) * ng_ref[...]
                o = o * _silu(gate_ref[bi, rows, sl].astype(F32))
            o_ref[bi, rows, sl] = o.astype(o_ref.dtype)

    for ci in range(n_chunks):
        for bi in range(bsz):
            chunk_body(bi, ci)
    sfin_ref[...] = s_scr[...]


def hgrn_scan(p, cols, lb, s0, seq, tb, *, reverse, o_fwd=None, gate_col=None, norm_g=None):
    bsz, n_heads, dv, dk = s0.shape
    width = n_heads * dk
    nt = seq // tb
    final = o_fwd is not None
    p3 = p.reshape(bsz, seq, p.shape[1])
    tmap = (lambda t: nt - 1 - t) if reverse else (lambda t: t)
    colspec = lambda cb: pl.BlockSpec((bsz, tb, width), lambda t: (0, tmap(t), cb))
    state = pl.BlockSpec((bsz, n_heads, dv, dk), lambda t: (0, 0, 0, 0))
    in_specs = [colspec(cols[0]), colspec(cols[1]), colspec(cols[2]),
                pl.BlockSpec((1, width), lambda t: (0, 0)), state]
    args = [p3, p3, p3, lb.reshape(1, width), s0]
    if final:
        in_specs += [colspec(0), colspec(gate_col), pl.BlockSpec((1, dk), lambda t: (0, 0))]
        args += [o_fwd.reshape(bsz, seq, width), p3, norm_g.reshape(1, dk)]
    o, s_fin = pl.pallas_call(
        functools.partial(_hgrn_kernel, reverse=reverse, n_chunks=tb // HG_CHUNK, final=final),
        out_shape=(jax.ShapeDtypeStruct((bsz, seq, width), BF16),
                   jax.ShapeDtypeStruct((bsz, n_heads, dv, dk), F32)),
        grid=(nt,),
        in_specs=in_specs,
        out_specs=(colspec(0), state),
        scratch_shapes=[pltpu.VMEM((bsz, n_heads, dv, dk), F32)],
        compiler_params=_cparams("arbitrary"),
        name="hgrn_bwd" if reverse else "hgrn_fwd",
    )(*args)
    return o.reshape(bsz * seq, width), s_fin


def _shortconv_kernel(p_ref, prev_ref, next_ref, w_ref, b_ref, o_ref, *, tiles_per_batch):
    i = pl.program_id(0)
    ti = i % tiles_per_batch
    p = p_ref[...].astype(F32)
    tm = p.shape[0]
    row = lax.broadcasted_iota(I32, (tm, 1), 0)
    prev_row = jnp.where(ti == 0, 0.0, prev_ref[HALO_ROWS - 1:HALO_ROWS, :].astype(F32))
    next_row = jnp.where(ti == tiles_per_batch - 1, 0.0, next_ref[0:1, :].astype(F32))
    p_prev = jnp.where(row == 0, prev_row, pltpu.roll(p, 1, axis=0))
    p_next = jnp.where(row == tm - 1, next_row, pltpu.roll(p, tm - 1, axis=0))
    u = w_ref[0:1, :] * p_prev + w_ref[1:2, :] * p + w_ref[2:3, :] * p_next + b_ref[...]
    o_ref[...] = jnp.swapaxes(u.reshape(tm // DFT_P, DFT_P, u.shape[1]), 0, 1).astype(o_ref.dtype)


def short_conv(p, width, conv_w, conv_b, seq, tm, cw):
    t = p.shape[0]
    nt = t // tm
    tiles_per_batch = seq // tm
    sub = tm // HALO_ROWS
    ta = tm // DFT_P
    return pl.pallas_call(
        functools.partial(_shortconv_kernel, tiles_per_batch=tiles_per_batch),
        out_shape=jax.ShapeDtypeStruct((DFT_P, t // DFT_P, width), BF16),
        grid=(nt, width // cw),
        in_specs=[pl.BlockSpec((tm, cw), lambda i, j: (i, j)),
                  pl.BlockSpec((HALO_ROWS, cw), lambda i, j: (jnp.maximum(i * sub - 1, 0), j)),
                  pl.BlockSpec((HALO_ROWS, cw), lambda i, j: (jnp.minimum((i + 1) * sub, t // HALO_ROWS - 1), j)),
                  pl.BlockSpec((HY_SHORT, cw), lambda i, j: (0, j)),
                  pl.BlockSpec((1, cw), lambda i, j: (0, j))],
        out_specs=pl.BlockSpec((DFT_P, ta, cw), lambda i, j: (0, i, j)),
        compiler_params=_cparams("arbitrary", "arbitrary"),
        name="short_conv",
    )(p, p, p, conv_w, conv_b.reshape(1, width))


def _filter_kernel(band_ref, w1t_ref, w1c_ref, w1s_ref, b1_ref, w2_ref, b2_ref, w3_ref, b3_ref,
                   w4f_ref, w4b_ref, fr_ref, delta_ref, k_ref, s_ref, *, seq):
    step = pl.program_id(0)
    gb, q, ncol = k_ref.shape
    half = q // 2
    width = delta_ref.shape[1]
    nrow = gb * q
    nf = gb * half

    def positions(shape, axis):
        r = lax.broadcasted_iota(I32, shape, axis)
        is_bwd = r >= nf
        rr = jnp.where(is_bwd, r - nf, r)
        j = lax.shift_right_logical(rr, int(math.log2(half)))
        a = (rr & (half - 1)) + jnp.where(is_bwd, half, 0)
        n = (a * DFT_P + step * gb + j).astype(F32)
        t = jnp.where(is_bwd, 2.0 * seq - n, n)
        return n, t, t / float(max(seq - 1, 1))

    _, t_l, tn_l = positions((1, nrow), 1)
    ang = (2.0 * math.pi / seq) * t_l * band_ref[...]
    fr = fr_ref[...]
    pre = (w1t_ref[...] * tn_l + _dot_hi(w1c_ref[...], jnp.cos(ang)) - _dot_hi(w1s_ref[...], jnp.sin(ang))
           + b1_ref[...])
    act = jnp.sin(fr * pre)
    act = jnp.sin(fr * (_dot_hi(w2_ref[...], act) + b2_ref[...]))
    act = jnp.sin(fr * (_dot_hi(w3_ref[...], act) + b3_ref[...])).astype(BF16)
    n_s, _, tn_s = positions((nrow, 1), 0)
    delta = jnp.concatenate([delta_ref[...]] * (ncol // width), axis=1)
    hf = _dot_tn(act[:, :nf], w4f_ref[...]) * jnp.exp(-tn_s[:nf] * delta)
    hb = _dot_tn(act[:, nf:], w4b_ref[...]) * jnp.exp(-tn_s[nf:] * delta)
    hb = jnp.where(n_s[nf:] == float(seq), 0.0, hb)
    k_ref[:, :half, :] = hf.reshape(gb, half, ncol).astype(k_ref.dtype)
    k_ref[:, half:, :] = hb.reshape(gb, half, ncol).astype(k_ref.dtype)
    tot = jnp.sum(jnp.abs(hf), axis=0, keepdims=True) + jnp.sum(jnp.abs(hb), axis=0, keepdims=True)

    @pl.when(step == 0)
    def _():
        s_ref[...] = jnp.zeros_like(s_ref)

    s_ref[...] += tot


def hyena_filter_taps(seq, w1, b1, w2, b2, w3, b3, w4, freq, width):
    emb = w1.shape[0]
    hid = w1.shape[1]
    bands = (emb - 1) // 2
    q = 2 * seq // DFT_P
    ncol = HY_ORDER * width
    band = np.linspace(1e-4, bands - 1, bands, dtype=np.float32).reshape(bands, 1)
    min_decay = math.log(HY_DECAY_TARGET) / HY_SLOW_PCT
    max_decay = math.log(HY_DECAY_TARGET) / HY_FAST_PCT
    delta = np.abs(np.linspace(min_decay, max_decay, width, dtype=np.float32)).reshape(1, width)
    w1t = w1.astype(F32).T
    col = lambda v: v.reshape(hid, 1).astype(F32)
    w4r = w4.astype(BF16).reshape(hid, HY_ORDER, 2, width)
    w4f = w4r[:, :, 0, :].reshape(hid, ncol)
    w4b = w4r[:, :, 1, :].reshape(hid, ncol)
    gb = FILTER_GROUP
    const = lambda shape: pl.BlockSpec(shape, lambda i: tuple(0 for _ in shape))
    return pl.pallas_call(
        functools.partial(_filter_kernel, seq=seq),
        out_shape=(jax.ShapeDtypeStruct((DFT_P, q, ncol), BF16),
                   jax.ShapeDtypeStruct((1, ncol), F32)),
        grid=(DFT_P // gb,),
        in_specs=[const((bands, 1)), const((hid, 1)), const((hid, bands)), const((hid, bands)), const((hid, 1)),
                  const((hid, hid)), const((hid, 1)), const((hid, hid)), const((hid, 1)),
                  const((hid, ncol)), const((hid, ncol)), const((hid, 1)), const((1, width))],
        out_specs=(pl.BlockSpec((gb, q, ncol), lambda i: (i, 0, 0)),
                   pl.BlockSpec((1, ncol), lambda i: (0, 0))),
        compiler_params=_cparams("arbitrary"),
        name="hyena_filter",
    )(jnp.asarray(band), w1t[:, 0:1], w1t[:, 1:1 + bands], w1t[:, 1 + bands:1 + 2 * bands], col(b1),
      w2.astype(F32).T, col(b2), w3.astype(F32).T, col(b3), w4f, w4b, col(freq), jnp.asarray(delta))


def _dft_tables(seq):
    p = DFT_P
    n_fft = 2 * seq
    q = n_fft // p
    qh = q // 2
    ka = np.arange(q)
    nn = np.arange(q)[None, :] * p + np.arange(p)[:, None]
    ang = ((ka[None, :, None] * nn[:, None, :]) % n_fft) * (2.0 * np.pi / n_fft)
    mr, mi = np.cos(ang), -np.sin(ang)
    m1c = np.concatenate([np.concatenate([mr[:, :, :qh], -mi[:, :, :qh]], axis=2),
                          np.concatenate([mi[:, :, :qh], mr[:, :, :qh]], axis=2)], axis=1)
    m1r = np.concatenate([mr, mi], axis=1)
    gr = np.swapaxes(mr[:, :, :qh], 1, 2) / n_fft
    gi = -np.swapaxes(mi[:, :, :qh], 1, 2) / n_fft
    m4 = np.concatenate([np.concatenate([gr, -gi], axis=2), np.concatenate([gi, gr], axis=2)], axis=1)
    kb = np.arange(p)
    ang2 = 2.0 * np.pi * ((kb[:, None] * kb[None, :]) % p) / p
    fr, fi = np.cos(ang2), -np.sin(ang2)
    m2 = np.block([[fr, -fi], [fi, fr]])
    m3 = np.block([[fr, fi], [-fi, fr]])
    return tuple(jnp.asarray(m.astype(np.float32).astype(BF16)) for m in (m1c, m1r, m2, m3, m4))


def _bmm_kernel(w_ref, x_ref, o_ref, *, shared_w):
    for j in range(x_ref.shape[0]):
        w = w_ref[...] if shared_w else w_ref[j]
        o_ref[j] = _dot(w, x_ref[j]).astype(o_ref.dtype)


def batched_left_matmul(w, x, col_block, ncols, name, gb):
    g, k = x.shape[0], x.shape[1]
    shared = w.ndim == 2
    m = w.shape[-2]
    wspec = (pl.BlockSpec((m, k), lambda i: (0, 0)) if shared
             else pl.BlockSpec((gb, m, k), lambda i: (i, 0, 0)))
    return pl.pallas_call(
        functools.partial(_bmm_kernel, shared_w=shared),
        out_shape=jax.ShapeDtypeStruct((g, m, ncols), BF16),
        grid=(g // gb,),
        in_specs=[wspec, pl.BlockSpec((gb, k, ncols), lambda i: (i, 0, col_block))],
        out_specs=pl.BlockSpec((gb, m, ncols), lambda i: (i, 0, 0)),
        compiler_params=_cparams("arbitrary"),
        name=name,
    )(w, x)


def _dft_mid_kernel(m2_ref, m3_ref, x_ref, k_ref, o_ref):
    half = x_ref.shape[1] // 2
    for j in range(x_ref.shape[0]):
        xf = _dot(m2_ref[...], x_ref[j])
        kf = _dot(m2_ref[...], k_ref[j])
        xr, xi = xf[:half], xf[half:]
        kr, ki = kf[:half], kf[half:]
        z = jnp.concatenate([xr * kr - xi * ki, xr * ki + xi * kr], axis=0).astype(BF16)
        o_ref[j] = _dot(m3_ref[...], z).astype(o_ref.dtype)


def dft_mid(m2, m3, x, kspec, kcol, ncols):
    g, r = x.shape[0], x.shape[1]
    gb = DFT_GROUP
    return pl.pallas_call(
        _dft_mid_kernel,
        out_shape=jax.ShapeDtypeStruct((g, r, ncols), BF16),
        grid=(g // gb,),
        in_specs=[pl.BlockSpec((r, r), lambda i: (0, 0)),
                  pl.BlockSpec((r, r), lambda i: (0, 0)),
                  pl.BlockSpec((gb, r, ncols), lambda i: (i, 0, 0)),
                  pl.BlockSpec((gb, r, ncols), lambda i: (i, 0, kcol))],
        out_specs=pl.BlockSpec((gb, r, ncols), lambda i: (i, 0, 0)),
        compiler_params=_cparams("arbitrary"),
        name="dft_mid",
    )(m2, m3, x, kspec)


def _dft_out_kernel(m4_ref, y_ref, inv_ref, skip_ref, v_ref, mul_ref, o_ref, *, token_order):
    res = []
    for j in range(y_ref.shape[0]):
        conv = _dot(m4_ref[j], y_ref[j]) * inv_ref[...] + v_ref[j].astype(F32) * skip_ref[...]
        res.append(mul_ref[j].astype(F32) * conv)
    if token_order:
        o_ref[...] = jnp.swapaxes(jnp.stack(res, axis=0), 0, 1).astype(o_ref.dtype)
    else:
        for j, r in enumerate(res):
            o_ref[j] = r.astype(o_ref.dtype)


def dft_out(m4, y, inv_l1, skip, u, v_col, mul, mul_col, ncols, token_order):
    g, r = y.shape[0], y.shape[1]
    rows = m4.shape[1]
    gb = DFT_GROUP
    out_shape, out_spec = (((rows, g, ncols), pl.BlockSpec((rows, gb, ncols), lambda i: (0, i, 0))) if token_order
                           else ((g, rows, ncols), pl.BlockSpec((gb, rows, ncols), lambda i: (i, 0, 0))))
    return pl.pallas_call(
        functools.partial(_dft_out_kernel, token_order=token_order),
        out_shape=jax.ShapeDtypeStruct(out_shape, BF16),
        grid=(g // gb,),
        in_specs=[pl.BlockSpec((gb, rows, r), lambda i: (i, 0, 0)),
                  pl.BlockSpec((gb, r, ncols), lambda i: (i, 0, 0)),
                  pl.BlockSpec((1, ncols), lambda i: (0, 0)),
                  pl.BlockSpec((1, ncols), lambda i: (0, 0)),
                  pl.BlockSpec((gb, rows, ncols), lambda i: (i, 0, v_col)),
                  pl.BlockSpec((gb, rows, ncols), lambda i: (i, 0, mul_col))],
        out_specs=out_spec,
        compiler_params=_cparams("arbitrary"),
        name="dft_out",
    )(m4, y, inv_l1, skip, u, mul)


def _swap_ab(x):
    g1, r, c = x.shape
    g2 = r // 2
    return x.reshape(g1, 2, g2, c).transpose(2, 1, 0, 3).reshape(g2, 2 * g1, c)


def hyena_branch(u, bsz, seq, width, taps, l1, skip):
    m1c, m1r, m2, m3, m4 = _dft_tables(seq)
    ncol = HY_ORDER * width
    ks1 = _swap_ab(batched_left_matmul(m1r, taps, 0, ncol, "dft_k1", FILTER_GROUP))
    inv_l1 = 1.0 / l1
    z = None
    for order in range(HY_ORDER):
        src, src_col = (u, 0) if order == 0 else (z, 0)
        s1 = batched_left_matmul(m1c, src, src_col, width, "dft_s1", DFT_GROUP)
        mid = dft_mid(m2, m3, _swap_ab(s1), ks1, order, width)
        z = dft_out(m4, _swap_ab(mid), inv_l1[:, order * width:(order + 1) * width],
                    skip[order].reshape(1, width).astype(F32), src, src_col, u, order + 1, width,
                    token_order=order == HY_ORDER - 1)
    return z.reshape(bsz * seq, width)


def _pack_pairs(x):
    w = x.shape[1] // 2
    u = lax.bitcast_convert_type(x, U32)
    r = (u + U32(0x7FFF) + ((u >> 16) & U32(1))) >> 16
    return r[:, :w] | (r[:, w:] << 16)


def _unpack_pairs(p):
    lo = lax.bitcast_convert_type(p << 16, F32)
    hi = lax.bitcast_convert_type(p & U32(0xFFFF0000), F32)
    return jnp.concatenate([lo, hi], axis=1)


def _pack_pairs_native(x):
    w = x.shape[1] // 2
    return lax.bitcast_convert_type(pltpu.pack_elementwise([x[:, :w], x[:, w:]], packed_dtype=BF16), U32)


def _unpack_pairs_native(p):
    pi = lax.bitcast_convert_type(p, I32)
    halves = [pltpu.unpack_elementwise(pi, index=i, packed_dtype=BF16, unpacked_dtype=F32) for i in range(2)]
    return jnp.concatenate(halves, axis=1)


def _merge_kernel(x_ref, er_ref, ec_ref, yhy_ref, yhg_ref, ghy_ref, ghg_ref, why_ref, whg_ref, wo_ref,
                  g1_ref, n2_ref, sh2_ref, sc2_ref, g2_ref, rwh_ref, rwl_ref, sgu_ref, sd_ref,
                  xres_ref, h2p_ref, lg_ref):
    x = x_ref[...]
    rows, gw, d = x.shape
    half = d // 2
    xp = jnp.concatenate([x[:, :, :half] + er_ref[...], x[:, :, half:] + ec_ref[...]], axis=-1)
    xp = xp.reshape(rows * gw, d)
    m = (jax.nn.sigmoid(ghy_ref[...].astype(F32)) * _dot(yhy_ref[...], why_ref[...])
         + jax.nn.sigmoid(ghg_ref[...].astype(F32)) * _dot(yhg_ref[...], whg_ref[...]))
    x1 = xp + g1_ref[...] * _dot(m.astype(BF16), wo_ref[...])
    ms = jnp.mean(x1 * x1, axis=-1, keepdims=True)
    h2 = x1 * lax.rsqrt(ms + NORM_EPS) * n2_ref[...] * (1.0 + sc2_ref[...]) + sh2_ref[...]
    h_hi, h_lo = _split_bf16(h2)
    lg_ref[...] = _dot_nt(rwh_ref[...], h_hi) + (_dot_nt(rwl_ref[...], h_hi) + _dot_nt(rwh_ref[...], h_lo))
    gu = _dot(h_hi, sgu_ref[...])
    fs = gu.shape[1] // 2
    shared = _dot((_silu(gu[:, :fs]) * gu[:, fs:]).astype(BF16), sd_ref[...])
    xres_ref[...] = x1 + g2_ref[...] * shared
    h2p_ref[...] = _pack_pairs(h2)


def merge_stage(x, emb_r, emb_c, y_hy, y_hg, p, gate_cols, w_hy_out, w_hg_out, w_out, g1, norm2_g,
                sh2, sc2, g2, router_wt, sh_gate_up, sh_down, tm):
    b, s, d = x.shape
    rows_per_batch = s // GRID_W
    rt = tm // GRID_W
    tiles_per_batch = rows_per_batch // rt
    x3 = x.reshape(b * rows_per_batch, GRID_W, d)
    wb = y_hy.shape[1]
    ne = router_wt.shape[0]
    fs2 = sh_gate_up.shape[1]
    rw_hi, rw_lo = _split_bf16(router_wt)
    tok = lambda cb, w: pl.BlockSpec((tm, w), lambda i: (i, cb))
    const = lambda shape: pl.BlockSpec(shape, lambda i: tuple(0 for _ in shape))
    per_b = pl.BlockSpec((None, 1, d), lambda i: (i // tiles_per_batch, 0, 0))
    return pl.pallas_call(
        _merge_kernel,
        out_shape=(jax.ShapeDtypeStruct((b * s, d), F32),
                   jax.ShapeDtypeStruct((b * s, d // 2), U32),
                   jax.ShapeDtypeStruct((ne, b * s), F32)),
        grid=(b * tiles_per_batch,),
        in_specs=[pl.BlockSpec((rt, GRID_W, d), lambda i: (i, 0, 0)),
                  pl.BlockSpec((rt, 1, d // 2), lambda i: (i % tiles_per_batch, 0, 0)),
                  const((GRID_W, d // 2)),
                  tok(0, wb), tok(0, wb), tok(gate_cols[0], d), tok(gate_cols[1], d),
                  const((wb, d)), const((wb, d)), const((d, d)),
                  per_b, const((1, d)), per_b, per_b, per_b,
                  const((ne, d)), const((ne, d)), const((d, fs2)), const((fs2 // 2, d))],
        out_specs=(pl.BlockSpec((tm, d), lambda i: (i, 0)),
                   pl.BlockSpec((tm, d // 2), lambda i: (i, 0)),
                   pl.BlockSpec((ne, tm), lambda i: (0, i))),
        compiler_params=_cparams("arbitrary"),
        name="merge",
    )(x3, emb_r, emb_c, y_hy, y_hg, p, p, w_hy_out, w_hg_out, w_out, g1, norm2_g.reshape(1, d),
      sh2, sc2, g2, rw_hi, rw_lo, sh_gate_up, sh_down)


def _route_kernel(lg_ref, bias_ref, eidx_ref, wsel_ref, rank_ref, cnt_ref, carry):
    ne, tr = lg_ref.shape
    gsz = ne // N_GROUPS
    neg = -jnp.inf

    @pl.when(pl.program_id(0) == 0)
    def _():
        carry[...] = jnp.zeros_like(carry)

    scores = jax.nn.sigmoid(lg_ref[...])
    biased = scores + bias_ref[...]
    riota = lax.broadcasted_iota(I32, (gsz, tr), 0).astype(F32)
    gs = []
    for g in range(N_GROUPS):
        vg = biased[g * gsz:(g + 1) * gsz]
        m1 = jnp.max(vg, axis=0, keepdims=True)
        i1 = jnp.min(jnp.where(vg == m1, riota, float(gsz)), axis=0, keepdims=True)
        m2 = jnp.max(jnp.where(riota == i1, neg, vg), axis=0, keepdims=True)
        gs.append(m1 + m2)
    cur = jnp.concatenate(gs, axis=0)
    giota = lax.broadcasted_iota(I32, (N_GROUPS, tr), 0).astype(F32)
    gsel = jnp.zeros((N_GROUPS, tr), F32)
    for _ in range(TOPK_GROUPS):
        m = jnp.max(cur, axis=0, keepdims=True)
        idx = jnp.min(jnp.where(cur == m, giota, float(N_GROUPS)), axis=0, keepdims=True)
        hit = giota == idx
        gsel = jnp.where(hit, 1.0, gsel)
        cur = jnp.where(hit, neg, cur)
    cur = jnp.concatenate([jnp.where(gsel[g:g + 1] > 0.0, biased[g * gsz:(g + 1) * gsz], neg)
                           for g in range(N_GROUPS)], axis=0)
    eiota = lax.broadcasted_iota(I32, (ne, tr), 0).astype(F32)
    chosen = jnp.zeros((ne, tr), F32)
    idxs, ws = [], []
    for _ in range(TOP_K):
        m = jnp.max(cur, axis=0, keepdims=True)
        idx = jnp.min(jnp.where(cur == m, eiota, float(ne)), axis=0, keepdims=True)
        hit = eiota == idx
        idxs.append(idx)
        ws.append(jnp.sum(jnp.where(hit, scores, 0.0), axis=0, keepdims=True))
        chosen = jnp.where(hit, 1.0, chosen)
        cur = jnp.where(hit, neg, cur)
    w = jnp.concatenate(ws, axis=0)
    wsel_ref[...] = w / jnp.sum(w, axis=0, keepdims=True) * ROUTED_SCALE
    eidx_ref[...] = jnp.concatenate(idxs, axis=0).astype(I32)
    srow = lax.broadcasted_iota(I32, (tr, tr), 0)
    scol = lax.broadcasted_iota(I32, (tr, tr), 1)
    before = (srow < scol).astype(BF16)
    base = carry[...] + _dot(chosen.astype(BF16), before)
    ranks = [jnp.sum(jnp.where(eiota == idx, base, 0.0), axis=0, keepdims=True) for idx in idxs]
    rank_ref[...] = jnp.concatenate(ranks, axis=0).astype(I32)
    carry[...] += jnp.sum(chosen, axis=1, keepdims=True)
    cnt_ref[...] = carry[...]


def route(logits_t, router_bias, tr):
    ne, t = logits_t.shape
    return pl.pallas_call(
        _route_kernel,
        out_shape=(jax.ShapeDtypeStruct((TOP_K, t), I32),
                   jax.ShapeDtypeStruct((TOP_K, t), F32),
                   jax.ShapeDtypeStruct((TOP_K, t), I32),
                   jax.ShapeDtypeStruct((ne, 1), F32)),
        grid=(t // tr,),
        in_specs=[pl.BlockSpec((ne, tr), lambda i: (0, i)),
                  pl.BlockSpec((ne, 1), lambda i: (0, 0))],
        out_specs=(pl.BlockSpec((TOP_K, tr), lambda i: (0, i)),
                   pl.BlockSpec((TOP_K, tr), lambda i: (0, i)),
                   pl.BlockSpec((TOP_K, tr), lambda i: (0, i)),
                   pl.BlockSpec((ne, 1), lambda i: (0, 0))),
        scratch_shapes=[pltpu.VMEM((ne, 1), F32)],
        compiler_params=_cparams("arbitrary"),
        name="route",
    )(logits_t, router_bias.reshape(ne, 1).astype(F32))


def _dest_kernel(cnt_ref, eidx_ref, rank_ref, dest_ref, be_ref, nv_ref, nb_ref, start_scr):
    ne = cnt_ref.shape[0]
    tr = eidx_ref.shape[1]

    @pl.when(pl.program_id(0) == 0)
    def _():
        cnt = jnp.broadcast_to(cnt_ref[...], (ne, LANES))
        padded = jnp.floor((cnt + float(MOE_ROWS - 1)) / float(MOE_ROWS)) * float(MOE_ROWS)
        r = lax.broadcasted_iota(I32, (ne, ne), 0)
        c = lax.broadcasted_iota(I32, (ne, ne), 1)
        start = _dot_hi((c < r).astype(F32), padded)
        start_scr[...] = start
        end = start[:, 0:1] + padded[:, 0:1]
        used = start[:, 0:1] + cnt[:, 0:1]
        nbl = be_ref.shape[1]
        blk_row = (lax.broadcasted_iota(I32, (1, nbl), 1) * MOE_ROWS).astype(F32)
        total = jnp.max(end, axis=0, keepdims=True)
        last_row = total - float(MOE_ROWS)
        blk_row_c = jnp.minimum(blk_row, last_row)
        e_of = jnp.sum((end <= blk_row_c).astype(F32), axis=0, keepdims=True)
        e_of = jnp.minimum(e_of, float(ne - 1))
        eio = lax.broadcasted_iota(I32, (ne, nbl), 0).astype(F32)
        used_e = jnp.sum(jnp.where(eio == e_of, used, 0.0), axis=0, keepdims=True)
        valid = jnp.clip(used_e - blk_row_c, 0.0, float(MOE_ROWS))
        be_ref[...] = e_of.astype(I32)
        nv_ref[...] = jnp.where(blk_row <= last_row, valid, 0.0).astype(I32)
        nb_ref[...] = jnp.broadcast_to(total / float(MOE_ROWS), nb_ref.shape).astype(I32)

    eiota = lax.broadcasted_iota(I32, (ne, tr), 0)
    start_col = start_scr[:, 0:1]
    rows = []
    for k in range(TOP_K):
        hit = eiota == eidx_ref[k:k + 1, :]
        rows.append(jnp.sum(jnp.where(hit, start_col, 0.0), axis=0, keepdims=True))
    dest_ref[...] = jnp.concatenate(rows, axis=0).astype(I32) + rank_ref[...]


def dispatch_plan(counts, eidx, rank, tr, n_blocks):
    ne = counts.shape[0]
    t = eidx.shape[1]
    nbl = pl.cdiv(n_blocks, LANES) * LANES
    return pl.pallas_call(
        _dest_kernel,
        out_shape=(jax.ShapeDtypeStruct((TOP_K, t), I32),
                   jax.ShapeDtypeStruct((1, nbl), I32),
                   jax.ShapeDtypeStruct((1, nbl), I32),
                   jax.ShapeDtypeStruct((1, LANES), I32)),
        grid=(t // tr,),
        in_specs=[pl.BlockSpec((ne, 1), lambda i: (0, 0)),
                  pl.BlockSpec((TOP_K, tr), lambda i: (0, i)),
                  pl.BlockSpec((TOP_K, tr), lambda i: (0, i))],
        out_specs=(pl.BlockSpec((TOP_K, tr), lambda i: (0, i)),
                   pl.BlockSpec((1, nbl), lambda i: (0, 0)),
                   pl.BlockSpec((1, nbl), lambda i: (0, 0)),
                   pl.BlockSpec((1, LANES), lambda i: (0, 0))),
        scratch_shapes=[pltpu.VMEM((ne, LANES), F32)],
        compiler_params=_cparams("arbitrary"),
        name="dispatch_plan",
    )(counts, eidx, rank)


def _sc_workers():
    info = plsc.get_sparse_core_info()
    return info.num_cores, info.num_cores * info.num_subcores


def scatter_rows(dest_flat, h2p, n_rows):
    t, w = h2p.shape
    n_cores, n_workers = _sc_workers()
    per_worker = t // n_workers
    mesh = plsc.VectorSubcoreMesh(core_axis_name="c", subcore_axis_name="s")

    @functools.partial(
        pl.kernel, mesh=mesh, out_type=jax.ShapeDtypeStruct((n_rows, w), U32),
        scratch_types=[pltpu.VMEM((TOP_K, SC_ROWS), I32), pltpu.VMEM((SC_ROWS, w), U32), pltpu.SemaphoreType.DMA])
    def body(h_hbm, dest_hbm, xs_hbm, idx_v, rows_v, sem):
        base = (lax.axis_index("s") * n_cores + lax.axis_index("c")) * per_worker

        @pl.loop(0, per_worker // SC_ROWS)
        def _(ci):
            off = pl.multiple_of(base + ci * SC_ROWS, SC_ROWS)
            pltpu.sync_copy(h_hbm.at[pl.ds(off, SC_ROWS)], rows_v)
            for k in range(TOP_K):
                pltpu.sync_copy(dest_hbm.at[pl.ds(k * t + off, SC_ROWS)], idx_v.at[k])
            copies = [pltpu.async_copy(rows_v, xs_hbm.at[idx_v.at[k]], sem) for k in range(TOP_K)]
            for c in copies:
                c.wait()

    return body(h2p, dest_flat)


def gather_rows(idx_flat, table):
    n = idx_flat.shape[0]
    w = table.shape[1]
    n_cores, n_workers = _sc_workers()
    per_worker = n // n_workers
    mesh = plsc.VectorSubcoreMesh(core_axis_name="c", subcore_axis_name="s")

    ch = SC_ROWS // 2
    n_chunks = per_worker // ch

    @functools.partial(
        pl.kernel, mesh=mesh, out_type=jax.ShapeDtypeStruct((n, w), table.dtype),
        scratch_types=[pltpu.VMEM((2, ch), I32), pltpu.VMEM((2, ch, w), table.dtype), pltpu.SemaphoreType.DMA((2,))])
    def body(table_hbm, idx_hbm, out_hbm, idx_v, rows_v, sem):
        base = (lax.axis_index("s") * n_cores + lax.axis_index("c")) * per_worker

        def read(b):
            return pltpu.make_async_copy(table_hbm.at[idx_v.at[b]], rows_v.at[b], sem.at[b])

        def start(c, b):
            off = pl.multiple_of(base + c * ch, ch)
            pltpu.sync_copy(idx_hbm.at[pl.ds(off, ch)], idx_v.at[b])
            read(b).start()

        def finish(c, b):
            read(b).wait()
            pltpu.sync_copy(rows_v.at[b], out_hbm.at[pl.ds(pl.multiple_of(base + c * ch, ch), ch)])

        start(0, 0)

        @pl.loop(0, n_chunks, step=2)
        def _(c):
            start(c + 1, 1)
            finish(c, 0)

            @pl.when(c + 2 < n_chunks)
            def _():
                start(c + 2, 0)

            finish(c + 1, 1)

    return body(table, idx_flat)


def _gmm_kernel(be_ref, nv_ref, nb_ref, first_ref, run_ref, slot_ref, rexp_ref, xs_ref, wg_hbm, wu_hbm, wd_hbm,
                y_ref, wg_buf, wu_buf, wd_buf, sem):
    def weight_copies(e, s):
        return (pltpu.make_async_copy(wg_hbm.at[e], wg_buf.at[s], sem.at[s]),
                pltpu.make_async_copy(wu_hbm.at[e], wu_buf.at[s], sem.at[s]),
                pltpu.make_async_copy(wd_hbm.at[e], wd_buf.at[s], sem.at[s]))

    def fetch(run, slot):
        e = rexp_ref[run]

        @pl.when(e >= 0)
        def _():
            for c in weight_copies(e, slot):
                c.start()

    j0 = pl.program_id(0) * MOE_STEP_BLOCKS

    @pl.when(j0 < nb_ref[0])
    def _():
        for sb in range(MOE_STEP_BLOCKS):
            j = j0 + sb

            @pl.when(j == 0)
            def _():
                for a in range(WEIGHT_AHEAD):
                    fetch(a, a)

            @pl.when((j < nb_ref[0]) & (first_ref[j] == 1))
            def _():
                s = slot_ref[j]
                for c in weight_copies(be_ref[j], s):
                    c.wait()
                fetch(run_ref[j] + WEIGHT_AHEAD, (s + WEIGHT_AHEAD) % WEIGHT_SLOTS)

        for sb in range(MOE_STEP_BLOCKS):
            j = j0 + sb
            s = slot_ref[j]
            rows = slice(sb * MOE_ROWS, (sb + 1) * MOE_ROWS)
            x = _unpack_pairs_native(xs_ref[rows, :])
            row = lax.broadcasted_iota(I32, (x.shape[0], 1), 0)
            x = jnp.where(row < nv_ref[j], x, 0.0)
            hmid = _silu(_dot(x, wg_buf[s])) * _dot(x, wu_buf[s])
            y_ref[rows, :] = _pack_pairs_native(_dot(hmid, wd_buf[s]))


def grouped_mlp(block_e, block_valid, n_used, xs, w_gate, w_up, w_down, n_blocks):
    ne, d, f = w_gate.shape
    w = xs.shape[1]
    jj = jnp.arange(block_e.shape[0], dtype=I32)
    active = jj < n_used[0]
    first = (active & ((jj == 0) | (block_e != jnp.roll(block_e, 1)))).astype(I32)
    run = jnp.cumsum(first) - 1
    slot = (run % WEIGHT_SLOTS).astype(I32)
    nbl = block_e.shape[0]
    run_expert = jnp.full((nbl + WEIGHT_SLOTS,), -1, I32).at[jnp.where(first == 1, run, nbl)].set(
        jnp.where(first == 1, block_e, -1))
    run = run.astype(I32)
    step_rows = MOE_STEP_BLOCKS * MOE_ROWS
    last = lambda g, nb: jnp.minimum(g, (nb[0] - 1) // MOE_STEP_BLOCKS)
    row_block = pl.BlockSpec((step_rows, w), lambda g, be, nv, nb, fi, rn, sl, rx: (last(g, nb), 0))
    grid_spec = pltpu.PrefetchScalarGridSpec(
        num_scalar_prefetch=7,
        grid=(n_blocks // MOE_STEP_BLOCKS,),
        in_specs=[row_block,
                  pl.BlockSpec(memory_space=pl.ANY),
                  pl.BlockSpec(memory_space=pl.ANY),
                  pl.BlockSpec(memory_space=pl.ANY)],
        out_specs=row_block,
        scratch_shapes=[pltpu.VMEM((WEIGHT_SLOTS, d, f), F32), pltpu.VMEM((WEIGHT_SLOTS, d, f), F32),
                        pltpu.VMEM((WEIGHT_SLOTS, f, d), F32), pltpu.SemaphoreType.DMA((WEIGHT_SLOTS,))],
    )
    return pl.pallas_call(
        _gmm_kernel,
        out_shape=jax.ShapeDtypeStruct(xs.shape, U32),
        grid_spec=grid_spec,
        compiler_params=_cparams("arbitrary"),
        name="grouped_mlp",
    )(block_e, block_valid, n_used, first, run, slot, run_expert, xs, w_gate, w_up, w_down)


def _combine_kernel(y_ref, xres_ref, wt_ref, g2_ref, fg_ref, o_ref):
    wt = wt_ref[...]
    routed = jnp.zeros(xres_ref.shape, F32)
    for k in range(TOP_K):
        routed = routed + wt[:, k:k + 1] * _unpack_pairs_native(y_ref[k])
    x2 = xres_ref[...] + g2_ref[...] * routed
    ms = jnp.mean(x2 * x2, axis=-1, keepdims=True)
    o_ref[...] = x2 * lax.rsqrt(ms + NORM_EPS) * fg_ref[...]


def _combine_into_kernel(prev_ref, *refs):
    del prev_ref
    _combine_kernel(*refs)


def combine(y_tok, xres, wsel_t, g2, final_g, seq, tm, first_tile, prev_out):
    t, d = xres.shape
    tiles_per_batch = seq // tm
    tile = lambda i: i + first_tile
    in_specs = [pl.BlockSpec((TOP_K, tm, d // 2), lambda i: (0, i, 0)),
                pl.BlockSpec((tm, d), lambda i: (tile(i), 0)),
                pl.BlockSpec((tm, TOP_K), lambda i: (tile(i), 0)),
                pl.BlockSpec((None, 1, d), lambda i: (tile(i) // tiles_per_batch, 0, 0)),
                pl.BlockSpec((1, d), lambda i: (0, 0))]
    args = [y_tok, xres, wsel_t, g2, final_g.reshape(1, d)]
    body, aliases = _combine_kernel, {}
    if prev_out is not None:
        body, aliases = _combine_into_kernel, {0: 0}
        in_specs = [pl.BlockSpec(memory_space=pl.ANY)] + in_specs
        args = [prev_out] + args
    return pl.pallas_call(
        body,
        out_shape=jax.ShapeDtypeStruct((t, d), F32),
        grid=(y_tok.shape[1] // tm,),
        in_specs=in_specs,
        out_specs=pl.BlockSpec((tm, d), lambda i: (tile(i), 0)),
        input_output_aliases=aliases,
        compiler_params=_cparams("arbitrary"),
        name="combine",
    )(*args)


def _pos_tables(rows, cols, dim):
    quarter = dim // 4
    omega = 1.0 / (POS_BASE ** (np.arange(quarter, dtype=np.float32) / quarter))
    ang_r = np.arange(rows, dtype=np.float32)[:, None] * omega
    ang_c = np.arange(cols, dtype=np.float32)[:, None] * omega
    emb_r = np.concatenate([np.sin(ang_r), np.cos(ang_r)], axis=-1).astype(np.float32)
    emb_c = np.concatenate([np.sin(ang_c), np.cos(ang_c)], axis=-1).astype(np.float32)
    return jnp.asarray(emb_r.reshape(rows, 1, dim // 2)), jnp.asarray(emb_c)


def kernel(x, c, ctx, c_ctx, norm1_g, norm2_g, ada_w, ada_b, w_in, hy_conv_w, hy_conv_b, hy_f_w1, hy_f_b1, hy_f_w2, hy_f_b2, hy_f_w3, hy_f_b3, hy_f_w4, hy_f_freq, hy_skip, hg_lb_logits, hg_norm_g, w_hy_out, w_hg_out, w_out, router_w, router_bias, exp_w_gate, exp_w_up, exp_w_down, sh_w_gate, sh_w_up, sh_w_down, final_g):
    bsz, seq, d = x.shape
    n_ctx = ctx.shape[1]
    hy_w = w_hy_out.shape[1]
    hg_w = w_hg_out.shape[1]
    dk = hg_norm_g.shape[1]
    n_heads = hg_w // dk
    ne = router_w.shape[2]
    l = 0

    c_rows = jnp.zeros((SUBLANES, d), F32).at[:bsz].set(c).at[bsz].set(c_ctx)
    mods = ada_vectors(c_rows, ada_w[l], ada_b[l])
    sh1, sc1, g1, sh2, sc2, g2 = [mods[:bsz, j * d:(j + 1) * d].reshape(bsz, 1, d) for j in range(N_ADA)]
    csh1 = jnp.broadcast_to(mods[bsz, 0:d].reshape(1, 1, d), (bsz, 1, d))
    csc1 = jnp.broadcast_to(mods[bsz, d:2 * d].reshape(1, 1, d), (bsz, 1, d))

    emb_r, emb_c = _pos_tables(seq // GRID_W, GRID_W, d)
    w_in_b = w_in[l].astype(BF16)
    hy_proj = 3 * hy_w
    p = in_projection(x, emb_r, emb_c, norm1_g[l], sh1, sc1, w_in_b, TOKEN_TILE)
    hg_cols = slice(hy_proj, hy_proj + 5 * hg_w)
    zero_r = jnp.zeros((n_ctx // GRID_W, 1, d // 2), F32)
    zero_c = jnp.zeros((GRID_W, d // 2), F32)
    pc = in_projection(ctx, zero_r, zero_c, norm1_g[l], csh1, csc1, w_in_b[:, hg_cols], n_ctx)

    lbs = jnp.cumsum(jax.nn.softmax(hg_lb_logits.astype(F32), axis=0), axis=0)
    lb_f, lb_b = lbs[l, 0], lbs[l, 1]
    zero_state = jnp.zeros((bsz, n_heads, dk, dk), F32)
    base = hy_proj // hg_w
    _, st_f = hgrn_scan(pc, (0, 1, 2), lb_f, zero_state, n_ctx, n_ctx, reverse=False)
    _, st_b = hgrn_scan(pc, (0, 1, 3), lb_b, zero_state, n_ctx, n_ctx, reverse=True)
    o_f, _ = hgrn_scan(p, (base, base + 1, base + 2), lb_f, st_f, seq, HG_TIME_BLOCK, reverse=False)
    y_hg, _ = hgrn_scan(p, (base, base + 1, base + 3), lb_b, st_b, seq, HG_TIME_BLOCK, reverse=True,
                        o_fwd=o_f, gate_col=base + 4, norm_g=hg_norm_g[l])

    u = short_conv(p, hy_proj, hy_conv_w[l], hy_conv_b[l], seq, CONV_TILE, hy_w)
    taps, l1 = hyena_filter_taps(seq, hy_f_w1[l], hy_f_b1[l], hy_f_w2[l], hy_f_b2[l], hy_f_w3[l], hy_f_b3[l],
                                 hy_f_w4[l], hy_f_freq[l], hy_w)
    y_hy = hyena_branch(u, bsz, seq, hy_w, taps, l1, hy_skip[l])

    gate_base = (hy_proj + 5 * hg_w) // d
    sh_gu = jnp.concatenate([sh_w_gate[l], sh_w_up[l]], axis=1).astype(BF16)
    xres, h2p, logits_t = merge_stage(
        x, emb_r, emb_c, y_hy, y_hg, p, (gate_base, gate_base + 1),
        w_hy_out[l].astype(BF16), w_hg_out[l].astype(BF16), w_out[l].astype(BF16), g1, norm2_g[l],
        sh2, sc2, g2, router_w[l].T.astype(F32), sh_gu, sh_w_down[l].astype(BF16), TOKEN_TILE)

    t = bsz * seq
    eidx, wsel, rank, counts = route(logits_t, router_bias[l], TOKEN_TILE)
    n_rows = t * TOP_K + ne * (MOE_ROWS - 1)
    n_blocks = pl.cdiv(pl.cdiv(n_rows, MOE_ROWS), MOE_STEP_BLOCKS) * MOE_STEP_BLOCKS
    dest, block_e, block_valid, n_used = dispatch_plan(counts, eidx, rank, TOKEN_TILE, n_blocks)

    dest_flat = dest.reshape(-1)
    xs = scatter_rows(dest_flat, h2p, n_blocks * MOE_ROWS)
    ys = grouped_mlp(block_e.reshape(-1), block_valid.reshape(-1), n_used.reshape(-1)[:1], xs,
                     exp_w_gate[l], exp_w_up[l], exp_w_down[l], n_blocks)
    wsel_t = wsel.T
    out = None
    for h in range(GATHER_SPLIT):
        lo = h * (t // GATHER_SPLIT)
        rng = dest[:, lo:lo + t // GATHER_SPLIT]
        y_tok = gather_rows(rng.reshape(-1), ys).reshape(TOP_K, t // GATHER_SPLIT, d // 2)
        out = combine(y_tok, xres, wsel_t, g2, final_g, seq, TOKEN_TILE, lo // TOKEN_TILE, out)
    return out.reshape(bsz, seq, d)
```

```python
import functools
import math

import numpy as np
import jax
import jax.numpy as jnp
from jax import lax
from jax.experimental import pallas as pl
from jax.experimental.pallas import tpu as pltpu
from jax.experimental.pallas import tpu_sc as plsc

F32 = jnp.float32
BF16 = jnp.bfloat16
U32 = jnp.uint32
I32 = jnp.int32
HIGHEST = lax.Precision.HIGHEST

GRID_W = 64
POS_BASE = 10000.0
NORM_EPS = 1e-6
N_ADA = 6
HY_ORDER = 2
HY_SHORT = 3
HY_DECAY_TARGET = 1e-2
HY_FAST_PCT = 0.3
HY_SLOW_PCT = 1.5
HG_HEADS = 4
HG_CHUNK = 64
N_GROUPS = 8
TOPK_GROUPS = 4
TOP_K = 8
ROUTED_SCALE = 2.5

LANES = 128
SUBLANES = 8
VMEM_LIMIT = 56 * 1024 * 1024

TOKEN_TILE = 512
HG_TIME_BLOCK = 512
HALO_ROWS = 16
CONV_TILE = 2048
DFT_P = 128
FILTER_GROUP = 8
DFT_GROUP = 16
MOE_ROWS = 256
MOE_STEP_BLOCKS = 4
WEIGHT_AHEAD = 3
WEIGHT_SLOTS = WEIGHT_AHEAD + MOE_STEP_BLOCKS
SC_ROWS = 128
GATHER_SPLIT = 4


def _cparams(*sem):
    return pltpu.CompilerParams(dimension_semantics=sem, vmem_limit_bytes=VMEM_LIMIT)


def _dot(a, b):
    return jnp.dot(a, b, preferred_element_type=F32)


def _dot_hi(a, b):
    return jnp.dot(a, b, preferred_element_type=F32, precision=HIGHEST)


def _dot_nt(a, b):
    return lax.dot_general(a, b, (((1,), (1,)), ((), ())), preferred_element_type=F32)


def _dot_tn(a, b):
    return lax.dot_general(a, b, (((0,), (0,)), ((), ())), preferred_element_type=F32)


def _silu(x):
    return x * jax.nn.sigmoid(x)


def _split_bf16(x):
    hi = x.astype(BF16)
    return hi, (x - hi.astype(F32)).astype(BF16)


def _ada_kernel(c_ref, w_ref, b_ref, o_ref):
    o_ref[...] = _dot_hi(_silu(c_ref[...]), w_ref[...]) + b_ref[...]


def ada_vectors(c_rows, ada_w, ada_b):
    r, d = c_rows.shape
    n = ada_w.shape[1]
    bn = 1024
    return pl.pallas_call(
        _ada_kernel,
        out_shape=jax.ShapeDtypeStruct((r, n), F32),
        grid=(n // bn,),
        in_specs=[pl.BlockSpec((r, d), lambda j: (0, 0)),
                  pl.BlockSpec((d, bn), lambda j: (0, j)),
                  pl.BlockSpec((1, bn), lambda j: (0, j))],
        out_specs=pl.BlockSpec((r, bn), lambda j: (0, j)),
        compiler_params=_cparams("arbitrary"),
        name="ada_vectors",
    )(c_rows, ada_w, ada_b.reshape(1, n))


def _inproj_kernel(x_ref, er_ref, ec_ref, g_ref, sh_ref, sc_ref, w_ref, o_ref, *, col_chunk):
    x = x_ref[...]
    rows, gw, d = x.shape
    half = d // 2
    xp = jnp.concatenate([x[:, :, :half] + er_ref[...], x[:, :, half:] + ec_ref[...]], axis=-1)
    xp = xp.reshape(rows * gw, d)
    ms = jnp.mean(xp * xp, axis=-1, keepdims=True)
    y = xp * lax.rsqrt(ms + NORM_EPS) * g_ref[...]
    h = (y * (1.0 + sc_ref[...]) + sh_ref[...]).astype(BF16)
    n = o_ref.shape[1]
    for j in range(n // col_chunk):
        sl = slice(j * col_chunk, (j + 1) * col_chunk)
        o_ref[:, sl] = _dot(h, w_ref[:, sl]).astype(o_ref.dtype)


def in_projection(x, emb_r, emb_c, norm_g, shift, scale, w_bf16, tm):
    b, s, d = x.shape
    n = w_bf16.shape[1]
    rows_per_batch = s // GRID_W
    rt = tm // GRID_W
    tiles_per_batch = rows_per_batch // rt
    x3 = x.reshape(b * rows_per_batch, GRID_W, d)
    col_chunk = 512
    return pl.pallas_call(
        functools.partial(_inproj_kernel, col_chunk=col_chunk),
        out_shape=jax.ShapeDtypeStruct((b * s, n), BF16),
        grid=(b * tiles_per_batch,),
        in_specs=[pl.BlockSpec((rt, GRID_W, d), lambda i: (i, 0, 0)),
                  pl.BlockSpec((rt, 1, d // 2), lambda i: (i % tiles_per_batch, 0, 0)),
                  pl.BlockSpec((GRID_W, d // 2), lambda i: (0, 0)),
                  pl.BlockSpec((1, d), lambda i: (0, 0)),
                  pl.BlockSpec((None, 1, d), lambda i: (i // tiles_per_batch, 0, 0)),
                  pl.BlockSpec((None, 1, d), lambda i: (i // tiles_per_batch, 0, 0)),
                  pl.BlockSpec((d, n), lambda i: (0, 0))],
        out_specs=pl.BlockSpec((tm, n), lambda i: (i, 0)),
        compiler_params=_cparams("arbitrary"),
        name="in_projection",
    )(x3, emb_r, emb_c, norm_g.reshape(1, d), shift, scale, w_bf16)


def _hgrn_kernel(*refs, reverse, n_chunks, final):
    if final:
        (q_ref, i_ref, f_ref, lb_ref, s0_ref, of_ref, gate_ref, ng_ref, o_ref, sfin_ref, s_scr) = refs
    else:
        (q_ref, i_ref, f_ref, lb_ref, s0_ref, o_ref, sfin_ref, s_scr) = refs
    cs = HG_CHUNK
    bsz, n_heads, _, dk = s_scr.shape

    @pl.when(pl.program_id(0) == 0)
    def _():
        s_scr[...] = s0_ref[...]

    row = lax.broadcasted_iota(I32, (cs, cs), 0)
    col = lax.broadcasted_iota(I32, (cs, cs), 1)
    tri = (col >= row) if reverse else (col <= row)
    tri_b = tri.astype(BF16)
    end_row = 0 if reverse else cs - 1
    mid_row = cs // 2 if reverse else cs // 2 - 1

    def chunk_body(bi, ci):
        c = (n_chunks - 1 - ci) if reverse else ci
        rows = slice(c * cs, (c + 1) * cs)
        lb = lb_ref[...]
        f = lb + (1.0 - lb) * jax.nn.sigmoid(f_ref[bi, rows, :].astype(F32))
        lf_hi, lf_lo = _split_bf16(jnp.log(f))
        b_all = _dot(tri_b, lf_hi) + _dot(tri_b, lf_lo)
        k_all = 1.0 - f
        q_all = _silu(q_ref[bi, rows, :].astype(F32))
        for h in range(n_heads):
            sl = slice(h * dk, (h + 1) * dk)
            b = b_all[:, sl]
            q = q_all[:, sl]
            k = k_all[:, sl]
            v = i_ref[bi, rows, sl]
            b_end = b[end_row:end_row + 1]
            b_mid = b[mid_row:mid_row + 1]
            qd = (q * jnp.exp(b - b_mid)).astype(BF16)
            kd = (k * jnp.exp(b_mid - b)).astype(BF16)
            att = jnp.where(tri, _dot_nt(qd, kd), 0.0).astype(BF16)
            st = s_scr[bi, h]
            qe = (q * jnp.exp(b)).astype(BF16)
            o = _dot(att, v) + _dot_nt(qe, st.astype(BF16))
            ke = (k * jnp.exp(b_end - b)).astype(BF16)
            s_scr[bi, h] = st * jnp.exp(b_end) + _dot_tn(v, ke)
            if final:
                o = o + of_ref[bi, rows, sl].astype(F32)
                o = o * lax.rsqrt(jnp.mean(o * o, axis=-1, keepdims=True) + NORM_EPS) * ng_ref[...]
                o = o * _silu(gate_ref[bi, rows, sl].astype(F32))
            o_ref[bi, rows, sl] = o.astype(o_ref.dtype)

    for ci in range(n_chunks):
        for bi in range(bsz):
            chunk_body(bi, ci)
    sfin_ref[...] = s_scr[...]


def hgrn_scan(p, cols, lb, s0, seq, tb, *, reverse, o_fwd=None, gate_col=None, norm_g=None):
    bsz, n_heads, dv, dk = s0.shape
    width = n_heads * dk
    nt = seq // tb
    final = o_fwd is not None
    p3 = p.reshape(bsz, seq, p.shape[1])
    tmap = (lambda t: nt - 1 - t) if reverse else (lambda t: t)
    colspec = lambda cb: pl.BlockSpec((bsz, tb, width), lambda t: (0, tmap(t), cb))
    state = pl.BlockSpec((bsz, n_heads, dv, dk), lambda t: (0, 0, 0, 0))
    in_specs = [colspec(cols[0]), colspec(cols[1]), colspec(cols[2]),
                pl.BlockSpec((1, width), lambda t: (0, 0)), state]
    args = [p3, p3, p3, lb.reshape(1, width), s0]
    if final:
        in_specs += [colspec(0), colspec(gate_col), pl.BlockSpec((1, dk), lambda t: (0, 0))]
        args += [o_fwd.reshape(bsz, seq, width), p3, norm_g.reshape(1, dk)]
    o, s_fin = pl.pallas_call(
        functools.partial(_hgrn_kernel, reverse=reverse, n_chunks=tb // HG_CHUNK, final=final),
        out_shape=(jax.ShapeDtypeStruct((bsz, seq, width), BF16),
                   jax.ShapeDtypeStruct((bsz, n_heads, dv, dk), F32)),
        grid=(nt,),
        in_specs=in_specs,
        out_specs=(colspec(0), state),
        scratch_shapes=[pltpu.VMEM((bsz, n_heads, dv, dk), F32)],
        compiler_params=_cparams("arbitrary"),
        name="hgrn_bwd" if reverse else "hgrn_fwd",
    )(*args)
    return o.reshape(bsz * seq, width), s_fin


def _shortconv_kernel(p_ref, prev_ref, next_ref, w_ref, b_ref, o_ref, *, tiles_per_batch):
    i = pl.program_id(0)
    ti = i % tiles_per_batch
    p = p_ref[...].astype(F32)
    tm = p.shape[0]
    row = lax.broadcasted_iota(I32, (tm, 1), 0)
    prev_row = jnp.where(ti == 0, 0.0, prev_ref[HALO_ROWS - 1:HALO_ROWS, :].astype(F32))
    next_row = jnp.where(ti == tiles_per_batch - 1, 0.0, next_ref[0:1, :].astype(F32))
    p_prev = jnp.where(row == 0, prev_row, pltpu.roll(p, 1, axis=0))
    p_next = jnp.where(row == tm - 1, next_row, pltpu.roll(p, tm - 1, axis=0))
    u = w_ref[0:1, :] * p_prev + w_ref[1:2, :] * p + w_ref[2:3, :] * p_next + b_ref[...]
    o_ref[...] = jnp.swapaxes(u.reshape(tm // DFT_P, DFT_P, u.shape[1]), 0, 1).astype(o_ref.dtype)


def short_conv(p, width, conv_w, conv_b, seq, tm, cw):
    t = p.shape[0]
    nt = t // tm
    tiles_per_batch = seq // tm
    sub = tm // HALO_ROWS
    ta = tm // DFT_P
    return pl.pallas_call(
        functools.partial(_shortconv_kernel, tiles_per_batch=tiles_per_batch),
        out_shape=jax.ShapeDtypeStruct((DFT_P, t // DFT_P, width), BF16),
        grid=(nt, width // cw),
        in_specs=[pl.BlockSpec((tm, cw), lambda i, j: (i, j)),
                  pl.BlockSpec((HALO_ROWS, cw), lambda i, j: (jnp.maximum(i * sub - 1, 0), j)),
                  pl.BlockSpec((HALO_ROWS, cw), lambda i, j: (jnp.minimum((i + 1) * sub, t // HALO_ROWS - 1), j)),
                  pl.BlockSpec((HY_SHORT, cw), lambda i, j: (0, j)),
                  pl.BlockSpec((1, cw), lambda i, j: (0, j))],
        out_specs=pl.BlockSpec((DFT_P, ta, cw), lambda i, j: (0, i, j)),
        compiler_params=_cparams("arbitrary", "arbitrary"),
        name="short_conv",
    )(p, p, p, conv_w, conv_b.reshape(1, width))


def _filter_kernel(band_ref, w1t_ref, w1c_ref, w1s_ref, b1_ref, w2_ref, b2_ref, w3_ref, b3_ref,
                   w4f_ref, w4b_ref, fr_ref, delta_ref, k_ref, s_ref, *, seq):
    step = pl.program_id(0)
    gb, q, ncol = k_ref.shape
    half = q // 2
    width = delta_ref.shape[1]
    nrow = gb * q
    nf = gb * half

    def positions(shape, axis):
        r = lax.broadcasted_iota(I32, shape, axis)
        is_bwd = r >= nf
        rr = jnp.where(is_bwd, r - nf, r)
        j = lax.shift_right_logical(rr, int(math.log2(half)))
        a = (rr & (half - 1)) + jnp.where(is_bwd, half, 0)
        n = (a * DFT_P + step * gb + j).astype(F32)
        t = jnp.where(is_bwd, 2.0 * seq - n, n)
        return n, t, t / float(max(seq - 1, 1))

    _, t_l, tn_l = positions((1, nrow), 1)
    ang = (2.0 * math.pi / seq) * t_l * band_ref[...]
    fr = fr_ref[...]
    pre = (w1t_ref[...] * tn_l + _dot_hi(w1c_ref[...], jnp.cos(ang)) - _dot_hi(w1s_ref[...], jnp.sin(ang))
           + b1_ref[...])
    act = jnp.sin(fr * pre)
    act = jnp.sin(fr * (_dot_hi(w2_ref[...], act) + b2_ref[...]))
    act = jnp.sin(fr * (_dot_hi(w3_ref[...], act) + b3_ref[...])).astype(BF16)
    n_s, _, tn_s = positions((nrow, 1), 0)
    delta = jnp.concatenate([delta_ref[...]] * (ncol // width), axis=1)
    hf = _dot_tn(act[:, :nf], w4f_ref[...]) * jnp.exp(-tn_s[:nf] * delta)
    hb = _dot_tn(act[:, nf:], w4b_ref[...]) * jnp.exp(-tn_s[nf:] * delta)
    hb = jnp.where(n_s[nf:] == float(seq), 0.0, hb)
    k_ref[:, :half, :] = hf.reshape(gb, half, ncol).astype(k_ref.dtype)
    k_ref[:, half:, :] = hb.reshape(gb, half, ncol).astype(k_ref.dtype)
    tot = jnp.sum(jnp.abs(hf), axis=0, keepdims=True) + jnp.sum(jnp.abs(hb), axis=0, keepdims=True)

    @pl.when(step == 0)
    def _():
        s_ref[...] = jnp.zeros_like(s_ref)

    s_ref[...] += tot


def hyena_filter_taps(seq, w1, b1, w2, b2, w3, b3, w4, freq, width):
    emb = w1.shape[0]
    hid = w1.shape[1]
    bands = (emb - 1) // 2
    q = 2 * seq // DFT_P
    ncol = HY_ORDER * width
    band = np.linspace(1e-4, bands - 1, bands, dtype=np.float32).reshape(bands, 1)
    min_decay = math.log(HY_DECAY_TARGET) / HY_SLOW_PCT
    max_decay = math.log(HY_DECAY_TARGET) / HY_FAST_PCT
    delta = np.abs(np.linspace(min_decay, max_decay, width, dtype=np.float32)).reshape(1, width)
    w1t = w1.astype(F32).T
    col = lambda v: v.reshape(hid, 1).astype(F32)
    w4r = w4.astype(BF16).reshape(hid, HY_ORDER, 2, width)
    w4f = w4r[:, :, 0, :].reshape(hid, ncol)
    w4b = w4r[:, :, 1, :].reshape(hid, ncol)
    gb = FILTER_GROUP
    const = lambda shape: pl.BlockSpec(shape, lambda i: tuple(0 for _ in shape))
    return pl.pallas_call(
        functools.partial(_filter_kernel, seq=seq),
        out_shape=(jax.ShapeDtypeStruct((DFT_P, q, ncol), BF16),
                   jax.ShapeDtypeStruct((1, ncol), F32)),
        grid=(DFT_P // gb,),
        in_specs=[const((bands, 1)), const((hid, 1)), const((hid, bands)), const((hid, bands)), const((hid, 1)),
                  const((hid, hid)), const((hid, 1)), const((hid, hid)), const((hid, 1)),
                  const((hid, ncol)), const((hid, ncol)), const((hid, 1)), const((1, width))],
        out_specs=(pl.BlockSpec((gb, q, ncol), lambda i: (i, 0, 0)),
                   pl.BlockSpec((1, ncol), lambda i: (0, 0))),
        compiler_params=_cparams("arbitrary"),
        name="hyena_filter",
    )(jnp.asarray(band), w1t[:, 0:1], w1t[:, 1:1 + bands], w1t[:, 1 + bands:1 + 2 * bands], col(b1),
      w2.astype(F32).T, col(b2), w3.astype(F32).T, col(b3), w4f, w4b, col(freq), jnp.asarray(delta))


def _dft_tables(seq):
    p = DFT_P
    n_fft = 2 * seq
    q = n_fft // p
    qh = q // 2
    ka = np.arange(q)
    nn = np.arange(q)[None, :] * p + np.arange(p)[:, None]
    ang = ((ka[None, :, None] * nn[:, None, :]) % n_fft) * (2.0 * np.pi / n_fft)
    mr, mi = np.cos(ang), -np.sin(ang)
    m1c = np.concatenate([np.concatenate([mr[:, :, :qh], -mi[:, :, :qh]], axis=2),
                          np.concatenate([mi[:, :, :qh], mr[:, :, :qh]], axis=2)], axis=1)
    m1r = np.concatenate([mr, mi], axis=1)
    gr = np.swapaxes(mr[:, :, :qh], 1, 2) / n_fft
    gi = -np.swapaxes(mi[:, :, :qh], 1, 2) / n_fft
    m4 = np.concatenate([np.concatenate([gr, -gi], axis=2), np.concatenate([gi, gr], axis=2)], axis=1)
    kb = np.arange(p)
    ang2 = 2.0 * np.pi * ((kb[:, None] * kb[None, :]) % p) / p
    fr, fi = np.cos(ang2), -np.sin(ang2)
    m2 = np.block([[fr, -fi], [fi, fr]])
    m3 = np.block([[fr, fi], [-fi, fr]])
    return tuple(jnp.asarray(m.astype(np.float32).astype(BF16)) for m in (m1c, m1r, m2, m3, m4))


def _store_swapped(o_ref, res):
    g2 = o_ref.shape[0]
    stacked = jnp.stack(res, axis=0)
    for ri in range(2):
        o_ref[:, ri, :, :] = jnp.swapaxes(stacked[:, ri * g2:(ri + 1) * g2, :], 0, 1).astype(o_ref.dtype)


def _swapped_out(g, m, ncols, gb, n_col_blocks=1):
    shape = (m // 2, 2, g, ncols * n_col_blocks)
    if n_col_blocks == 1:
        return shape, pl.BlockSpec((m // 2, 2, gb, ncols), lambda i: (0, 0, i, 0))
    return shape, pl.BlockSpec((m // 2, 2, gb, ncols), lambda i, j: (0, 0, i, j))


def _bmm_kernel(w_ref, x_ref, o_ref, *, shared_w):
    res = []
    for j in range(x_ref.shape[0]):
        w = w_ref[...] if shared_w else w_ref[j]
        res.append(_dot(w, x_ref[j]))
    _store_swapped(o_ref, res)


def batched_left_matmul(w, x, col_block, ncols, name, gb, n_col_blocks=1):
    g, k = x.shape[0], x.shape[1]
    shared = w.ndim == 2
    m = w.shape[-2]
    out_shape, out_spec = _swapped_out(g, m, ncols, gb, n_col_blocks)
    if n_col_blocks == 1:
        grid = (g // gb,)
        wspec = (pl.BlockSpec((m, k), lambda i: (0, 0)) if shared else pl.BlockSpec((gb, m, k), lambda i: (i, 0, 0)))
        xspec = pl.BlockSpec((gb, k, ncols), lambda i: (i, 0, col_block))
        sem = ("arbitrary",)
    else:
        grid = (g // gb, n_col_blocks)
        wspec = (pl.BlockSpec((m, k), lambda i, j: (0, 0)) if shared
                 else pl.BlockSpec((gb, m, k), lambda i, j: (i, 0, 0)))
        xspec = pl.BlockSpec((gb, k, ncols), lambda i, j: (i, 0, j))
        sem = ("arbitrary", "arbitrary")
    out = pl.pallas_call(
        functools.partial(_bmm_kernel, shared_w=shared),
        out_shape=jax.ShapeDtypeStruct(out_shape, BF16),
        grid=grid,
        in_specs=[wspec, xspec],
        out_specs=out_spec,
        compiler_params=_cparams(*sem),
        name=name,
    )(w, x)
    return out.reshape(m // 2, 2 * g, out_shape[3])


def _dft_mid_kernel(m2_ref, m3_ref, x_ref, k_ref, o_ref):
    half = x_ref.shape[1] // 2
    res = []
    for j in range(x_ref.shape[0]):
        xf = _dot(m2_ref[...], x_ref[j])
        kf = _dot(m2_ref[...], k_ref[j])
        xr, xi = xf[:half], xf[half:]
        kr, ki = kf[:half], kf[half:]
        z = jnp.concatenate([xr * kr - xi * ki, xr * ki + xi * kr], axis=0).astype(BF16)
        res.append(_dot(m3_ref[...], z))
    _store_swapped(o_ref, res)


def dft_mid(m2, m3, x, kspec, kcol, ncols):
    g, r = x.shape[0], x.shape[1]
    gb = DFT_GROUP
    out_shape, out_spec = _swapped_out(g, r, ncols, gb)
    out = pl.pallas_call(
        _dft_mid_kernel,
        out_shape=jax.ShapeDtypeStruct(out_shape, BF16),
        grid=(g // gb,),
        in_specs=[pl.BlockSpec((r, r), lambda i: (0, 0)),
                  pl.BlockSpec((r, r), lambda i: (0, 0)),
                  pl.BlockSpec((gb, r, ncols), lambda i: (i, 0, 0)),
                  pl.BlockSpec((gb, r, ncols), lambda i: (i, 0, kcol))],
        out_specs=out_spec,
        compiler_params=_cparams("arbitrary"),
        name="dft_mid",
    )(m2, m3, x, kspec)
    return out.reshape(r // 2, 2 * g, ncols)


def _dft_out_kernel(m4_ref, y_ref, inv_ref, skip_ref, v_ref, mul_ref, o_ref, *, token_order):
    res = []
    for j in range(y_ref.shape[0]):
        conv = _dot(m4_ref[j], y_ref[j]) * inv_ref[...] + v_ref[j].astype(F32) * skip_ref[...]
        res.append(mul_ref[j].astype(F32) * conv)
    if token_order:
        o_ref[...] = jnp.swapaxes(jnp.stack(res, axis=0), 0, 1).astype(o_ref.dtype)
    else:
        for j, r in enumerate(res):
            o_ref[j] = r.astype(o_ref.dtype)


def dft_out(m4, y, inv_l1, skip, u, v_col, mul, mul_col, ncols, token_order):
    g, r = y.shape[0], y.shape[1]
    rows = m4.shape[1]
    gb = DFT_GROUP
    out_shape, out_spec = (((rows, g, ncols), pl.BlockSpec((rows, gb, ncols), lambda i: (0, i, 0))) if token_order
                           else ((g, rows, ncols), pl.BlockSpec((gb, rows, ncols), lambda i: (i, 0, 0))))
    return pl.pallas_call(
        functools.partial(_dft_out_kernel, token_order=token_order),
        out_shape=jax.ShapeDtypeStruct(out_shape, BF16),
        grid=(g // gb,),
        in_specs=[pl.BlockSpec((gb, rows, r), lambda i: (i, 0, 0)),
                  pl.BlockSpec((gb, r, ncols), lambda i: (i, 0, 0)),
                  pl.BlockSpec((1, ncols), lambda i: (0, 0)),
                  pl.BlockSpec((1, ncols), lambda i: (0, 0)),
                  pl.BlockSpec((gb, rows, ncols), lambda i: (i, 0, v_col)),
                  pl.BlockSpec((gb, rows, ncols), lambda i: (i, 0, mul_col))],
        out_specs=out_spec,
        compiler_params=_cparams("arbitrary"),
        name="dft_out",
    )(m4, y, inv_l1, skip, u, mul)


def hyena_branch(u, bsz, seq, width, taps, l1, skip):
    m1c, m1r, m2, m3, m4 = _dft_tables(seq)
    ks1 = batched_left_matmul(m1r, taps, 0, width, "dft_k1", DFT_GROUP, n_col_blocks=HY_ORDER)
    inv_l1 = 1.0 / l1
    z = None
    for order in range(HY_ORDER):
        src, src_col = (u, 0) if order == 0 else (z, 0)
        s1 = batched_left_matmul(m1c, src, src_col, width, "dft_s1", DFT_GROUP)
        mid = dft_mid(m2, m3, s1, ks1, order, width)
        z = dft_out(m4, mid, inv_l1[:, order * width:(order + 1) * width],
                    skip[order].reshape(1, width).astype(F32), src, src_col, u, order + 1, width,
                    token_order=order == HY_ORDER - 1)
    return z.reshape(bsz * seq, width)


def _pack_pairs(x):
    w = x.shape[1] // 2
    u = lax.bitcast_convert_type(x, U32)
    r = (u + U32(0x7FFF) + ((u >> 16) & U32(1))) >> 16
    return r[:, :w] | (r[:, w:] << 16)


def _unpack_pairs(p):
    lo = lax.bitcast_convert_type(p << 16, F32)
    hi = lax.bitcast_convert_type(p & U32(0xFFFF0000), F32)
    return jnp.concatenate([lo, hi], axis=1)


def _pack_pairs_native(x):
    w = x.shape[1] // 2
    return lax.bitcast_convert_type(pltpu.pack_elementwise([x[:, :w], x[:, w:]], packed_dtype=BF16), U32)


def _unpack_pairs_native(p):
    pi = lax.bitcast_convert_type(p, I32)
    halves = [pltpu.unpack_elementwise(pi, index=i, packed_dtype=BF16, unpacked_dtype=F32) for i in range(2)]
    return jnp.concatenate(halves, axis=1)


def _merge_kernel(x_ref, er_ref, ec_ref, yhy_ref, yhg_ref, ghy_ref, ghg_ref, why_ref, whg_ref, wo_ref,
                  g1_ref, n2_ref, sh2_ref, sc2_ref, g2_ref, rwh_ref, rwl_ref, sgu_ref, sd_ref,
                  xres_ref, h2p_ref, lg_ref):
    x = x_ref[...]
    rows, gw, d = x.shape
    half = d // 2
    xp = jnp.concatenate([x[:, :, :half] + er_ref[...], x[:, :, half:] + ec_ref[...]], axis=-1)
    xp = xp.reshape(rows * gw, d)
    m = (jax.nn.sigmoid(ghy_ref[...].astype(F32)) * _dot(yhy_ref[...], why_ref[...])
         + jax.nn.sigmoid(ghg_ref[...].astype(F32)) * _dot(yhg_ref[...], whg_ref[...]))
    x1 = xp + g1_ref[...] * _dot(m.astype(BF16), wo_ref[...])
    ms = jnp.mean(x1 * x1, axis=-1, keepdims=True)
    h2 = x1 * lax.rsqrt(ms + NORM_EPS) * n2_ref[...] * (1.0 + sc2_ref[...]) + sh2_ref[...]
    h_hi, h_lo = _split_bf16(h2)
    lg_ref[...] = _dot_nt(rwh_ref[...], h_hi) + (_dot_nt(rwl_ref[...], h_hi) + _dot_nt(rwh_ref[...], h_lo))
    gu = _dot(h_hi, sgu_ref[...])
    fs = gu.shape[1] // 2
    shared = _dot((_silu(gu[:, :fs]) * gu[:, fs:]).astype(BF16), sd_ref[...])
    xres_ref[...] = x1 + g2_ref[...] * shared
    h2p_ref[...] = _pack_pairs(h2)


def merge_stage(x, emb_r, emb_c, y_hy, y_hg, p, gate_cols, w_hy_out, w_hg_out, w_out, g1, norm2_g,
                sh2, sc2, g2, router_wt, sh_gate_up, sh_down, tm):
    b, s, d = x.shape
    rows_per_batch = s // GRID_W
    rt = tm // GRID_W
    tiles_per_batch = rows_per_batch // rt
    x3 = x.reshape(b * rows_per_batch, GRID_W, d)
    wb = y_hy.shape[1]
    ne = router_wt.shape[0]
    fs2 = sh_gate_up.shape[1]
    rw_hi, rw_lo = _split_bf16(router_wt)
    tok = lambda cb, w: pl.BlockSpec((tm, w), lambda i: (i, cb))
    const = lambda shape: pl.BlockSpec(shape, lambda i: tuple(0 for _ in shape))
    per_b = pl.BlockSpec((None, 1, d), lambda i: (i // tiles_per_batch, 0, 0))
    return pl.pallas_call(
        _merge_kernel,
        out_shape=(jax.ShapeDtypeStruct((b * s, d), F32),
                   jax.ShapeDtypeStruct((b * s, d // 2), U32),
                   jax.ShapeDtypeStruct((ne, b * s), F32)),
        grid=(b * tiles_per_batch,),
        in_specs=[pl.BlockSpec((rt, GRID_W, d), lambda i: (i, 0, 0)),
                  pl.BlockSpec((rt, 1, d // 2), lambda i: (i % tiles_per_batch, 0, 0)),
                  const((GRID_W, d // 2)),
                  tok(0, wb), tok(0, wb), tok(gate_cols[0], d), tok(gate_cols[1], d),
                  const((wb, d)), const((wb, d)), const((d, d)),
                  per_b, const((1, d)), per_b, per_b, per_b,
                  const((ne, d)), const((ne, d)), const((d, fs2)), const((fs2 // 2, d))],
        out_specs=(pl.BlockSpec((tm, d), lambda i: (i, 0)),
                   pl.BlockSpec((tm, d // 2), lambda i: (i, 0)),
                   pl.BlockSpec((ne, tm), lambda i: (0, i))),
        compiler_params=_cparams("arbitrary"),
        name="merge",
    )(x3, emb_r, emb_c, y_hy, y_hg, p, p, w_hy_out, w_hg_out, w_out, g1, norm2_g.reshape(1, d),
      sh2, sc2, g2, rw_hi, rw_lo, sh_gate_up, sh_down)


def _route_kernel(lg_ref, bias_ref, eidx_ref, wsel_ref, rank_ref, cnt_ref, carry):
    ne, tr = lg_ref.shape
    gsz = ne // N_GROUPS
    neg = -jnp.inf

    @pl.when(pl.program_id(0) == 0)
    def _():
        carry[...] = jnp.zeros_like(carry)

    scores = jax.nn.sigmoid(lg_ref[...])
    biased = scores + bias_ref[...]
    riota = lax.broadcasted_iota(I32, (gsz, tr), 0).astype(F32)
    gs = []
    for g in range(N_GROUPS):
        vg = biased[g * gsz:(g + 1) * gsz]
        m1 = jnp.max(vg, axis=0, keepdims=True)
        i1 = jnp.min(jnp.where(vg == m1, riota, float(gsz)), axis=0, keepdims=True)
        m2 = jnp.max(jnp.where(riota == i1, neg, vg), axis=0, keepdims=True)
        gs.append(m1 + m2)
    cur = jnp.concatenate(gs, axis=0)
    giota = lax.broadcasted_iota(I32, (N_GROUPS, tr), 0).astype(F32)
    gsel = jnp.zeros((N_GROUPS, tr), F32)
    for _ in range(TOPK_GROUPS):
        m = jnp.max(cur, axis=0, keepdims=True)
        idx = jnp.min(jnp.where(cur == m, giota, float(N_GROUPS)), axis=0, keepdims=True)
        hit = giota == idx
        gsel = jnp.where(hit, 1.0, gsel)
        cur = jnp.where(hit, neg, cur)
    cur = jnp.concatenate([jnp.where(gsel[g:g + 1] > 0.0, biased[g * gsz:(g + 1) * gsz], neg)
                           for g in range(N_GROUPS)], axis=0)
    eiota = lax.broadcasted_iota(I32, (ne, tr), 0).astype(F32)
    chosen = jnp.zeros((ne, tr), F32)
    idxs, ws = [], []
    for _ in range(TOP_K):
        m = jnp.max(cur, axis=0, keepdims=True)
        idx = jnp.min(jnp.where(cur == m, eiota, float(ne)), axis=0, keepdims=True)
        hit = eiota == idx
        idxs.append(idx)
        ws.append(jnp.sum(jnp.where(hit, scores, 0.0), axis=0, keepdims=True))
        chosen = jnp.where(hit, 1.0, chosen)
        cur = jnp.where(hit, neg, cur)
    w = jnp.concatenate(ws, axis=0)
    wsel_ref[...] = w / jnp.sum(w, axis=0, keepdims=True) * ROUTED_SCALE
    eidx_ref[...] = jnp.concatenate(idxs, axis=0).astype(I32)
    srow = lax.broadcasted_iota(I32, (tr, tr), 0)
    scol = lax.broadcasted_iota(I32, (tr, tr), 1)
    before = (srow < scol).astype(BF16)
    base = carry[...] + _dot(chosen.astype(BF16), before)
    ranks = [jnp.sum(jnp.where(eiota == idx, base, 0.0), axis=0, keepdims=True) for idx in idxs]
    rank_ref[...] = jnp.concatenate(ranks, axis=0).astype(I32)
    carry[...] += jnp.sum(chosen, axis=1, keepdims=True)
    cnt_ref[...] = carry[...]


def route(logits_t, router_bias, tr):
    ne, t = logits_t.shape
    return pl.pallas_call(
        _route_kernel,
        out_shape=(jax.ShapeDtypeStruct((TOP_K, t), I32),
                   jax.ShapeDtypeStruct((TOP_K, t), F32),
                   jax.ShapeDtypeStruct((TOP_K, t), I32),
                   jax.ShapeDtypeStruct((ne, 1), F32)),
        grid=(t // tr,),
        in_specs=[pl.BlockSpec((ne, tr), lambda i: (0, i)),
                  pl.BlockSpec((ne, 1), lambda i: (0, 0))],
        out_specs=(pl.BlockSpec((TOP_K, tr), lambda i: (0, i)),
                   pl.BlockSpec((TOP_K, tr), lambda i: (0, i)),
                   pl.BlockSpec((TOP_K, tr), lambda i: (0, i)),
                   pl.BlockSpec((ne, 1), lambda i: (0, 0))),
        scratch_shapes=[pltpu.VMEM((ne, 1), F32)],
        compiler_params=_cparams("arbitrary"),
        name="route",
    )(logits_t, router_bias.reshape(ne, 1).astype(F32))


def _dest_kernel(cnt_ref, eidx_ref, rank_ref, dest_ref, be_ref, nv_ref, nb_ref, start_scr):
    ne = cnt_ref.shape[0]
    tr = eidx_ref.shape[1]

    @pl.when(pl.program_id(0) == 0)
    def _():
        cnt = jnp.broadcast_to(cnt_ref[...], (ne, LANES))
        padded = jnp.floor((cnt + float(MOE_ROWS - 1)) / float(MOE_ROWS)) * float(MOE_ROWS)
        r = lax.broadcasted_iota(I32, (ne, ne), 0)
        c = lax.broadcasted_iota(I32, (ne, ne), 1)
        start = _dot_hi((c < r).astype(F32), padded)
        start_scr[...] = start
        end = start[:, 0:1] + padded[:, 0:1]
        used = start[:, 0:1] + cnt[:, 0:1]
        nbl = be_ref.shape[1]
        blk_row = (lax.broadcasted_iota(I32, (1, nbl), 1) * MOE_ROWS).astype(F32)
        total = jnp.max(end, axis=0, keepdims=True)
        last_row = total - float(MOE_ROWS)
        blk_row_c = jnp.minimum(blk_row, last_row)
        e_of = jnp.sum((end <= blk_row_c).astype(F32), axis=0, keepdims=True)
        e_of = jnp.minimum(e_of, float(ne - 1))
        eio = lax.broadcasted_iota(I32, (ne, nbl), 0).astype(F32)
        used_e = jnp.sum(jnp.where(eio == e_of, used, 0.0), axis=0, keepdims=True)
        valid = jnp.clip(used_e - blk_row_c, 0.0, float(MOE_ROWS))
        be_ref[...] = e_of.astype(I32)
        nv_ref[...] = jnp.where(blk_row <= last_row, valid, 0.0).astype(I32)
        nb_ref[...] = jnp.broadcast_to(total / float(MOE_ROWS), nb_ref.shape).astype(I32)

    eiota = lax.broadcasted_iota(I32, (ne, tr), 0)
    start_col = start_scr[:, 0:1]
    rows = []
    for k in range(TOP_K):
        hit = eiota == eidx_ref[k:k + 1, :]
        rows.append(jnp.sum(jnp.where(hit, start_col, 0.0), axis=0, keepdims=True))
    dest_ref[...] = jnp.concatenate(rows, axis=0).astype(I32) + rank_ref[...]


def dispatch_plan(counts, eidx, rank, tr, n_blocks):
    ne = counts.shape[0]
    t = eidx.shape[1]
    nbl = pl.cdiv(n_blocks, LANES) * LANES
    return pl.pallas_call(
        _dest_kernel,
        out_shape=(jax.ShapeDtypeStruct((TOP_K, t), I32),
                   jax.ShapeDtypeStruct((1, nbl), I32),
                   jax.ShapeDtypeStruct((1, nbl), I32),
                   jax.ShapeDtypeStruct((1, LANES), I32)),
        grid=(t // tr,),
        in_specs=[pl.BlockSpec((ne, 1), lambda i: (0, 0)),
                  pl.BlockSpec((TOP_K, tr), lambda i: (0, i)),
                  pl.BlockSpec((TOP_K, tr), lambda i: (0, i))],
        out_specs=(pl.BlockSpec((TOP_K, tr), lambda i: (0, i)),
                   pl.BlockSpec((1, nbl), lambda i: (0, 0)),
                   pl.BlockSpec((1, nbl), lambda i: (0, 0)),
                   pl.BlockSpec((1, LANES), lambda i: (0, 0))),
        scratch_shapes=[pltpu.VMEM((ne, LANES), F32)],
        compiler_params=_cparams("arbitrary"),
        name="dispatch_plan",
    )(counts, eidx, rank)


def _sc_workers():
    info = plsc.get_sparse_core_info()
    return info.num_cores, info.num_cores * info.num_subcores


def scatter_rows(dest_flat, h2p, n_rows):
    t, w = h2p.shape
    n_cores, n_workers = _sc_workers()
    per_worker = t // n_workers
    mesh = plsc.VectorSubcoreMesh(core_axis_name="c", subcore_axis_name="s")

    @functools.partial(
        pl.kernel, mesh=mesh, out_type=jax.ShapeDtypeStruct((n_rows, w), U32),
        scratch_types=[pltpu.VMEM((TOP_K, SC_ROWS), I32), pltpu.VMEM((SC_ROWS, w), U32), pltpu.SemaphoreType.DMA])
    def body(h_hbm, dest_hbm, xs_hbm, idx_v, rows_v, sem):
        base = (lax.axis_index("s") * n_cores + lax.axis_index("c")) * per_worker

        @pl.loop(0, per_worker // SC_ROWS)
        def _(ci):
            off = pl.multiple_of(base + ci * SC_ROWS, SC_ROWS)
            pltpu.sync_copy(h_hbm.at[pl.ds(off, SC_ROWS)], rows_v)
            for k in range(TOP_K):
                pltpu.sync_copy(dest_hbm.at[pl.ds(k * t + off, SC_ROWS)], idx_v.at[k])
            copies = [pltpu.async_copy(rows_v, xs_hbm.at[idx_v.at[k]], sem) for k in range(TOP_K)]
            for c in copies:
                c.wait()

    return body(h2p, dest_flat)


def gather_rows(idx_flat, table):
    n = idx_flat.shape[0]
    w = table.shape[1]
    n_cores, n_workers = _sc_workers()
    per_worker = n // n_workers
    mesh = plsc.VectorSubcoreMesh(core_axis_name="c", subcore_axis_name="s")

    ch = SC_ROWS // 2
    n_chunks = per_worker // ch

    @functools.partial(
        pl.kernel, mesh=mesh, out_type=jax.ShapeDtypeStruct((n, w), table.dtype),
        scratch_types=[pltpu.VMEM((2, ch), I32), pltpu.VMEM((2, ch, w), table.dtype), pltpu.SemaphoreType.DMA((2,))])
    def body(table_hbm, idx_hbm, out_hbm, idx_v, rows_v, sem):
        base = (lax.axis_index("s") * n_cores + lax.axis_index("c")) * per_worker

        def read(b):
            return pltpu.make_async_copy(table_hbm.at[idx_v.at[b]], rows_v.at[b], sem.at[b])

        def start(c, b):
            off = pl.multiple_of(base + c * ch, ch)
            pltpu.sync_copy(idx_hbm.at[pl.ds(off, ch)], idx_v.at[b])
            read(b).start()

        def finish(c, b):
            read(b).wait()
            pltpu.sync_copy(rows_v.at[b], out_hbm.at[pl.ds(pl.multiple_of(base + c * ch, ch), ch)])

        start(0, 0)

        @pl.loop(0, n_chunks, step=2)
        def _(c):
            start(c + 1, 1)
            finish(c, 0)

            @pl.when(c + 2 < n_chunks)
            def _():
                start(c + 2, 0)

            finish(c + 1, 1)

    return body(table, idx_flat)


def _gmm_kernel(be_ref, nv_ref, nb_ref, first_ref, run_ref, slot_ref, rexp_ref, xs_ref, wg_hbm, wu_hbm, wd_hbm,
                y_ref, wg_buf, wu_buf, wd_buf, sem):
    def weight_copies(e, s):
        return (pltpu.make_async_copy(wg_hbm.at[e], wg_buf.at[s], sem.at[s]),
                pltpu.make_async_copy(wu_hbm.at[e], wu_buf.at[s], sem.at[s]),
                pltpu.make_async_copy(wd_hbm.at[e], wd_buf.at[s], sem.at[s]))

    def fetch(run, slot):
        e = rexp_ref[run]

        @pl.when(e >= 0)
        def _():
            for c in weight_copies(e, slot):
                c.start()

    j0 = pl.program_id(0) * MOE_STEP_BLOCKS

    @pl.when(j0 < nb_ref[0])
    def _():
        for sb in range(MOE_STEP_BLOCKS):
            j = j0 + sb

            @pl.when(j == 0)
            def _():
                for a in range(WEIGHT_AHEAD):
                    fetch(a, a)

            @pl.when((j < nb_ref[0]) & (first_ref[j] == 1))
            def _():
                s = slot_ref[j]
                for c in weight_copies(be_ref[j], s):
                    c.wait()
                fetch(run_ref[j] + WEIGHT_AHEAD, (s + WEIGHT_AHEAD) % WEIGHT_SLOTS)

        for sb in range(MOE_STEP_BLOCKS):
            j = j0 + sb
            s = slot_ref[j]
            rows = slice(sb * MOE_ROWS, (sb + 1) * MOE_ROWS)
            x = _unpack_pairs_native(xs_ref[rows, :])
            row = lax.broadcasted_iota(I32, (x.shape[0], 1), 0)
            x = jnp.where(row < nv_ref[j], x, 0.0)
            hmid = _silu(_dot(x, wg_buf[s])) * _dot(x, wu_buf[s])
            y_ref[rows, :] = _pack_pairs_native(_dot(hmid, wd_buf[s]))


def grouped_mlp(block_e, block_valid, n_used, xs, w_gate, w_up, w_down, n_blocks):
    ne, d, f = w_gate.shape
    w = xs.shape[1]
    jj = jnp.arange(block_e.shape[0], dtype=I32)
    active = jj < n_used[0]
    first = (active & ((jj == 0) | (block_e != jnp.roll(block_e, 1)))).astype(I32)
    run = jnp.cumsum(first) - 1
    slot = (run % WEIGHT_SLOTS).astype(I32)
    nbl = block_e.shape[0]
    run_expert = jnp.full((nbl + WEIGHT_SLOTS,), -1, I32).at[jnp.where(first == 1, run, nbl)].set(
        jnp.where(first == 1, block_e, -1))
    run = run.astype(I32)
    step_rows = MOE_STEP_BLOCKS * MOE_ROWS
    last = lambda g, nb: jnp.minimum(g, (nb[0] - 1) // MOE_STEP_BLOCKS)
    row_block = pl.BlockSpec((step_rows, w), lambda g, be, nv, nb, fi, rn, sl, rx: (last(g, nb), 0))
    grid_spec = pltpu.PrefetchScalarGridSpec(
        num_scalar_prefetch=7,
        grid=(n_blocks // MOE_STEP_BLOCKS,),
        in_specs=[row_block,
                  pl.BlockSpec(memory_space=pl.ANY),
                  pl.BlockSpec(memory_space=pl.ANY),
                  pl.BlockSpec(memory_space=pl.ANY)],
        out_specs=row_block,
        scratch_shapes=[pltpu.VMEM((WEIGHT_SLOTS, d, f), F32), pltpu.VMEM((WEIGHT_SLOTS, d, f), F32),
                        pltpu.VMEM((WEIGHT_SLOTS, f, d), F32), pltpu.SemaphoreType.DMA((WEIGHT_SLOTS,))],
    )
    return pl.pallas_call(
        _gmm_kernel,
        out_shape=jax.ShapeDtypeStruct(xs.shape, U32),
        grid_spec=grid_spec,
        compiler_params=_cparams("arbitrary"),
        name="grouped_mlp",
    )(block_e, block_valid, n_used, first, run, slot, run_expert, xs, w_gate, w_up, w_down)


def _combine_kernel(y_ref, xres_ref, wt_ref, g2_ref, fg_ref, o_ref):
    wt = wt_ref[...]
    routed = jnp.zeros(xres_ref.shape, F32)
    for k in range(TOP_K):
        routed = routed + wt[:, k:k + 1] * _unpack_pairs_native(y_ref[k])
    x2 = xres_ref[...] + g2_ref[...] * routed
    ms = jnp.mean(x2 * x2, axis=-1, keepdims=True)
    o_ref[...] = x2 * lax.rsqrt(ms + NORM_EPS) * fg_ref[...]


def _combine_into_kernel(prev_ref, *refs):
    del prev_ref
    _combine_kernel(*refs)


def combine(y_tok, xres, wsel_t, g2, final_g, seq, tm, first_tile, prev_out):
    t, d = xres.shape
    tiles_per_batch = seq // tm
    tile = lambda i: i + first_tile
    in_specs = [pl.BlockSpec((TOP_K, tm, d // 2), lambda i: (0, i, 0)),
                pl.BlockSpec((tm, d), lambda i: (tile(i), 0)),
                pl.BlockSpec((tm, TOP_K), lambda i: (tile(i), 0)),
                pl.BlockSpec((None, 1, d), lambda i: (tile(i) // tiles_per_batch, 0, 0)),
                pl.BlockSpec((1, d), lambda i: (0, 0))]
    args = [y_tok, xres, wsel_t, g2, final_g.reshape(1, d)]
    body, aliases = _combine_kernel, {}
    if prev_out is not None:
        body, aliases = _combine_into_kernel, {0: 0}
        in_specs = [pl.BlockSpec(memory_space=pl.ANY)] + in_specs
        args = [prev_out] + args
    return pl.pallas_call(
        body,
        out_shape=jax.ShapeDtypeStruct((t, d), F32),
        grid=(y_tok.shape[1] // tm,),
        in_specs=in_specs,
        out_specs=pl.BlockSpec((tm, d), lambda i: (tile(i), 0)),
        input_output_aliases=aliases,
        compiler_params=_cparams("arbitrary"),
        name="combine",
    )(*args)


def _pos_tables(rows, cols, dim):
    quarter = dim // 4
    omega = 1.0 / (POS_BASE ** (np.arange(quarter, dtype=np.float32) / quarter))
    ang_r = np.arange(rows, dtype=np.float32)[:, None] * omega
    ang_c = np.arange(cols, dtype=np.float32)[:, None] * omega
    emb_r = np.concatenate([np.sin(ang_r), np.cos(ang_r)], axis=-1).astype(np.float32)
    emb_c = np.concatenate([np.sin(ang_c), np.cos(ang_c)], axis=-1).astype(np.float32)
    return jnp.asarray(emb_r.reshape(rows, 1, dim // 2)), jnp.asarray(emb_c)


def kernel(x, c, ctx, c_ctx, norm1_g, norm2_g, ada_w, ada_b, w_in, hy_conv_w, hy_conv_b, hy_f_w1, hy_f_b1, hy_f_w2, hy_f_b2, hy_f_w3, hy_f_b3, hy_f_w4, hy_f_freq, hy_skip, hg_lb_logits, hg_norm_g, w_hy_out, w_hg_out, w_out, router_w, router_bias, exp_w_gate, exp_w_up, exp_w_down, sh_w_gate, sh_w_up, sh_w_down, final_g):
    bsz, seq, d = x.shape
    n_ctx = ctx.shape[1]
    hy_w = w_hy_out.shape[1]
    hg_w = w_hg_out.shape[1]
    dk = hg_norm_g.shape[1]
    n_heads = hg_w // dk
    ne = router_w.shape[2]
    l = 0

    c_rows = jnp.zeros((SUBLANES, d), F32).at[:bsz].set(c).at[bsz].set(c_ctx)
    mods = ada_vectors(c_rows, ada_w[l], ada_b[l])
    sh1, sc1, g1, sh2, sc2, g2 = [mods[:bsz, j * d:(j + 1) * d].reshape(bsz, 1, d) for j in range(N_ADA)]
    csh1 = jnp.broadcast_to(mods[bsz, 0:d].reshape(1, 1, d), (bsz, 1, d))
    csc1 = jnp.broadcast_to(mods[bsz, d:2 * d].reshape(1, 1, d), (bsz, 1, d))

    emb_r, emb_c = _pos_tables(seq // GRID_W, GRID_W, d)
    w_in_b = w_in[l].astype(BF16)
    hy_proj = 3 * hy_w
    p = in_projection(x, emb_r, emb_c, norm1_g[l], sh1, sc1, w_in_b, TOKEN_TILE)
    hg_cols = slice(hy_proj, hy_proj + 5 * hg_w)
    zero_r = jnp.zeros((n_ctx // GRID_W, 1, d // 2), F32)
    zero_c = jnp.zeros((GRID_W, d // 2), F32)
    pc = in_projection(ctx, zero_r, zero_c, norm1_g[l], csh1, csc1, w_in_b[:, hg_cols], n_ctx)

    lbs = jnp.cumsum(jax.nn.softmax(hg_lb_logits.astype(F32), axis=0), axis=0)
    lb_f, lb_b = lbs[l, 0], lbs[l, 1]
    zero_state = jnp.zeros((bsz, n_heads, dk, dk), F32)
    base = hy_proj // hg_w
    _, st_f = hgrn_scan(pc, (0, 1, 2), lb_f, zero_state, n_ctx, n_ctx, reverse=False)
    _, st_b = hgrn_scan(pc, (0, 1, 3), lb_b, zero_state, n_ctx, n_ctx, reverse=True)
    o_f, _ = hgrn_scan(p, (base, base + 1, base + 2), lb_f, st_f, seq, HG_TIME_BLOCK, reverse=False)
    y_hg, _ = hgrn_scan(p, (base, base + 1, base + 3), lb_b, st_b, seq, HG_TIME_BLOCK, reverse=True,
                        o_fwd=o_f, gate_col=base + 4, norm_g=hg_norm_g[l])

    u = short_conv(p, hy_proj, hy_conv_w[l], hy_conv_b[l], seq, CONV_TILE, hy_w)
    taps, l1 = hyena_filter_taps(seq, hy_f_w1[l], hy_f_b1[l], hy_f_w2[l], hy_f_b2[l], hy_f_w3[l], hy_f_b3[l],
                                 hy_f_w4[l], hy_f_freq[l], hy_w)
    y_hy = hyena_branch(u, bsz, seq, hy_w, taps, l1, hy_skip[l])

    gate_base = (hy_proj + 5 * hg_w) // d
    sh_gu = jnp.concatenate([sh_w_gate[l], sh_w_up[l]], axis=1).astype(BF16)
    xres, h2p, logits_t = merge_stage(
        x, emb_r, emb_c, y_hy, y_hg, p, (gate_base, gate_base + 1),
        w_hy_out[l].astype(BF16), w_hg_out[l].astype(BF16), w_out[l].astype(BF16), g1, norm2_g[l],
        sh2, sc2, g2, router_w[l].T.astype(F32), sh_gu, sh_w_down[l].astype(BF16), TOKEN_TILE)

    t = bsz * seq
    eidx, wsel, rank, counts = route(logits_t, router_bias[l], TOKEN_TILE)
    n_rows = t * TOP_K + ne * (MOE_ROWS - 1)
    n_blocks = pl.cdiv(pl.cdiv(n_rows, MOE_ROWS), MOE_STEP_BLOCKS) * MOE_STEP_BLOCKS
    dest, block_e, block_valid, n_used = dispatch_plan(counts, eidx, rank, TOKEN_TILE, n_blocks)

    dest_flat = dest.reshape(-1)
    xs = scatter_rows(dest_flat, h2p, n_blocks * MOE_ROWS)
    ys = grouped_mlp(block_e.reshape(-1), block_valid.reshape(-1), n_used.reshape(-1)[:1], xs,
                     exp_w_gate[l], exp_w_up[l], exp_w_down[l], n_blocks)
    wsel_t = wsel.T
    out = None
    for h in range(GATHER_SPLIT):
        lo = h * (t // GATHER_SPLIT)
        rng = dest[:, lo:lo + t // GATHER_SPLIT]
        y_tok = gather_rows(rng.reshape(-1), ys).reshape(TOP_K, t // GATHER_SPLIT, d // 2)
        out = combine(y_tok, xres, wsel_t, g2, final_g, seq, TOKEN_TILE, lo // TOKEN_TILE, out)
    return out.reshape(bsz, seq, d)
```

```python
import functools
import math

import numpy as np
import jax
import jax.numpy as jnp
from jax import lax
from jax.experimental import pallas as pl
from jax.experimental.pallas import tpu as pltpu
from jax.experimental.pallas import tpu_sc as plsc

F32 = jnp.float32
BF16 = jnp.bfloat16
U32 = jnp.uint32
I32 = jnp.int32
HIGHEST = lax.Precision.HIGHEST

GRID_W = 64
POS_BASE = 10000.0
NORM_EPS = 1e-6
N_ADA = 6
HY_ORDER = 2
HY_SHORT = 3
HY_DECAY_TARGET = 1e-2
HY_FAST_PCT = 0.3
HY_SLOW_PCT = 1.5
HG_HEADS = 4
HG_CHUNK = 64
N_GROUPS = 8
TOPK_GROUPS = 4
TOP_K = 8
ROUTED_SCALE = 2.5

LANES = 128
SUBLANES = 8
VMEM_LIMIT = 56 * 1024 * 1024

TOKEN_TILE = 512
HG_TIME_BLOCK = 512
HALO_ROWS = 16
CONV_TILE = 2048
DFT_P = 128
FILTER_GROUP = 8
DFT_GROUP = 16
MOE_ROWS = 256
MOE_STEP_BLOCKS = 4
WEIGHT_AHEAD = 3
WEIGHT_SLOTS = WEIGHT_AHEAD + MOE_STEP_BLOCKS
SC_ROWS = 128
GATHER_SPLIT = 4


def _cparams(*sem):
    return pltpu.CompilerParams(dimension_semantics=sem, vmem_limit_bytes=VMEM_LIMIT)


def _dot(a, b):
    return jnp.dot(a, b, preferred_element_type=F32)


def _dot_hi(a, b):
    return jnp.dot(a, b, preferred_element_type=F32, precision=HIGHEST)


def _dot_nt(a, b):
    return lax.dot_general(a, b, (((1,), (1,)), ((), ())), preferred_element_type=F32)


def _dot_tn(a, b):
    return lax.dot_general(a, b, (((0,), (0,)), ((), ())), preferred_element_type=F32)


def _silu(x):
    return x * jax.nn.sigmoid(x)


def _split_bf16(x):
    hi = x.astype(BF16)
    return hi, (x - hi.astype(F32)).astype(BF16)


def _ada_kernel(c_ref, w_ref, b_ref, o_ref):
    o_ref[...] = _dot_hi(_silu(c_ref[...]), w_ref[...]) + b_ref[...]


def ada_vectors(c_rows, ada_w, ada_b):
    r, d = c_rows.shape
    n = ada_w.shape[1]
    bn = 1024
    return pl.pallas_call(
        _ada_kernel,
        out_shape=jax.ShapeDtypeStruct((r, n), F32),
        grid=(n // bn,),
        in_specs=[pl.BlockSpec((r, d), lambda j: (0, 0)),
                  pl.BlockSpec((d, bn), lambda j: (0, j)),
                  pl.BlockSpec((1, bn), lambda j: (0, j))],
        out_specs=pl.BlockSpec((r, bn), lambda j: (0, j)),
        compiler_params=_cparams("arbitrary"),
        name="ada_vectors",
    )(c_rows, ada_w, ada_b.reshape(1, n))


def _inproj_kernel(x_ref, er_ref, ec_ref, g_ref, sh_ref, sc_ref, w_ref, o_ref, *, col_chunk):
    x = x_ref[...]
    rows, gw, d = x.shape
    half = d // 2
    xp = jnp.concatenate([x[:, :, :half] + er_ref[...], x[:, :, half:] + ec_ref[...]], axis=-1)
    xp = xp.reshape(rows * gw, d)
    ms = jnp.mean(xp * xp, axis=-1, keepdims=True)
    y = xp * lax.rsqrt(ms + NORM_EPS) * g_ref[...]
    h = (y * (1.0 + sc_ref[...]) + sh_ref[...]).astype(BF16)
    n = o_ref.shape[1]
    for j in range(n // col_chunk):
        sl = slice(j * col_chunk, (j + 1) * col_chunk)
        o_ref[:, sl] = _dot(h, w_ref[:, sl]).astype(o_ref.dtype)


def in_projection(x, emb_r, emb_c, norm_g, shift, scale, w_bf16, tm):
    b, s, d = x.shape
    n = w_bf16.shape[1]
    rows_per_batch = s // GRID_W
    rt = tm // GRID_W
    tiles_per_batch = rows_per_batch // rt
    x3 = x.reshape(b * rows_per_batch, GRID_W, d)
    col_chunk = 512
    return pl.pallas_call(
        functools.partial(_inproj_kernel, col_chunk=col_chunk),
        out_shape=jax.ShapeDtypeStruct((b * s, n), BF16),
        grid=(b * tiles_per_batch,),
        in_specs=[pl.BlockSpec((rt, GRID_W, d), lambda i: (i, 0, 0)),
                  pl.BlockSpec((rt, 1, d // 2), lambda i: (i % tiles_per_batch, 0, 0)),
                  pl.BlockSpec((GRID_W, d // 2), lambda i: (0, 0)),
                  pl.BlockSpec((1, d), lambda i: (0, 0)),
                  pl.BlockSpec((None, 1, d), lambda i: (i // tiles_per_batch, 0, 0)),
                  pl.BlockSpec((None, 1, d), lambda i: (i // tiles_per_batch, 0, 0)),
                  pl.BlockSpec((d, n), lambda i: (0, 0))],
        out_specs=pl.BlockSpec((tm, n), lambda i: (i, 0)),
        compiler_params=_cparams("arbitrary"),
        name="in_projection",
    )(x3, emb_r, emb_c, norm_g.reshape(1, d), shift, scale, w_bf16)


def _hgrn_kernel(*refs, reverse, n_chunks, final):
    if final:
        (q_ref, i_ref, f_ref, lb_ref, s0_ref, of_ref, gate_ref, ng_ref, o_ref, sfin_ref, s_scr) = refs
    else:
        (q_ref, i_ref, f_ref, lb_ref, s0_ref, o_ref, sfin_ref, s_scr) = refs
    cs = HG_CHUNK
    bsz, n_heads, _, dk = s_scr.shape

    @pl.when(pl.program_id(0) == 0)
    def _():
        s_scr[...] = s0_ref[...]

    row = lax.broadcasted_iota(I32, (cs, cs), 0)
    col = lax.broadcasted_iota(I32, (cs, cs), 1)
    tri = (col >= row) if reverse else (col <= row)
    tri_b = tri.astype(BF16)
    end_row = 0 if reverse else cs - 1
    mid_row = cs // 2 if reverse else cs // 2 - 1

    def chunk_body(bi, ci):
        c = (n_chunks - 1 - ci) if reverse else ci
        rows = slice(c * cs, (c + 1) * cs)
        lb = lb_ref[...]
        f = lb + (1.0 - lb) * jax.nn.sigmoid(f_ref[bi, rows, :].astype(F32))
        lf_hi, lf_lo = _split_bf16(jnp.log(f))
        b_all = _dot(tri_b, lf_hi) + _dot(tri_b, lf_lo)
        k_all = 1.0 - f
        q_all = _silu(q_ref[bi, rows, :].astype(F32))
        for h in range(n_heads):
            sl = slice(h * dk, (h + 1) * dk)
            b = b_all[:, sl]
            q = q_all[:, sl]
            k = k_all[:, sl]
            v = i_ref[bi, rows, sl]
            b_end = b[end_row:end_row + 1]
            b_mid = b[mid_row:mid_row + 1]
            qd = (q * jnp.exp(b - b_mid)).astype(BF16)
            kd = (k * jnp.exp(b_mid - b)).astype(BF16)
            att = jnp.where(tri, _dot_nt(qd, kd), 0.0).astype(BF16)
            st = s_scr[bi, h]
            qe = (q * jnp.exp(b)).astype(BF16)
            o = _dot(jnp.concatenate([qe, att], axis=1), jnp.concatenate([st.astype(BF16), v], axis=0))
            ke = (k * jnp.exp(b_end - b)).astype(BF16)
            dec = jnp.transpose(jnp.broadcast_to(jnp.exp(b_end), (SUBLANES, dk)))[:, 0:1]
            s_scr[bi, h] = st * dec + _dot_tn(ke, v)
            if final:
                o = o + of_ref[bi, rows, sl].astype(F32)
                o = o * lax.rsqrt(jnp.mean(o * o, axis=-1, keepdims=True) + NORM_EPS) * ng_ref[...]
                o = o * _silu(gate_ref[bi, rows, sl].astype(F32))
            o_ref[bi, rows, sl] = o.astype(o_ref.dtype)

    for ci in range(n_chunks):
        for bi in range(bsz):
            chunk_body(bi, ci)
    sfin_ref[...] = s_scr[...]


def hgrn_scan(p, cols, lb, s0, seq, tb, *, reverse, o_fwd=None, gate_col=None, norm_g=None):
    bsz, n_heads, dv, dk = s0.shape
    width = n_heads * dk
    nt = seq // tb
    final = o_fwd is not None
    p3 = p.reshape(bsz, seq, p.shape[1])
    tmap = (lambda t: nt - 1 - t) if reverse else (lambda t: t)
    colspec = lambda cb: pl.BlockSpec((bsz, tb, width), lambda t: (0, tmap(t), cb))
    state = pl.BlockSpec((bsz, n_heads, dv, dk), lambda t: (0, 0, 0, 0))
    in_specs = [colspec(cols[0]), colspec(cols[1]), colspec(cols[2]),
                pl.BlockSpec((1, width), lambda t: (0, 0)), state]
    args = [p3, p3, p3, lb.reshape(1, width), s0]
    if final:
        in_specs += [colspec(0), colspec(gate_col), pl.BlockSpec((1, dk), lambda t: (0, 0))]
        args += [o_fwd.reshape(bsz, seq, width), p3, norm_g.reshape(1, dk)]
    o, s_fin = pl.pallas_call(
        functools.partial(_hgrn_kernel, reverse=reverse, n_chunks=tb // HG_CHUNK, final=final),
        out_shape=(jax.ShapeDtypeStruct((bsz, seq, width), BF16),
                   jax.ShapeDtypeStruct((bsz, n_heads, dv, dk), F32)),
        grid=(nt,),
        in_specs=in_specs,
        out_specs=(colspec(0), state),
        scratch_shapes=[pltpu.VMEM((bsz, n_heads, dv, dk), F32)],
        compiler_params=_cparams("arbitrary"),
        name="hgrn_bwd" if reverse else "hgrn_fwd",
    )(*args)
    return o.reshape(bsz * seq, width), s_fin


def _shortconv_kernel(p_ref, prev_ref, next_ref, w_ref, b_ref, o_ref, *, tiles_per_batch):
    i = pl.program_id(0)
    ti = i % tiles_per_batch
    p = p_ref[...].astype(F32)
    tm = p.shape[0]
    row = lax.broadcasted_iota(I32, (tm, 1), 0)
    prev_row = jnp.where(ti == 0, 0.0, prev_ref[HALO_ROWS - 1:HALO_ROWS, :].astype(F32))
    next_row = jnp.where(ti == tiles_per_batch - 1, 0.0, next_ref[0:1, :].astype(F32))
    p_prev = jnp.where(row == 0, prev_row, pltpu.roll(p, 1, axis=0))
    p_next = jnp.where(row == tm - 1, next_row, pltpu.roll(p, tm - 1, axis=0))
    u = w_ref[0:1, :] * p_prev + w_ref[1:2, :] * p + w_ref[2:3, :] * p_next + b_ref[...]
    o_ref[...] = jnp.swapaxes(u.reshape(tm // DFT_P, DFT_P, u.shape[1]), 0, 1).astype(o_ref.dtype)


def short_conv(p, width, conv_w, conv_b, seq, tm, cw):
    t = p.shape[0]
    nt = t // tm
    tiles_per_batch = seq // tm
    sub = tm // HALO_ROWS
    ta = tm // DFT_P
    return pl.pallas_call(
        functools.partial(_shortconv_kernel, tiles_per_batch=tiles_per_batch),
        out_shape=jax.ShapeDtypeStruct((DFT_P, t // DFT_P, width), BF16),
        grid=(nt, width // cw),
        in_specs=[pl.BlockSpec((tm, cw), lambda i, j: (i, j)),
                  pl.BlockSpec((HALO_ROWS, cw), lambda i, j: (jnp.maximum(i * sub - 1, 0), j)),
                  pl.BlockSpec((HALO_ROWS, cw), lambda i, j: (jnp.minimum((i + 1) * sub, t // HALO_ROWS - 1), j)),
                  pl.BlockSpec((HY_SHORT, cw), lambda i, j: (0, j)),
                  pl.BlockSpec((1, cw), lambda i, j: (0, j))],
        out_specs=pl.BlockSpec((DFT_P, ta, cw), lambda i, j: (0, i, j)),
        compiler_params=_cparams("arbitrary", "arbitrary"),
        name="short_conv",
    )(p, p, p, conv_w, conv_b.reshape(1, width))


def _filter_kernel(band_ref, w1t_ref, w1c_ref, w1s_ref, b1_ref, w2_ref, b2_ref, w3_ref, b3_ref,
                   w4f_ref, w4b_ref, fr_ref, delta_ref, k_ref, s_ref, *, seq):
    step = pl.program_id(0)
    gb, q, ncol = k_ref.shape
    half = q // 2
    width = delta_ref.shape[1]
    nrow = gb * q
    nf = gb * half

    def positions(shape, axis):
        r = lax.broadcasted_iota(I32, shape, axis)
        is_bwd = r >= nf
        rr = jnp.where(is_bwd, r - nf, r)
        j = lax.shift_right_logical(rr, int(math.log2(half)))
        a = (rr & (half - 1)) + jnp.where(is_bwd, half, 0)
        n = (a * DFT_P + step * gb + j).astype(F32)
        t = jnp.where(is_bwd, 2.0 * seq - n, n)
        return n, t, t / float(max(seq - 1, 1))

    _, t_l, tn_l = positions((1, nrow), 1)
    ang = (2.0 * math.pi / seq) * t_l * band_ref[...]
    fr = fr_ref[...]
    pre = (w1t_ref[...] * tn_l + _dot_hi(w1c_ref[...], jnp.cos(ang)) - _dot_hi(w1s_ref[...], jnp.sin(ang))
           + b1_ref[...])
    act = jnp.sin(fr * pre)
    act = jnp.sin(fr * (_dot_hi(w2_ref[...], act) + b2_ref[...]))
    act = jnp.sin(fr * (_dot_hi(w3_ref[...], act) + b3_ref[...])).astype(BF16)
    n_s, _, tn_s = positions((nrow, 1), 0)
    delta = jnp.concatenate([delta_ref[...]] * (ncol // width), axis=1)
    hf = _dot_tn(act[:, :nf], w4f_ref[...]) * jnp.exp(-tn_s[:nf] * delta)
    hb = _dot_tn(act[:, nf:], w4b_ref[...]) * jnp.exp(-tn_s[nf:] * delta)
    hb = jnp.where(n_s[nf:] == float(seq), 0.0, hb)
    k_ref[:, :half, :] = hf.reshape(gb, half, ncol).astype(k_ref.dtype)
    k_ref[:, half:, :] = hb.reshape(gb, half, ncol).astype(k_ref.dtype)
    tot = jnp.sum(jnp.abs(hf), axis=0, keepdims=True) + jnp.sum(jnp.abs(hb), axis=0, keepdims=True)

    @pl.when(step == 0)
    def _():
        s_ref[...] = jnp.zeros_like(s_ref)

    s_ref[...] += tot


def hyena_filter_taps(seq, w1, b1, w2, b2, w3, b3, w4, freq, width):
    emb = w1.shape[0]
    hid = w1.shape[1]
    bands = (emb - 1) // 2
    q = 2 * seq // DFT_P
    ncol = HY_ORDER * width
    band = np.linspace(1e-4, bands - 1, bands, dtype=np.float32).reshape(bands, 1)
    min_decay = math.log(HY_DECAY_TARGET) / HY_SLOW_PCT
    max_decay = math.log(HY_DECAY_TARGET) / HY_FAST_PCT
    delta = np.abs(np.linspace(min_decay, max_decay, width, dtype=np.float32)).reshape(1, width)
    w1t = w1.astype(F32).T
    col = lambda v: v.reshape(hid, 1).astype(F32)
    w4r = w4.astype(BF16).reshape(hid, HY_ORDER, 2, width)
    w4f = w4r[:, :, 0, :].reshape(hid, ncol)
    w4b = w4r[:, :, 1, :].reshape(hid, ncol)
    gb = FILTER_GROUP
    const = lambda shape: pl.BlockSpec(shape, lambda i: tuple(0 for _ in shape))
    return pl.pallas_call(
        functools.partial(_filter_kernel, seq=seq),
        out_shape=(jax.ShapeDtypeStruct((DFT_P, q, ncol), BF16),
                   jax.ShapeDtypeStruct((1, ncol), F32)),
        grid=(DFT_P // gb,),
        in_specs=[const((bands, 1)), const((hid, 1)), const((hid, bands)), const((hid, bands)), const((hid, 1)),
                  const((hid, hid)), const((hid, 1)), const((hid, hid)), const((hid, 1)),
                  const((hid, ncol)), const((hid, ncol)), const((hid, 1)), const((1, width))],
        out_specs=(pl.BlockSpec((gb, q, ncol), lambda i: (i, 0, 0)),
                   pl.BlockSpec((1, ncol), lambda i: (0, 0))),
        compiler_params=_cparams("arbitrary"),
        name="hyena_filter",
    )(jnp.asarray(band), w1t[:, 0:1], w1t[:, 1:1 + bands], w1t[:, 1 + bands:1 + 2 * bands], col(b1),
      w2.astype(F32).T, col(b2), w3.astype(F32).T, col(b3), w4f, w4b, col(freq), jnp.asarray(delta))


def _dft_tables(seq):
    p = DFT_P
    n_fft = 2 * seq
    q = n_fft // p
    qh = q // 2
    ka = np.arange(q)
    nn = np.arange(q)[None, :] * p + np.arange(p)[:, None]
    ang = ((ka[None, :, None] * nn[:, None, :]) % n_fft) * (2.0 * np.pi / n_fft)
    mr, mi = np.cos(ang), -np.sin(ang)
    m1c = np.concatenate([np.concatenate([mr[:, :, :qh], -mi[:, :, :qh]], axis=2),
                          np.concatenate([mi[:, :, :qh], mr[:, :, :qh]], axis=2)], axis=1)
    m1r = np.concatenate([mr, mi], axis=1)
    gr = np.swapaxes(mr[:, :, :qh], 1, 2) / n_fft
    gi = -np.swapaxes(mi[:, :, :qh], 1, 2) / n_fft
    m4 = np.concatenate([np.concatenate([gr, -gi], axis=2), np.concatenate([gi, gr], axis=2)], axis=1)
    kb = np.arange(p)
    ang2 = 2.0 * np.pi * ((kb[:, None] * kb[None, :]) % p) / p
    fr, fi = np.cos(ang2), -np.sin(ang2)
    m2 = np.block([[fr, -fi], [fi, fr]])
    m3 = np.block([[fr, fi], [-fi, fr]])
    return tuple(jnp.asarray(m.astype(np.float32).astype(BF16)) for m in (m1c, m1r, m2, m3, m4))


def _store_swapped(o_ref, res):
    g2 = o_ref.shape[0]
    stacked = jnp.stack(res, axis=0)
    for ri in range(2):
        o_ref[:, ri, :, :] = jnp.swapaxes(stacked[:, ri * g2:(ri + 1) * g2, :], 0, 1).astype(o_ref.dtype)


def _swapped_out(g, m, ncols, gb, n_col_blocks=1):
    shape = (m // 2, 2, g, ncols * n_col_blocks)
    if n_col_blocks == 1:
        return shape, pl.BlockSpec((m // 2, 2, gb, ncols), lambda i: (0, 0, i, 0))
    return shape, pl.BlockSpec((m // 2, 2, gb, ncols), lambda i, j: (0, 0, i, j))


def _bmm_kernel(w_ref, x_ref, o_ref, *, shared_w):
    res = []
    for j in range(x_ref.shape[0]):
        w = w_ref[...] if shared_w else w_ref[j]
        res.append(_dot(w, x_ref[j]))
    _store_swapped(o_ref, res)


def batched_left_matmul(w, x, col_block, ncols, name, gb, n_col_blocks=1):
    g, k = x.shape[0], x.shape[1]
    shared = w.ndim == 2
    m = w.shape[-2]
    out_shape, out_spec = _swapped_out(g, m, ncols, gb, n_col_blocks)
    if n_col_blocks == 1:
        grid = (g // gb,)
        wspec = (pl.BlockSpec((m, k), lambda i: (0, 0)) if shared else pl.BlockSpec((gb, m, k), lambda i: (i, 0, 0)))
        xspec = pl.BlockSpec((gb, k, ncols), lambda i: (i, 0, col_block))
        sem = ("arbitrary",)
    else:
        grid = (g // gb, n_col_blocks)
        wspec = (pl.BlockSpec((m, k), lambda i, j: (0, 0)) if shared
                 else pl.BlockSpec((gb, m, k), lambda i, j: (i, 0, 0)))
        xspec = pl.BlockSpec((gb, k, ncols), lambda i, j: (i, 0, j))
        sem = ("arbitrary", "arbitrary")
    out = pl.pallas_call(
        functools.partial(_bmm_kernel, shared_w=shared),
        out_shape=jax.ShapeDtypeStruct(out_shape, BF16),
        grid=grid,
        in_specs=[wspec, xspec],
        out_specs=out_spec,
        compiler_params=_cparams(*sem),
        name=name,
    )(w, x)
    return out.reshape(m // 2, 2 * g, out_shape[3])


def _dft_mid_kernel(m2_ref, m3_ref, x_ref, k_ref, o_ref):
    half = x_ref.shape[1] // 2
    res = []
    for j in range(x_ref.shape[0]):
        xf = _dot(m2_ref[...], x_ref[j])
        kf = _dot(m2_ref[...], k_ref[j])
        xr, xi = xf[:half], xf[half:]
        kr, ki = kf[:half], kf[half:]
        z = jnp.concatenate([xr * kr - xi * ki, xr * ki + xi * kr], axis=0).astype(BF16)
        res.append(_dot(m3_ref[...], z))
    _store_swapped(o_ref, res)


def dft_mid(m2, m3, x, kspec, kcol, ncols):
    g, r = x.shape[0], x.shape[1]
    gb = DFT_GROUP
    out_shape, out_spec = _swapped_out(g, r, ncols, gb)
    out = pl.pallas_call(
        _dft_mid_kernel,
        out_shape=jax.ShapeDtypeStruct(out_shape, BF16),
        grid=(g // gb,),
        in_specs=[pl.BlockSpec((r, r), lambda i: (0, 0)),
                  pl.BlockSpec((r, r), lambda i: (0, 0)),
                  pl.BlockSpec((gb, r, ncols), lambda i: (i, 0, 0)),
                  pl.BlockSpec((gb, r, ncols), lambda i: (i, 0, kcol))],
        out_specs=out_spec,
        compiler_params=_cparams("arbitrary"),
        name="dft_mid",
    )(m2, m3, x, kspec)
    return out.reshape(r // 2, 2 * g, ncols)


def _dft_out_kernel(m4_ref, y_ref, inv_ref, skip_ref, v_ref, mul_ref, o_ref, *, token_order):
    res = []
    for j in range(y_ref.shape[0]):
        conv = _dot(m4_ref[j], y_ref[j]) * inv_ref[...] + v_ref[j].astype(F32) * skip_ref[...]
        res.append(mul_ref[j].astype(F32) * conv)
    if token_order:
        o_ref[...] = jnp.swapaxes(jnp.stack(res, axis=0), 0, 1).astype(o_ref.dtype)
    else:
        for j, r in enumerate(res):
            o_ref[j] = r.astype(o_ref.dtype)


def dft_out(m4, y, inv_l1, skip, u, v_col, mul, mul_col, ncols, token_order):
    g, r = y.shape[0], y.shape[1]
    rows = m4.shape[1]
    gb = DFT_GROUP
    out_shape, out_spec = (((rows, g, ncols), pl.BlockSpec((rows, gb, ncols), lambda i: (0, i, 0))) if token_order
                           else ((g, rows, ncols), pl.BlockSpec((gb, rows, ncols), lambda i: (i, 0, 0))))
    return pl.pallas_call(
        functools.partial(_dft_out_kernel, token_order=token_order),
        out_shape=jax.ShapeDtypeStruct(out_shape, BF16),
        grid=(g // gb,),
        in_specs=[pl.BlockSpec((gb, rows, r), lambda i: (i, 0, 0)),
                  pl.BlockSpec((gb, r, ncols), lambda i: (i, 0, 0)),
                  pl.BlockSpec((1, ncols), lambda i: (0, 0)),
                  pl.BlockSpec((1, ncols), lambda i: (0, 0)),
                  pl.BlockSpec((gb, rows, ncols), lambda i: (i, 0, v_col)),
                  pl.BlockSpec((gb, rows, ncols), lambda i: (i, 0, mul_col))],
        out_specs=out_spec,
        compiler_params=_cparams("arbitrary"),
        name="dft_out",
    )(m4, y, inv_l1, skip, u, mul)


def hyena_branch(u, bsz, seq, width, taps, l1, skip):
    m1c, m1r, m2, m3, m4 = _dft_tables(seq)
    ks1 = batched_left_matmul(m1r, taps, 0, width, "dft_k1", DFT_GROUP, n_col_blocks=HY_ORDER)
    inv_l1 = 1.0 / l1
    z = None
    for order in range(HY_ORDER):
        src, src_col = (u, 0) if order == 0 else (z, 0)
        s1 = batched_left_matmul(m1c, src, src_col, width, "dft_s1", DFT_GROUP)
        mid = dft_mid(m2, m3, s1, ks1, order, width)
        z = dft_out(m4, mid, inv_l1[:, order * width:(order + 1) * width],
                    skip[order].reshape(1, width).astype(F32), src, src_col, u, order + 1, width,
                    token_order=order == HY_ORDER - 1)
    return z.reshape(bsz * seq, width)


def _pack_pairs(x):
    w = x.shape[1] // 2
    u = lax.bitcast_convert_type(x, U32)
    r = (u + U32(0x7FFF) + ((u >> 16) & U32(1))) >> 16
    return r[:, :w] | (r[:, w:] << 16)


def _unpack_pairs(p):
    lo = lax.bitcast_convert_type(p << 16, F32)
    hi = lax.bitcast_convert_type(p & U32(0xFFFF0000), F32)
    return jnp.concatenate([lo, hi], axis=1)


def _pack_pairs_native(x):
    w = x.shape[1] // 2
    return lax.bitcast_convert_type(pltpu.pack_elementwise([x[:, :w], x[:, w:]], packed_dtype=BF16), U32)


def _unpack_pairs_native(p):
    pi = lax.bitcast_convert_type(p, I32)
    halves = [pltpu.unpack_elementwise(pi, index=i, packed_dtype=BF16, unpacked_dtype=F32) for i in range(2)]
    return jnp.concatenate(halves, axis=1)


def _merge_kernel(x_ref, er_ref, ec_ref, yhy_ref, yhg_ref, ghy_ref, ghg_ref, why_ref, whg_ref, wo_ref,
                  g1_ref, n2_ref, sh2_ref, sc2_ref, g2_ref, rwh_ref, rwl_ref, sgu_ref, sd_ref,
                  xres_ref, h2p_ref, lg_ref):
    x = x_ref[...]
    rows, gw, d = x.shape
    half = d // 2
    xp = jnp.concatenate([x[:, :, :half] + er_ref[...], x[:, :, half:] + ec_ref[...]], axis=-1)
    xp = xp.reshape(rows * gw, d)
    m = (jax.nn.sigmoid(ghy_ref[...].astype(F32)) * _dot(yhy_ref[...], why_ref[...])
         + jax.nn.sigmoid(ghg_ref[...].astype(F32)) * _dot(yhg_ref[...], whg_ref[...]))
    x1 = xp + g1_ref[...] * _dot(m.astype(BF16), wo_ref[...])
    ms = jnp.mean(x1 * x1, axis=-1, keepdims=True)
    h2 = x1 * lax.rsqrt(ms + NORM_EPS) * n2_ref[...] * (1.0 + sc2_ref[...]) + sh2_ref[...]
    h_hi, h_lo = _split_bf16(h2)
    lg_ref[...] = _dot_nt(rwh_ref[...], h_hi) + (_dot_nt(rwl_ref[...], h_hi) + _dot_nt(rwh_ref[...], h_lo))
    gu = _dot(h_hi, sgu_ref[...])
    fs = gu.shape[1] // 2
    shared = _dot((_silu(gu[:, :fs]) * gu[:, fs:]).astype(BF16), sd_ref[...])
    xres_ref[...] = x1 + g2_ref[...] * shared
    h2p_ref[...] = _pack_pairs(h2)


def merge_stage(x, emb_r, emb_c, y_hy, y_hg, p, gate_cols, w_hy_out, w_hg_out, w_out, g1, norm2_g,
                sh2, sc2, g2, router_wt, sh_gate_up, sh_down, tm):
    b, s, d = x.shape
    rows_per_batch = s // GRID_W
    rt = tm // GRID_W
    tiles_per_batch = rows_per_batch // rt
    x3 = x.reshape(b * rows_per_batch, GRID_W, d)
    wb = y_hy.shape[1]
    ne = router_wt.shape[0]
    fs2 = sh_gate_up.shape[1]
    rw_hi, rw_lo = _split_bf16(router_wt)
    tok = lambda cb, w: pl.BlockSpec((tm, w), lambda i: (i, cb))
    const = lambda shape: pl.BlockSpec(shape, lambda i: tuple(0 for _ in shape))
    per_b = pl.BlockSpec((None, 1, d), lambda i: (i // tiles_per_batch, 0, 0))
    return pl.pallas_call(
        _merge_kernel,
        out_shape=(jax.ShapeDtypeStruct((b * s, d), F32),
                   jax.ShapeDtypeStruct((b * s, d // 2), U32),
                   jax.ShapeDtypeStruct((ne, b * s), F32)),
        grid=(b * tiles_per_batch,),
        in_specs=[pl.BlockSpec((rt, GRID_W, d), lambda i: (i, 0, 0)),
                  pl.BlockSpec((rt, 1, d // 2), lambda i: (i % tiles_per_batch, 0, 0)),
                  const((GRID_W, d // 2)),
                  tok(0, wb), tok(0, wb), tok(gate_cols[0], d), tok(gate_cols[1], d),
                  const((wb, d)), const((wb, d)), const((d, d)),
                  per_b, const((1, d)), per_b, per_b, per_b,
                  const((ne, d)), const((ne, d)), const((d, fs2)), const((fs2 // 2, d))],
        out_specs=(pl.BlockSpec((tm, d), lambda i: (i, 0)),
                   pl.BlockSpec((tm, d // 2), lambda i: (i, 0)),
                   pl.BlockSpec((ne, tm), lambda i: (0, i))),
        compiler_params=_cparams("arbitrary"),
        name="merge",
    )(x3, emb_r, emb_c, y_hy, y_hg, p, p, w_hy_out, w_hg_out, w_out, g1, norm2_g.reshape(1, d),
      sh2, sc2, g2, rw_hi, rw_lo, sh_gate_up, sh_down)


def _route_kernel(lg_ref, bias_ref, eidx_ref, wsel_ref, rank_ref, cnt_ref, carry):
    ne, tr = lg_ref.shape
    gsz = ne // N_GROUPS
    neg = -jnp.inf

    @pl.when(pl.program_id(0) == 0)
    def _():
        carry[...] = jnp.zeros_like(carry)

    scores = jax.nn.sigmoid(lg_ref[...])
    biased = scores + bias_ref[...]
    riota = lax.broadcasted_iota(I32, (gsz, tr), 0).astype(F32)
    gs = []
    for g in range(N_GROUPS):
        vg = biased[g * gsz:(g + 1) * gsz]
        m1 = jnp.max(vg, axis=0, keepdims=True)
        i1 = jnp.min(jnp.where(vg == m1, riota, float(gsz)), axis=0, keepdims=True)
        m2 = jnp.max(jnp.where(riota == i1, neg, vg), axis=0, keepdims=True)
        gs.append(m1 + m2)
    cur = jnp.concatenate(gs, axis=0)
    giota = lax.broadcasted_iota(I32, (N_GROUPS, tr), 0).astype(F32)
    gsel = jnp.zeros((N_GROUPS, tr), F32)
    for _ in range(TOPK_GROUPS):
        m = jnp.max(cur, axis=0, keepdims=True)
        idx = jnp.min(jnp.where(cur == m, giota, float(N_GROUPS)), axis=0, keepdims=True)
        hit = giota == idx
        gsel = jnp.where(hit, 1.0, gsel)
        cur = jnp.where(hit, neg, cur)
    cur = jnp.concatenate([jnp.where(gsel[g:g + 1] > 0.0, biased[g * gsz:(g + 1) * gsz], neg)
                           for g in range(N_GROUPS)], axis=0)
    eiota = lax.broadcasted_iota(I32, (ne, tr), 0).astype(F32)
    chosen = jnp.zeros((ne, tr), F32)
    idxs, ws = [], []
    for _ in range(TOP_K):
        m = jnp.max(cur, axis=0, keepdims=True)
        idx = jnp.min(jnp.where(cur == m, eiota, float(ne)), axis=0, keepdims=True)
        hit = eiota == idx
        idxs.append(idx)
        ws.append(jnp.sum(jnp.where(hit, scores, 0.0), axis=0, keepdims=True))
        chosen = jnp.where(hit, 1.0, chosen)
        cur = jnp.where(hit, neg, cur)
    w = jnp.concatenate(ws, axis=0)
    wsel_ref[...] = w / jnp.sum(w, axis=0, keepdims=True) * ROUTED_SCALE
    eidx_ref[...] = jnp.concatenate(idxs, axis=0).astype(I32)
    srow = lax.broadcasted_iota(I32, (tr, tr), 0)
    scol = lax.broadcasted_iota(I32, (tr, tr), 1)
    before = (srow < scol).astype(BF16)
    base = carry[...] + _dot(chosen.astype(BF16), before)
    ranks = [jnp.sum(jnp.where(eiota == idx, base, 0.0), axis=0, keepdims=True) for idx in idxs]
    rank_ref[...] = jnp.concatenate(ranks, axis=0).astype(I32)
    carry[...] += jnp.sum(chosen, axis=1, keepdims=True)
    cnt_ref[...] = carry[...]


def route(logits_t, router_bias, tr):
    ne, t = logits_t.shape
    return pl.pallas_call(
        _route_kernel,
        out_shape=(jax.ShapeDtypeStruct((TOP_K, t), I32),
                   jax.ShapeDtypeStruct((TOP_K, t), F32),
                   jax.ShapeDtypeStruct((TOP_K, t), I32),
                   jax.ShapeDtypeStruct((ne, 1), F32)),
        grid=(t // tr,),
        in_specs=[pl.BlockSpec((ne, tr), lambda i: (0, i)),
                  pl.BlockSpec((ne, 1), lambda i: (0, 0))],
        out_specs=(pl.BlockSpec((TOP_K, tr), lambda i: (0, i)),
                   pl.BlockSpec((TOP_K, tr), lambda i: (0, i)),
                   pl.BlockSpec((TOP_K, tr), lambda i: (0, i)),
                   pl.BlockSpec((ne, 1), lambda i: (0, 0))),
        scratch_shapes=[pltpu.VMEM((ne, 1), F32)],
        compiler_params=_cparams("arbitrary"),
        name="route",
    )(logits_t, router_bias.reshape(ne, 1).astype(F32))


def _dest_kernel(cnt_ref, eidx_ref, rank_ref, dest_ref, be_ref, nv_ref, nb_ref, start_scr):
    ne = cnt_ref.shape[0]
    tr = eidx_ref.shape[1]

    @pl.when(pl.program_id(0) == 0)
    def _():
        cnt = jnp.broadcast_to(cnt_ref[...], (ne, LANES))
        padded = jnp.floor((cnt + float(MOE_ROWS - 1)) / float(MOE_ROWS)) * float(MOE_ROWS)
        r = lax.broadcasted_iota(I32, (ne, ne), 0)
        c = lax.broadcasted_iota(I32, (ne, ne), 1)
        start = _dot_hi((c < r).astype(F32), padded)
        start_scr[...] = start
        end = start[:, 0:1] + padded[:, 0:1]
        used = start[:, 0:1] + cnt[:, 0:1]
        nbl = be_ref.shape[1]
        blk_row = (lax.broadcasted_iota(I32, (1, nbl), 1) * MOE_ROWS).astype(F32)
        total = jnp.max(end, axis=0, keepdims=True)
        last_row = total - float(MOE_ROWS)
        blk_row_c = jnp.minimum(blk_row, last_row)
        e_of = jnp.sum((end <= blk_row_c).astype(F32), axis=0, keepdims=True)
        e_of = jnp.minimum(e_of, float(ne - 1))
        eio = lax.broadcasted_iota(I32, (ne, nbl), 0).astype(F32)
        used_e = jnp.sum(jnp.where(eio == e_of, used, 0.0), axis=0, keepdims=True)
        valid = jnp.clip(used_e - blk_row_c, 0.0, float(MOE_ROWS))
        be_ref[...] = e_of.astype(I32)
        nv_ref[...] = jnp.where(blk_row <= last_row, valid, 0.0).astype(I32)
        nb_ref[...] = jnp.broadcast_to(total / float(MOE_ROWS), nb_ref.shape).astype(I32)

    eiota = lax.broadcasted_iota(I32, (ne, tr), 0)
    start_col = start_scr[:, 0:1]
    rows = []
    for k in range(TOP_K):
        hit = eiota == eidx_ref[k:k + 1, :]
        rows.append(jnp.sum(jnp.where(hit, start_col, 0.0), axis=0, keepdims=True))
    dest_ref[...] = jnp.concatenate(rows, axis=0).astype(I32) + rank_ref[...]


def dispatch_plan(counts, eidx, rank, tr, n_blocks):
    ne = counts.shape[0]
    t = eidx.shape[1]
    nbl = pl.cdiv(n_blocks, LANES) * LANES
    return pl.pallas_call(
        _dest_kernel,
        out_shape=(jax.ShapeDtypeStruct((TOP_K, t), I32),
                   jax.ShapeDtypeStruct((1, nbl), I32),
                   jax.ShapeDtypeStruct((1, nbl), I32),
                   jax.ShapeDtypeStruct((1, LANES), I32)),
        grid=(t // tr,),
        in_specs=[pl.BlockSpec((ne, 1), lambda i: (0, 0)),
                  pl.BlockSpec((TOP_K, tr), lambda i: (0, i)),
                  pl.BlockSpec((TOP_K, tr), lambda i: (0, i))],
        out_specs=(pl.BlockSpec((TOP_K, tr), lambda i: (0, i)),
                   pl.BlockSpec((1, nbl), lambda i: (0, 0)),
                   pl.BlockSpec((1, nbl), lambda i: (0, 0)),
                   pl.BlockSpec((1, LANES), lambda i: (0, 0))),
        scratch_shapes=[pltpu.VMEM((ne, LANES), F32)],
        compiler_params=_cparams("arbitrary"),
        name="dispatch_plan",
    )(counts, eidx, rank)


def _sc_workers():
    info = plsc.get_sparse_core_info()
    return info.num_cores, info.num_cores * info.num_subcores


def scatter_rows(dest_flat, h2p, n_rows):
    t, w = h2p.shape
    n_cores, n_workers = _sc_workers()
    per_worker = t // n_workers
    mesh = plsc.VectorSubcoreMesh(core_axis_name="c", subcore_axis_name="s")

    @functools.partial(
        pl.kernel, mesh=mesh, out_type=jax.ShapeDtypeStruct((n_rows, w), U32),
        scratch_types=[pltpu.VMEM((TOP_K, SC_ROWS), I32), pltpu.VMEM((SC_ROWS, w), U32), pltpu.SemaphoreType.DMA])
    def body(h_hbm, dest_hbm, xs_hbm, idx_v, rows_v, sem):
        base = (lax.axis_index("s") * n_cores + lax.axis_index("c")) * per_worker

        @pl.loop(0, per_worker // SC_ROWS)
        def _(ci):
            off = pl.multiple_of(base + ci * SC_ROWS, SC_ROWS)
            pltpu.sync_copy(h_hbm.at[pl.ds(off, SC_ROWS)], rows_v)
            for k in range(TOP_K):
                pltpu.sync_copy(dest_hbm.at[pl.ds(k * t + off, SC_ROWS)], idx_v.at[k])
            copies = [pltpu.async_copy(rows_v, xs_hbm.at[idx_v.at[k]], sem) for k in range(TOP_K)]
            for c in copies:
                c.wait()

    return body(h2p, dest_flat)


def gather_rows(idx_flat, table):
    n = idx_flat.shape[0]
    w = table.shape[1]
    n_cores, n_workers = _sc_workers()
    per_worker = n // n_workers
    mesh = plsc.VectorSubcoreMesh(core_axis_name="c", subcore_axis_name="s")

    ch = SC_ROWS // 2
    n_chunks = per_worker // ch

    @functools.partial(
        pl.kernel, mesh=mesh, out_type=jax.ShapeDtypeStruct((n, w), table.dtype),
        scratch_types=[pltpu.VMEM((2, ch), I32), pltpu.VMEM((2, ch, w), table.dtype), pltpu.SemaphoreType.DMA((2,))])
    def body(table_hbm, idx_hbm, out_hbm, idx_v, rows_v, sem):
        base = (lax.axis_index("s") * n_cores + lax.axis_index("c")) * per_worker

        def read(b):
            return pltpu.make_async_copy(table_hbm.at[idx_v.at[b]], rows_v.at[b], sem.at[b])

        def start(c, b):
            off = pl.multiple_of(base + c * ch, ch)
            pltpu.sync_copy(idx_hbm.at[pl.ds(off, ch)], idx_v.at[b])
            read(b).start()

        def finish(c, b):
            read(b).wait()
            pltpu.sync_copy(rows_v.at[b], out_hbm.at[pl.ds(pl.multiple_of(base + c * ch, ch), ch)])

        start(0, 0)

        @pl.loop(0, n_chunks, step=2)
        def _(c):
            start(c + 1, 1)
            finish(c, 0)

            @pl.when(c + 2 < n_chunks)
            def _():
                start(c + 2, 0)

            finish(c + 1, 1)

    return body(table, idx_flat)


def _gmm_kernel(be_ref, nv_ref, nb_ref, first_ref, run_ref, slot_ref, rexp_ref, xs_ref, wg_hbm, wu_hbm, wd_hbm,
                y_ref, wg_buf, wu_buf, wd_buf, sem):
    def weight_copies(e, s):
        return (pltpu.make_async_copy(wg_hbm.at[e], wg_buf.at[s], sem.at[s]),
                pltpu.make_async_copy(wu_hbm.at[e], wu_buf.at[s], sem.at[s]),
                pltpu.make_async_copy(wd_hbm.at[e], wd_buf.at[s], sem.at[s]))

    def fetch(run, slot):
        e = rexp_ref[run]

        @pl.when(e >= 0)
        def _():
            for c in weight_copies(e, slot):
                c.start()

    j0 = pl.program_id(0) * MOE_STEP_BLOCKS

    @pl.when(j0 < nb_ref[0])
    def _():
        for sb in range(MOE_STEP_BLOCKS):
            j = j0 + sb

            @pl.when(j == 0)
            def _():
                for a in range(WEIGHT_AHEAD):
                    fetch(a, a)

            @pl.when((j < nb_ref[0]) & (first_ref[j] == 1))
            def _():
                s = slot_ref[j]
                for c in weight_copies(be_ref[j], s):
                    c.wait()
                fetch(run_ref[j] + WEIGHT_AHEAD, (s + WEIGHT_AHEAD) % WEIGHT_SLOTS)

        for sb in range(MOE_STEP_BLOCKS):
            j = j0 + sb
            s = slot_ref[j]
            rows = slice(sb * MOE_ROWS, (sb + 1) * MOE_ROWS)
            x = _unpack_pairs_native(xs_ref[rows, :])
            row = lax.broadcasted_iota(I32, (x.shape[0], 1), 0)
            x = jnp.where(row < nv_ref[j], x, 0.0)
            hmid = _silu(_dot(x, wg_buf[s])) * _dot(x, wu_buf[s])
            y_ref[rows, :] = _pack_pairs_native(_dot(hmid, wd_buf[s]))


def grouped_mlp(block_e, block_valid, n_used, xs, w_gate, w_up, w_down, n_blocks):
    ne, d, f = w_gate.shape
    w = xs.shape[1]
    jj = jnp.arange(block_e.shape[0], dtype=I32)
    active = jj < n_used[0]
    first = (active & ((jj == 0) | (block_e != jnp.roll(block_e, 1)))).astype(I32)
    run = jnp.cumsum(first) - 1
    slot = (run % WEIGHT_SLOTS).astype(I32)
    nbl = block_e.shape[0]
    run_expert = jnp.full((nbl + WEIGHT_SLOTS,), -1, I32).at[jnp.where(first == 1, run, nbl)].set(
        jnp.where(first == 1, block_e, -1))
    run = run.astype(I32)
    step_rows = MOE_STEP_BLOCKS * MOE_ROWS
    last = lambda g, nb: jnp.minimum(g, (nb[0] - 1) // MOE_STEP_BLOCKS)
    row_block = pl.BlockSpec((step_rows, w), lambda g, be, nv, nb, fi, rn, sl, rx: (last(g, nb), 0))
    grid_spec = pltpu.PrefetchScalarGridSpec(
        num_scalar_prefetch=7,
        grid=(n_blocks // MOE_STEP_BLOCKS,),
        in_specs=[row_block,
                  pl.BlockSpec(memory_space=pl.ANY),
                  pl.BlockSpec(memory_space=pl.ANY),
                  pl.BlockSpec(memory_space=pl.ANY)],
        out_specs=row_block,
        scratch_shapes=[pltpu.VMEM((WEIGHT_SLOTS, d, f), F32), pltpu.VMEM((WEIGHT_SLOTS, d, f), F32),
                        pltpu.VMEM((WEIGHT_SLOTS, f, d), F32), pltpu.SemaphoreType.DMA((WEIGHT_SLOTS,))],
    )
    return pl.pallas_call(
        _gmm_kernel,
        out_shape=jax.ShapeDtypeStruct(xs.shape, U32),
        grid_spec=grid_spec,
        compiler_params=_cparams("arbitrary"),
        name="grouped_mlp",
    )(block_e, block_valid, n_used, first, run, slot, run_expert, xs, w_gate, w_up, w_down)


def _combine_kernel(y_ref, xres_ref, wt_ref, g2_ref, fg_ref, o_ref):
    wt = wt_ref[...]
    routed = jnp.zeros(xres_ref.shape, F32)
    for k in range(TOP_K):
        routed = routed + wt[:, k:k + 1] * _unpack_pairs_native(y_ref[k])
    x2 = xres_ref[...] + g2_ref[...] * routed
    ms = jnp.mean(x2 * x2, axis=-1, keepdims=True)
    o_ref[...] = x2 * lax.rsqrt(ms + NORM_EPS) * fg_ref[...]


def _combine_into_kernel(prev_ref, *refs):
    del prev_ref
    _combine_kernel(*refs)


def combine(y_tok, xres, wsel_t, g2, final_g, seq, tm, first_tile, prev_out):
    t, d = xres.shape
    tiles_per_batch = seq // tm
    tile = lambda i: i + first_tile
    in_specs = [pl.BlockSpec((TOP_K, tm, d // 2), lambda i: (0, i, 0)),
                pl.BlockSpec((tm, d), lambda i: (tile(i), 0)),
                pl.BlockSpec((tm, TOP_K), lambda i: (tile(i), 0)),
                pl.BlockSpec((None, 1, d), lambda i: (tile(i) // tiles_per_batch, 0, 0)),
                pl.BlockSpec((1, d), lambda i: (0, 0))]
    args = [y_tok, xres, wsel_t, g2, final_g.reshape(1, d)]
    body, aliases = _combine_kernel, {}
    if prev_out is not None:
        body, aliases = _combine_into_kernel, {0: 0}
        in_specs = [pl.BlockSpec(memory_space=pl.ANY)] + in_specs
        args = [prev_out] + args
    return pl.pallas_call(
        body,
        out_shape=jax.ShapeDtypeStruct((t, d), F32),
        grid=(y_tok.shape[1] // tm,),
        in_specs=in_specs,
        out_specs=pl.BlockSpec((tm, d), lambda i: (tile(i), 0)),
        input_output_aliases=aliases,
        compiler_params=_cparams("arbitrary"),
        name="combine",
    )(*args)


def _pos_tables(rows, cols, dim):
    quarter = dim // 4
    omega = 1.0 / (POS_BASE ** (np.arange(quarter, dtype=np.float32) / quarter))
    ang_r = np.arange(rows, dtype=np.float32)[:, None] * omega
    ang_c = np.arange(cols, dtype=np.float32)[:, None] * omega
    emb_r = np.concatenate([np.sin(ang_r), np.cos(ang_r)], axis=-1).astype(np.float32)
    emb_c = np.concatenate([np.sin(ang_c), np.cos(ang_c)], axis=-1).astype(np.float32)
    return jnp.asarray(emb_r.reshape(rows, 1, dim // 2)), jnp.asarray(emb_c)


def kernel(x, c, ctx, c_ctx, norm1_g, norm2_g, ada_w, ada_b, w_in, hy_conv_w, hy_conv_b, hy_f_w1, hy_f_b1, hy_f_w2, hy_f_b2, hy_f_w3, hy_f_b3, hy_f_w4, hy_f_freq, hy_skip, hg_lb_logits, hg_norm_g, w_hy_out, w_hg_out, w_out, router_w, router_bias, exp_w_gate, exp_w_up, exp_w_down, sh_w_gate, sh_w_up, sh_w_down, final_g):
    bsz, seq, d = x.shape
    n_ctx = ctx.shape[1]
    hy_w = w_hy_out.shape[1]
    hg_w = w_hg_out.shape[1]
    dk = hg_norm_g.shape[1]
    n_heads = hg_w // dk
    ne = router_w.shape[2]
    l = 0

    c_rows = jnp.zeros((SUBLANES, d), F32).at[:bsz].set(c).at[bsz].set(c_ctx)
    mods = ada_vectors(c_rows, ada_w[l], ada_b[l])
    sh1, sc1, g1, sh2, sc2, g2 = [mods[:bsz, j * d:(j + 1) * d].reshape(bsz, 1, d) for j in range(N_ADA)]
    csh1 = jnp.broadcast_to(mods[bsz, 0:d].reshape(1, 1, d), (bsz, 1, d))
    csc1 = jnp.broadcast_to(mods[bsz, d:2 * d].reshape(1, 1, d), (bsz, 1, d))

    emb_r, emb_c = _pos_tables(seq // GRID_W, GRID_W, d)
    w_in_b = w_in[l].astype(BF16)
    hy_proj = 3 * hy_w
    p = in_projection(x, emb_r, emb_c, norm1_g[l], sh1, sc1, w_in_b, TOKEN_TILE)
    hg_cols = slice(hy_proj, hy_proj + 5 * hg_w)
    zero_r = jnp.zeros((n_ctx // GRID_W, 1, d // 2), F32)
    zero_c = jnp.zeros((GRID_W, d // 2), F32)
    pc = in_projection(ctx, zero_r, zero_c, norm1_g[l], csh1, csc1, w_in_b[:, hg_cols], n_ctx)

    lbs = jnp.cumsum(jax.nn.softmax(hg_lb_logits.astype(F32), axis=0), axis=0)
    lb_f, lb_b = lbs[l, 0], lbs[l, 1]
    zero_state = jnp.zeros((bsz, n_heads, dk, dk), F32)
    base = hy_proj // hg_w
    _, st_f = hgrn_scan(pc, (0, 1, 2), lb_f, zero_state, n_ctx, n_ctx, reverse=False)
    _, st_b = hgrn_scan(pc, (0, 1, 3), lb_b, zero_state, n_ctx, n_ctx, reverse=True)
    o_f, _ = hgrn_scan(p, (base, base + 1, base + 2), lb_f, st_f, seq, HG_TIME_BLOCK, reverse=False)
    y_hg, _ = hgrn_scan(p, (base, base + 1, base + 3), lb_b, st_b, seq, HG_TIME_BLOCK, reverse=True,
                        o_fwd=o_f, gate_col=base + 4, norm_g=hg_norm_g[l])

    u = short_conv(p, hy_proj, hy_conv_w[l], hy_conv_b[l], seq, CONV_TILE, hy_w)
    taps, l1 = hyena_filter_taps(seq, hy_f_w1[l], hy_f_b1[l], hy_f_w2[l], hy_f_b2[l], hy_f_w3[l], hy_f_b3[l],
                                 hy_f_w4[l], hy_f_freq[l], hy_w)
    y_hy = hyena_branch(u, bsz, seq, hy_w, taps, l1, hy_skip[l])

    gate_base = (hy_proj + 5 * hg_w) // d
    sh_gu = jnp.concatenate([sh_w_gate[l], sh_w_up[l]], axis=1).astype(BF16)
    xres, h2p, logits_t = merge_stage(
        x, emb_r, emb_c, y_hy, y_hg, p, (gate_base, gate_base + 1),
        w_hy_out[l].astype(BF16), w_hg_out[l].astype(BF16), w_out[l].astype(BF16), g1, norm2_g[l],
        sh2, sc2, g2, router_w[l].T.astype(F32), sh_gu, sh_w_down[l].astype(BF16), TOKEN_TILE)

    t = bsz * seq
    eidx, wsel, rank, counts = route(logits_t, router_bias[l], TOKEN_TILE)
    n_rows = t * TOP_K + ne * (MOE_ROWS - 1)
    n_blocks = pl.cdiv(pl.cdiv(n_rows, MOE_ROWS), MOE_STEP_BLOCKS) * MOE_STEP_BLOCKS
    dest, block_e, block_valid, n_used = dispatch_plan(counts, eidx, rank, TOKEN_TILE, n_blocks)

    dest_flat = dest.reshape(-1)
    xs = scatter_rows(dest_flat, h2p, n_blocks * MOE_ROWS)
    ys = grouped_mlp(block_e.reshape(-1), block_valid.reshape(-1), n_used.reshape(-1)[:1], xs,
                     exp_w_gate[l], exp_w_up[l], exp_w_down[l], n_blocks)
    wsel_t = wsel.T
    out = None
    for h in range(GATHER_SPLIT):
        lo = h * (t // GATHER_SPLIT)
        rng = dest[:, lo:lo + t // GATHER_SPLIT]
        y_tok = gather_rows(rng.reshape(-1), ys).reshape(TOP_K, t // GATHER_SPLIT, d // 2)
        out = combine(y_tok, xres, wsel_t, g2, final_g, seq, TOKEN_TILE, lo // TOKEN_TILE, out)
    return out.reshape(bsz, seq, d)
```

```python
import functools
import math

import numpy as np
import jax
import jax.numpy as jnp
from jax import lax
from jax.experimental import pallas as pl
from jax.experimental.pallas import tpu as pltpu
from jax.experimental.pallas import tpu_sc as plsc

F32 = jnp.float32
BF16 = jnp.bfloat16
U32 = jnp.uint32
I32 = jnp.int32
HIGHEST = lax.Precision.HIGHEST

GRID_W = 64
POS_BASE = 10000.0
NORM_EPS = 1e-6
N_ADA = 6
HY_ORDER = 2
HY_SHORT = 3
HY_DECAY_TARGET = 1e-2
HY_FAST_PCT = 0.3
HY_SLOW_PCT = 1.5
HG_HEADS = 4
HG_CHUNK = 64
N_GROUPS = 8
TOPK_GROUPS = 4
TOP_K = 8
ROUTED_SCALE = 2.5

LANES = 128
SUBLANES = 8
VMEM_LIMIT = 56 * 1024 * 1024

TOKEN_TILE = 512
HG_TIME_BLOCK = 512
HALO_ROWS = 16
CONV_TILE = 2048
DFT_P = 128
FILTER_GROUP = 8
DFT_GROUP = 16
MOE_ROWS = 256
MOE_STEP_BLOCKS = 4
WEIGHT_AHEAD = 3
WEIGHT_SLOTS = WEIGHT_AHEAD + MOE_STEP_BLOCKS
SC_ROWS = 128
GATHER_SPLIT = 4


def _cparams(*sem):
    return pltpu.CompilerParams(dimension_semantics=sem, vmem_limit_bytes=VMEM_LIMIT)


def _dot(a, b):
    return jnp.dot(a, b, preferred_element_type=F32)


def _dot_hi(a, b):
    return jnp.dot(a, b, preferred_element_type=F32, precision=HIGHEST)


def _dot_nt(a, b):
    return lax.dot_general(a, b, (((1,), (1,)), ((), ())), preferred_element_type=F32)


def _dot_tn(a, b):
    return lax.dot_general(a, b, (((0,), (0,)), ((), ())), preferred_element_type=F32)


def _silu(x):
    return x * jax.nn.sigmoid(x)


def _split_bf16(x):
    hi = x.astype(BF16)
    return hi, (x - hi.astype(F32)).astype(BF16)


def _ada_kernel(c_ref, w_ref, b_ref, o_ref):
    o_ref[...] = _dot_hi(_silu(c_ref[...]), w_ref[...]) + b_ref[...]


def ada_vectors(c_rows, ada_w, ada_b):
    r, d = c_rows.shape
    n = ada_w.shape[1]
    bn = 1024
    return pl.pallas_call(
        _ada_kernel,
        out_shape=jax.ShapeDtypeStruct((r, n), F32),
        grid=(n // bn,),
        in_specs=[pl.BlockSpec((r, d), lambda j: (0, 0)),
                  pl.BlockSpec((d, bn), lambda j: (0, j)),
                  pl.BlockSpec((1, bn), lambda j: (0, j))],
        out_specs=pl.BlockSpec((r, bn), lambda j: (0, j)),
        compiler_params=_cparams("arbitrary"),
        name="ada_vectors",
    )(c_rows, ada_w, ada_b.reshape(1, n))


def _inproj_kernel(x_ref, er_ref, ec_ref, g_ref, sh_ref, sc_ref, w_ref, o_ref, *, col_chunk):
    x = x_ref[...]
    rows, gw, d = x.shape
    half = d // 2
    xp = jnp.concatenate([x[:, :, :half] + er_ref[...], x[:, :, half:] + ec_ref[...]], axis=-1)
    xp = xp.reshape(rows * gw, d)
    ms = jnp.mean(xp * xp, axis=-1, keepdims=True)
    y = xp * lax.rsqrt(ms + NORM_EPS) * g_ref[...]
    h = (y * (1.0 + sc_ref[...]) + sh_ref[...]).astype(BF16)
    n = o_ref.shape[1]
    for j in range(n // col_chunk):
        sl = slice(j * col_chunk, (j + 1) * col_chunk)
        o_ref[:, sl] = _dot(h, w_ref[:, sl]).astype(o_ref.dtype)


def in_projection(x, emb_r, emb_c, norm_g, shift, scale, w_bf16, tm):
    b, s, d = x.shape
    n = w_bf16.shape[1]
    rows_per_batch = s // GRID_W
    rt = tm // GRID_W
    tiles_per_batch = rows_per_batch // rt
    x3 = x.reshape(b * rows_per_batch, GRID_W, d)
    col_chunk = 512
    return pl.pallas_call(
        functools.partial(_inproj_kernel, col_chunk=col_chunk),
        out_shape=jax.ShapeDtypeStruct((b * s, n), BF16),
        grid=(b * tiles_per_batch,),
        in_specs=[pl.BlockSpec((rt, GRID_W, d), lambda i: (i, 0, 0)),
                  pl.BlockSpec((rt, 1, d // 2), lambda i: (i % tiles_per_batch, 0, 0)),
                  pl.BlockSpec((GRID_W, d // 2), lambda i: (0, 0)),
                  pl.BlockSpec((1, d), lambda i: (0, 0)),
                  pl.BlockSpec((None, 1, d), lambda i: (i // tiles_per_batch, 0, 0)),
                  pl.BlockSpec((None, 1, d), lambda i: (i // tiles_per_batch, 0, 0)),
                  pl.BlockSpec((d, n), lambda i: (0, 0))],
        out_specs=pl.BlockSpec((tm, n), lambda i: (i, 0)),
        compiler_params=_cparams("arbitrary"),
        name="in_projection",
    )(x3, emb_r, emb_c, norm_g.reshape(1, d), shift, scale, w_bf16)


def _hgrn_kernel(*refs, reverse, n_chunks, final):
    if final:
        (q_ref, i_ref, f_ref, lb_ref, s0_ref, of_ref, gate_ref, ng_ref, o_ref, sfin_ref, s_scr) = refs
    else:
        (q_ref, i_ref, f_ref, lb_ref, s0_ref, o_ref, sfin_ref, s_scr) = refs
    cs = HG_CHUNK
    bsz, n_heads, _, dk = s_scr.shape

    @pl.when(pl.program_id(0) == 0)
    def _():
        s_scr[...] = s0_ref[...]

    row = lax.broadcasted_iota(I32, (cs, cs), 0)
    col = lax.broadcasted_iota(I32, (cs, cs), 1)
    tri = (col >= row) if reverse else (col <= row)
    tri_b = tri.astype(BF16)
    end_row = 0 if reverse else cs - 1
    mid_row = cs // 2 if reverse else cs // 2 - 1

    def chunk_body(bi, ci):
        c = (n_chunks - 1 - ci) if reverse else ci
        rows = slice(c * cs, (c + 1) * cs)
        lb = lb_ref[...]
        f = lb + (1.0 - lb) * jax.nn.sigmoid(f_ref[bi, rows, :].astype(F32))
        lf_hi, lf_lo = _split_bf16(jnp.log(f))
        b_all = _dot(tri_b, lf_hi) + _dot(tri_b, lf_lo)
        k_all = 1.0 - f
        q_all = _silu(q_ref[bi, rows, :].astype(F32))
        for h in range(n_heads):
            sl = slice(h * dk, (h + 1) * dk)
            b = b_all[:, sl]
            q = q_all[:, sl]
            k = k_all[:, sl]
            v = i_ref[bi, rows, sl]
            b_end = b[end_row:end_row + 1]
            b_mid = b[mid_row:mid_row + 1]
            qd = (q * jnp.exp(b - b_mid)).astype(BF16)
            kd = (k * jnp.exp(b_mid - b)).astype(BF16)
            att = jnp.where(tri, _dot_nt(qd, kd), 0.0).astype(BF16)
            st = s_scr[bi, h]
            qe = (q * jnp.exp(b)).astype(BF16)
            o = _dot(jnp.concatenate([qe, att], axis=1), jnp.concatenate([st.astype(BF16), v], axis=0))
            ke = (k * jnp.exp(b_end - b)).astype(BF16)
            dec = jnp.transpose(jnp.broadcast_to(jnp.exp(b_end), (SUBLANES, dk)))[:, 0:1]
            s_scr[bi, h] = st * dec + _dot_tn(ke, v)
            if final:
                o = o + of_ref[bi, rows, sl].astype(F32)
                o = o * lax.rsqrt(jnp.mean(o * o, axis=-1, keepdims=True) + NORM_EPS) * ng_ref[...]
                o = o * _silu(gate_ref[bi, rows, sl].astype(F32))
            o_ref[bi, rows, sl] = o.astype(o_ref.dtype)

    for ci in range(n_chunks):
        for bi in range(bsz):
            chunk_body(bi, ci)
    sfin_ref[...] = s_scr[...]


def hgrn_scan(p, cols, lb, s0, seq, tb, *, reverse, o_fwd=None, gate_col=None, norm_g=None):
    bsz, n_heads, dv, dk = s0.shape
    width = n_heads * dk
    nt = seq // tb
    final = o_fwd is not None
    p3 = p.reshape(bsz, seq, p.shape[1])
    tmap = (lambda t: nt - 1 - t) if reverse else (lambda t: t)
    colspec = lambda cb: pl.BlockSpec((bsz, tb, width), lambda t: (0, tmap(t), cb))
    state = pl.BlockSpec((bsz, n_heads, dv, dk), lambda t: (0, 0, 0, 0))
    in_specs = [colspec(cols[0]), colspec(cols[1]), colspec(cols[2]),
                pl.BlockSpec((1, width), lambda t: (0, 0)), state]
    args = [p3, p3, p3, lb.reshape(1, width), s0]
    if final:
        in_specs += [colspec(0), colspec(gate_col), pl.BlockSpec((1, dk), lambda t: (0, 0))]
        args += [o_fwd.reshape(bsz, seq, width), p3, norm_g.reshape(1, dk)]
    o, s_fin = pl.pallas_call(
        functools.partial(_hgrn_kernel, reverse=reverse, n_chunks=tb // HG_CHUNK, final=final),
        out_shape=(jax.ShapeDtypeStruct((bsz, seq, width), BF16),
                   jax.ShapeDtypeStruct((bsz, n_heads, dv, dk), F32)),
        grid=(nt,),
        in_specs=in_specs,
        out_specs=(colspec(0), state),
        scratch_shapes=[pltpu.VMEM((bsz, n_heads, dv, dk), F32)],
        compiler_params=_cparams("arbitrary"),
        name="hgrn_bwd" if reverse else "hgrn_fwd",
    )(*args)
    return o.reshape(bsz * seq, width), s_fin


def _shortconv_kernel(p_ref, prev_ref, next_ref, w_ref, b_ref, o_ref, *, tiles_per_batch):
    i = pl.program_id(0)
    ti = i % tiles_per_batch
    p = p_ref[...].astype(F32)
    tm, cw = p.shape
    ta = tm // DFT_P
    prev_row = jnp.where(ti == 0, 0.0, prev_ref[HALO_ROWS - 1:HALO_ROWS, :].astype(F32))
    next_row = jnp.where(ti == tiles_per_batch - 1, 0.0, next_ref[0:1, :].astype(F32))
    p3 = jnp.swapaxes(p.reshape(ta, DFT_P, cw), 0, 1)
    arow = lax.broadcasted_iota(I32, (ta, 1), 0)
    prev_edge = jnp.where(arow == 0, prev_row, pltpu.roll(p3[DFT_P - 1], 1, axis=0))
    next_edge = jnp.where(arow == ta - 1, next_row, pltpu.roll(p3[0], ta - 1, axis=0))
    p_prev = jnp.concatenate([prev_edge[None], p3[:-1]], axis=0)
    p_next = jnp.concatenate([p3[1:], next_edge[None]], axis=0)
    u = w_ref[0:1, :] * p_prev + w_ref[1:2, :] * p3 + w_ref[2:3, :] * p_next + b_ref[...]
    o_ref[...] = u.astype(o_ref.dtype)


def short_conv(p, width, conv_w, conv_b, seq, tm, cw):
    t = p.shape[0]
    nt = t // tm
    tiles_per_batch = seq // tm
    sub = tm // HALO_ROWS
    ta = tm // DFT_P
    return pl.pallas_call(
        functools.partial(_shortconv_kernel, tiles_per_batch=tiles_per_batch),
        out_shape=jax.ShapeDtypeStruct((DFT_P, t // DFT_P, width), BF16),
        grid=(nt, width // cw),
        in_specs=[pl.BlockSpec((tm, cw), lambda i, j: (i, j)),
                  pl.BlockSpec((HALO_ROWS, cw), lambda i, j: (jnp.maximum(i * sub - 1, 0), j)),
                  pl.BlockSpec((HALO_ROWS, cw), lambda i, j: (jnp.minimum((i + 1) * sub, t // HALO_ROWS - 1), j)),
                  pl.BlockSpec((HY_SHORT, cw), lambda i, j: (0, j)),
                  pl.BlockSpec((1, cw), lambda i, j: (0, j))],
        out_specs=pl.BlockSpec((DFT_P, ta, cw), lambda i, j: (0, i, j)),
        compiler_params=_cparams("arbitrary", "arbitrary"),
        name="short_conv",
    )(p, p, p, conv_w, conv_b.reshape(1, width))


def _filter_kernel(band_ref, w1t_ref, w1c_ref, w1s_ref, b1_ref, w2_ref, b2_ref, w3_ref, b3_ref,
                   w4f_ref, w4b_ref, fr_ref, delta_ref, k_ref, s_ref, *, seq):
    step = pl.program_id(0)
    gb, q, ncol = k_ref.shape
    half = q // 2
    width = delta_ref.shape[1]
    nrow = gb * q
    nf = gb * half

    def positions(shape, axis):
        r = lax.broadcasted_iota(I32, shape, axis)
        is_bwd = r >= nf
        rr = jnp.where(is_bwd, r - nf, r)
        j = lax.shift_right_logical(rr, int(math.log2(half)))
        a = (rr & (half - 1)) + jnp.where(is_bwd, half, 0)
        n = (a * DFT_P + step * gb + j).astype(F32)
        t = jnp.where(is_bwd, 2.0 * seq - n, n)
        return n, t, t / float(max(seq - 1, 1))

    _, t_l, tn_l = positions((1, nrow), 1)
    ang = (2.0 * math.pi / seq) * t_l * band_ref[...]
    fr = fr_ref[...]
    pre = (w1t_ref[...] * tn_l + _dot_hi(w1c_ref[...], jnp.cos(ang)) - _dot_hi(w1s_ref[...], jnp.sin(ang))
           + b1_ref[...])
    act = jnp.sin(fr * pre)
    act = jnp.sin(fr * (_dot_hi(w2_ref[...], act) + b2_ref[...]))
    act = jnp.sin(fr * (_dot_hi(w3_ref[...], act) + b3_ref[...])).astype(BF16)
    n_s, _, tn_s = positions((nrow, 1), 0)
    delta = jnp.concatenate([delta_ref[...]] * (ncol // width), axis=1)
    hf = _dot_tn(act[:, :nf], w4f_ref[...]) * jnp.exp(-tn_s[:nf] * delta)
    hb = _dot_tn(act[:, nf:], w4b_ref[...]) * jnp.exp(-tn_s[nf:] * delta)
    hb = jnp.where(n_s[nf:] == float(seq), 0.0, hb)
    k_ref[:, :half, :] = hf.reshape(gb, half, ncol).astype(k_ref.dtype)
    k_ref[:, half:, :] = hb.reshape(gb, half, ncol).astype(k_ref.dtype)
    tot = jnp.sum(jnp.abs(hf), axis=0, keepdims=True) + jnp.sum(jnp.abs(hb), axis=0, keepdims=True)

    @pl.when(step == 0)
    def _():
        s_ref[...] = jnp.zeros_like(s_ref)

    s_ref[...] += tot


def hyena_filter_taps(seq, w1, b1, w2, b2, w3, b3, w4, freq, width):
    emb = w1.shape[0]
    hid = w1.shape[1]
    bands = (emb - 1) // 2
    q = 2 * seq // DFT_P
    ncol = HY_ORDER * width
    band = np.linspace(1e-4, bands - 1, bands, dtype=np.float32).reshape(bands, 1)
    min_decay = math.log(HY_DECAY_TARGET) / HY_SLOW_PCT
    max_decay = math.log(HY_DECAY_TARGET) / HY_FAST_PCT
    delta = np.abs(np.linspace(min_decay, max_decay, width, dtype=np.float32)).reshape(1, width)
    w1t = w1.astype(F32).T
    col = lambda v: v.reshape(hid, 1).astype(F32)
    w4r = w4.astype(BF16).reshape(hid, HY_ORDER, 2, width)
    w4f = w4r[:, :, 0, :].reshape(hid, ncol)
    w4b = w4r[:, :, 1, :].reshape(hid, ncol)
    gb = FILTER_GROUP
    const = lambda shape: pl.BlockSpec(shape, lambda i: tuple(0 for _ in shape))
    return pl.pallas_call(
        functools.partial(_filter_kernel, seq=seq),
        out_shape=(jax.ShapeDtypeStruct((DFT_P, q, ncol), BF16),
                   jax.ShapeDtypeStruct((1, ncol), F32)),
        grid=(DFT_P // gb,),
        in_specs=[const((bands, 1)), const((hid, 1)), const((hid, bands)), const((hid, bands)), const((hid, 1)),
                  const((hid, hid)), const((hid, 1)), const((hid, hid)), const((hid, 1)),
                  const((hid, ncol)), const((hid, ncol)), const((hid, 1)), const((1, width))],
        out_specs=(pl.BlockSpec((gb, q, ncol), lambda i: (i, 0, 0)),
                   pl.BlockSpec((1, ncol), lambda i: (0, 0))),
        compiler_params=_cparams("arbitrary"),
        name="hyena_filter",
    )(jnp.asarray(band), w1t[:, 0:1], w1t[:, 1:1 + bands], w1t[:, 1 + bands:1 + 2 * bands], col(b1),
      w2.astype(F32).T, col(b2), w3.astype(F32).T, col(b3), w4f, w4b, col(freq), jnp.asarray(delta))


def _dft_tables(seq):
    p = DFT_P
    n_fft = 2 * seq
    q = n_fft // p
    qh = q // 2
    ka = np.arange(q)
    nn = np.arange(q)[None, :] * p + np.arange(p)[:, None]
    ang = ((ka[None, :, None] * nn[:, None, :]) % n_fft) * (2.0 * np.pi / n_fft)
    mr, mi = np.cos(ang), -np.sin(ang)
    m1c = np.concatenate([np.concatenate([mr[:, :, :qh], -mi[:, :, :qh]], axis=2),
                          np.concatenate([mi[:, :, :qh], mr[:, :, :qh]], axis=2)], axis=1)
    m1r = np.concatenate([mr, mi], axis=1)
    gr = np.swapaxes(mr[:, :, :qh], 1, 2) / n_fft
    gi = -np.swapaxes(mi[:, :, :qh], 1, 2) / n_fft
    m4 = np.concatenate([np.concatenate([gr, -gi], axis=2), np.concatenate([gi, gr], axis=2)], axis=1)
    kb = np.arange(p)
    ang2 = 2.0 * np.pi * ((kb[:, None] * kb[None, :]) % p) / p
    fr, fi = np.cos(ang2), -np.sin(ang2)
    m2 = np.block([[fr, -fi], [fi, fr]])
    m3 = np.block([[fr, fi], [-fi, fr]])
    return tuple(jnp.asarray(m.astype(np.float32).astype(BF16)) for m in (m1c, m1r, m2, m3, m4))


def _store_swapped(o_ref, res):
    g2 = o_ref.shape[0]
    stacked = jnp.stack(res, axis=0)
    for ri in range(2):
        o_ref[:, ri, :, :] = jnp.swapaxes(stacked[:, ri * g2:(ri + 1) * g2, :], 0, 1).astype(o_ref.dtype)


def _swapped_out(g, m, ncols, gb, n_col_blocks=1):
    shape = (m // 2, 2, g, ncols * n_col_blocks)
    if n_col_blocks == 1:
        return shape, pl.BlockSpec((m // 2, 2, gb, ncols), lambda i: (0, 0, i, 0))
    return shape, pl.BlockSpec((m // 2, 2, gb, ncols), lambda i, j: (0, 0, i, j))


def _bmm_kernel(w_ref, x_ref, o_ref, *, shared_w):
    res = []
    for j in range(x_ref.shape[0]):
        w = w_ref[...] if shared_w else w_ref[j]
        res.append(_dot(w, x_ref[j]))
    _store_swapped(o_ref, res)


def batched_left_matmul(w, x, col_block, ncols, name, gb, n_col_blocks=1):
    g, k = x.shape[0], x.shape[1]
    shared = w.ndim == 2
    m = w.shape[-2]
    out_shape, out_spec = _swapped_out(g, m, ncols, gb, n_col_blocks)
    if n_col_blocks == 1:
        grid = (g // gb,)
        wspec = (pl.BlockSpec((m, k), lambda i: (0, 0)) if shared else pl.BlockSpec((gb, m, k), lambda i: (i, 0, 0)))
        xspec = pl.BlockSpec((gb, k, ncols), lambda i: (i, 0, col_block))
        sem = ("arbitrary",)
    else:
        grid = (g // gb, n_col_blocks)
        wspec = (pl.BlockSpec((m, k), lambda i, j: (0, 0)) if shared
                 else pl.BlockSpec((gb, m, k), lambda i, j: (i, 0, 0)))
        xspec = pl.BlockSpec((gb, k, ncols), lambda i, j: (i, 0, j))
        sem = ("arbitrary", "arbitrary")
    out = pl.pallas_call(
        functools.partial(_bmm_kernel, shared_w=shared),
        out_shape=jax.ShapeDtypeStruct(out_shape, BF16),
        grid=grid,
        in_specs=[wspec, xspec],
        out_specs=out_spec,
        compiler_params=_cparams(*sem),
        name=name,
    )(w, x)
    return out.reshape(m // 2, 2 * g, out_shape[3])


def _dft_mid_kernel(m2_ref, m3_ref, x_ref, k_ref, o_ref):
    half = x_ref.shape[1] // 2
    res = []
    for j in range(x_ref.shape[0]):
        xf = _dot(m2_ref[...], x_ref[j])
        kf = _dot(m2_ref[...], k_ref[j])
        xr, xi = xf[:half], xf[half:]
        kr, ki = kf[:half], kf[half:]
        z = jnp.concatenate([xr * kr - xi * ki, xr * ki + xi * kr], axis=0).astype(BF16)
        res.append(_dot(m3_ref[...], z))
    _store_swapped(o_ref, res)


def dft_mid(m2, m3, x, kspec, kcol, ncols):
    g, r = x.shape[0], x.shape[1]
    gb = DFT_GROUP
    out_shape, out_spec = _swapped_out(g, r, ncols, gb)
    out = pl.pallas_call(
        _dft_mid_kernel,
        out_shape=jax.ShapeDtypeStruct(out_shape, BF16),
        grid=(g // gb,),
        in_specs=[pl.BlockSpec((r, r), lambda i: (0, 0)),
                  pl.BlockSpec((r, r), lambda i: (0, 0)),
                  pl.BlockSpec((gb, r, ncols), lambda i: (i, 0, 0)),
                  pl.BlockSpec((gb, r, ncols), lambda i: (i, 0, kcol))],
        out_specs=out_spec,
        compiler_params=_cparams("arbitrary"),
        name="dft_mid",
    )(m2, m3, x, kspec)
    return out.reshape(r // 2, 2 * g, ncols)


def _dft_out_kernel(m4_ref, y_ref, inv_ref, skip_ref, v_ref, mul_ref, o_ref, *, token_order):
    res = []
    for j in range(y_ref.shape[0]):
        conv = _dot(m4_ref[j], y_ref[j]) * inv_ref[...] + v_ref[j].astype(F32) * skip_ref[...]
        res.append(mul_ref[j].astype(F32) * conv)
    if token_order:
        o_ref[...] = jnp.swapaxes(jnp.stack(res, axis=0), 0, 1).astype(o_ref.dtype)
    else:
        for j, r in enumerate(res):
            o_ref[j] = r.astype(o_ref.dtype)


def dft_out(m4, y, inv_l1, skip, u, v_col, mul, mul_col, ncols, token_order):
    g, r = y.shape[0], y.shape[1]
    rows = m4.shape[1]
    gb = DFT_GROUP
    out_shape, out_spec = (((rows, g, ncols), pl.BlockSpec((rows, gb, ncols), lambda i: (0, i, 0))) if token_order
                           else ((g, rows, ncols), pl.BlockSpec((gb, rows, ncols), lambda i: (i, 0, 0))))
    return pl.pallas_call(
        functools.partial(_dft_out_kernel, token_order=token_order),
        out_shape=jax.ShapeDtypeStruct(out_shape, BF16),
        grid=(g // gb,),
        in_specs=[pl.BlockSpec((gb, rows, r), lambda i: (i, 0, 0)),
                  pl.BlockSpec((gb, r, ncols), lambda i: (i, 0, 0)),
                  pl.BlockSpec((1, ncols), lambda i: (0, 0)),
                  pl.BlockSpec((1, ncols), lambda i: (0, 0)),
                  pl.BlockSpec((gb, rows, ncols), lambda i: (i, 0, v_col)),
                  pl.BlockSpec((gb, rows, ncols), lambda i: (i, 0, mul_col))],
        out_specs=out_spec,
        compiler_params=_cparams("arbitrary"),
        name="dft_out",
    )(m4, y, inv_l1, skip, u, mul)


def hyena_branch(u, bsz, seq, width, taps, l1, skip):
    m1c, m1r, m2, m3, m4 = _dft_tables(seq)
    ks1 = batched_left_matmul(m1r, taps, 0, width, "dft_k1", DFT_GROUP, n_col_blocks=HY_ORDER)
    inv_l1 = 1.0 / l1
    z = None
    for order in range(HY_ORDER):
        src, src_col = (u, 0) if order == 0 else (z, 0)
        s1 = batched_left_matmul(m1c, src, src_col, width, "dft_s1", DFT_GROUP)
        mid = dft_mid(m2, m3, s1, ks1, order, width)
        z = dft_out(m4, mid, inv_l1[:, order * width:(order + 1) * width],
                    skip[order].reshape(1, width).astype(F32), src, src_col, u, order + 1, width,
                    token_order=order == HY_ORDER - 1)
    return z.reshape(bsz * seq, width)


def _pack_pairs(x):
    w = x.shape[1] // 2
    u = lax.bitcast_convert_type(x, U32)
    r = (u + U32(0x7FFF) + ((u >> 16) & U32(1))) >> 16
    return r[:, :w] | (r[:, w:] << 16)


def _unpack_pairs(p):
    lo = lax.bitcast_convert_type(p << 16, F32)
    hi = lax.bitcast_convert_type(p & U32(0xFFFF0000), F32)
    return jnp.concatenate([lo, hi], axis=1)


def _pack_pairs_native(x):
    w = x.shape[1] // 2
    return lax.bitcast_convert_type(pltpu.pack_elementwise([x[:, :w], x[:, w:]], packed_dtype=BF16), U32)


def _unpack_pairs_native(p):
    pi = lax.bitcast_convert_type(p, I32)
    halves = [pltpu.unpack_elementwise(pi, index=i, packed_dtype=BF16, unpacked_dtype=F32) for i in range(2)]
    return jnp.concatenate(halves, axis=1)


def _merge_kernel(x_ref, er_ref, ec_ref, yhy_ref, yhg_ref, ghy_ref, ghg_ref, why_ref, whg_ref, wo_ref,
                  g1_ref, n2_ref, sh2_ref, sc2_ref, g2_ref, rwh_ref, rwl_ref, sgu_ref, sd_ref,
                  xres_ref, h2p_ref, lg_ref):
    x = x_ref[...]
    rows, gw, d = x.shape
    half = d // 2
    xp = jnp.concatenate([x[:, :, :half] + er_ref[...], x[:, :, half:] + ec_ref[...]], axis=-1)
    xp = xp.reshape(rows * gw, d)
    m = (jax.nn.sigmoid(ghy_ref[...].astype(F32)) * _dot(yhy_ref[...], why_ref[...])
         + jax.nn.sigmoid(ghg_ref[...].astype(F32)) * _dot(yhg_ref[...], whg_ref[...]))
    x1 = xp + g1_ref[...] * _dot(m.astype(BF16), wo_ref[...])
    ms = jnp.mean(x1 * x1, axis=-1, keepdims=True)
    h2 = x1 * lax.rsqrt(ms + NORM_EPS) * n2_ref[...] * (1.0 + sc2_ref[...]) + sh2_ref[...]
    h_hi, h_lo = _split_bf16(h2)
    lg_ref[...] = _dot_nt(rwh_ref[...], h_hi) + (_dot_nt(rwl_ref[...], h_hi) + _dot_nt(rwh_ref[...], h_lo))
    gu = _dot(h_hi, sgu_ref[...])
    fs = gu.shape[1] // 2
    shared = _dot((_silu(gu[:, :fs]) * gu[:, fs:]).astype(BF16), sd_ref[...])
    xres_ref[...] = x1 + g2_ref[...] * shared
    h2p_ref[...] = _pack_pairs(h2)


def merge_stage(x, emb_r, emb_c, y_hy, y_hg, p, gate_cols, w_hy_out, w_hg_out, w_out, g1, norm2_g,
                sh2, sc2, g2, router_wt, sh_gate_up, sh_down, tm):
    b, s, d = x.shape
    rows_per_batch = s // GRID_W
    rt = tm // GRID_W
    tiles_per_batch = rows_per_batch // rt
    x3 = x.reshape(b * rows_per_batch, GRID_W, d)
    wb = y_hy.shape[1]
    ne = router_wt.shape[0]
    fs2 = sh_gate_up.shape[1]
    rw_hi, rw_lo = _split_bf16(router_wt)
    tok = lambda cb, w: pl.BlockSpec((tm, w), lambda i: (i, cb))
    const = lambda shape: pl.BlockSpec(shape, lambda i: tuple(0 for _ in shape))
    per_b = pl.BlockSpec((None, 1, d), lambda i: (i // tiles_per_batch, 0, 0))
    return pl.pallas_call(
        _merge_kernel,
        out_shape=(jax.ShapeDtypeStruct((b * s, d), F32),
                   jax.ShapeDtypeStruct((b * s, d // 2), U32),
                   jax.ShapeDtypeStruct((ne, b * s), F32)),
        grid=(b * tiles_per_batch,),
        in_specs=[pl.BlockSpec((rt, GRID_W, d), lambda i: (i, 0, 0)),
                  pl.BlockSpec((rt, 1, d // 2), lambda i: (i % tiles_per_batch, 0, 0)),
                  const((GRID_W, d // 2)),
                  tok(0, wb), tok(0, wb), tok(gate_cols[0], d), tok(gate_cols[1], d),
                  const((wb, d)), const((wb, d)), const((d, d)),
                  per_b, const((1, d)), per_b, per_b, per_b,
                  const((ne, d)), const((ne, d)), const((d, fs2)), const((fs2 // 2, d))],
        out_specs=(pl.BlockSpec((tm, d), lambda i: (i, 0)),
                   pl.BlockSpec((tm, d // 2), lambda i: (i, 0)),
                   pl.BlockSpec((ne, tm), lambda i: (0, i))),
        compiler_params=_cparams("arbitrary"),
        name="merge",
    )(x3, emb_r, emb_c, y_hy, y_hg, p, p, w_hy_out, w_hg_out, w_out, g1, norm2_g.reshape(1, d),
      sh2, sc2, g2, rw_hi, rw_lo, sh_gate_up, sh_down)


def _route_kernel(lg_ref, bias_ref, eidx_ref, wsel_ref, rank_ref, cnt_ref, carry):
    ne, tr = lg_ref.shape
    gsz = ne // N_GROUPS
    neg = -jnp.inf

    @pl.when(pl.program_id(0) == 0)
    def _():
        carry[...] = jnp.zeros_like(carry)

    scores = jax.nn.sigmoid(lg_ref[...])
    biased = scores + bias_ref[...]
    riota = lax.broadcasted_iota(I32, (gsz, tr), 0).astype(F32)
    gs = []
    for g in range(N_GROUPS):
        vg = biased[g * gsz:(g + 1) * gsz]
        m1 = jnp.max(vg, axis=0, keepdims=True)
        i1 = jnp.min(jnp.where(vg == m1, riota, float(gsz)), axis=0, keepdims=True)
        m2 = jnp.max(jnp.where(riota == i1, neg, vg), axis=0, keepdims=True)
        gs.append(m1 + m2)
    cur = jnp.concatenate(gs, axis=0)
    giota = lax.broadcasted_iota(I32, (N_GROUPS, tr), 0).astype(F32)
    gsel = jnp.zeros((N_GROUPS, tr), F32)
    for _ in range(TOPK_GROUPS):
        m = jnp.max(cur, axis=0, keepdims=True)
        idx = jnp.min(jnp.where(cur == m, giota, float(N_GROUPS)), axis=0, keepdims=True)
        hit = giota == idx
        gsel = jnp.where(hit, 1.0, gsel)
        cur = jnp.where(hit, neg, cur)
    cur = jnp.concatenate([jnp.where(gsel[g:g + 1] > 0.0, biased[g * gsz:(g + 1) * gsz], neg)
                           for g in range(N_GROUPS)], axis=0)
    eiota = lax.broadcasted_iota(I32, (ne, tr), 0).astype(F32)
    chosen = jnp.zeros((ne, tr), F32)
    idxs, ws = [], []
    for _ in range(TOP_K):
        m = jnp.max(cur, axis=0, keepdims=True)
        idx = jnp.min(jnp.where(cur == m, eiota, float(ne)), axis=0, keepdims=True)
        hit = eiota == idx
        idxs.append(idx)
        ws.append(jnp.sum(jnp.where(hit, scores, 0.0), axis=0, keepdims=True))
        chosen = jnp.where(hit, 1.0, chosen)
        cur = jnp.where(hit, neg, cur)
    w = jnp.concatenate(ws, axis=0)
    wsel_ref[...] = w / jnp.sum(w, axis=0, keepdims=True) * ROUTED_SCALE
    eidx_ref[...] = jnp.concatenate(idxs, axis=0).astype(I32)
    srow = lax.broadcasted_iota(I32, (tr, tr), 0)
    scol = lax.broadcasted_iota(I32, (tr, tr), 1)
    before = (srow < scol).astype(BF16)
    base = carry[...] + _dot(chosen.astype(BF16), before)
    ranks = [jnp.sum(jnp.where(eiota == idx, base, 0.0), axis=0, keepdims=True) for idx in idxs]
    rank_ref[...] = jnp.concatenate(ranks, axis=0).astype(I32)
    carry[...] += jnp.sum(chosen, axis=1, keepdims=True)
    cnt_ref[...] = carry[...]


def route(logits_t, router_bias, tr):
    ne, t = logits_t.shape
    return pl.pallas_call(
        _route_kernel,
        out_shape=(jax.ShapeDtypeStruct((TOP_K, t), I32),
                   jax.ShapeDtypeStruct((TOP_K, t), F32),
                   jax.ShapeDtypeStruct((TOP_K, t), I32),
                   jax.ShapeDtypeStruct((ne, 1), F32)),
        grid=(t // tr,),
        in_specs=[pl.BlockSpec((ne, tr), lambda i: (0, i)),
                  pl.BlockSpec((ne, 1), lambda i: (0, 0))],
        out_specs=(pl.BlockSpec((TOP_K, tr), lambda i: (0, i)),
                   pl.BlockSpec((TOP_K, tr), lambda i: (0, i)),
                   pl.BlockSpec((TOP_K, tr), lambda i: (0, i)),
                   pl.BlockSpec((ne, 1), lambda i: (0, 0))),
        scratch_shapes=[pltpu.VMEM((ne, 1), F32)],
        compiler_params=_cparams("arbitrary"),
        name="route",
    )(logits_t, router_bias.reshape(ne, 1).astype(F32))


def _dest_kernel(cnt_ref, eidx_ref, rank_ref, dest_ref, be_ref, nv_ref, nb_ref, start_scr):
    ne = cnt_ref.shape[0]
    tr = eidx_ref.shape[1]

    @pl.when(pl.program_id(0) == 0)
    def _():
        cnt = jnp.broadcast_to(cnt_ref[...], (ne, LANES))
        padded = jnp.floor((cnt + float(MOE_ROWS - 1)) / float(MOE_ROWS)) * float(MOE_ROWS)
        r = lax.broadcasted_iota(I32, (ne, ne), 0)
        c = lax.broadcasted_iota(I32, (ne, ne), 1)
        start = _dot_hi((c < r).astype(F32), padded)
        start_scr[...] = start
        end = start[:, 0:1] + padded[:, 0:1]
        used = start[:, 0:1] + cnt[:, 0:1]
        nbl = be_ref.shape[1]
        blk_row = (lax.broadcasted_iota(I32, (1, nbl), 1) * MOE_ROWS).astype(F32)
        total = jnp.max(end, axis=0, keepdims=True)
        last_row = total - float(MOE_ROWS)
        blk_row_c = jnp.minimum(blk_row, last_row)
        e_of = jnp.sum((end <= blk_row_c).astype(F32), axis=0, keepdims=True)
        e_of = jnp.minimum(e_of, float(ne - 1))
        eio = lax.broadcasted_iota(I32, (ne, nbl), 0).astype(F32)
        used_e = jnp.sum(jnp.where(eio == e_of, used, 0.0), axis=0, keepdims=True)
        valid = jnp.clip(used_e - blk_row_c, 0.0, float(MOE_ROWS))
        be_ref[...] = e_of.astype(I32)
        nv_ref[...] = jnp.where(blk_row <= last_row, valid, 0.0).astype(I32)
        nb_ref[...] = jnp.broadcast_to(total / float(MOE_ROWS), nb_ref.shape).astype(I32)

    eiota = lax.broadcasted_iota(I32, (ne, tr), 0)
    start_col = start_scr[:, 0:1]
    rows = []
    for k in range(TOP_K):
        hit = eiota == eidx_ref[k:k + 1, :]
        rows.append(jnp.sum(jnp.where(hit, start_col, 0.0), axis=0, keepdims=True))
    dest_ref[...] = jnp.concatenate(rows, axis=0).astype(I32) + rank_ref[...]


def dispatch_plan(counts, eidx, rank, tr, n_blocks):
    ne = counts.shape[0]
    t = eidx.shape[1]
    nbl = pl.cdiv(n_blocks, LANES) * LANES
    return pl.pallas_call(
        _dest_kernel,
        out_shape=(jax.ShapeDtypeStruct((TOP_K, t), I32),
                   jax.ShapeDtypeStruct((1, nbl), I32),
                   jax.ShapeDtypeStruct((1, nbl), I32),
                   jax.ShapeDtypeStruct((1, LANES), I32)),
        grid=(t // tr,),
        in_specs=[pl.BlockSpec((ne, 1), lambda i: (0, 0)),
                  pl.BlockSpec((TOP_K, tr), lambda i: (0, i)),
                  pl.BlockSpec((TOP_K, tr), lambda i: (0, i))],
        out_specs=(pl.BlockSpec((TOP_K, tr), lambda i: (0, i)),
                   pl.BlockSpec((1, nbl), lambda i: (0, 0)),
                   pl.BlockSpec((1, nbl), lambda i: (0, 0)),
                   pl.BlockSpec((1, LANES), lambda i: (0, 0))),
        scratch_shapes=[pltpu.VMEM((ne, LANES), F32)],
        compiler_params=_cparams("arbitrary"),
        name="dispatch_plan",
    )(counts, eidx, rank)


def _sc_workers():
    info = plsc.get_sparse_core_info()
    return info.num_cores, info.num_cores * info.num_subcores


def scatter_rows(dest_flat, h2p, n_rows):
    t, w = h2p.shape
    n_cores, n_workers = _sc_workers()
    per_worker = t // n_workers
    mesh = plsc.VectorSubcoreMesh(core_axis_name="c", subcore_axis_name="s")

    @functools.partial(
        pl.kernel, mesh=mesh, out_type=jax.ShapeDtypeStruct((n_rows, w), U32),
        scratch_types=[pltpu.VMEM((TOP_K, SC_ROWS), I32), pltpu.VMEM((SC_ROWS, w), U32), pltpu.SemaphoreType.DMA])
    def body(h_hbm, dest_hbm, xs_hbm, idx_v, rows_v, sem):
        base = (lax.axis_index("s") * n_cores + lax.axis_index("c")) * per_worker

        @pl.loop(0, per_worker // SC_ROWS)
        def _(ci):
            off = pl.multiple_of(base + ci * SC_ROWS, SC_ROWS)
            pltpu.sync_copy(h_hbm.at[pl.ds(off, SC_ROWS)], rows_v)
            for k in range(TOP_K):
                pltpu.sync_copy(dest_hbm.at[pl.ds(k * t + off, SC_ROWS)], idx_v.at[k])
            copies = [pltpu.async_copy(rows_v, xs_hbm.at[idx_v.at[k]], sem) for k in range(TOP_K)]
            for c in copies:
                c.wait()

    return body(h2p, dest_flat)


def gather_rows(idx_flat, table):
    n = idx_flat.shape[0]
    w = table.shape[1]
    n_cores, n_workers = _sc_workers()
    per_worker = n // n_workers
    mesh = plsc.VectorSubcoreMesh(core_axis_name="c", subcore_axis_name="s")

    ch = SC_ROWS // 2
    n_chunks = per_worker // ch

    @functools.partial(
        pl.kernel, mesh=mesh, out_type=jax.ShapeDtypeStruct((n, w), table.dtype),
        scratch_types=[pltpu.VMEM((2, ch), I32), pltpu.VMEM((2, ch, w), table.dtype), pltpu.SemaphoreType.DMA((2,))])
    def body(table_hbm, idx_hbm, out_hbm, idx_v, rows_v, sem):
        base = (lax.axis_index("s") * n_cores + lax.axis_index("c")) * per_worker

        def read(b):
            return pltpu.make_async_copy(table_hbm.at[idx_v.at[b]], rows_v.at[b], sem.at[b])

        def start(c, b):
            off = pl.multiple_of(base + c * ch, ch)
            pltpu.sync_copy(idx_hbm.at[pl.ds(off, ch)], idx_v.at[b])
            read(b).start()

        def finish(c, b):
            read(b).wait()
            pltpu.sync_copy(rows_v.at[b], out_hbm.at[pl.ds(pl.multiple_of(base + c * ch, ch), ch)])

        start(0, 0)

        @pl.loop(0, n_chunks, step=2)
        def _(c):
            start(c + 1, 1)
            finish(c, 0)

            @pl.when(c + 2 < n_chunks)
            def _():
                start(c + 2, 0)

            finish(c + 1, 1)

    return body(table, idx_flat)


def _gmm_kernel(be_ref, nv_ref, nb_ref, first_ref, run_ref, slot_ref, rexp_ref, xs_ref, wg_hbm, wu_hbm, wd_hbm,
                y_ref, wg_buf, wu_buf, wd_buf, sem):
    def weight_copies(e, s):
        return (pltpu.make_async_copy(wg_hbm.at[e], wg_buf.at[s], sem.at[s]),
                pltpu.make_async_copy(wu_hbm.at[e], wu_buf.at[s], sem.at[s]),
                pltpu.make_async_copy(wd_hbm.at[e], wd_buf.at[s], sem.at[s]))

    def fetch(run, slot):
        e = rexp_ref[run]

        @pl.when(e >= 0)
        def _():
            for c in weight_copies(e, slot):
                c.start()

    j0 = pl.program_id(0) * MOE_STEP_BLOCKS

    @pl.when(j0 < nb_ref[0])
    def _():
        for sb in range(MOE_STEP_BLOCKS):
            j = j0 + sb

            @pl.when(j == 0)
            def _():
                for a in range(WEIGHT_AHEAD):
                    fetch(a, a)

            @pl.when((j < nb_ref[0]) & (first_ref[j] == 1))
            def _():
                s = slot_ref[j]
                for c in weight_copies(be_ref[j], s):
                    c.wait()
                fetch(run_ref[j] + WEIGHT_AHEAD, (s + WEIGHT_AHEAD) % WEIGHT_SLOTS)

        for sb in range(MOE_STEP_BLOCKS):
            j = j0 + sb
            s = slot_ref[j]
            rows = slice(sb * MOE_ROWS, (sb + 1) * MOE_ROWS)
            x = _unpack_pairs_native(xs_ref[rows, :])
            row = lax.broadcasted_iota(I32, (x.shape[0], 1), 0)
            x = jnp.where(row < nv_ref[j], x, 0.0)
            hmid = _silu(_dot(x, wg_buf[s])) * _dot(x, wu_buf[s])
            y_ref[rows, :] = _pack_pairs_native(_dot(hmid, wd_buf[s]))


def grouped_mlp(block_e, block_valid, n_used, xs, w_gate, w_up, w_down, n_blocks):
    ne, d, f = w_gate.shape
    w = xs.shape[1]
    jj = jnp.arange(block_e.shape[0], dtype=I32)
    active = jj < n_used[0]
    first = (active & ((jj == 0) | (block_e != jnp.roll(block_e, 1)))).astype(I32)
    run = jnp.cumsum(first) - 1
    slot = (run % WEIGHT_SLOTS).astype(I32)
    nbl = block_e.shape[0]
    run_expert = jnp.full((nbl + WEIGHT_SLOTS,), -1, I32).at[jnp.where(first == 1, run, nbl)].set(
        jnp.where(first == 1, block_e, -1))
    run = run.astype(I32)
    step_rows = MOE_STEP_BLOCKS * MOE_ROWS
    last = lambda g, nb: jnp.minimum(g, (nb[0] - 1) // MOE_STEP_BLOCKS)
    row_block = pl.BlockSpec((step_rows, w), lambda g, be, nv, nb, fi, rn, sl, rx: (last(g, nb), 0))
    grid_spec = pltpu.PrefetchScalarGridSpec(
        num_scalar_prefetch=7,
        grid=(n_blocks // MOE_STEP_BLOCKS,),
        in_specs=[row_block,
                  pl.BlockSpec(memory_space=pl.ANY),
                  pl.BlockSpec(memory_space=pl.ANY),
                  pl.BlockSpec(memory_space=pl.ANY)],
        out_specs=row_block,
        scratch_shapes=[pltpu.VMEM((WEIGHT_SLOTS, d, f), F32), pltpu.VMEM((WEIGHT_SLOTS, d, f), F32),
                        pltpu.VMEM((WEIGHT_SLOTS, f, d), F32), pltpu.SemaphoreType.DMA((WEIGHT_SLOTS,))],
    )
    return pl.pallas_call(
        _gmm_kernel,
        out_shape=jax.ShapeDtypeStruct(xs.shape, U32),
        grid_spec=grid_spec,
        compiler_params=_cparams("arbitrary"),
        name="grouped_mlp",
    )(block_e, block_valid, n_used, first, run, slot, run_expert, xs, w_gate, w_up, w_down)


def _combine_kernel(y_ref, xres_ref, wt_ref, g2_ref, fg_ref, o_ref):
    wt = wt_ref[...]
    routed = jnp.zeros(xres_ref.shape, F32)
    for k in range(TOP_K):
        routed = routed + wt[:, k:k + 1] * _unpack_pairs_native(y_ref[k])
    x2 = xres_ref[...] + g2_ref[...] * routed
    ms = jnp.mean(x2 * x2, axis=-1, keepdims=True)
    o_ref[...] = x2 * lax.rsqrt(ms + NORM_EPS) * fg_ref[...]


def _combine_into_kernel(prev_ref, *refs):
    del prev_ref
    _combine_kernel(*refs)


def combine(y_tok, xres, wsel_t, g2, final_g, seq, tm, first_tile, prev_out):
    t, d = xres.shape
    tiles_per_batch = seq // tm
    tile = lambda i: i + first_tile
    in_specs = [pl.BlockSpec((TOP_K, tm, d // 2), lambda i: (0, i, 0)),
                pl.BlockSpec((tm, d), lambda i: (tile(i), 0)),
                pl.BlockSpec((tm, TOP_K), lambda i: (tile(i), 0)),
                pl.BlockSpec((None, 1, d), lambda i: (tile(i) // tiles_per_batch, 0, 0)),
                pl.BlockSpec((1, d), lambda i: (0, 0))]
    args = [y_tok, xres, wsel_t, g2, final_g.reshape(1, d)]
    body, aliases = _combine_kernel, {}
    if prev_out is not None:
        body, aliases = _combine_into_kernel, {0: 0}
        in_specs = [pl.BlockSpec(memory_space=pl.ANY)] + in_specs
        args = [prev_out] + args
    return pl.pallas_call(
        body,
        out_shape=jax.ShapeDtypeStruct((t, d), F32),
        grid=(y_tok.shape[1] // tm,),
        in_specs=in_specs,
        out_specs=pl.BlockSpec((tm, d), lambda i: (tile(i), 0)),
        input_output_aliases=aliases,
        compiler_params=_cparams("arbitrary"),
        name="combine",
    )(*args)


def _pos_tables(rows, cols, dim):
    quarter = dim // 4
    omega = 1.0 / (POS_BASE ** (np.arange(quarter, dtype=np.float32) / quarter))
    ang_r = np.arange(rows, dtype=np.float32)[:, None] * omega
    ang_c = np.arange(cols, dtype=np.float32)[:, None] * omega
    emb_r = np.concatenate([np.sin(ang_r), np.cos(ang_r)], axis=-1).astype(np.float32)
    emb_c = np.concatenate([np.sin(ang_c), np.cos(ang_c)], axis=-1).astype(np.float32)
    return jnp.asarray(emb_r.reshape(rows, 1, dim // 2)), jnp.asarray(emb_c)


def kernel(x, c, ctx, c_ctx, norm1_g, norm2_g, ada_w, ada_b, w_in, hy_conv_w, hy_conv_b, hy_f_w1, hy_f_b1, hy_f_w2, hy_f_b2, hy_f_w3, hy_f_b3, hy_f_w4, hy_f_freq, hy_skip, hg_lb_logits, hg_norm_g, w_hy_out, w_hg_out, w_out, router_w, router_bias, exp_w_gate, exp_w_up, exp_w_down, sh_w_gate, sh_w_up, sh_w_down, final_g):
    bsz, seq, d = x.shape
    n_ctx = ctx.shape[1]
    hy_w = w_hy_out.shape[1]
    hg_w = w_hg_out.shape[1]
    dk = hg_norm_g.shape[1]
    n_heads = hg_w // dk
    ne = router_w.shape[2]
    l = 0

    c_rows = jnp.zeros((SUBLANES, d), F32).at[:bsz].set(c).at[bsz].set(c_ctx)
    mods = ada_vectors(c_rows, ada_w[l], ada_b[l])
    sh1, sc1, g1, sh2, sc2, g2 = [mods[:bsz, j * d:(j + 1) * d].reshape(bsz, 1, d) for j in range(N_ADA)]
    csh1 = jnp.broadcast_to(mods[bsz, 0:d].reshape(1, 1, d), (bsz, 1, d))
    csc1 = jnp.broadcast_to(mods[bsz, d:2 * d].reshape(1, 1, d), (bsz, 1, d))

    emb_r, emb_c = _pos_tables(seq // GRID_W, GRID_W, d)
    w_in_b = w_in[l].astype(BF16)
    hy_proj = 3 * hy_w
    p = in_projection(x, emb_r, emb_c, norm1_g[l], sh1, sc1, w_in_b, TOKEN_TILE)
    hg_cols = slice(hy_proj, hy_proj + 5 * hg_w)
    zero_r = jnp.zeros((n_ctx // GRID_W, 1, d // 2), F32)
    zero_c = jnp.zeros((GRID_W, d // 2), F32)
    pc = in_projection(ctx, zero_r, zero_c, norm1_g[l], csh1, csc1, w_in_b[:, hg_cols], n_ctx)

    lbs = jnp.cumsum(jax.nn.softmax(hg_lb_logits.astype(F32), axis=0), axis=0)
    lb_f, lb_b = lbs[l, 0], lbs[l, 1]
    zero_state = jnp.zeros((bsz, n_heads, dk, dk), F32)
    base = hy_proj // hg_w
    _, st_f = hgrn_scan(pc, (0, 1, 2), lb_f, zero_state, n_ctx, n_ctx, reverse=False)
    _, st_b = hgrn_scan(pc, (0, 1, 3), lb_b, zero_state, n_ctx, n_ctx, reverse=True)
    o_f, _ = hgrn_scan(p, (base, base + 1, base + 2), lb_f, st_f, seq, HG_TIME_BLOCK, reverse=False)
    y_hg, _ = hgrn_scan(p, (base, base + 1, base + 3), lb_b, st_b, seq, HG_TIME_BLOCK, reverse=True,
                        o_fwd=o_f, gate_col=base + 4, norm_g=hg_norm_g[l])

    u = short_conv(p, hy_proj, hy_conv_w[l], hy_conv_b[l], seq, CONV_TILE, hy_w)
    taps, l1 = hyena_filter_taps(seq, hy_f_w1[l], hy_f_b1[l], hy_f_w2[l], hy_f_b2[l], hy_f_w3[l], hy_f_b3[l],
                                 hy_f_w4[l], hy_f_freq[l], hy_w)
    y_hy = hyena_branch(u, bsz, seq, hy_w, taps, l1, hy_skip[l])

    gate_base = (hy_proj + 5 * hg_w) // d
    sh_gu = jnp.concatenate([sh_w_gate[l], sh_w_up[l]], axis=1).astype(BF16)
    xres, h2p, logits_t = merge_stage(
        x, emb_r, emb_c, y_hy, y_hg, p, (gate_base, gate_base + 1),
        w_hy_out[l].astype(BF16), w_hg_out[l].astype(BF16), w_out[l].astype(BF16), g1, norm2_g[l],
        sh2, sc2, g2, router_w[l].T.astype(F32), sh_gu, sh_w_down[l].astype(BF16), TOKEN_TILE)

    t = bsz * seq
    eidx, wsel, rank, counts = route(logits_t, router_bias[l], TOKEN_TILE)
    n_rows = t * TOP_K + ne * (MOE_ROWS - 1)
    n_blocks = pl.cdiv(pl.cdiv(n_rows, MOE_ROWS), MOE_STEP_BLOCKS) * MOE_STEP_BLOCKS
    dest, block_e, block_valid, n_used = dispatch_plan(counts, eidx, rank, TOKEN_TILE, n_blocks)

    dest_flat = dest.reshape(-1)
    xs = scatter_rows(dest_flat, h2p, n_blocks * MOE_ROWS)
    ys = grouped_mlp(block_e.reshape(-1), block_valid.reshape(-1), n_used.reshape(-1)[:1], xs,
                     exp_w_gate[l], exp_w_up[l], exp_w_down[l], n_blocks)
    wsel_t = wsel.T
    out = None
    for h in range(GATHER_SPLIT):
        lo = h * (t // GATHER_SPLIT)
        rng = dest[:, lo:lo + t // GATHER_SPLIT]
        y_tok = gather_rows(rng.reshape(-1), ys).reshape(TOP_K, t // GATHER_SPLIT, d // 2)
        out = combine(y_tok, xres, wsel_t, g2, final_g, seq, TOKEN_TILE, lo // TOKEN_TILE, out)
    return out.reshape(bsz, seq, d)
```

```python
import functools
import math

import numpy as np
import jax
import jax.numpy as jnp
from jax import lax
from jax.experimental import pallas as pl
from jax.experimental.pallas import tpu as pltpu
from jax.experimental.pallas import tpu_sc as plsc

F32 = jnp.float32
BF16 = jnp.bfloat16
U32 = jnp.uint32
I32 = jnp.int32
HIGHEST = lax.Precision.HIGHEST

GRID_W = 64
POS_BASE = 10000.0
NORM_EPS = 1e-6
N_ADA = 6
HY_ORDER = 2
HY_SHORT = 3
HY_DECAY_TARGET = 1e-2
HY_FAST_PCT = 0.3
HY_SLOW_PCT = 1.5
HG_HEADS = 4
HG_CHUNK = 64
N_GROUPS = 8
TOPK_GROUPS = 4
TOP_K = 8
ROUTED_SCALE = 2.5

LANES = 128
SUBLANES = 8
VMEM_LIMIT = 56 * 1024 * 1024

TOKEN_TILE = 512
HG_TIME_BLOCK = 512
HALO_ROWS = 16
CONV_TILE = 2048
DFT_P = 128
FILTER_GROUP = 8
DFT_GROUP = 16
MOE_ROWS = 256
MOE_STEP_BLOCKS = 4
WEIGHT_AHEAD = 3
WEIGHT_SLOTS = WEIGHT_AHEAD + MOE_STEP_BLOCKS
SC_ROWS = 128
GATHER_SPLIT = 4


def _cparams(*sem):
    return pltpu.CompilerParams(dimension_semantics=sem, vmem_limit_bytes=VMEM_LIMIT)


def _dot(a, b):
    return jnp.dot(a, b, preferred_element_type=F32)


def _dot_hi(a, b):
    return jnp.dot(a, b, preferred_element_type=F32, precision=HIGHEST)


def _dot_nt(a, b):
    return lax.dot_general(a, b, (((1,), (1,)), ((), ())), preferred_element_type=F32)


def _dot_tn(a, b):
    return lax.dot_general(a, b, (((0,), (0,)), ((), ())), preferred_element_type=F32)


def _silu(x):
    return x * jax.nn.sigmoid(x)


def _split_bf16(x):
    hi = x.astype(BF16)
    return hi, (x - hi.astype(F32)).astype(BF16)


def _ada_kernel(c_ref, w_ref, b_ref, o_ref):
    o_ref[...] = _dot_hi(_silu(c_ref[...]), w_ref[...]) + b_ref[...]


def ada_vectors(c_rows, ada_w, ada_b):
    r, d = c_rows.shape
    n = ada_w.shape[1]
    bn = 1024
    return pl.pallas_call(
        _ada_kernel,
        out_shape=jax.ShapeDtypeStruct((r, n), F32),
        grid=(n // bn,),
        in_specs=[pl.BlockSpec((r, d), lambda j: (0, 0)),
                  pl.BlockSpec((d, bn), lambda j: (0, j)),
                  pl.BlockSpec((1, bn), lambda j: (0, j))],
        out_specs=pl.BlockSpec((r, bn), lambda j: (0, j)),
        compiler_params=_cparams("arbitrary"),
        name="ada_vectors",
    )(c_rows, ada_w, ada_b.reshape(1, n))


def _inproj_kernel(x_ref, er_ref, ec_ref, g_ref, sh_ref, sc_ref, w_ref, o_ref, *, col_chunk):
    x = x_ref[...]
    rows, gw, d = x.shape
    half = d // 2
    xp = jnp.concatenate([x[:, :, :half] + er_ref[...], x[:, :, half:] + ec_ref[...]], axis=-1)
    xp = xp.reshape(rows * gw, d)
    ms = jnp.mean(xp * xp, axis=-1, keepdims=True)
    y = xp * lax.rsqrt(ms + NORM_EPS) * g_ref[...]
    h = (y * (1.0 + sc_ref[...]) + sh_ref[...]).astype(BF16)
    n = o_ref.shape[1]
    for j in range(n // col_chunk):
        sl = slice(j * col_chunk, (j + 1) * col_chunk)
        o_ref[:, sl] = _dot(h, w_ref[:, sl]).astype(o_ref.dtype)


def in_projection(x, emb_r, emb_c, norm_g, shift, scale, w_bf16, tm):
    b, s, d = x.shape
    n = w_bf16.shape[1]
    rows_per_batch = s // GRID_W
    rt = tm // GRID_W
    tiles_per_batch = rows_per_batch // rt
    x3 = x.reshape(b * rows_per_batch, GRID_W, d)
    col_chunk = 512
    return pl.pallas_call(
        functools.partial(_inproj_kernel, col_chunk=col_chunk),
        out_shape=jax.ShapeDtypeStruct((b * s, n), BF16),
        grid=(b * tiles_per_batch,),
        in_specs=[pl.BlockSpec((rt, GRID_W, d), lambda i: (i, 0, 0)),
                  pl.BlockSpec((rt, 1, d // 2), lambda i: (i % tiles_per_batch, 0, 0)),
                  pl.BlockSpec((GRID_W, d // 2), lambda i: (0, 0)),
                  pl.BlockSpec((1, d), lambda i: (0, 0)),
                  pl.BlockSpec((None, 1, d), lambda i: (i // tiles_per_batch, 0, 0)),
                  pl.BlockSpec((None, 1, d), lambda i: (i // tiles_per_batch, 0, 0)),
                  pl.BlockSpec((d, n), lambda i: (0, 0))],
        out_specs=pl.BlockSpec((tm, n), lambda i: (i, 0)),
        compiler_params=_cparams("arbitrary"),
        name="in_projection",
    )(x3, emb_r, emb_c, norm_g.reshape(1, d), shift, scale, w_bf16)


def _hgrn_kernel(*refs, reverse, n_chunks, final):
    if final:
        (q_ref, i_ref, f_ref, lb_ref, s0_ref, of_ref, gate_ref, ng_ref, o_ref, sfin_ref, s_scr) = refs
    else:
        (q_ref, i_ref, f_ref, lb_ref, s0_ref, o_ref, sfin_ref, s_scr) = refs
    cs = HG_CHUNK
    bsz, n_heads, _, dk = s_scr.shape

    @pl.when(pl.program_id(0) == 0)
    def _():
        s_scr[...] = s0_ref[...]

    row = lax.broadcasted_iota(I32, (cs, cs), 0)
    col = lax.broadcasted_iota(I32, (cs, cs), 1)
    tri = (col >= row) if reverse else (col <= row)
    tri_b = tri.astype(BF16)
    end_row = 0 if reverse else cs - 1
    mid_row = cs // 2 if reverse else cs // 2 - 1

    def chunk_body(bi, ci):
        c = (n_chunks - 1 - ci) if reverse else ci
        rows = slice(c * cs, (c + 1) * cs)
        lb = lb_ref[...]
        f = lb + (1.0 - lb) * jax.nn.sigmoid(f_ref[bi, rows, :].astype(F32))
        lf_hi, lf_lo = _split_bf16(jnp.log(f))
        b_all = _dot(tri_b, lf_hi) + _dot(tri_b, lf_lo)
        k_all = 1.0 - f
        q_all = _silu(q_ref[bi, rows, :].astype(F32))
        for h in range(n_heads):
            sl = slice(h * dk, (h + 1) * dk)
            b = b_all[:, sl]
            q = q_all[:, sl]
            k = k_all[:, sl]
            v = i_ref[bi, rows, sl]
            b_end = b[end_row:end_row + 1]
            b_mid = b[mid_row:mid_row + 1]
            qd = (q * jnp.exp(b - b_mid)).astype(BF16)
            kd = (k * jnp.exp(b_mid - b)).astype(BF16)
            att = jnp.where(tri, _dot_nt(qd, kd), 0.0).astype(BF16)
            st = s_scr[bi, h]
            qe = (q * jnp.exp(b)).astype(BF16)
            o = _dot(jnp.concatenate([qe, att], axis=1), jnp.concatenate([st.astype(BF16), v], axis=0))
            ke = (k * jnp.exp(b_end - b)).astype(BF16)
            dec = jnp.transpose(jnp.broadcast_to(jnp.exp(b_end), (SUBLANES, dk)))[:, 0:1]
            s_scr[bi, h] = st * dec + _dot_tn(ke, v)
            if final:
                o = o + of_ref[bi, rows, sl].astype(F32)
                o = o * lax.rsqrt(jnp.mean(o * o, axis=-1, keepdims=True) + NORM_EPS) * ng_ref[...]
                o = o * _silu(gate_ref[bi, rows, sl].astype(F32))
            o_ref[bi, rows, sl] = o.astype(o_ref.dtype)

    for ci in range(n_chunks):
        for bi in range(bsz):
            chunk_body(bi, ci)
    sfin_ref[...] = s_scr[...]


def hgrn_scan(p, cols, lb, s0, seq, tb, *, reverse, o_fwd=None, gate_col=None, norm_g=None):
    bsz, n_heads, dv, dk = s0.shape
    width = n_heads * dk
    nt = seq // tb
    final = o_fwd is not None
    p3 = p.reshape(bsz, seq, p.shape[1])
    tmap = (lambda t: nt - 1 - t) if reverse else (lambda t: t)
    colspec = lambda cb: pl.BlockSpec((bsz, tb, width), lambda t: (0, tmap(t), cb))
    state = pl.BlockSpec((bsz, n_heads, dv, dk), lambda t: (0, 0, 0, 0))
    in_specs = [colspec(cols[0]), colspec(cols[1]), colspec(cols[2]),
                pl.BlockSpec((1, width), lambda t: (0, 0)), state]
    args = [p3, p3, p3, lb.reshape(1, width), s0]
    if final:
        in_specs += [colspec(0), colspec(gate_col), pl.BlockSpec((1, dk), lambda t: (0, 0))]
        args += [o_fwd.reshape(bsz, seq, width), p3, norm_g.reshape(1, dk)]
    o, s_fin = pl.pallas_call(
        functools.partial(_hgrn_kernel, reverse=reverse, n_chunks=tb // HG_CHUNK, final=final),
        out_shape=(jax.ShapeDtypeStruct((bsz, seq, width), BF16),
                   jax.ShapeDtypeStruct((bsz, n_heads, dv, dk), F32)),
        grid=(nt,),
        in_specs=in_specs,
        out_specs=(colspec(0), state),
        scratch_shapes=[pltpu.VMEM((bsz, n_heads, dv, dk), F32)],
        compiler_params=_cparams("arbitrary"),
        name="hgrn_bwd" if reverse else "hgrn_fwd",
    )(*args)
    return o.reshape(bsz * seq, width), s_fin


def _shortconv_kernel(p_ref, prev_ref, next_ref, w_ref, b_ref, o_ref, *, tiles_per_batch):
    i = pl.program_id(0)
    ti = i % tiles_per_batch
    p = p_ref[...].astype(F32)
    tm, cw = p.shape
    ta = tm // DFT_P
    prev_row = jnp.where(ti == 0, 0.0, prev_ref[HALO_ROWS - 1:HALO_ROWS, :].astype(F32))
    next_row = jnp.where(ti == tiles_per_batch - 1, 0.0, next_ref[0:1, :].astype(F32))
    p3 = jnp.swapaxes(p.reshape(ta, DFT_P, cw), 0, 1)
    arow = lax.broadcasted_iota(I32, (ta, 1), 0)
    prev_edge = jnp.where(arow == 0, prev_row, pltpu.roll(p3[DFT_P - 1], 1, axis=0))
    next_edge = jnp.where(arow == ta - 1, next_row, pltpu.roll(p3[0], ta - 1, axis=0))
    p_prev = jnp.concatenate([prev_edge[None], p3[:-1]], axis=0)
    p_next = jnp.concatenate([p3[1:], next_edge[None]], axis=0)
    u = w_ref[0:1, :] * p_prev + w_ref[1:2, :] * p3 + w_ref[2:3, :] * p_next + b_ref[...]
    o_ref[...] = u.astype(o_ref.dtype)


def short_conv(p, width, conv_w, conv_b, seq, tm, cw):
    t = p.shape[0]
    nt = t // tm
    tiles_per_batch = seq // tm
    sub = tm // HALO_ROWS
    ta = tm // DFT_P
    return pl.pallas_call(
        functools.partial(_shortconv_kernel, tiles_per_batch=tiles_per_batch),
        out_shape=jax.ShapeDtypeStruct((DFT_P, t // DFT_P, width), BF16),
        grid=(nt, width // cw),
        in_specs=[pl.BlockSpec((tm, cw), lambda i, j: (i, j)),
                  pl.BlockSpec((HALO_ROWS, cw), lambda i, j: (jnp.maximum(i * sub - 1, 0), j)),
                  pl.BlockSpec((HALO_ROWS, cw), lambda i, j: (jnp.minimum((i + 1) * sub, t // HALO_ROWS - 1), j)),
                  pl.BlockSpec((HY_SHORT, cw), lambda i, j: (0, j)),
                  pl.BlockSpec((1, cw), lambda i, j: (0, j))],
        out_specs=pl.BlockSpec((DFT_P, ta, cw), lambda i, j: (0, i, j)),
        compiler_params=_cparams("arbitrary", "arbitrary"),
        name="short_conv",
    )(p, p, p, conv_w, conv_b.reshape(1, width))


def _filter_kernel(band_ref, w1t_ref, w1c_ref, w1s_ref, b1_ref, w2_ref, b2_ref, w3_ref, b3_ref,
                   w4f_ref, w4b_ref, fr_ref, delta_ref, k_ref, s_ref, *, seq):
    step = pl.program_id(0)
    gb, q, ncol = k_ref.shape
    half = q // 2
    width = delta_ref.shape[1]
    nrow = gb * q
    nf = gb * half

    def positions(shape, axis):
        r = lax.broadcasted_iota(I32, shape, axis)
        is_bwd = r >= nf
        rr = jnp.where(is_bwd, r - nf, r)
        j = lax.shift_right_logical(rr, int(math.log2(half)))
        a = (rr & (half - 1)) + jnp.where(is_bwd, half, 0)
        n = (a * DFT_P + step * gb + j).astype(F32)
        t = jnp.where(is_bwd, 2.0 * seq - n, n)
        return n, t, t / float(max(seq - 1, 1))

    _, t_l, tn_l = positions((1, nrow), 1)
    ang = (2.0 * math.pi / seq) * t_l * band_ref[...]
    fr = fr_ref[...]
    pre = (w1t_ref[...] * tn_l + _dot_hi(w1c_ref[...], jnp.cos(ang)) - _dot_hi(w1s_ref[...], jnp.sin(ang))
           + b1_ref[...])
    act = jnp.sin(fr * pre)
    act = jnp.sin(fr * (_dot_hi(w2_ref[...], act) + b2_ref[...]))
    act = jnp.sin(fr * (_dot_hi(w3_ref[...], act) + b3_ref[...])).astype(BF16)
    n_s, _, tn_s = positions((nrow, 1), 0)
    delta = jnp.concatenate([delta_ref[...]] * (ncol // width), axis=1)
    hf = _dot_tn(act[:, :nf], w4f_ref[...]) * jnp.exp(-tn_s[:nf] * delta)
    hb = _dot_tn(act[:, nf:], w4b_ref[...]) * jnp.exp(-tn_s[nf:] * delta)
    hb = jnp.where(n_s[nf:] == float(seq), 0.0, hb)
    k_ref[:, :half, :] = hf.reshape(gb, half, ncol).astype(k_ref.dtype)
    k_ref[:, half:, :] = hb.reshape(gb, half, ncol).astype(k_ref.dtype)
    tot = jnp.sum(jnp.abs(hf), axis=0, keepdims=True) + jnp.sum(jnp.abs(hb), axis=0, keepdims=True)

    @pl.when(step == 0)
    def _():
        s_ref[...] = jnp.zeros_like(s_ref)

    s_ref[...] += tot


def hyena_filter_taps(seq, w1, b1, w2, b2, w3, b3, w4, freq, width):
    emb = w1.shape[0]
    hid = w1.shape[1]
    bands = (emb - 1) // 2
    q = 2 * seq // DFT_P
    ncol = HY_ORDER * width
    band = np.linspace(1e-4, bands - 1, bands, dtype=np.float32).reshape(bands, 1)
    min_decay = math.log(HY_DECAY_TARGET) / HY_SLOW_PCT
    max_decay = math.log(HY_DECAY_TARGET) / HY_FAST_PCT
    delta = np.abs(np.linspace(min_decay, max_decay, width, dtype=np.float32)).reshape(1, width)
    w1t = w1.astype(F32).T
    col = lambda v: v.reshape(hid, 1).astype(F32)
    w4r = w4.astype(BF16).reshape(hid, HY_ORDER, 2, width)
    w4f = w4r[:, :, 0, :].reshape(hid, ncol)
    w4b = w4r[:, :, 1, :].reshape(hid, ncol)
    gb = FILTER_GROUP
    const = lambda shape: pl.BlockSpec(shape, lambda i: tuple(0 for _ in shape))
    return pl.pallas_call(
        functools.partial(_filter_kernel, seq=seq),
        out_shape=(jax.ShapeDtypeStruct((DFT_P, q, ncol), BF16),
                   jax.ShapeDtypeStruct((1, ncol), F32)),
        grid=(DFT_P // gb,),
        in_specs=[const((bands, 1)), const((hid, 1)), const((hid, bands)), const((hid, bands)), const((hid, 1)),
                  const((hid, hid)), const((hid, 1)), const((hid, hid)), const((hid, 1)),
                  const((hid, ncol)), const((hid, ncol)), const((hid, 1)), const((1, width))],
        out_specs=(pl.BlockSpec((gb, q, ncol), lambda i: (i, 0, 0)),
                   pl.BlockSpec((1, ncol), lambda i: (0, 0))),
        compiler_params=_cparams("arbitrary"),
        name="hyena_filter",
    )(jnp.asarray(band), w1t[:, 0:1], w1t[:, 1:1 + bands], w1t[:, 1 + bands:1 + 2 * bands], col(b1),
      w2.astype(F32).T, col(b2), w3.astype(F32).T, col(b3), w4f, w4b, col(freq), jnp.asarray(delta))


def _dft_tables(seq):
    p = DFT_P
    n_fft = 2 * seq
    q = n_fft // p
    qh = q // 2
    ka = np.arange(q)
    nn = np.arange(q)[None, :] * p + np.arange(p)[:, None]
    ang = ((ka[None, :, None] * nn[:, None, :]) % n_fft) * (2.0 * np.pi / n_fft)
    mr, mi = np.cos(ang), -np.sin(ang)
    m1c = np.concatenate([np.concatenate([mr[:, :, :qh], -mi[:, :, :qh]], axis=2),
                          np.concatenate([mi[:, :, :qh], mr[:, :, :qh]], axis=2)], axis=1)
    m1r = np.concatenate([mr, mi], axis=1)
    gr = np.swapaxes(mr[:, :, :qh], 1, 2) / n_fft
    gi = -np.swapaxes(mi[:, :, :qh], 1, 2) / n_fft
    m4 = np.concatenate([np.concatenate([gr, -gi], axis=2), np.concatenate([gi, gr], axis=2)], axis=1)
    kb = np.arange(p)
    ang2 = 2.0 * np.pi * ((kb[:, None] * kb[None, :]) % p) / p
    fr, fi = np.cos(ang2), -np.sin(ang2)
    m2 = np.block([[fr, -fi], [fi, fr]])
    m3 = np.block([[fr, fi], [-fi, fr]])
    return tuple(jnp.asarray(m.astype(np.float32).astype(BF16)) for m in (m1c, m1r, m2, m3, m4))


def _store_swapped(o_ref, res):
    g2 = o_ref.shape[0]
    stacked = jnp.stack(res, axis=0)
    for ri in range(2):
        o_ref[:, ri, :, :] = jnp.swapaxes(stacked[:, ri * g2:(ri + 1) * g2, :], 0, 1).astype(o_ref.dtype)


def _swapped_out(g, m, ncols, gb, n_col_blocks=1):
    shape = (m // 2, 2, g, ncols * n_col_blocks)
    if n_col_blocks == 1:
        return shape, pl.BlockSpec((m // 2, 2, gb, ncols), lambda i: (0, 0, i, 0))
    return shape, pl.BlockSpec((m // 2, 2, gb, ncols), lambda i, j: (0, 0, i, j))


def _bmm_kernel(w_ref, x_ref, o_ref, *, shared_w):
    res = []
    for j in range(x_ref.shape[0]):
        w = w_ref[...] if shared_w else w_ref[j]
        res.append(_dot(w, x_ref[j]))
    _store_swapped(o_ref, res)


def batched_left_matmul(w, x, col_block, ncols, name, gb, n_col_blocks=1):
    g, k = x.shape[0], x.shape[1]
    shared = w.ndim == 2
    m = w.shape[-2]
    out_shape, out_spec = _swapped_out(g, m, ncols, gb, n_col_blocks)
    if n_col_blocks == 1:
        grid = (g // gb,)
        wspec = (pl.BlockSpec((m, k), lambda i: (0, 0)) if shared else pl.BlockSpec((gb, m, k), lambda i: (i, 0, 0)))
        xspec = pl.BlockSpec((gb, k, ncols), lambda i: (i, 0, col_block))
        sem = ("arbitrary",)
    else:
        grid = (g // gb, n_col_blocks)
        wspec = (pl.BlockSpec((m, k), lambda i, j: (0, 0)) if shared
                 else pl.BlockSpec((gb, m, k), lambda i, j: (i, 0, 0)))
        xspec = pl.BlockSpec((gb, k, ncols), lambda i, j: (i, 0, j))
        sem = ("arbitrary", "arbitrary")
    out = pl.pallas_call(
        functools.partial(_bmm_kernel, shared_w=shared),
        out_shape=jax.ShapeDtypeStruct(out_shape, BF16),
        grid=grid,
        in_specs=[wspec, xspec],
        out_specs=out_spec,
        compiler_params=_cparams(*sem),
        name=name,
    )(w, x)
    return out.reshape(m // 2, 2 * g, out_shape[3])


def _dft_mid_kernel(m2_ref, m3_ref, x_ref, k_ref, o_ref):
    half = x_ref.shape[1] // 2
    res = []
    for j in range(x_ref.shape[0]):
        xf = _dot(m2_ref[...], x_ref[j])
        kf = _dot(m2_ref[...], k_ref[j])
        xr, xi = xf[:half], xf[half:]
        kr, ki = kf[:half], kf[half:]
        z = jnp.concatenate([xr * kr - xi * ki, xr * ki + xi * kr], axis=0).astype(BF16)
        res.append(_dot(m3_ref[...], z))
    _store_swapped(o_ref, res)


def dft_mid(m2, m3, x, kspec, kcol, ncols):
    g, r = x.shape[0], x.shape[1]
    gb = DFT_GROUP
    out_shape, out_spec = _swapped_out(g, r, ncols, gb)
    out = pl.pallas_call(
        _dft_mid_kernel,
        out_shape=jax.ShapeDtypeStruct(out_shape, BF16),
        grid=(g // gb,),
        in_specs=[pl.BlockSpec((r, r), lambda i: (0, 0)),
                  pl.BlockSpec((r, r), lambda i: (0, 0)),
                  pl.BlockSpec((gb, r, ncols), lambda i: (i, 0, 0)),
                  pl.BlockSpec((gb, r, ncols), lambda i: (i, 0, kcol))],
        out_specs=out_spec,
        compiler_params=_cparams("arbitrary"),
        name="dft_mid",
    )(m2, m3, x, kspec)
    return out.reshape(r // 2, 2 * g, ncols)


def _dft_out_kernel(m4_ref, y_ref, inv_ref, skip_ref, v_ref, mul_ref, *rest, last):
    res = []
    for j in range(y_ref.shape[0]):
        conv = _dot(m4_ref[j], y_ref[j]) * inv_ref[...] + v_ref[j].astype(F32) * skip_ref[...]
        res.append(mul_ref[j].astype(F32) * conv)
    if last:
        (o_ref,) = rest
        o_ref[...] = jnp.swapaxes(jnp.stack(res, axis=0), 0, 1).astype(o_ref.dtype)
    else:
        m1_ref, o_ref, s1_ref = rest
        nxt = []
        for j, r in enumerate(res):
            zb = r.astype(o_ref.dtype)
            o_ref[j] = zb
            nxt.append(_dot(m1_ref[j], zb))
        _store_swapped(s1_ref, nxt)


def dft_out(m4, y, inv_l1, skip, u, v_col, mul, mul_col, ncols, m1c=None):
    g, r = y.shape[0], y.shape[1]
    rows = m4.shape[1]
    gb = DFT_GROUP
    in_specs = [pl.BlockSpec((gb, rows, r), lambda i: (i, 0, 0)),
                pl.BlockSpec((gb, r, ncols), lambda i: (i, 0, 0)),
                pl.BlockSpec((1, ncols), lambda i: (0, 0)),
                pl.BlockSpec((1, ncols), lambda i: (0, 0)),
                pl.BlockSpec((gb, rows, ncols), lambda i: (i, 0, v_col)),
                pl.BlockSpec((gb, rows, ncols), lambda i: (i, 0, mul_col))]
    args = [m4, y, inv_l1, skip, u, mul]
    if m1c is None:
        out_shape = jax.ShapeDtypeStruct((rows, g, ncols), BF16)
        out_specs = pl.BlockSpec((rows, gb, ncols), lambda i: (0, i, 0))
    else:
        m = m1c.shape[1]
        s_shape, s_spec = _swapped_out(g, m, ncols, gb)
        in_specs.append(pl.BlockSpec((gb, m, rows), lambda i: (i, 0, 0)))
        args.append(m1c)
        out_shape = (jax.ShapeDtypeStruct((g, rows, ncols), BF16), jax.ShapeDtypeStruct(s_shape, BF16))
        out_specs = (pl.BlockSpec((gb, rows, ncols), lambda i: (i, 0, 0)), s_spec)
    out = pl.pallas_call(
        functools.partial(_dft_out_kernel, last=m1c is None),
        out_shape=out_shape,
        grid=(g // gb,),
        in_specs=in_specs,
        out_specs=out_specs,
        compiler_params=_cparams("arbitrary"),
        name="dft_out",
    )(*args)
    if m1c is None:
        return out
    z, s1 = out
    return z, s1.reshape(m1c.shape[1] // 2, 2 * g, ncols)


def hyena_branch(u, bsz, seq, width, taps, l1, skip):
    m1c, m1r, m2, m3, m4 = _dft_tables(seq)
    ks1 = batched_left_matmul(m1r, taps, 0, width, "dft_k1", DFT_GROUP, n_col_blocks=HY_ORDER)
    inv_l1 = 1.0 / l1
    src = u
    s1 = batched_left_matmul(m1c, u, 0, width, "dft_s1", DFT_GROUP)
    for order in range(HY_ORDER):
        mid = dft_mid(m2, m3, s1, ks1, order, width)
        last = order == HY_ORDER - 1
        out = dft_out(m4, mid, inv_l1[:, order * width:(order + 1) * width],
                      skip[order].reshape(1, width).astype(F32), src, 0, u, order + 1, width,
                      m1c=None if last else m1c)
        if not last:
            src, s1 = out
    return out.reshape(bsz * seq, width)


def _pack_pairs(x):
    w = x.shape[1] // 2
    u = lax.bitcast_convert_type(x, U32)
    r = (u + U32(0x7FFF) + ((u >> 16) & U32(1))) >> 16
    return r[:, :w] | (r[:, w:] << 16)


def _unpack_pairs(p):
    lo = lax.bitcast_convert_type(p << 16, F32)
    hi = lax.bitcast_convert_type(p & U32(0xFFFF0000), F32)
    return jnp.concatenate([lo, hi], axis=1)


def _pack_pairs_native(x):
    w = x.shape[1] // 2
    return lax.bitcast_convert_type(pltpu.pack_elementwise([x[:, :w], x[:, w:]], packed_dtype=BF16), U32)


def _unpack_pairs_native(p):
    pi = lax.bitcast_convert_type(p, I32)
    halves = [pltpu.unpack_elementwise(pi, index=i, packed_dtype=BF16, unpacked_dtype=F32) for i in range(2)]
    return jnp.concatenate(halves, axis=1)


def _merge_kernel(x_ref, er_ref, ec_ref, yhy_ref, yhg_ref, ghy_ref, ghg_ref, why_ref, whg_ref, wo_ref,
                  g1_ref, n2_ref, sh2_ref, sc2_ref, g2_ref, rwh_ref, rwl_ref, sgu_ref, sd_ref,
                  xres_ref, h2p_ref, lg_ref):
    x = x_ref[...]
    rows, gw, d = x.shape
    half = d // 2
    xp = jnp.concatenate([x[:, :, :half] + er_ref[...], x[:, :, half:] + ec_ref[...]], axis=-1)
    xp = xp.reshape(rows * gw, d)
    m = (jax.nn.sigmoid(ghy_ref[...].astype(F32)) * _dot(yhy_ref[...], why_ref[...])
         + jax.nn.sigmoid(ghg_ref[...].astype(F32)) * _dot(yhg_ref[...], whg_ref[...]))
    x1 = xp + g1_ref[...] * _dot(m.astype(BF16), wo_ref[...])
    ms = jnp.mean(x1 * x1, axis=-1, keepdims=True)
    h2 = x1 * lax.rsqrt(ms + NORM_EPS) * n2_ref[...] * (1.0 + sc2_ref[...]) + sh2_ref[...]
    h_hi, h_lo = _split_bf16(h2)
    lg_ref[...] = _dot_nt(rwh_ref[...], h_hi) + (_dot_nt(rwl_ref[...], h_hi) + _dot_nt(rwh_ref[...], h_lo))
    gu = _dot(h_hi, sgu_ref[...])
    fs = gu.shape[1] // 2
    shared = _dot((_silu(gu[:, :fs]) * gu[:, fs:]).astype(BF16), sd_ref[...])
    xres_ref[...] = x1 + g2_ref[...] * shared
    h2p_ref[...] = _pack_pairs(h2)


def merge_stage(x, emb_r, emb_c, y_hy, y_hg, p, gate_cols, w_hy_out, w_hg_out, w_out, g1, norm2_g,
                sh2, sc2, g2, router_wt, sh_gate_up, sh_down, tm):
    b, s, d = x.shape
    rows_per_batch = s // GRID_W
    rt = tm // GRID_W
    tiles_per_batch = rows_per_batch // rt
    x3 = x.reshape(b * rows_per_batch, GRID_W, d)
    wb = y_hy.shape[1]
    ne = router_wt.shape[0]
    fs2 = sh_gate_up.shape[1]
    rw_hi, rw_lo = _split_bf16(router_wt)
    tok = lambda cb, w: pl.BlockSpec((tm, w), lambda i: (i, cb))
    const = lambda shape: pl.BlockSpec(shape, lambda i: tuple(0 for _ in shape))
    per_b = pl.BlockSpec((None, 1, d), lambda i: (i // tiles_per_batch, 0, 0))
    return pl.pallas_call(
        _merge_kernel,
        out_shape=(jax.ShapeDtypeStruct((b * s, d), F32),
                   jax.ShapeDtypeStruct((b * s, d // 2), U32),
                   jax.ShapeDtypeStruct((ne, b * s), F32)),
        grid=(b * tiles_per_batch,),
        in_specs=[pl.BlockSpec((rt, GRID_W, d), lambda i: (i, 0, 0)),
                  pl.BlockSpec((rt, 1, d // 2), lambda i: (i % tiles_per_batch, 0, 0)),
                  const((GRID_W, d // 2)),
                  tok(0, wb), tok(0, wb), tok(gate_cols[0], d), tok(gate_cols[1], d),
                  const((wb, d)), const((wb, d)), const((d, d)),
                  per_b, const((1, d)), per_b, per_b, per_b,
                  const((ne, d)), const((ne, d)), const((d, fs2)), const((fs2 // 2, d))],
        out_specs=(pl.BlockSpec((tm, d), lambda i: (i, 0)),
                   pl.BlockSpec((tm, d // 2), lambda i: (i, 0)),
                   pl.BlockSpec((ne, tm), lambda i: (0, i))),
        compiler_params=_cparams("arbitrary"),
        name="merge",
    )(x3, emb_r, emb_c, y_hy, y_hg, p, p, w_hy_out, w_hg_out, w_out, g1, norm2_g.reshape(1, d),
      sh2, sc2, g2, rw_hi, rw_lo, sh_gate_up, sh_down)


def _route_kernel(lg_ref, bias_ref, eidx_ref, wsel_ref, rank_ref, cnt_ref, carry):
    ne, tr = lg_ref.shape
    gsz = ne // N_GROUPS
    neg = -jnp.inf

    @pl.when(pl.program_id(0) == 0)
    def _():
        carry[...] = jnp.zeros_like(carry)

    scores = jax.nn.sigmoid(lg_ref[...])
    biased = scores + bias_ref[...]
    riota = lax.broadcasted_iota(I32, (gsz, tr), 0).astype(F32)
    gs = []
    for g in range(N_GROUPS):
        vg = biased[g * gsz:(g + 1) * gsz]
        m1 = jnp.max(vg, axis=0, keepdims=True)
        i1 = jnp.min(jnp.where(vg == m1, riota, float(gsz)), axis=0, keepdims=True)
        m2 = jnp.max(jnp.where(riota == i1, neg, vg), axis=0, keepdims=True)
        gs.append(m1 + m2)
    cur = jnp.concatenate(gs, axis=0)
    giota = lax.broadcasted_iota(I32, (N_GROUPS, tr), 0).astype(F32)
    gsel = jnp.zeros((N_GROUPS, tr), F32)
    for _ in range(TOPK_GROUPS):
        m = jnp.max(cur, axis=0, keepdims=True)
        idx = jnp.min(jnp.where(cur == m, giota, float(N_GROUPS)), axis=0, keepdims=True)
        hit = giota == idx
        gsel = jnp.where(hit, 1.0, gsel)
        cur = jnp.where(hit, neg, cur)
    cur = jnp.concatenate([jnp.where(gsel[g:g + 1] > 0.0, biased[g * gsz:(g + 1) * gsz], neg)
                           for g in range(N_GROUPS)], axis=0)
    eiota = lax.broadcasted_iota(I32, (ne, tr), 0).astype(F32)
    chosen = jnp.zeros((ne, tr), F32)
    idxs, ws = [], []
    for _ in range(TOP_K):
        m = jnp.max(cur, axis=0, keepdims=True)
        idx = jnp.min(jnp.where(cur == m, eiota, float(ne)), axis=0, keepdims=True)
        hit = eiota == idx
        idxs.append(idx)
        ws.append(jnp.sum(jnp.where(hit, scores, 0.0), axis=0, keepdims=True))
        chosen = jnp.where(hit, 1.0, chosen)
        cur = jnp.where(hit, neg, cur)
    w = jnp.concatenate(ws, axis=0)
    wsel_ref[...] = w / jnp.sum(w, axis=0, keepdims=True) * ROUTED_SCALE
    eidx_ref[...] = jnp.concatenate(idxs, axis=0).astype(I32)
    srow = lax.broadcasted_iota(I32, (tr, tr), 0)
    scol = lax.broadcasted_iota(I32, (tr, tr), 1)
    before = (srow < scol).astype(BF16)
    base = carry[...] + _dot(chosen.astype(BF16), before)
    ranks = [jnp.sum(jnp.where(eiota == idx, base, 0.0), axis=0, keepdims=True) for idx in idxs]
    rank_ref[...] = jnp.concatenate(ranks, axis=0).astype(I32)
    carry[...] += jnp.sum(chosen, axis=1, keepdims=True)
    cnt_ref[...] = carry[...]


def route(logits_t, router_bias, tr):
    ne, t = logits_t.shape
    return pl.pallas_call(
        _route_kernel,
        out_shape=(jax.ShapeDtypeStruct((TOP_K, t), I32),
                   jax.ShapeDtypeStruct((TOP_K, t), F32),
                   jax.ShapeDtypeStruct((TOP_K, t), I32),
                   jax.ShapeDtypeStruct((ne, 1), F32)),
        grid=(t // tr,),
        in_specs=[pl.BlockSpec((ne, tr), lambda i: (0, i)),
                  pl.BlockSpec((ne, 1), lambda i: (0, 0))],
        out_specs=(pl.BlockSpec((TOP_K, tr), lambda i: (0, i)),
                   pl.BlockSpec((TOP_K, tr), lambda i: (0, i)),
                   pl.BlockSpec((TOP_K, tr), lambda i: (0, i)),
                   pl.BlockSpec((ne, 1), lambda i: (0, 0))),
        scratch_shapes=[pltpu.VMEM((ne, 1), F32)],
        compiler_params=_cparams("arbitrary"),
        name="route",
    )(logits_t, router_bias.reshape(ne, 1).astype(F32))


def _dest_kernel(cnt_ref, eidx_ref, rank_ref, dest_ref, be_ref, nv_ref, nb_ref, start_scr):
    ne = cnt_ref.shape[0]
    tr = eidx_ref.shape[1]

    @pl.when(pl.program_id(0) == 0)
    def _():
        cnt = jnp.broadcast_to(cnt_ref[...], (ne, LANES))
        padded = jnp.floor((cnt + float(MOE_ROWS - 1)) / float(MOE_ROWS)) * float(MOE_ROWS)
        r = lax.broadcasted_iota(I32, (ne, ne), 0)
        c = lax.broadcasted_iota(I32, (ne, ne), 1)
        start = _dot_hi((c < r).astype(F32), padded)
        start_scr[...] = start
        end = start[:, 0:1] + padded[:, 0:1]
        used = start[:, 0:1] + cnt[:, 0:1]
        nbl = be_ref.shape[1]
        blk_row = (lax.broadcasted_iota(I32, (1, nbl), 1) * MOE_ROWS).astype(F32)
        total = jnp.max(end, axis=0, keepdims=True)
        last_row = total - float(MOE_ROWS)
        blk_row_c = jnp.minimum(blk_row, last_row)
        e_of = jnp.sum((end <= blk_row_c).astype(F32), axis=0, keepdims=True)
        e_of = jnp.minimum(e_of, float(ne - 1))
        eio = lax.broadcasted_iota(I32, (ne, nbl), 0).astype(F32)
        used_e = jnp.sum(jnp.where(eio == e_of, used, 0.0), axis=0, keepdims=True)
        valid = jnp.clip(used_e - blk_row_c, 0.0, float(MOE_ROWS))
        be_ref[...] = e_of.astype(I32)
        nv_ref[...] = jnp.where(blk_row <= last_row, valid, 0.0).astype(I32)
        nb_ref[...] = jnp.broadcast_to(total / float(MOE_ROWS), nb_ref.shape).astype(I32)

    eiota = lax.broadcasted_iota(I32, (ne, tr), 0)
    start_col = start_scr[:, 0:1]
    rows = []
    for k in range(TOP_K):
        hit = eiota == eidx_ref[k:k + 1, :]
        rows.append(jnp.sum(jnp.where(hit, start_col, 0.0), axis=0, keepdims=True))
    dest_ref[...] = jnp.concatenate(rows, axis=0).astype(I32) + rank_ref[...]


def dispatch_plan(counts, eidx, rank, tr, n_blocks):
    ne = counts.shape[0]
    t = eidx.shape[1]
    nbl = pl.cdiv(n_blocks, LANES) * LANES
    return pl.pallas_call(
        _dest_kernel,
        out_shape=(jax.ShapeDtypeStruct((TOP_K, t), I32),
                   jax.ShapeDtypeStruct((1, nbl), I32),
                   jax.ShapeDtypeStruct((1, nbl), I32),
                   jax.ShapeDtypeStruct((1, LANES), I32)),
        grid=(t // tr,),
        in_specs=[pl.BlockSpec((ne, 1), lambda i: (0, 0)),
                  pl.BlockSpec((TOP_K, tr), lambda i: (0, i)),
                  pl.BlockSpec((TOP_K, tr), lambda i: (0, i))],
        out_specs=(pl.BlockSpec((TOP_K, tr), lambda i: (0, i)),
                   pl.BlockSpec((1, nbl), lambda i: (0, 0)),
                   pl.BlockSpec((1, nbl), lambda i: (0, 0)),
                   pl.BlockSpec((1, LANES), lambda i: (0, 0))),
        scratch_shapes=[pltpu.VMEM((ne, LANES), F32)],
        compiler_params=_cparams("arbitrary"),
        name="dispatch_plan",
    )(counts, eidx, rank)


def _sc_workers():
    info = plsc.get_sparse_core_info()
    return info.num_cores, info.num_cores * info.num_subcores


def scatter_rows(dest_flat, h2p, n_rows):
    t, w = h2p.shape
    n_cores, n_workers = _sc_workers()
    per_worker = t // n_workers
    mesh = plsc.VectorSubcoreMesh(core_axis_name="c", subcore_axis_name="s")

    @functools.partial(
        pl.kernel, mesh=mesh, out_type=jax.ShapeDtypeStruct((n_rows, w), U32),
        scratch_types=[pltpu.VMEM((TOP_K, SC_ROWS), I32), pltpu.VMEM((SC_ROWS, w), U32), pltpu.SemaphoreType.DMA])
    def body(h_hbm, dest_hbm, xs_hbm, idx_v, rows_v, sem):
        base = (lax.axis_index("s") * n_cores + lax.axis_index("c")) * per_worker

        @pl.loop(0, per_worker // SC_ROWS)
        def _(ci):
            off = pl.multiple_of(base + ci * SC_ROWS, SC_ROWS)
            pltpu.sync_copy(h_hbm.at[pl.ds(off, SC_ROWS)], rows_v)
            for k in range(TOP_K):
                pltpu.sync_copy(dest_hbm.at[pl.ds(k * t + off, SC_ROWS)], idx_v.at[k])
            copies = [pltpu.async_copy(rows_v, xs_hbm.at[idx_v.at[k]], sem) for k in range(TOP_K)]
            for c in copies:
                c.wait()

    return body(h2p, dest_flat)


def gather_rows(idx_flat, table):
    n = idx_flat.shape[0]
    w = table.shape[1]
    n_cores, n_workers = _sc_workers()
    per_worker = n // n_workers
    mesh = plsc.VectorSubcoreMesh(core_axis_name="c", subcore_axis_name="s")

    ch = SC_ROWS // 2
    n_chunks = per_worker // ch

    @functools.partial(
        pl.kernel, mesh=mesh, out_type=jax.ShapeDtypeStruct((n, w), table.dtype),
        scratch_types=[pltpu.VMEM((2, ch), I32), pltpu.VMEM((2, ch, w), table.dtype), pltpu.SemaphoreType.DMA((2,))])
    def body(table_hbm, idx_hbm, out_hbm, idx_v, rows_v, sem):
        base = (lax.axis_index("s") * n_cores + lax.axis_index("c")) * per_worker

        def read(b):
            return pltpu.make_async_copy(table_hbm.at[idx_v.at[b]], rows_v.at[b], sem.at[b])

        def start(c, b):
            off = pl.multiple_of(base + c * ch, ch)
            pltpu.sync_copy(idx_hbm.at[pl.ds(off, ch)], idx_v.at[b])
            read(b).start()

        def finish(c, b):
            read(b).wait()
            pltpu.sync_copy(rows_v.at[b], out_hbm.at[pl.ds(pl.multiple_of(base + c * ch, ch), ch)])

        start(0, 0)

        @pl.loop(0, n_chunks, step=2)
        def _(c):
            start(c + 1, 1)
            finish(c, 0)

            @pl.when(c + 2 < n_chunks)
            def _():
                start(c + 2, 0)

            finish(c + 1, 1)

    return body(table, idx_flat)


def _gmm_kernel(be_ref, nv_ref, nb_ref, first_ref, run_ref, slot_ref, rexp_ref, xs_ref, wg_hbm, wu_hbm, wd_hbm,
                y_ref, wg_buf, wu_buf, wd_buf, sem):
    def weight_copies(e, s):
        return (pltpu.make_async_copy(wg_hbm.at[e], wg_buf.at[s], sem.at[s]),
                pltpu.make_async_copy(wu_hbm.at[e], wu_buf.at[s], sem.at[s]),
                pltpu.make_async_copy(wd_hbm.at[e], wd_buf.at[s], sem.at[s]))

    def fetch(run, slot):
        e = rexp_ref[run]

        @pl.when(e >= 0)
        def _():
            for c in weight_copies(e, slot):
                c.start()

    j0 = pl.program_id(0) * MOE_STEP_BLOCKS

    @pl.when(j0 < nb_ref[0])
    def _():
        for sb in range(MOE_STEP_BLOCKS):
            j = j0 + sb

            @pl.when(j == 0)
            def _():
                for a in range(WEIGHT_AHEAD):
                    fetch(a, a)

            @pl.when((j < nb_ref[0]) & (first_ref[j] == 1))
            def _():
                s = slot_ref[j]
                for c in weight_copies(be_ref[j], s):
                    c.wait()
                fetch(run_ref[j] + WEIGHT_AHEAD, (s + WEIGHT_AHEAD) % WEIGHT_SLOTS)

        for sb in range(MOE_STEP_BLOCKS):
            j = j0 + sb
            s = slot_ref[j]
            rows = slice(sb * MOE_ROWS, (sb + 1) * MOE_ROWS)
            x = _unpack_pairs_native(xs_ref[rows, :])
            row = lax.broadcasted_iota(I32, (x.shape[0], 1), 0)
            x = jnp.where(row < nv_ref[j], x, 0.0)
            hmid = _silu(_dot(x, wg_buf[s])) * _dot(x, wu_buf[s])
            y_ref[rows, :] = _pack_pairs_native(_dot(hmid, wd_buf[s]))


def grouped_mlp(block_e, block_valid, n_used, xs, w_gate, w_up, w_down, n_blocks):
    ne, d, f = w_gate.shape
    w = xs.shape[1]
    jj = jnp.arange(block_e.shape[0], dtype=I32)
    active = jj < n_used[0]
    first = (active & ((jj == 0) | (block_e != jnp.roll(block_e, 1)))).astype(I32)
    run = jnp.cumsum(first) - 1
    slot = (run % WEIGHT_SLOTS).astype(I32)
    nbl = block_e.shape[0]
    run_expert = jnp.full((nbl + WEIGHT_SLOTS,), -1, I32).at[jnp.where(first == 1, run, nbl)].set(
        jnp.where(first == 1, block_e, -1))
    run = run.astype(I32)
    step_rows = MOE_STEP_BLOCKS * MOE_ROWS
    last = lambda g, nb: jnp.minimum(g, (nb[0] - 1) // MOE_STEP_BLOCKS)
    row_block = pl.BlockSpec((step_rows, w), lambda g, be, nv, nb, fi, rn, sl, rx: (last(g, nb), 0))
    grid_spec = pltpu.PrefetchScalarGridSpec(
        num_scalar_prefetch=7,
        grid=(n_blocks // MOE_STEP_BLOCKS,),
        in_specs=[row_block,
                  pl.BlockSpec(memory_space=pl.ANY),
                  pl.BlockSpec(memory_space=pl.ANY),
                  pl.BlockSpec(memory_space=pl.ANY)],
        out_specs=row_block,
        scratch_shapes=[pltpu.VMEM((WEIGHT_SLOTS, d, f), F32), pltpu.VMEM((WEIGHT_SLOTS, d, f), F32),
                        pltpu.VMEM((WEIGHT_SLOTS, f, d), F32), pltpu.SemaphoreType.DMA((WEIGHT_SLOTS,))],
    )
    return pl.pallas_call(
        _gmm_kernel,
        out_shape=jax.ShapeDtypeStruct(xs.shape, U32),
        grid_spec=grid_spec,
        compiler_params=_cparams("arbitrary"),
        name="grouped_mlp",
    )(block_e, block_valid, n_used, first, run, slot, run_expert, xs, w_gate, w_up, w_down)


def _combine_kernel(y_ref, xres_ref, wt_ref, g2_ref, fg_ref, o_ref):
    wt = wt_ref[...]
    routed = jnp.zeros(xres_ref.shape, F32)
    for k in range(TOP_K):
        routed = routed + wt[:, k:k + 1] * _unpack_pairs_native(y_ref[k])
    x2 = xres_ref[...] + g2_ref[...] * routed
    ms = jnp.mean(x2 * x2, axis=-1, keepdims=True)
    o_ref[...] = x2 * lax.rsqrt(ms + NORM_EPS) * fg_ref[...]


def _combine_into_kernel(prev_ref, *refs):
    del prev_ref
    _combine_kernel(*refs)


def combine(y_tok, xres, wsel_t, g2, final_g, seq, tm, first_tile, prev_out):
    t, d = xres.shape
    tiles_per_batch = seq // tm
    tile = lambda i: i + first_tile
    in_specs = [pl.BlockSpec((TOP_K, tm, d // 2), lambda i: (0, i, 0)),
                pl.BlockSpec((tm, d), lambda i: (tile(i), 0)),
                pl.BlockSpec((tm, TOP_K), lambda i: (tile(i), 0)),
                pl.BlockSpec((None, 1, d), lambda i: (tile(i) // tiles_per_batch, 0, 0)),
                pl.BlockSpec((1, d), lambda i: (0, 0))]
    args = [y_tok, xres, wsel_t, g2, final_g.reshape(1, d)]
    body, aliases = _combine_kernel, {}
    if prev_out is not None:
        body, aliases = _combine_into_kernel, {0: 0}
        in_specs = [pl.BlockSpec(memory_space=pl.ANY)] + in_specs
        args = [prev_out] + args
    return pl.pallas_call(
        body,
        out_shape=jax.ShapeDtypeStruct((t, d), F32),
        grid=(y_tok.shape[1] // tm,),
        in_specs=in_specs,
        out_specs=pl.BlockSpec((tm, d), lambda i: (tile(i), 0)),
        input_output_aliases=aliases,
        compiler_params=_cparams("arbitrary"),
        name="combine",
    )(*args)


def _pos_tables(rows, cols, dim):
    quarter = dim // 4
    omega = 1.0 / (POS_BASE ** (np.arange(quarter, dtype=np.float32) / quarter))
    ang_r = np.arange(rows, dtype=np.float32)[:, None] * omega
    ang_c = np.arange(cols, dtype=np.float32)[:, None] * omega
    emb_r = np.concatenate([np.sin(ang_r), np.cos(ang_r)], axis=-1).astype(np.float32)
    emb_c = np.concatenate([np.sin(ang_c), np.cos(ang_c)], axis=-1).astype(np.float32)
    return jnp.asarray(emb_r.reshape(rows, 1, dim // 2)), jnp.asarray(emb_c)


def kernel(x, c, ctx, c_ctx, norm1_g, norm2_g, ada_w, ada_b, w_in, hy_conv_w, hy_conv_b, hy_f_w1, hy_f_b1, hy_f_w2, hy_f_b2, hy_f_w3, hy_f_b3, hy_f_w4, hy_f_freq, hy_skip, hg_lb_logits, hg_norm_g, w_hy_out, w_hg_out, w_out, router_w, router_bias, exp_w_gate, exp_w_up, exp_w_down, sh_w_gate, sh_w_up, sh_w_down, final_g):
    bsz, seq, d = x.shape
    n_ctx = ctx.shape[1]
    hy_w = w_hy_out.shape[1]
    hg_w = w_hg_out.shape[1]
    dk = hg_norm_g.shape[1]
    n_heads = hg_w // dk
    ne = router_w.shape[2]
    l = 0

    c_rows = jnp.zeros((SUBLANES, d), F32).at[:bsz].set(c).at[bsz].set(c_ctx)
    mods = ada_vectors(c_rows, ada_w[l], ada_b[l])
    sh1, sc1, g1, sh2, sc2, g2 = [mods[:bsz, j * d:(j + 1) * d].reshape(bsz, 1, d) for j in range(N_ADA)]
    csh1 = jnp.broadcast_to(mods[bsz, 0:d].reshape(1, 1, d), (bsz, 1, d))
    csc1 = jnp.broadcast_to(mods[bsz, d:2 * d].reshape(1, 1, d), (bsz, 1, d))

    emb_r, emb_c = _pos_tables(seq // GRID_W, GRID_W, d)
    w_in_b = w_in[l].astype(BF16)
    hy_proj = 3 * hy_w
    p = in_projection(x, emb_r, emb_c, norm1_g[l], sh1, sc1, w_in_b, TOKEN_TILE)
    hg_cols = slice(hy_proj, hy_proj + 5 * hg_w)
    zero_r = jnp.zeros((n_ctx // GRID_W, 1, d // 2), F32)
    zero_c = jnp.zeros((GRID_W, d // 2), F32)
    pc = in_projection(ctx, zero_r, zero_c, norm1_g[l], csh1, csc1, w_in_b[:, hg_cols], n_ctx)

    lbs = jnp.cumsum(jax.nn.softmax(hg_lb_logits.astype(F32), axis=0), axis=0)
    lb_f, lb_b = lbs[l, 0], lbs[l, 1]
    zero_state = jnp.zeros((bsz, n_heads, dk, dk), F32)
    base = hy_proj // hg_w
    _, st_f = hgrn_scan(pc, (0, 1, 2), lb_f, zero_state, n_ctx, n_ctx, reverse=False)
    _, st_b = hgrn_scan(pc, (0, 1, 3), lb_b, zero_state, n_ctx, n_ctx, reverse=True)
    o_f, _ = hgrn_scan(p, (base, base + 1, base + 2), lb_f, st_f, seq, HG_TIME_BLOCK, reverse=False)
    y_hg, _ = hgrn_scan(p, (base, base + 1, base + 3), lb_b, st_b, seq, HG_TIME_BLOCK, reverse=True,
                        o_fwd=o_f, gate_col=base + 4, norm_g=hg_norm_g[l])

    u = short_conv(p, hy_proj, hy_conv_w[l], hy_conv_b[l], seq, CONV_TILE, hy_w)
    taps, l1 = hyena_filter_taps(seq, hy_f_w1[l], hy_f_b1[l], hy_f_w2[l], hy_f_b2[l], hy_f_w3[l], hy_f_b3[l],
                                 hy_f_w4[l], hy_f_freq[l], hy_w)
    y_hy = hyena_branch(u, bsz, seq, hy_w, taps, l1, hy_skip[l])

    gate_base = (hy_proj + 5 * hg_w) // d
    sh_gu = jnp.concatenate([sh_w_gate[l], sh_w_up[l]], axis=1).astype(BF16)
    xres, h2p, logits_t = merge_stage(
        x, emb_r, emb_c, y_hy, y_hg, p, (gate_base, gate_base + 1),
        w_hy_out[l].astype(BF16), w_hg_out[l].astype(BF16), w_out[l].astype(BF16), g1, norm2_g[l],
        sh2, sc2, g2, router_w[l].T.astype(F32), sh_gu, sh_w_down[l].astype(BF16), TOKEN_TILE)

    t = bsz * seq
    eidx, wsel, rank, counts = route(logits_t, router_bias[l], TOKEN_TILE)
    n_rows = t * TOP_K + ne * (MOE_ROWS - 1)
    n_blocks = pl.cdiv(pl.cdiv(n_rows, MOE_ROWS), MOE_STEP_BLOCKS) * MOE_STEP_BLOCKS
    dest, block_e, block_valid, n_used = dispatch_plan(counts, eidx, rank, TOKEN_TILE, n_blocks)

    dest_flat = dest.reshape(-1)
    xs = scatter_rows(dest_flat, h2p, n_blocks * MOE_ROWS)
    ys = grouped_mlp(block_e.reshape(-1), block_valid.reshape(-1), n_used.reshape(-1)[:1], xs,
                     exp_w_gate[l], exp_w_up[l], exp_w_down[l], n_blocks)
    wsel_t = wsel.T
    out = None
    for h in range(GATHER_SPLIT):
        lo = h * (t // GATHER_SPLIT)
        rng = dest[:, lo:lo + t // GATHER_SPLIT]
        y_tok = gather_rows(rng.reshape(-1), ys).reshape(TOP_K, t // GATHER_SPLIT, d // 2)
        out = combine(y_tok, xres, wsel_t, g2, final_g, seq, TOKEN_TILE, lo // TOKEN_TILE, out)
    return out.reshape(bsz, seq, d)
```

```python
import functools
import math

import numpy as np
import jax
import jax.numpy as jnp
from jax import lax
from jax.experimental import pallas as pl
from jax.experimental.pallas import tpu as pltpu
from jax.experimental.pallas import tpu_sc as plsc

F32 = jnp.float32
BF16 = jnp.bfloat16
U32 = jnp.uint32
I32 = jnp.int32
HIGHEST = lax.Precision.HIGHEST

GRID_W = 64
POS_BASE = 10000.0
NORM_EPS = 1e-6
N_ADA = 6
HY_ORDER = 2
HY_SHORT = 3
HY_DECAY_TARGET = 1e-2
HY_FAST_PCT = 0.3
HY_SLOW_PCT = 1.5
HG_HEADS = 4
HG_CHUNK = 64
N_GROUPS = 8
TOPK_GROUPS = 4
TOP_K = 8
ROUTED_SCALE = 2.5

LANES = 128
SUBLANES = 8
VMEM_LIMIT = 56 * 1024 * 1024

TOKEN_TILE = 512
HG_TIME_BLOCK = 512
HALO_ROWS = 16
CONV_TILE = 2048
DFT_P = 128
FILTER_GROUP = 8
DFT_GROUP = 16
MOE_ROWS = 256
MOE_STEP_BLOCKS = 4
WEIGHT_AHEAD = 5
WEIGHT_SLOTS = WEIGHT_AHEAD + MOE_STEP_BLOCKS
SC_ROWS = 128
GATHER_SPLIT = 4


def _cparams(*sem):
    return pltpu.CompilerParams(dimension_semantics=sem, vmem_limit_bytes=VMEM_LIMIT)


def _dot(a, b):
    return jnp.dot(a, b, preferred_element_type=F32)


def _dot_hi(a, b):
    return jnp.dot(a, b, preferred_element_type=F32, precision=HIGHEST)


def _dot_nt(a, b):
    return lax.dot_general(a, b, (((1,), (1,)), ((), ())), preferred_element_type=F32)


def _dot_tn(a, b):
    return lax.dot_general(a, b, (((0,), (0,)), ((), ())), preferred_element_type=F32)


def _silu(x):
    return x * jax.nn.sigmoid(x)


def _split_bf16(x):
    hi = x.astype(BF16)
    return hi, (x - hi.astype(F32)).astype(BF16)


def _ada_kernel(c_ref, w_ref, b_ref, o_ref):
    o_ref[...] = _dot_hi(_silu(c_ref[...]), w_ref[...]) + b_ref[...]


def ada_vectors(c_rows, ada_w, ada_b):
    r, d = c_rows.shape
    n = ada_w.shape[1]
    bn = 1024
    return pl.pallas_call(
        _ada_kernel,
        out_shape=jax.ShapeDtypeStruct((r, n), F32),
        grid=(n // bn,),
        in_specs=[pl.BlockSpec((r, d), lambda j: (0, 0)),
                  pl.BlockSpec((d, bn), lambda j: (0, j)),
                  pl.BlockSpec((1, bn), lambda j: (0, j))],
        out_specs=pl.BlockSpec((r, bn), lambda j: (0, j)),
        compiler_params=_cparams("arbitrary"),
        name="ada_vectors",
    )(c_rows, ada_w, ada_b.reshape(1, n))


def _inproj_kernel(x_ref, er_ref, ec_ref, g_ref, sh_ref, sc_ref, w_ref, o_ref, *, col_chunk):
    x = x_ref[...]
    rows, gw, d = x.shape
    half = d // 2
    xp = jnp.concatenate([x[:, :, :half] + er_ref[...], x[:, :, half:] + ec_ref[...]], axis=-1)
    xp = xp.reshape(rows * gw, d)
    ms = jnp.mean(xp * xp, axis=-1, keepdims=True)
    y = xp * lax.rsqrt(ms + NORM_EPS) * g_ref[...]
    h = (y * (1.0 + sc_ref[...]) + sh_ref[...]).astype(BF16)
    n = o_ref.shape[1]
    for j in range(n // col_chunk):
        sl = slice(j * col_chunk, (j + 1) * col_chunk)
        o_ref[:, sl] = _dot(h, w_ref[:, sl]).astype(o_ref.dtype)


def in_projection(x, emb_r, emb_c, norm_g, shift, scale, w_bf16, tm):
    b, s, d = x.shape
    n = w_bf16.shape[1]
    rows_per_batch = s // GRID_W
    rt = tm // GRID_W
    tiles_per_batch = rows_per_batch // rt
    x3 = x.reshape(b * rows_per_batch, GRID_W, d)
    col_chunk = 512
    return pl.pallas_call(
        functools.partial(_inproj_kernel, col_chunk=col_chunk),
        out_shape=jax.ShapeDtypeStruct((b * s, n), BF16),
        grid=(b * tiles_per_batch,),
        in_specs=[pl.BlockSpec((rt, GRID_W, d), lambda i: (i, 0, 0)),
                  pl.BlockSpec((rt, 1, d // 2), lambda i: (i % tiles_per_batch, 0, 0)),
                  pl.BlockSpec((GRID_W, d // 2), lambda i: (0, 0)),
                  pl.BlockSpec((1, d), lambda i: (0, 0)),
                  pl.BlockSpec((None, 1, d), lambda i: (i // tiles_per_batch, 0, 0)),
                  pl.BlockSpec((None, 1, d), lambda i: (i // tiles_per_batch, 0, 0)),
                  pl.BlockSpec((d, n), lambda i: (0, 0))],
        out_specs=pl.BlockSpec((tm, n), lambda i: (i, 0)),
        compiler_params=_cparams("arbitrary"),
        name="in_projection",
    )(x3, emb_r, emb_c, norm_g.reshape(1, d), shift, scale, w_bf16)


def _hgrn_kernel(*refs, reverse, n_chunks, final):
    if final:
        (q_ref, i_ref, f_ref, lb_ref, s0_ref, of_ref, gate_ref, ng_ref, o_ref, sfin_ref, s_scr) = refs
    else:
        (q_ref, i_ref, f_ref, lb_ref, s0_ref, o_ref, sfin_ref, s_scr) = refs
    cs = HG_CHUNK
    bsz, n_heads, _, dk = s_scr.shape

    @pl.when(pl.program_id(0) == 0)
    def _():
        s_scr[...] = s0_ref[...]

    row = lax.broadcasted_iota(I32, (cs, cs), 0)
    col = lax.broadcasted_iota(I32, (cs, cs), 1)
    tri = (col >= row) if reverse else (col <= row)
    tri_b = tri.astype(BF16)
    end_row = 0 if reverse else cs - 1
    mid_row = cs // 2 if reverse else cs // 2 - 1

    def chunk_body(bi, ci):
        c = (n_chunks - 1 - ci) if reverse else ci
        rows = slice(c * cs, (c + 1) * cs)
        lb = lb_ref[...]
        f = lb + (1.0 - lb) * jax.nn.sigmoid(f_ref[bi, rows, :].astype(F32))
        lf_hi, lf_lo = _split_bf16(jnp.log(f))
        b_all = _dot(tri_b, lf_hi) + _dot(tri_b, lf_lo)
        k_all = 1.0 - f
        q_all = _silu(q_ref[bi, rows, :].astype(F32))
        for h in range(n_heads):
            sl = slice(h * dk, (h + 1) * dk)
            b = b_all[:, sl]
            q = q_all[:, sl]
            k = k_all[:, sl]
            v = i_ref[bi, rows, sl]
            b_end = b[end_row:end_row + 1]
            b_mid = b[mid_row:mid_row + 1]
            qd = (q * jnp.exp(b - b_mid)).astype(BF16)
            kd = (k * jnp.exp(b_mid - b)).astype(BF16)
            att = jnp.where(tri, _dot_nt(qd, kd), 0.0).astype(BF16)
            st = s_scr[bi, h]
            qe = (q * jnp.exp(b)).astype(BF16)
            o = _dot(jnp.concatenate([qe, att], axis=1), jnp.concatenate([st.astype(BF16), v], axis=0))
            ke = (k * jnp.exp(b_end - b)).astype(BF16)
            dec = jnp.transpose(jnp.broadcast_to(jnp.exp(b_end), (SUBLANES, dk)))[:, 0:1]
            s_scr[bi, h] = st * dec + _dot_tn(ke, v)
            if final:
                o = o + of_ref[bi, rows, sl].astype(F32)
                o = o * lax.rsqrt(jnp.mean(o * o, axis=-1, keepdims=True) + NORM_EPS) * ng_ref[...]
                o = o * _silu(gate_ref[bi, rows, sl].astype(F32))
            o_ref[bi, rows, sl] = o.astype(o_ref.dtype)

    for ci in range(n_chunks):
        for bi in range(bsz):
            chunk_body(bi, ci)
    sfin_ref[...] = s_scr[...]


def hgrn_scan(p, cols, lb, s0, seq, tb, *, reverse, o_fwd=None, gate_col=None, norm_g=None):
    bsz, n_heads, dv, dk = s0.shape
    width = n_heads * dk
    nt = seq // tb
    final = o_fwd is not None
    p3 = p.reshape(bsz, seq, p.shape[1])
    tmap = (lambda t: nt - 1 - t) if reverse else (lambda t: t)
    colspec = lambda cb: pl.BlockSpec((bsz, tb, width), lambda t: (0, tmap(t), cb))
    state = pl.BlockSpec((bsz, n_heads, dv, dk), lambda t: (0, 0, 0, 0))
    in_specs = [colspec(cols[0]), colspec(cols[1]), colspec(cols[2]),
                pl.BlockSpec((1, width), lambda t: (0, 0)), state]
    args = [p3, p3, p3, lb.reshape(1, width), s0]
    if final:
        in_specs += [colspec(0), colspec(gate_col), pl.BlockSpec((1, dk), lambda t: (0, 0))]
        args += [o_fwd.reshape(bsz, seq, width), p3, norm_g.reshape(1, dk)]
    o, s_fin = pl.pallas_call(
        functools.partial(_hgrn_kernel, reverse=reverse, n_chunks=tb // HG_CHUNK, final=final),
        out_shape=(jax.ShapeDtypeStruct((bsz, seq, width), BF16),
                   jax.ShapeDtypeStruct((bsz, n_heads, dv, dk), F32)),
        grid=(nt,),
        in_specs=in_specs,
        out_specs=(colspec(0), state),
        scratch_shapes=[pltpu.VMEM((bsz, n_heads, dv, dk), F32)],
        compiler_params=_cparams("arbitrary"),
        name="hgrn_bwd" if reverse else "hgrn_fwd",
    )(*args)
    return o.reshape(bsz * seq, width), s_fin


def _shortconv_kernel(p_ref, prev_ref, next_ref, w_ref, b_ref, o_ref, *, tiles_per_batch):
    i = pl.program_id(0)
    ti = i % tiles_per_batch
    p = p_ref[...].astype(F32)
    tm, cw = p.shape
    ta = tm // DFT_P
    prev_row = jnp.where(ti == 0, 0.0, prev_ref[HALO_ROWS - 1:HALO_ROWS, :].astype(F32))
    next_row = jnp.where(ti == tiles_per_batch - 1, 0.0, next_ref[0:1, :].astype(F32))
    p3 = jnp.swapaxes(p.reshape(ta, DFT_P, cw), 0, 1)
    arow = lax.broadcasted_iota(I32, (ta, 1), 0)
    prev_edge = jnp.where(arow == 0, prev_row, pltpu.roll(p3[DFT_P - 1], 1, axis=0))
    next_edge = jnp.where(arow == ta - 1, next_row, pltpu.roll(p3[0], ta - 1, axis=0))
    p_prev = jnp.concatenate([prev_edge[None], p3[:-1]], axis=0)
    p_next = jnp.concatenate([p3[1:], next_edge[None]], axis=0)
    u = w_ref[0:1, :] * p_prev + w_ref[1:2, :] * p3 + w_ref[2:3, :] * p_next + b_ref[...]
    o_ref[...] = u.astype(o_ref.dtype)


def short_conv(p, width, conv_w, conv_b, seq, tm, cw):
    t = p.shape[0]
    nt = t // tm
    tiles_per_batch = seq // tm
    sub = tm // HALO_ROWS
    ta = tm // DFT_P
    return pl.pallas_call(
        functools.partial(_shortconv_kernel, tiles_per_batch=tiles_per_batch),
        out_shape=jax.ShapeDtypeStruct((DFT_P, t // DFT_P, width), BF16),
        grid=(nt, width // cw),
        in_specs=[pl.BlockSpec((tm, cw), lambda i, j: (i, j)),
                  pl.BlockSpec((HALO_ROWS, cw), lambda i, j: (jnp.maximum(i * sub - 1, 0), j)),
                  pl.BlockSpec((HALO_ROWS, cw), lambda i, j: (jnp.minimum((i + 1) * sub, t // HALO_ROWS - 1), j)),
                  pl.BlockSpec((HY_SHORT, cw), lambda i, j: (0, j)),
                  pl.BlockSpec((1, cw), lambda i, j: (0, j))],
        out_specs=pl.BlockSpec((DFT_P, ta, cw), lambda i, j: (0, i, j)),
        compiler_params=_cparams("arbitrary", "arbitrary"),
        name="short_conv",
    )(p, p, p, conv_w, conv_b.reshape(1, width))


def _filter_kernel(band_ref, w1t_ref, w1c_ref, w1s_ref, b1_ref, w2_ref, b2_ref, w3_ref, b3_ref,
                   w4f_ref, w4b_ref, fr_ref, delta_ref, k_ref, s_ref, *, seq):
    step = pl.program_id(0)
    gb, q, ncol = k_ref.shape
    half = q // 2
    width = delta_ref.shape[1]
    nrow = gb * q
    nf = gb * half

    def positions(shape, axis):
        r = lax.broadcasted_iota(I32, shape, axis)
        is_bwd = r >= nf
        rr = jnp.where(is_bwd, r - nf, r)
        j = lax.shift_right_logical(rr, int(math.log2(half)))
        a = (rr & (half - 1)) + jnp.where(is_bwd, half, 0)
        n = (a * DFT_P + step * gb + j).astype(F32)
        t = jnp.where(is_bwd, 2.0 * seq - n, n)
        return n, t, t / float(max(seq - 1, 1))

    _, t_l, tn_l = positions((1, nrow), 1)
    ang = (2.0 * math.pi / seq) * t_l * band_ref[...]
    fr = fr_ref[...]
    pre = (w1t_ref[...] * tn_l + _dot_hi(w1c_ref[...], jnp.cos(ang)) - _dot_hi(w1s_ref[...], jnp.sin(ang))
           + b1_ref[...])
    act = jnp.sin(fr * pre)
    act = jnp.sin(fr * (_dot_hi(w2_ref[...], act) + b2_ref[...]))
    act = jnp.sin(fr * (_dot_hi(w3_ref[...], act) + b3_ref[...])).astype(BF16)
    n_s, _, tn_s = positions((nrow, 1), 0)
    delta = jnp.concatenate([delta_ref[...]] * (ncol // width), axis=1)
    hf = _dot_tn(act[:, :nf], w4f_ref[...]) * jnp.exp(-tn_s[:nf] * delta)
    hb = _dot_tn(act[:, nf:], w4b_ref[...]) * jnp.exp(-tn_s[nf:] * delta)
    hb = jnp.where(n_s[nf:] == float(seq), 0.0, hb)
    k_ref[:, :half, :] = hf.reshape(gb, half, ncol).astype(k_ref.dtype)
    k_ref[:, half:, :] = hb.reshape(gb, half, ncol).astype(k_ref.dtype)
    tot = jnp.sum(jnp.abs(hf), axis=0, keepdims=True) + jnp.sum(jnp.abs(hb), axis=0, keepdims=True)

    @pl.when(step == 0)
    def _():
        s_ref[...] = jnp.zeros_like(s_ref)

    s_ref[...] += tot


def hyena_filter_taps(seq, w1, b1, w2, b2, w3, b3, w4, freq, width):
    emb = w1.shape[0]
    hid = w1.shape[1]
    bands = (emb - 1) // 2
    q = 2 * seq // DFT_P
    ncol = HY_ORDER * width
    band = np.linspace(1e-4, bands - 1, bands, dtype=np.float32).reshape(bands, 1)
    min_decay = math.log(HY_DECAY_TARGET) / HY_SLOW_PCT
    max_decay = math.log(HY_DECAY_TARGET) / HY_FAST_PCT
    delta = np.abs(np.linspace(min_decay, max_decay, width, dtype=np.float32)).reshape(1, width)
    w1t = w1.astype(F32).T
    col = lambda v: v.reshape(hid, 1).astype(F32)
    w4r = w4.astype(BF16).reshape(hid, HY_ORDER, 2, width)
    w4f = w4r[:, :, 0, :].reshape(hid, ncol)
    w4b = w4r[:, :, 1, :].reshape(hid, ncol)
    gb = FILTER_GROUP
    const = lambda shape: pl.BlockSpec(shape, lambda i: tuple(0 for _ in shape))
    return pl.pallas_call(
        functools.partial(_filter_kernel, seq=seq),
        out_shape=(jax.ShapeDtypeStruct((DFT_P, q, ncol), BF16),
                   jax.ShapeDtypeStruct((1, ncol), F32)),
        grid=(DFT_P // gb,),
        in_specs=[const((bands, 1)), const((hid, 1)), const((hid, bands)), const((hid, bands)), const((hid, 1)),
                  const((hid, hid)), const((hid, 1)), const((hid, hid)), const((hid, 1)),
                  const((hid, ncol)), const((hid, ncol)), const((hid, 1)), const((1, width))],
        out_specs=(pl.BlockSpec((gb, q, ncol), lambda i: (i, 0, 0)),
                   pl.BlockSpec((1, ncol), lambda i: (0, 0))),
        compiler_params=_cparams("arbitrary"),
        name="hyena_filter",
    )(jnp.asarray(band), w1t[:, 0:1], w1t[:, 1:1 + bands], w1t[:, 1 + bands:1 + 2 * bands], col(b1),
      w2.astype(F32).T, col(b2), w3.astype(F32).T, col(b3), w4f, w4b, col(freq), jnp.asarray(delta))


def _dft_tables(seq):
    p = DFT_P
    n_fft = 2 * seq
    q = n_fft // p
    qh = q // 2
    ka = np.arange(q)
    nn = np.arange(q)[None, :] * p + np.arange(p)[:, None]
    ang = ((ka[None, :, None] * nn[:, None, :]) % n_fft) * (2.0 * np.pi / n_fft)
    mr, mi = np.cos(ang), -np.sin(ang)
    m1c = np.concatenate([np.concatenate([mr[:, :, :qh], -mi[:, :, :qh]], axis=2),
                          np.concatenate([mi[:, :, :qh], mr[:, :, :qh]], axis=2)], axis=1)
    m1r = np.concatenate([mr, mi], axis=1)
    gr = np.swapaxes(mr[:, :, :qh], 1, 2) / n_fft
    gi = -np.swapaxes(mi[:, :, :qh], 1, 2) / n_fft
    m4 = np.concatenate([np.concatenate([gr, -gi], axis=2), np.concatenate([gi, gr], axis=2)], axis=1)
    kb = np.arange(p)
    ang2 = 2.0 * np.pi * ((kb[:, None] * kb[None, :]) % p) / p
    fr, fi = np.cos(ang2), -np.sin(ang2)
    m2 = np.block([[fr, -fi], [fi, fr]])
    m3 = np.block([[fr, fi], [-fi, fr]])
    return tuple(jnp.asarray(m.astype(np.float32).astype(BF16)) for m in (m1c, m1r, m2, m3, m4))


def _store_swapped(o_ref, res):
    g2 = o_ref.shape[0]
    stacked = jnp.stack(res, axis=0)
    for ri in range(2):
        o_ref[:, ri, :, :] = jnp.swapaxes(stacked[:, ri * g2:(ri + 1) * g2, :], 0, 1).astype(o_ref.dtype)


def _swapped_out(g, m, ncols, gb, n_col_blocks=1):
    shape = (m // 2, 2, g, ncols * n_col_blocks)
    if n_col_blocks == 1:
        return shape, pl.BlockSpec((m // 2, 2, gb, ncols), lambda i: (0, 0, i, 0))
    return shape, pl.BlockSpec((m // 2, 2, gb, ncols), lambda i, j: (0, 0, i, j))


def _bmm_kernel(w_ref, x_ref, o_ref, *, shared_w):
    res = []
    for j in range(x_ref.shape[0]):
        w = w_ref[...] if shared_w else w_ref[j]
        res.append(_dot(w, x_ref[j]))
    _store_swapped(o_ref, res)


def batched_left_matmul(w, x, col_block, ncols, name, gb, n_col_blocks=1):
    g, k = x.shape[0], x.shape[1]
    shared = w.ndim == 2
    m = w.shape[-2]
    out_shape, out_spec = _swapped_out(g, m, ncols, gb, n_col_blocks)
    if n_col_blocks == 1:
        grid = (g // gb,)
        wspec = (pl.BlockSpec((m, k), lambda i: (0, 0)) if shared else pl.BlockSpec((gb, m, k), lambda i: (i, 0, 0)))
        xspec = pl.BlockSpec((gb, k, ncols), lambda i: (i, 0, col_block))
        sem = ("arbitrary",)
    else:
        grid = (g // gb, n_col_blocks)
        wspec = (pl.BlockSpec((m, k), lambda i, j: (0, 0)) if shared
                 else pl.BlockSpec((gb, m, k), lambda i, j: (i, 0, 0)))
        xspec = pl.BlockSpec((gb, k, ncols), lambda i, j: (i, 0, j))
        sem = ("arbitrary", "arbitrary")
    out = pl.pallas_call(
        functools.partial(_bmm_kernel, shared_w=shared),
        out_shape=jax.ShapeDtypeStruct(out_shape, BF16),
        grid=grid,
        in_specs=[wspec, xspec],
        out_specs=out_spec,
        compiler_params=_cparams(*sem),
        name=name,
    )(w, x)
    return out.reshape(m // 2, 2 * g, out_shape[3])


def _dft_mid_kernel(m2_ref, m3_ref, x_ref, k_ref, o_ref):
    half = x_ref.shape[1] // 2
    res = []
    for j in range(x_ref.shape[0]):
        xf = _dot(m2_ref[...], x_ref[j])
        kf = _dot(m2_ref[...], k_ref[j])
        xr, xi = xf[:half], xf[half:]
        kr, ki = kf[:half], kf[half:]
        z = jnp.concatenate([xr * kr - xi * ki, xr * ki + xi * kr], axis=0).astype(BF16)
        res.append(_dot(m3_ref[...], z))
    _store_swapped(o_ref, res)


def dft_mid(m2, m3, x, kspec, kcol, ncols):
    g, r = x.shape[0], x.shape[1]
    gb = DFT_GROUP
    out_shape, out_spec = _swapped_out(g, r, ncols, gb)
    out = pl.pallas_call(
        _dft_mid_kernel,
        out_shape=jax.ShapeDtypeStruct(out_shape, BF16),
        grid=(g // gb,),
        in_specs=[pl.BlockSpec((r, r), lambda i: (0, 0)),
                  pl.BlockSpec((r, r), lambda i: (0, 0)),
                  pl.BlockSpec((gb, r, ncols), lambda i: (i, 0, 0)),
                  pl.BlockSpec((gb, r, ncols), lambda i: (i, 0, kcol))],
        out_specs=out_spec,
        compiler_params=_cparams("arbitrary"),
        name="dft_mid",
    )(m2, m3, x, kspec)
    return out.reshape(r // 2, 2 * g, ncols)


def _dft_out_kernel(m4_ref, y_ref, inv_ref, skip_ref, v_ref, mul_ref, *rest, last):
    res = []
    for j in range(y_ref.shape[0]):
        conv = _dot(m4_ref[j], y_ref[j]) * inv_ref[...] + v_ref[j].astype(F32) * skip_ref[...]
        res.append(mul_ref[j].astype(F32) * conv)
    if last:
        (o_ref,) = rest
        o_ref[...] = jnp.swapaxes(jnp.stack(res, axis=0), 0, 1).astype(o_ref.dtype)
    else:
        m1_ref, o_ref, s1_ref = rest
        nxt = []
        for j, r in enumerate(res):
            zb = r.astype(o_ref.dtype)
            o_ref[j] = zb
            nxt.append(_dot(m1_ref[j], zb))
        _store_swapped(s1_ref, nxt)


def dft_out(m4, y, inv_l1, skip, u, v_col, mul, mul_col, ncols, m1c=None):
    g, r = y.shape[0], y.shape[1]
    rows = m4.shape[1]
    gb = DFT_GROUP
    in_specs = [pl.BlockSpec((gb, rows, r), lambda i: (i, 0, 0)),
                pl.BlockSpec((gb, r, ncols), lambda i: (i, 0, 0)),
                pl.BlockSpec((1, ncols), lambda i: (0, 0)),
                pl.BlockSpec((1, ncols), lambda i: (0, 0)),
                pl.BlockSpec((gb, rows, ncols), lambda i: (i, 0, v_col)),
                pl.BlockSpec((gb, rows, ncols), lambda i: (i, 0, mul_col))]
    args = [m4, y, inv_l1, skip, u, mul]
    if m1c is None:
        out_shape = jax.ShapeDtypeStruct((rows, g, ncols), BF16)
        out_specs = pl.BlockSpec((rows, gb, ncols), lambda i: (0, i, 0))
    else:
        m = m1c.shape[1]
        s_shape, s_spec = _swapped_out(g, m, ncols, gb)
        in_specs.append(pl.BlockSpec((gb, m, rows), lambda i: (i, 0, 0)))
        args.append(m1c)
        out_shape = (jax.ShapeDtypeStruct((g, rows, ncols), BF16), jax.ShapeDtypeStruct(s_shape, BF16))
        out_specs = (pl.BlockSpec((gb, rows, ncols), lambda i: (i, 0, 0)), s_spec)
    out = pl.pallas_call(
        functools.partial(_dft_out_kernel, last=m1c is None),
        out_shape=out_shape,
        grid=(g // gb,),
        in_specs=in_specs,
        out_specs=out_specs,
        compiler_params=_cparams("arbitrary"),
        name="dft_out",
    )(*args)
    if m1c is None:
        return out
    z, s1 = out
    return z, s1.reshape(m1c.shape[1] // 2, 2 * g, ncols)


def hyena_branch(u, bsz, seq, width, taps, l1, skip):
    m1c, m1r, m2, m3, m4 = _dft_tables(seq)
    ks1 = batched_left_matmul(m1r, taps, 0, width, "dft_k1", DFT_GROUP, n_col_blocks=HY_ORDER)
    inv_l1 = 1.0 / l1
    src = u
    s1 = batched_left_matmul(m1c, u, 0, width, "dft_s1", DFT_GROUP)
    for order in range(HY_ORDER):
        mid = dft_mid(m2, m3, s1, ks1, order, width)
        last = order == HY_ORDER - 1
        out = dft_out(m4, mid, inv_l1[:, order * width:(order + 1) * width],
                      skip[order].reshape(1, width).astype(F32), src, 0, u, order + 1, width,
                      m1c=None if last else m1c)
        if not last:
            src, s1 = out
    return out.reshape(bsz * seq, width)


def _pack_pairs(x):
    w = x.shape[1] // 2
    u = lax.bitcast_convert_type(x, U32)
    r = (u + U32(0x7FFF) + ((u >> 16) & U32(1))) >> 16
    return r[:, :w] | (r[:, w:] << 16)


def _unpack_pairs(p):
    lo = lax.bitcast_convert_type(p << 16, F32)
    hi = lax.bitcast_convert_type(p & U32(0xFFFF0000), F32)
    return jnp.concatenate([lo, hi], axis=1)


def _pack_pairs_native(x):
    w = x.shape[1] // 2
    return lax.bitcast_convert_type(pltpu.pack_elementwise([x[:, :w], x[:, w:]], packed_dtype=BF16), U32)


def _unpack_pairs_native(p):
    pi = lax.bitcast_convert_type(p, I32)
    halves = [pltpu.unpack_elementwise(pi, index=i, packed_dtype=BF16, unpacked_dtype=F32) for i in range(2)]
    return jnp.concatenate(halves, axis=1)


def _merge_kernel(x_ref, er_ref, ec_ref, yhy_ref, yhg_ref, ghy_ref, ghg_ref, why_ref, whg_ref, wo_ref,
                  g1_ref, n2_ref, sh2_ref, sc2_ref, g2_ref, rwh_ref, rwl_ref, sgu_ref, sd_ref,
                  xres_ref, h2p_ref, lg_ref):
    x = x_ref[...]
    rows, gw, d = x.shape
    half = d // 2
    xp = jnp.concatenate([x[:, :, :half] + er_ref[...], x[:, :, half:] + ec_ref[...]], axis=-1)
    xp = xp.reshape(rows * gw, d)
    m = (jax.nn.sigmoid(ghy_ref[...].astype(F32)) * _dot(yhy_ref[...], why_ref[...])
         + jax.nn.sigmoid(ghg_ref[...].astype(F32)) * _dot(yhg_ref[...], whg_ref[...]))
    x1 = xp + g1_ref[...] * _dot(m.astype(BF16), wo_ref[...])
    ms = jnp.mean(x1 * x1, axis=-1, keepdims=True)
    h2 = x1 * lax.rsqrt(ms + NORM_EPS) * n2_ref[...] * (1.0 + sc2_ref[...]) + sh2_ref[...]
    h_hi, h_lo = _split_bf16(h2)
    lg_ref[...] = _dot_nt(rwh_ref[...], h_hi) + (_dot_nt(rwl_ref[...], h_hi) + _dot_nt(rwh_ref[...], h_lo))
    gu = _dot(h_hi, sgu_ref[...])
    fs = gu.shape[1] // 2
    shared = _dot((_silu(gu[:, :fs]) * gu[:, fs:]).astype(BF16), sd_ref[...])
    xres_ref[...] = x1 + g2_ref[...] * shared
    h2p_ref[...] = _pack_pairs(h2)


def merge_stage(x, emb_r, emb_c, y_hy, y_hg, p, gate_cols, w_hy_out, w_hg_out, w_out, g1, norm2_g,
                sh2, sc2, g2, router_wt, sh_gate_up, sh_down, tm):
    b, s, d = x.shape
    rows_per_batch = s // GRID_W
    rt = tm // GRID_W
    tiles_per_batch = rows_per_batch // rt
    x3 = x.reshape(b * rows_per_batch, GRID_W, d)
    wb = y_hy.shape[1]
    ne = router_wt.shape[0]
    fs2 = sh_gate_up.shape[1]
    rw_hi, rw_lo = _split_bf16(router_wt)
    tok = lambda cb, w: pl.BlockSpec((tm, w), lambda i: (i, cb))
    const = lambda shape: pl.BlockSpec(shape, lambda i: tuple(0 for _ in shape))
    per_b = pl.BlockSpec((None, 1, d), lambda i: (i // tiles_per_batch, 0, 0))
    return pl.pallas_call(
        _merge_kernel,
        out_shape=(jax.ShapeDtypeStruct((b * s, d), F32),
                   jax.ShapeDtypeStruct((b * s, d // 2), U32),
                   jax.ShapeDtypeStruct((ne, b * s), F32)),
        grid=(b * tiles_per_batch,),
        in_specs=[pl.BlockSpec((rt, GRID_W, d), lambda i: (i, 0, 0)),
                  pl.BlockSpec((rt, 1, d // 2), lambda i: (i % tiles_per_batch, 0, 0)),
                  const((GRID_W, d // 2)),
                  tok(0, wb), tok(0, wb), tok(gate_cols[0], d), tok(gate_cols[1], d),
                  const((wb, d)), const((wb, d)), const((d, d)),
                  per_b, const((1, d)), per_b, per_b, per_b,
                  const((ne, d)), const((ne, d)), const((d, fs2)), const((fs2 // 2, d))],
        out_specs=(pl.BlockSpec((tm, d), lambda i: (i, 0)),
                   pl.BlockSpec((tm, d // 2), lambda i: (i, 0)),
                   pl.BlockSpec((ne, tm), lambda i: (0, i))),
        compiler_params=_cparams("arbitrary"),
        name="merge",
    )(x3, emb_r, emb_c, y_hy, y_hg, p, p, w_hy_out, w_hg_out, w_out, g1, norm2_g.reshape(1, d),
      sh2, sc2, g2, rw_hi, rw_lo, sh_gate_up, sh_down)


def _route_kernel(lg_ref, bias_ref, eidx_ref, wsel_ref, rank_ref, cnt_ref, carry):
    ne, tr = lg_ref.shape
    gsz = ne // N_GROUPS
    neg = -jnp.inf

    @pl.when(pl.program_id(0) == 0)
    def _():
        carry[...] = jnp.zeros_like(carry)

    scores = jax.nn.sigmoid(lg_ref[...])
    biased = scores + bias_ref[...]
    riota = lax.broadcasted_iota(I32, (gsz, tr), 0).astype(F32)
    gs = []
    for g in range(N_GROUPS):
        vg = biased[g * gsz:(g + 1) * gsz]
        m1 = jnp.max(vg, axis=0, keepdims=True)
        i1 = jnp.min(jnp.where(vg == m1, riota, float(gsz)), axis=0, keepdims=True)
        m2 = jnp.max(jnp.where(riota == i1, neg, vg), axis=0, keepdims=True)
        gs.append(m1 + m2)
    cur = jnp.concatenate(gs, axis=0)
    giota = lax.broadcasted_iota(I32, (N_GROUPS, tr), 0).astype(F32)
    gsel = jnp.zeros((N_GROUPS, tr), F32)
    for _ in range(TOPK_GROUPS):
        m = jnp.max(cur, axis=0, keepdims=True)
        idx = jnp.min(jnp.where(cur == m, giota, float(N_GROUPS)), axis=0, keepdims=True)
        hit = giota == idx
        gsel = jnp.where(hit, 1.0, gsel)
        cur = jnp.where(hit, neg, cur)
    cur = jnp.concatenate([jnp.where(gsel[g:g + 1] > 0.0, biased[g * gsz:(g + 1) * gsz], neg)
                           for g in range(N_GROUPS)], axis=0)
    eiota = lax.broadcasted_iota(I32, (ne, tr), 0).astype(F32)
    chosen = jnp.zeros((ne, tr), F32)
    idxs, ws = [], []
    for _ in range(TOP_K):
        m = jnp.max(cur, axis=0, keepdims=True)
        idx = jnp.min(jnp.where(cur == m, eiota, float(ne)), axis=0, keepdims=True)
        hit = eiota == idx
        idxs.append(idx)
        ws.append(jnp.sum(jnp.where(hit, scores, 0.0), axis=0, keepdims=True))
        chosen = jnp.where(hit, 1.0, chosen)
        cur = jnp.where(hit, neg, cur)
    w = jnp.concatenate(ws, axis=0)
    wsel_ref[...] = w / jnp.sum(w, axis=0, keepdims=True) * ROUTED_SCALE
    eidx_ref[...] = jnp.concatenate(idxs, axis=0).astype(I32)
    srow = lax.broadcasted_iota(I32, (tr, tr), 0)
    scol = lax.broadcasted_iota(I32, (tr, tr), 1)
    before = (srow < scol).astype(BF16)
    base = carry[...] + _dot(chosen.astype(BF16), before)
    ranks = [jnp.sum(jnp.where(eiota == idx, base, 0.0), axis=0, keepdims=True) for idx in idxs]
    rank_ref[...] = jnp.concatenate(ranks, axis=0).astype(I32)
    carry[...] += jnp.sum(chosen, axis=1, keepdims=True)
    cnt_ref[...] = carry[...]


def route(logits_t, router_bias, tr):
    ne, t = logits_t.shape
    return pl.pallas_call(
        _route_kernel,
        out_shape=(jax.ShapeDtypeStruct((TOP_K, t), I32),
                   jax.ShapeDtypeStruct((TOP_K, t), F32),
                   jax.ShapeDtypeStruct((TOP_K, t), I32),
                   jax.ShapeDtypeStruct((ne, 1), F32)),
        grid=(t // tr,),
        in_specs=[pl.BlockSpec((ne, tr), lambda i: (0, i)),
                  pl.BlockSpec((ne, 1), lambda i: (0, 0))],
        out_specs=(pl.BlockSpec((TOP_K, tr), lambda i: (0, i)),
                   pl.BlockSpec((TOP_K, tr), lambda i: (0, i)),
                   pl.BlockSpec((TOP_K, tr), lambda i: (0, i)),
                   pl.BlockSpec((ne, 1), lambda i: (0, 0))),
        scratch_shapes=[pltpu.VMEM((ne, 1), F32)],
        compiler_params=_cparams("arbitrary"),
        name="route",
    )(logits_t, router_bias.reshape(ne, 1).astype(F32))


def _dest_kernel(cnt_ref, eidx_ref, rank_ref, dest_ref, be_ref, nv_ref, nb_ref, start_scr):
    ne = cnt_ref.shape[0]
    tr = eidx_ref.shape[1]

    @pl.when(pl.program_id(0) == 0)
    def _():
        cnt = jnp.broadcast_to(cnt_ref[...], (ne, LANES))
        padded = jnp.floor((cnt + float(MOE_ROWS - 1)) / float(MOE_ROWS)) * float(MOE_ROWS)
        r = lax.broadcasted_iota(I32, (ne, ne), 0)
        c = lax.broadcasted_iota(I32, (ne, ne), 1)
        start = _dot_hi((c < r).astype(F32), padded)
        start_scr[...] = start
        end = start[:, 0:1] + padded[:, 0:1]
        used = start[:, 0:1] + cnt[:, 0:1]
        nbl = be_ref.shape[1]
        blk_row = (lax.broadcasted_iota(I32, (1, nbl), 1) * MOE_ROWS).astype(F32)
        total = jnp.max(end, axis=0, keepdims=True)
        last_row = total - float(MOE_ROWS)
        blk_row_c = jnp.minimum(blk_row, last_row)
        e_of = jnp.sum((end <= blk_row_c).astype(F32), axis=0, keepdims=True)
        e_of = jnp.minimum(e_of, float(ne - 1))
        eio = lax.broadcasted_iota(I32, (ne, nbl), 0).astype(F32)
        used_e = jnp.sum(jnp.where(eio == e_of, used, 0.0), axis=0, keepdims=True)
        valid = jnp.clip(used_e - blk_row_c, 0.0, float(MOE_ROWS))
        be_ref[...] = e_of.astype(I32)
        nv_ref[...] = jnp.where(blk_row <= last_row, valid, 0.0).astype(I32)
        nb_ref[...] = jnp.broadcast_to(total / float(MOE_ROWS), nb_ref.shape).astype(I32)

    eiota = lax.broadcasted_iota(I32, (ne, tr), 0)
    start_col = start_scr[:, 0:1]
    rows = []
    for k in range(TOP_K):
        hit = eiota == eidx_ref[k:k + 1, :]
        rows.append(jnp.sum(jnp.where(hit, start_col, 0.0), axis=0, keepdims=True))
    dest_ref[...] = jnp.concatenate(rows, axis=0).astype(I32) + rank_ref[...]


def dispatch_plan(counts, eidx, rank, tr, n_blocks):
    ne = counts.shape[0]
    t = eidx.shape[1]
    nbl = pl.cdiv(n_blocks, LANES) * LANES
    return pl.pallas_call(
        _dest_kernel,
        out_shape=(jax.ShapeDtypeStruct((TOP_K, t), I32),
                   jax.ShapeDtypeStruct((1, nbl), I32),
                   jax.ShapeDtypeStruct((1, nbl), I32),
                   jax.ShapeDtypeStruct((1, LANES), I32)),
        grid=(t // tr,),
        in_specs=[pl.BlockSpec((ne, 1), lambda i: (0, 0)),
                  pl.BlockSpec((TOP_K, tr), lambda i: (0, i)),
                  pl.BlockSpec((TOP_K, tr), lambda i: (0, i))],
        out_specs=(pl.BlockSpec((TOP_K, tr), lambda i: (0, i)),
                   pl.BlockSpec((1, nbl), lambda i: (0, 0)),
                   pl.BlockSpec((1, nbl), lambda i: (0, 0)),
                   pl.BlockSpec((1, LANES), lambda i: (0, 0))),
        scratch_shapes=[pltpu.VMEM((ne, LANES), F32)],
        compiler_params=_cparams("arbitrary"),
        name="dispatch_plan",
    )(counts, eidx, rank)


def _sc_workers():
    info = plsc.get_sparse_core_info()
    return info.num_cores, info.num_cores * info.num_subcores


def scatter_rows(dest_flat, h2p, n_rows):
    t, w = h2p.shape
    n_cores, n_workers = _sc_workers()
    per_worker = t // n_workers
    mesh = plsc.VectorSubcoreMesh(core_axis_name="c", subcore_axis_name="s")

    @functools.partial(
        pl.kernel, mesh=mesh, out_type=jax.ShapeDtypeStruct((n_rows, w), U32),
        scratch_types=[pltpu.VMEM((TOP_K, SC_ROWS), I32), pltpu.VMEM((SC_ROWS, w), U32), pltpu.SemaphoreType.DMA])
    def body(h_hbm, dest_hbm, xs_hbm, idx_v, rows_v, sem):
        base = (lax.axis_index("s") * n_cores + lax.axis_index("c")) * per_worker

        @pl.loop(0, per_worker // SC_ROWS)
        def _(ci):
            off = pl.multiple_of(base + ci * SC_ROWS, SC_ROWS)
            pltpu.sync_copy(h_hbm.at[pl.ds(off, SC_ROWS)], rows_v)
            for k in range(TOP_K):
                pltpu.sync_copy(dest_hbm.at[pl.ds(k * t + off, SC_ROWS)], idx_v.at[k])
            copies = [pltpu.async_copy(rows_v, xs_hbm.at[idx_v.at[k]], sem) for k in range(TOP_K)]
            for c in copies:
                c.wait()

    return body(h2p, dest_flat)


def gather_rows(idx_flat, table):
    n = idx_flat.shape[0]
    w = table.shape[1]
    n_cores, n_workers = _sc_workers()
    per_worker = n // n_workers
    mesh = plsc.VectorSubcoreMesh(core_axis_name="c", subcore_axis_name="s")

    ch = SC_ROWS // 2
    n_chunks = per_worker // ch

    @functools.partial(
        pl.kernel, mesh=mesh, out_type=jax.ShapeDtypeStruct((n, w), table.dtype),
        scratch_types=[pltpu.VMEM((2, ch), I32), pltpu.VMEM((2, ch, w), table.dtype), pltpu.SemaphoreType.DMA((2,))])
    def body(table_hbm, idx_hbm, out_hbm, idx_v, rows_v, sem):
        base = (lax.axis_index("s") * n_cores + lax.axis_index("c")) * per_worker

        def read(b):
            return pltpu.make_async_copy(table_hbm.at[idx_v.at[b]], rows_v.at[b], sem.at[b])

        def start(c, b):
            off = pl.multiple_of(base + c * ch, ch)
            pltpu.sync_copy(idx_hbm.at[pl.ds(off, ch)], idx_v.at[b])
            read(b).start()

        def finish(c, b):
            read(b).wait()
            pltpu.sync_copy(rows_v.at[b], out_hbm.at[pl.ds(pl.multiple_of(base + c * ch, ch), ch)])

        start(0, 0)

        @pl.loop(0, n_chunks, step=2)
        def _(c):
            start(c + 1, 1)
            finish(c, 0)

            @pl.when(c + 2 < n_chunks)
            def _():
                start(c + 2, 0)

            finish(c + 1, 1)

    return body(table, idx_flat)


def _gmm_kernel(be_ref, nv_ref, nb_ref, first_ref, run_ref, slot_ref, rexp_ref, xs_ref, wg_hbm, wu_hbm, wd_hbm,
                y_ref, wg_buf, wu_buf, wd_buf, sem):
    def weight_copies(e, s):
        return (pltpu.make_async_copy(wg_hbm.at[e], wg_buf.at[s], sem.at[s]),
                pltpu.make_async_copy(wu_hbm.at[e], wu_buf.at[s], sem.at[s]),
                pltpu.make_async_copy(wd_hbm.at[e], wd_buf.at[s], sem.at[s]))

    def fetch(run, slot):
        e = rexp_ref[run]

        @pl.when(e >= 0)
        def _():
            for c in weight_copies(e, slot):
                c.start()

    j0 = pl.program_id(0) * MOE_STEP_BLOCKS

    @pl.when(j0 < nb_ref[0])
    def _():
        for sb in range(MOE_STEP_BLOCKS):
            j = j0 + sb

            @pl.when(j == 0)
            def _():
                for a in range(WEIGHT_AHEAD):
                    fetch(a, a)

            @pl.when((j < nb_ref[0]) & (first_ref[j] == 1))
            def _():
                s = slot_ref[j]
                for c in weight_copies(be_ref[j], s):
                    c.wait()
                fetch(run_ref[j] + WEIGHT_AHEAD, (s + WEIGHT_AHEAD) % WEIGHT_SLOTS)

        for sb in range(MOE_STEP_BLOCKS):
            j = j0 + sb
            s = slot_ref[j]
            rows = slice(sb * MOE_ROWS, (sb + 1) * MOE_ROWS)
            x = _unpack_pairs_native(xs_ref[rows, :])
            row = lax.broadcasted_iota(I32, (x.shape[0], 1), 0)
            x = jnp.where(row < nv_ref[j], x, 0.0)
            hmid = _silu(_dot(x, wg_buf[s])) * _dot(x, wu_buf[s])
            y_ref[rows, :] = _pack_pairs_native(_dot(hmid, wd_buf[s]))


def grouped_mlp(block_e, block_valid, n_used, xs, w_gate, w_up, w_down, n_blocks):
    ne, d, f = w_gate.shape
    w = xs.shape[1]
    jj = jnp.arange(block_e.shape[0], dtype=I32)
    active = jj < n_used[0]
    first = (active & ((jj == 0) | (block_e != jnp.roll(block_e, 1)))).astype(I32)
    run = jnp.cumsum(first) - 1
    slot = (run % WEIGHT_SLOTS).astype(I32)
    nbl = block_e.shape[0]
    run_expert = jnp.full((nbl + WEIGHT_SLOTS,), -1, I32).at[jnp.where(first == 1, run, nbl)].set(
        jnp.where(first == 1, block_e, -1))
    run = run.astype(I32)
    step_rows = MOE_STEP_BLOCKS * MOE_ROWS
    last = lambda g, nb: jnp.minimum(g, (nb[0] - 1) // MOE_STEP_BLOCKS)
    row_block = pl.BlockSpec((step_rows, w), lambda g, be, nv, nb, fi, rn, sl, rx: (last(g, nb), 0))
    grid_spec = pltpu.PrefetchScalarGridSpec(
        num_scalar_prefetch=7,
        grid=(n_blocks // MOE_STEP_BLOCKS,),
        in_specs=[row_block,
                  pl.BlockSpec(memory_space=pl.ANY),
                  pl.BlockSpec(memory_space=pl.ANY),
                  pl.BlockSpec(memory_space=pl.ANY)],
        out_specs=row_block,
        scratch_shapes=[pltpu.VMEM((WEIGHT_SLOTS, d, f), F32), pltpu.VMEM((WEIGHT_SLOTS, d, f), F32),
                        pltpu.VMEM((WEIGHT_SLOTS, f, d), F32), pltpu.SemaphoreType.DMA((WEIGHT_SLOTS,))],
    )
    return pl.pallas_call(
        _gmm_kernel,
        out_shape=jax.ShapeDtypeStruct(xs.shape, U32),
        grid_spec=grid_spec,
        compiler_params=_cparams("arbitrary"),
        name="grouped_mlp",
    )(block_e, block_valid, n_used, first, run, slot, run_expert, xs, w_gate, w_up, w_down)


def _combine_kernel(y_ref, xres_ref, wt_ref, g2_ref, fg_ref, o_ref):
    wt = wt_ref[...]
    routed = jnp.zeros(xres_ref.shape, F32)
    for k in range(TOP_K):
        routed = routed + wt[:, k:k + 1] * _unpack_pairs_native(y_ref[k])
    x2 = xres_ref[...] + g2_ref[...] * routed
    ms = jnp.mean(x2 * x2, axis=-1, keepdims=True)
    o_ref[...] = x2 * lax.rsqrt(ms + NORM_EPS) * fg_ref[...]


def _combine_into_kernel(prev_ref, *refs):
    del prev_ref
    _combine_kernel(*refs)


def combine(y_tok, xres, wsel_t, g2, final_g, seq, tm, first_tile, prev_out):
    t, d = xres.shape
    tiles_per_batch = seq // tm
    tile = lambda i: i + first_tile
    in_specs = [pl.BlockSpec((TOP_K, tm, d // 2), lambda i: (0, i, 0)),
                pl.BlockSpec((tm, d), lambda i: (tile(i), 0)),
                pl.BlockSpec((tm, TOP_K), lambda i: (tile(i), 0)),
                pl.BlockSpec((None, 1, d), lambda i: (tile(i) // tiles_per_batch, 0, 0)),
                pl.BlockSpec((1, d), lambda i: (0, 0))]
    args = [y_tok, xres, wsel_t, g2, final_g.reshape(1, d)]
    body, aliases = _combine_kernel, {}
    if prev_out is not None:
        body, aliases = _combine_into_kernel, {0: 0}
        in_specs = [pl.BlockSpec(memory_space=pl.ANY)] + in_specs
        args = [prev_out] + args
    return pl.pallas_call(
        body,
        out_shape=jax.ShapeDtypeStruct((t, d), F32),
        grid=(y_tok.shape[1] // tm,),
        in_specs=in_specs,
        out_specs=pl.BlockSpec((tm, d), lambda i: (tile(i), 0)),
        input_output_aliases=aliases,
        compiler_params=_cparams("arbitrary"),
        name="combine",
    )(*args)


def _pos_tables(rows, cols, dim):
    quarter = dim // 4
    omega = 1.0 / (POS_BASE ** (np.arange(quarter, dtype=np.float32) / quarter))
    ang_r = np.arange(rows, dtype=np.float32)[:, None] * omega
    ang_c = np.arange(cols, dtype=np.float32)[:, None] * omega
    emb_r = np.concatenate([np.sin(ang_r), np.cos(ang_r)], axis=-1).astype(np.float32)
    emb_c = np.concatenate([np.sin(ang_c), np.cos(ang_c)], axis=-1).astype(np.float32)
    return jnp.asarray(emb_r.reshape(rows, 1, dim // 2)), jnp.asarray(emb_c)


def kernel(x, c, ctx, c_ctx, norm1_g, norm2_g, ada_w, ada_b, w_in, hy_conv_w, hy_conv_b, hy_f_w1, hy_f_b1, hy_f_w2, hy_f_b2, hy_f_w3, hy_f_b3, hy_f_w4, hy_f_freq, hy_skip, hg_lb_logits, hg_norm_g, w_hy_out, w_hg_out, w_out, router_w, router_bias, exp_w_gate, exp_w_up, exp_w_down, sh_w_gate, sh_w_up, sh_w_down, final_g):
    bsz, seq, d = x.shape
    n_ctx = ctx.shape[1]
    hy_w = w_hy_out.shape[1]
    hg_w = w_hg_out.shape[1]
    dk = hg_norm_g.shape[1]
    n_heads = hg_w // dk
    ne = router_w.shape[2]
    l = 0

    c_rows = jnp.zeros((SUBLANES, d), F32).at[:bsz].set(c).at[bsz].set(c_ctx)
    mods = ada_vectors(c_rows, ada_w[l], ada_b[l])
    sh1, sc1, g1, sh2, sc2, g2 = [mods[:bsz, j * d:(j + 1) * d].reshape(bsz, 1, d) for j in range(N_ADA)]
    csh1 = jnp.broadcast_to(mods[bsz, 0:d].reshape(1, 1, d), (bsz, 1, d))
    csc1 = jnp.broadcast_to(mods[bsz, d:2 * d].reshape(1, 1, d), (bsz, 1, d))

    emb_r, emb_c = _pos_tables(seq // GRID_W, GRID_W, d)
    w_in_b = w_in[l].astype(BF16)
    hy_proj = 3 * hy_w
    p = in_projection(x, emb_r, emb_c, norm1_g[l], sh1, sc1, w_in_b, TOKEN_TILE)
    hg_cols = slice(hy_proj, hy_proj + 5 * hg_w)
    zero_r = jnp.zeros((n_ctx // GRID_W, 1, d // 2), F32)
    zero_c = jnp.zeros((GRID_W, d // 2), F32)
    pc = in_projection(ctx, zero_r, zero_c, norm1_g[l], csh1, csc1, w_in_b[:, hg_cols], n_ctx)

    lbs = jnp.cumsum(jax.nn.softmax(hg_lb_logits.astype(F32), axis=0), axis=0)
    lb_f, lb_b = lbs[l, 0], lbs[l, 1]
    zero_state = jnp.zeros((bsz, n_heads, dk, dk), F32)
    base = hy_proj // hg_w
    _, st_f = hgrn_scan(pc, (0, 1, 2), lb_f, zero_state, n_ctx, n_ctx, reverse=False)
    _, st_b = hgrn_scan(pc, (0, 1, 3), lb_b, zero_state, n_ctx, n_ctx, reverse=True)
    o_f, _ = hgrn_scan(p, (base, base + 1, base + 2), lb_f, st_f, seq, HG_TIME_BLOCK, reverse=False)
    y_hg, _ = hgrn_scan(p, (base, base + 1, base + 3), lb_b, st_b, seq, HG_TIME_BLOCK, reverse=True,
                        o_fwd=o_f, gate_col=base + 4, norm_g=hg_norm_g[l])

    u = short_conv(p, hy_proj, hy_conv_w[l], hy_conv_b[l], seq, CONV_TILE, hy_w)
    taps, l1 = hyena_filter_taps(seq, hy_f_w1[l], hy_f_b1[l], hy_f_w2[l], hy_f_b2[l], hy_f_w3[l], hy_f_b3[l],
                                 hy_f_w4[l], hy_f_freq[l], hy_w)
    y_hy = hyena_branch(u, bsz, seq, hy_w, taps, l1, hy_skip[l])

    gate_base = (hy_proj + 5 * hg_w) // d
    sh_gu = jnp.concatenate([sh_w_gate[l], sh_w_up[l]], axis=1).astype(BF16)
    xres, h2p, logits_t = merge_stage(
        x, emb_r, emb_c, y_hy, y_hg, p, (gate_base, gate_base + 1),
        w_hy_out[l].astype(BF16), w_hg_out[l].astype(BF16), w_out[l].astype(BF16), g1, norm2_g[l],
        sh2, sc2, g2, router_w[l].T.astype(F32), sh_gu, sh_w_down[l].astype(BF16), TOKEN_TILE)

    t = bsz * seq
    eidx, wsel, rank, counts = route(logits_t, router_bias[l], TOKEN_TILE)
    n_rows = t * TOP_K + ne * (MOE_ROWS - 1)
    n_blocks = pl.cdiv(pl.cdiv(n_rows, MOE_ROWS), MOE_STEP_BLOCKS) * MOE_STEP_BLOCKS
    dest, block_e, block_valid, n_used = dispatch_plan(counts, eidx, rank, TOKEN_TILE, n_blocks)

    dest_flat = dest.reshape(-1)
    xs = scatter_rows(dest_flat, h2p, n_blocks * MOE_ROWS)
    ys = grouped_mlp(block_e.reshape(-1), block_valid.reshape(-1), n_used.reshape(-1)[:1], xs,
                     exp_w_gate[l], exp_w_up[l], exp_w_down[l], n_blocks)
    wsel_t = wsel.T
    out = None
    for h in range(GATHER_SPLIT):
        lo = h * (t // GATHER_SPLIT)
        rng = dest[:, lo:lo + t // GATHER_SPLIT]
        y_tok = gather_rows(rng.reshape(-1), ys).reshape(TOP_K, t // GATHER_SPLIT, d // 2)
        out = combine(y_tok, xres, wsel_t, g2, final_g, seq, TOKEN_TILE, lo // TOKEN_TILE, out)
    return out.reshape(bsz, seq, d)
```

```python
import functools
import math

import numpy as np
import jax
import jax.numpy as jnp
from jax import lax
from jax.experimental import pallas as pl
from jax.experimental.pallas import tpu as pltpu
from jax.experimental.pallas import tpu_sc as plsc

F32 = jnp.float32
BF16 = jnp.bfloat16
U32 = jnp.uint32
I32 = jnp.int32
HIGHEST = lax.Precision.HIGHEST

GRID_W = 64
POS_BASE = 10000.0
NORM_EPS = 1e-6
N_ADA = 6
HY_ORDER = 2
HY_SHORT = 3
HY_DECAY_TARGET = 1e-2
HY_FAST_PCT = 0.3
HY_SLOW_PCT = 1.5
HG_HEADS = 4
HG_CHUNK = 64
N_GROUPS = 8
TOPK_GROUPS = 4
TOP_K = 8
ROUTED_SCALE = 2.5

LANES = 128
SUBLANES = 8
VMEM_LIMIT = 56 * 1024 * 1024

TOKEN_TILE = 512
HG_TIME_BLOCK = 512
HALO_ROWS = 16
CONV_TILE = 2048
DFT_P = 128
FILTER_GROUP = 8
DFT_GROUP = 16
MOE_ROWS = 256
MOE_STEP_BLOCKS = 4
WEIGHT_AHEAD = 5
WEIGHT_SLOTS = WEIGHT_AHEAD + MOE_STEP_BLOCKS
SC_ROWS = 128
GATHER_SPLIT = 4


def _cparams(*sem):
    return pltpu.CompilerParams(dimension_semantics=sem, vmem_limit_bytes=VMEM_LIMIT)


def _dot(a, b):
    return jnp.dot(a, b, preferred_element_type=F32)


def _dot_hi(a, b):
    return jnp.dot(a, b, preferred_element_type=F32, precision=HIGHEST)


def _dot_nt(a, b):
    return lax.dot_general(a, b, (((1,), (1,)), ((), ())), preferred_element_type=F32)


def _dot_tn(a, b):
    return lax.dot_general(a, b, (((0,), (0,)), ((), ())), preferred_element_type=F32)


def _silu(x):
    return x * jax.nn.sigmoid(x)


def _split_bf16(x):
    hi = x.astype(BF16)
    return hi, (x - hi.astype(F32)).astype(BF16)


def _ada_kernel(c_ref, w_ref, b_ref, o_ref):
    o_ref[...] = _dot_hi(_silu(c_ref[...]), w_ref[...]) + b_ref[...]


def ada_vectors(c_rows, ada_w, ada_b):
    r, d = c_rows.shape
    n = ada_w.shape[1]
    bn = 1024
    return pl.pallas_call(
        _ada_kernel,
        out_shape=jax.ShapeDtypeStruct((r, n), F32),
        grid=(n // bn,),
        in_specs=[pl.BlockSpec((r, d), lambda j: (0, 0)),
                  pl.BlockSpec((d, bn), lambda j: (0, j)),
                  pl.BlockSpec((1, bn), lambda j: (0, j))],
        out_specs=pl.BlockSpec((r, bn), lambda j: (0, j)),
        compiler_params=_cparams("arbitrary"),
        name="ada_vectors",
    )(c_rows, ada_w, ada_b.reshape(1, n))


def _inproj_kernel(x_ref, er_ref, ec_ref, g_ref, sh_ref, sc_ref, w_ref, o_ref, *, col_chunk):
    x = x_ref[...]
    rows, gw, d = x.shape
    half = d // 2
    xp = jnp.concatenate([x[:, :, :half] + er_ref[...], x[:, :, half:] + ec_ref[...]], axis=-1)
    xp = xp.reshape(rows * gw, d)
    ms = jnp.mean(xp * xp, axis=-1, keepdims=True)
    y = xp * lax.rsqrt(ms + NORM_EPS) * g_ref[...]
    h = (y * (1.0 + sc_ref[...]) + sh_ref[...]).astype(BF16)
    n = o_ref.shape[1]
    for j in range(n // col_chunk):
        sl = slice(j * col_chunk, (j + 1) * col_chunk)
        o_ref[:, sl] = _dot(h, w_ref[:, sl]).astype(o_ref.dtype)


def in_projection(x, emb_r, emb_c, norm_g, shift, scale, w_bf16, tm):
    b, s, d = x.shape
    n = w_bf16.shape[1]
    rows_per_batch = s // GRID_W
    rt = tm // GRID_W
    tiles_per_batch = rows_per_batch // rt
    x3 = x.reshape(b * rows_per_batch, GRID_W, d)
    col_chunk = 512
    return pl.pallas_call(
        functools.partial(_inproj_kernel, col_chunk=col_chunk),
        out_shape=jax.ShapeDtypeStruct((b * s, n), BF16),
        grid=(b * tiles_per_batch,),
        in_specs=[pl.BlockSpec((rt, GRID_W, d), lambda i: (i, 0, 0)),
                  pl.BlockSpec((rt, 1, d // 2), lambda i: (i % tiles_per_batch, 0, 0)),
                  pl.BlockSpec((GRID_W, d // 2), lambda i: (0, 0)),
                  pl.BlockSpec((1, d), lambda i: (0, 0)),
                  pl.BlockSpec((None, 1, d), lambda i: (i // tiles_per_batch, 0, 0)),
                  pl.BlockSpec((None, 1, d), lambda i: (i // tiles_per_batch, 0, 0)),
                  pl.BlockSpec((d, n), lambda i: (0, 0))],
        out_specs=pl.BlockSpec((tm, n), lambda i: (i, 0)),
        compiler_params=_cparams("arbitrary"),
        name="in_projection",
    )(x3, emb_r, emb_c, norm_g.reshape(1, d), shift, scale, w_bf16)


def _hgrn_kernel(*refs, reverse, n_chunks, final):
    if final:
        (q_ref, i_ref, f_ref, lb_ref, s0_ref, of_ref, gate_ref, ng_ref, o_ref, sfin_ref, s_scr) = refs
    else:
        (q_ref, i_ref, f_ref, lb_ref, s0_ref, o_ref, sfin_ref, s_scr) = refs
    cs = HG_CHUNK
    bsz, n_heads, _, dk = s_scr.shape

    @pl.when(pl.program_id(0) == 0)
    def _():
        s_scr[...] = s0_ref[...]

    row = lax.broadcasted_iota(I32, (cs, cs), 0)
    col = lax.broadcasted_iota(I32, (cs, cs), 1)
    tri = (col >= row) if reverse else (col <= row)
    tri_b = tri.astype(BF16)
    end_row = 0 if reverse else cs - 1
    mid_row = cs // 2 if reverse else cs // 2 - 1

    def chunk_body(bi, ci):
        c = (n_chunks - 1 - ci) if reverse else ci
        rows = slice(c * cs, (c + 1) * cs)
        lb = lb_ref[...]
        f = lb + (1.0 - lb) * jax.nn.sigmoid(f_ref[bi, rows, :].astype(F32))
        lf_hi, lf_lo = _split_bf16(jnp.log(f))
        b_all = _dot(tri_b, lf_hi) + _dot(tri_b, lf_lo)
        k_all = 1.0 - f
        q_all = _silu(q_ref[bi, rows, :].astype(F32))
        for h in range(n_heads):
            sl = slice(h * dk, (h + 1) * dk)
            b = b_all[:, sl]
            q = q_all[:, sl]
            k = k_all[:, sl]
            v = i_ref[bi, rows, sl]
            b_end = b[end_row:end_row + 1]
            b_mid = b[mid_row:mid_row + 1]
            qd = (q * jnp.exp(b - b_mid)).astype(BF16)
            kd = (k * jnp.exp(b_mid - b)).astype(BF16)
            att = jnp.where(tri, _dot_nt(qd, kd), 0.0).astype(BF16)
            st = s_scr[bi, h]
            qe = (q * jnp.exp(b)).astype(BF16)
            o = _dot(jnp.concatenate([qe, att], axis=1), jnp.concatenate([st.astype(BF16), v], axis=0))
            ke = (k * jnp.exp(b_end - b)).astype(BF16)
            dec = jnp.transpose(jnp.broadcast_to(jnp.exp(b_end), (SUBLANES, dk)))[:, 0:1]
            s_scr[bi, h] = st * dec + _dot_tn(ke, v)
            if final:
                o = o + of_ref[bi, rows, sl].astype(F32)
                o = o * lax.rsqrt(jnp.mean(o * o, axis=-1, keepdims=True) + NORM_EPS) * ng_ref[...]
                o = o * _silu(gate_ref[bi, rows, sl].astype(F32))
            o_ref[bi, rows, sl] = o.astype(o_ref.dtype)

    for ci in range(n_chunks):
        for bi in range(bsz):
            chunk_body(bi, ci)
    sfin_ref[...] = s_scr[...]


def hgrn_scan(p, cols, lb, s0, seq, tb, *, reverse, o_fwd=None, gate_col=None, norm_g=None):
    bsz, n_heads, dv, dk = s0.shape
    width = n_heads * dk
    nt = seq // tb
    final = o_fwd is not None
    p3 = p.reshape(bsz, seq, p.shape[1])
    tmap = (lambda t: nt - 1 - t) if reverse else (lambda t: t)
    colspec = lambda cb: pl.BlockSpec((bsz, tb, width), lambda t: (0, tmap(t), cb))
    state = pl.BlockSpec((bsz, n_heads, dv, dk), lambda t: (0, 0, 0, 0))
    in_specs = [colspec(cols[0]), colspec(cols[1]), colspec(cols[2]),
                pl.BlockSpec((1, width), lambda t: (0, 0)), state]
    args = [p3, p3, p3, lb.reshape(1, width), s0]
    if final:
        in_specs += [colspec(0), colspec(gate_col), pl.BlockSpec((1, dk), lambda t: (0, 0))]
        args += [o_fwd.reshape(bsz, seq, width), p3, norm_g.reshape(1, dk)]
    o, s_fin = pl.pallas_call(
        functools.partial(_hgrn_kernel, reverse=reverse, n_chunks=tb // HG_CHUNK, final=final),
        out_shape=(jax.ShapeDtypeStruct((bsz, seq, width), BF16),
                   jax.ShapeDtypeStruct((bsz, n_heads, dv, dk), F32)),
        grid=(nt,),
        in_specs=in_specs,
        out_specs=(colspec(0), state),
        scratch_shapes=[pltpu.VMEM((bsz, n_heads, dv, dk), F32)],
        compiler_params=_cparams("arbitrary"),
        name="hgrn_bwd" if reverse else "hgrn_fwd",
    )(*args)
    return o.reshape(bsz * seq, width), s_fin


def _shortconv_kernel(p_ref, prev_ref, next_ref, w_ref, b_ref, o_ref, *, tiles_per_batch):
    i = pl.program_id(0)
    ti = i % tiles_per_batch
    p = p_ref[...].astype(F32)
    tm, cw = p.shape
    ta = tm // DFT_P
    prev_row = jnp.where(ti == 0, 0.0, prev_ref[HALO_ROWS - 1:HALO_ROWS, :].astype(F32))
    next_row = jnp.where(ti == tiles_per_batch - 1, 0.0, next_ref[0:1, :].astype(F32))
    p3 = jnp.swapaxes(p.reshape(ta, DFT_P, cw), 0, 1)
    arow = lax.broadcasted_iota(I32, (ta, 1), 0)
    prev_edge = jnp.where(arow == 0, prev_row, pltpu.roll(p3[DFT_P - 1], 1, axis=0))
    next_edge = jnp.where(arow == ta - 1, next_row, pltpu.roll(p3[0], ta - 1, axis=0))
    p_prev = jnp.concatenate([prev_edge[None], p3[:-1]], axis=0)
    p_next = jnp.concatenate([p3[1:], next_edge[None]], axis=0)
    u = w_ref[0:1, :] * p_prev + w_ref[1:2, :] * p3 + w_ref[2:3, :] * p_next + b_ref[...]
    o_ref[...] = u.astype(o_ref.dtype)


def short_conv(p, width, conv_w, conv_b, seq, tm, cw):
    t = p.shape[0]
    nt = t // tm
    tiles_per_batch = seq // tm
    sub = tm // HALO_ROWS
    ta = tm // DFT_P
    return pl.pallas_call(
        functools.partial(_shortconv_kernel, tiles_per_batch=tiles_per_batch),
        out_shape=jax.ShapeDtypeStruct((DFT_P, t // DFT_P, width), BF16),
        grid=(nt, width // cw),
        in_specs=[pl.BlockSpec((tm, cw), lambda i, j: (i, j)),
                  pl.BlockSpec((HALO_ROWS, cw), lambda i, j: (jnp.maximum(i * sub - 1, 0), j)),
                  pl.BlockSpec((HALO_ROWS, cw), lambda i, j: (jnp.minimum((i + 1) * sub, t // HALO_ROWS - 1), j)),
                  pl.BlockSpec((HY_SHORT, cw), lambda i, j: (0, j)),
                  pl.BlockSpec((1, cw), lambda i, j: (0, j))],
        out_specs=pl.BlockSpec((DFT_P, ta, cw), lambda i, j: (0, i, j)),
        compiler_params=_cparams("arbitrary", "arbitrary"),
        name="short_conv",
    )(p, p, p, conv_w, conv_b.reshape(1, width))


def _filter_kernel(band_ref, w1t_ref, w1c_ref, w1s_ref, b1_ref, w2_ref, b2_ref, w3_ref, b3_ref,
                   w4f_ref, w4b_ref, fr_ref, delta_ref, k_ref, s_ref, *, seq):
    step = pl.program_id(0)
    gb, q, ncol = k_ref.shape
    half = q // 2
    width = delta_ref.shape[1]
    nrow = gb * q
    nf = gb * half

    def positions(shape, axis):
        r = lax.broadcasted_iota(I32, shape, axis)
        is_bwd = r >= nf
        rr = jnp.where(is_bwd, r - nf, r)
        j = lax.shift_right_logical(rr, int(math.log2(half)))
        a = (rr & (half - 1)) + jnp.where(is_bwd, half, 0)
        n = (a * DFT_P + step * gb + j).astype(F32)
        t = jnp.where(is_bwd, 2.0 * seq - n, n)
        return n, t, t / float(max(seq - 1, 1))

    _, t_l, tn_l = positions((1, nrow), 1)
    ang = (2.0 * math.pi / seq) * t_l * band_ref[...]
    fr = fr_ref[...]
    pre = (w1t_ref[...] * tn_l + _dot_hi(w1c_ref[...], jnp.cos(ang)) - _dot_hi(w1s_ref[...], jnp.sin(ang))
           + b1_ref[...])
    act = jnp.sin(fr * pre)
    act = jnp.sin(fr * (_dot_hi(w2_ref[...], act) + b2_ref[...]))
    act = jnp.sin(fr * (_dot_hi(w3_ref[...], act) + b3_ref[...])).astype(BF16)
    n_s, _, tn_s = positions((nrow, 1), 0)
    delta = jnp.concatenate([delta_ref[...]] * (ncol // width), axis=1)
    hf = _dot_tn(act[:, :nf], w4f_ref[...]) * jnp.exp(-tn_s[:nf] * delta)
    hb = _dot_tn(act[:, nf:], w4b_ref[...]) * jnp.exp(-tn_s[nf:] * delta)
    hb = jnp.where(n_s[nf:] == float(seq), 0.0, hb)
    k_ref[:, :half, :] = hf.reshape(gb, half, ncol).astype(k_ref.dtype)
    k_ref[:, half:, :] = hb.reshape(gb, half, ncol).astype(k_ref.dtype)
    tot = jnp.sum(jnp.abs(hf), axis=0, keepdims=True) + jnp.sum(jnp.abs(hb), axis=0, keepdims=True)

    @pl.when(step == 0)
    def _():
        s_ref[...] = jnp.zeros_like(s_ref)

    s_ref[...] += tot


def hyena_filter_taps(seq, w1, b1, w2, b2, w3, b3, w4, freq, width):
    emb = w1.shape[0]
    hid = w1.shape[1]
    bands = (emb - 1) // 2
    q = 2 * seq // DFT_P
    ncol = HY_ORDER * width
    band = np.linspace(1e-4, bands - 1, bands, dtype=np.float32).reshape(bands, 1)
    min_decay = math.log(HY_DECAY_TARGET) / HY_SLOW_PCT
    max_decay = math.log(HY_DECAY_TARGET) / HY_FAST_PCT
    delta = np.abs(np.linspace(min_decay, max_decay, width, dtype=np.float32)).reshape(1, width)
    w1t = w1.astype(F32).T
    col = lambda v: v.reshape(hid, 1).astype(F32)
    w4r = w4.astype(BF16).reshape(hid, HY_ORDER, 2, width)
    w4f = w4r[:, :, 0, :].reshape(hid, ncol)
    w4b = w4r[:, :, 1, :].reshape(hid, ncol)
    gb = FILTER_GROUP
    const = lambda shape: pl.BlockSpec(shape, lambda i: tuple(0 for _ in shape))
    return pl.pallas_call(
        functools.partial(_filter_kernel, seq=seq),
        out_shape=(jax.ShapeDtypeStruct((DFT_P, q, ncol), BF16),
                   jax.ShapeDtypeStruct((1, ncol), F32)),
        grid=(DFT_P // gb,),
        in_specs=[const((bands, 1)), const((hid, 1)), const((hid, bands)), const((hid, bands)), const((hid, 1)),
                  const((hid, hid)), const((hid, 1)), const((hid, hid)), const((hid, 1)),
                  const((hid, ncol)), const((hid, ncol)), const((hid, 1)), const((1, width))],
        out_specs=(pl.BlockSpec((gb, q, ncol), lambda i: (i, 0, 0)),
                   pl.BlockSpec((1, ncol), lambda i: (0, 0))),
        compiler_params=_cparams("arbitrary"),
        name="hyena_filter",
    )(jnp.asarray(band), w1t[:, 0:1], w1t[:, 1:1 + bands], w1t[:, 1 + bands:1 + 2 * bands], col(b1),
      w2.astype(F32).T, col(b2), w3.astype(F32).T, col(b3), w4f, w4b, col(freq), jnp.asarray(delta))


def _dft_tables(seq):
    p = DFT_P
    n_fft = 2 * seq
    q = n_fft // p
    qh = q // 2
    ka = np.arange(q)
    nn = np.arange(q)[None, :] * p + np.arange(p)[:, None]
    ang = ((ka[None, :, None] * nn[:, None, :]) % n_fft) * (2.0 * np.pi / n_fft)
    mr, mi = np.cos(ang), -np.sin(ang)
    m1c = np.concatenate([np.concatenate([mr[:, :, :qh], -mi[:, :, :qh]], axis=2),
                          np.concatenate([mi[:, :, :qh], mr[:, :, :qh]], axis=2)], axis=1)
    m1r = np.concatenate([mr, mi], axis=1)
    gr = np.swapaxes(mr[:, :, :qh], 1, 2) / n_fft
    gi = -np.swapaxes(mi[:, :, :qh], 1, 2) / n_fft
    m4 = np.concatenate([np.concatenate([gr, -gi], axis=2), np.concatenate([gi, gr], axis=2)], axis=1)
    kb = np.arange(p)
    ang2 = 2.0 * np.pi * ((kb[:, None] * kb[None, :]) % p) / p
    fr, fi = np.cos(ang2), -np.sin(ang2)
    m2 = np.block([[fr, -fi], [fi, fr]])
    m3 = np.block([[fr, fi], [-fi, fr]])
    return tuple(jnp.asarray(m.astype(np.float32).astype(BF16)) for m in (m1c, m1r, m2, m3, m4))


def _store_swapped(o_ref, res):
    g2 = o_ref.shape[0]
    stacked = jnp.stack(res, axis=0)
    for ri in range(2):
        o_ref[:, ri, :, :] = jnp.swapaxes(stacked[:, ri * g2:(ri + 1) * g2, :], 0, 1).astype(o_ref.dtype)


def _swapped_out(g, m, ncols, gb, n_col_blocks=1):
    shape = (m // 2, 2, g, ncols * n_col_blocks)
    if n_col_blocks == 1:
        return shape, pl.BlockSpec((m // 2, 2, gb, ncols), lambda i: (0, 0, i, 0))
    return shape, pl.BlockSpec((m // 2, 2, gb, ncols), lambda i, j: (0, 0, i, j))


def _bmm_kernel(w_ref, x_ref, o_ref, *, shared_w):
    res = []
    for j in range(x_ref.shape[0]):
        w = w_ref[...] if shared_w else w_ref[j]
        res.append(_dot(w, x_ref[j]))
    _store_swapped(o_ref, res)


def batched_left_matmul(w, x, col_block, ncols, name, gb, n_col_blocks=1):
    g, k = x.shape[0], x.shape[1]
    shared = w.ndim == 2
    m = w.shape[-2]
    out_shape, out_spec = _swapped_out(g, m, ncols, gb, n_col_blocks)
    if n_col_blocks == 1:
        grid = (g // gb,)
        wspec = (pl.BlockSpec((m, k), lambda i: (0, 0)) if shared else pl.BlockSpec((gb, m, k), lambda i: (i, 0, 0)))
        xspec = pl.BlockSpec((gb, k, ncols), lambda i: (i, 0, col_block))
        sem = ("arbitrary",)
    else:
        grid = (g // gb, n_col_blocks)
        wspec = (pl.BlockSpec((m, k), lambda i, j: (0, 0)) if shared
                 else pl.BlockSpec((gb, m, k), lambda i, j: (i, 0, 0)))
        xspec = pl.BlockSpec((gb, k, ncols), lambda i, j: (i, 0, j))
        sem = ("arbitrary", "arbitrary")
    out = pl.pallas_call(
        functools.partial(_bmm_kernel, shared_w=shared),
        out_shape=jax.ShapeDtypeStruct(out_shape, BF16),
        grid=grid,
        in_specs=[wspec, xspec],
        out_specs=out_spec,
        compiler_params=_cparams(*sem),
        name=name,
    )(w, x)
    return out.reshape(m // 2, 2 * g, out_shape[3])


def _dft_mid_kernel(m2_ref, m3_ref, x_ref, k_ref, o_ref):
    half = x_ref.shape[1] // 2
    res = []
    for j in range(x_ref.shape[0]):
        xf = _dot(m2_ref[...], x_ref[j])
        kf = _dot(m2_ref[...], k_ref[j])
        xr, xi = xf[:half], xf[half:]
        kr, ki = kf[:half], kf[half:]
        z = jnp.concatenate([xr * kr - xi * ki, xr * ki + xi * kr], axis=0).astype(BF16)
        res.append(_dot(m3_ref[...], z))
    _store_swapped(o_ref, res)


def dft_mid(m2, m3, x, kspec, kcol, ncols):
    g, r = x.shape[0], x.shape[1]
    gb = DFT_GROUP
    out_shape, out_spec = _swapped_out(g, r, ncols, gb)
    out = pl.pallas_call(
        _dft_mid_kernel,
        out_shape=jax.ShapeDtypeStruct(out_shape, BF16),
        grid=(g // gb,),
        in_specs=[pl.BlockSpec((r, r), lambda i: (0, 0)),
                  pl.BlockSpec((r, r), lambda i: (0, 0)),
                  pl.BlockSpec((gb, r, ncols), lambda i: (i, 0, 0)),
                  pl.BlockSpec((gb, r, ncols), lambda i: (i, 0, kcol))],
        out_specs=out_spec,
        compiler_params=_cparams("arbitrary"),
        name="dft_mid",
    )(m2, m3, x, kspec)
    return out.reshape(r // 2, 2 * g, ncols)


def _dft_out_kernel(m4_ref, y_ref, inv_ref, skip_ref, v_ref, mul_ref, *rest, last):
    res = []
    for j in range(y_ref.shape[0]):
        conv = _dot(m4_ref[j], y_ref[j]) * inv_ref[...] + v_ref[j].astype(F32) * skip_ref[...]
        res.append(mul_ref[j].astype(F32) * conv)
    if last:
        (o_ref,) = rest
        o_ref[...] = jnp.swapaxes(jnp.stack(res, axis=0), 0, 1).astype(o_ref.dtype)
    else:
        m1_ref, o_ref, s1_ref = rest
        nxt = []
        for j, r in enumerate(res):
            zb = r.astype(o_ref.dtype)
            o_ref[j] = zb
            nxt.append(_dot(m1_ref[j], zb))
        _store_swapped(s1_ref, nxt)


def dft_out(m4, y, inv_l1, skip, u, v_col, mul, mul_col, ncols, m1c=None):
    g, r = y.shape[0], y.shape[1]
    rows = m4.shape[1]
    gb = DFT_GROUP
    in_specs = [pl.BlockSpec((gb, rows, r), lambda i: (i, 0, 0)),
                pl.BlockSpec((gb, r, ncols), lambda i: (i, 0, 0)),
                pl.BlockSpec((1, ncols), lambda i: (0, 0)),
                pl.BlockSpec((1, ncols), lambda i: (0, 0)),
                pl.BlockSpec((gb, rows, ncols), lambda i: (i, 0, v_col)),
                pl.BlockSpec((gb, rows, ncols), lambda i: (i, 0, mul_col))]
    args = [m4, y, inv_l1, skip, u, mul]
    if m1c is None:
        out_shape = jax.ShapeDtypeStruct((rows, g, ncols), BF16)
        out_specs = pl.BlockSpec((rows, gb, ncols), lambda i: (0, i, 0))
    else:
        m = m1c.shape[1]
        s_shape, s_spec = _swapped_out(g, m, ncols, gb)
        in_specs.append(pl.BlockSpec((gb, m, rows), lambda i: (i, 0, 0)))
        args.append(m1c)
        out_shape = (jax.ShapeDtypeStruct((g, rows, ncols), BF16), jax.ShapeDtypeStruct(s_shape, BF16))
        out_specs = (pl.BlockSpec((gb, rows, ncols), lambda i: (i, 0, 0)), s_spec)
    out = pl.pallas_call(
        functools.partial(_dft_out_kernel, last=m1c is None),
        out_shape=out_shape,
        grid=(g // gb,),
        in_specs=in_specs,
        out_specs=out_specs,
        compiler_params=_cparams("arbitrary"),
        name="dft_out",
    )(*args)
    if m1c is None:
        return out
    z, s1 = out
    return z, s1.reshape(m1c.shape[1] // 2, 2 * g, ncols)


def hyena_branch(u, bsz, seq, width, taps, l1, skip):
    m1c, m1r, m2, m3, m4 = _dft_tables(seq)
    ks1 = batched_left_matmul(m1r, taps, 0, width, "dft_k1", DFT_GROUP, n_col_blocks=HY_ORDER)
    inv_l1 = 1.0 / l1
    src = u
    s1 = batched_left_matmul(m1c, u, 0, width, "dft_s1", DFT_GROUP)
    for order in range(HY_ORDER):
        mid = dft_mid(m2, m3, s1, ks1, order, width)
        last = order == HY_ORDER - 1
        out = dft_out(m4, mid, inv_l1[:, order * width:(order + 1) * width],
                      skip[order].reshape(1, width).astype(F32), src, 0, u, order + 1, width,
                      m1c=None if last else m1c)
        if not last:
            src, s1 = out
    return out.reshape(bsz * seq, width)


def _pack_pairs(x):
    w = x.shape[1] // 2
    u = lax.bitcast_convert_type(x, U32)
    r = (u + U32(0x7FFF) + ((u >> 16) & U32(1))) >> 16
    return r[:, :w] | (r[:, w:] << 16)


def _unpack_pairs(p):
    lo = lax.bitcast_convert_type(p << 16, F32)
    hi = lax.bitcast_convert_type(p & U32(0xFFFF0000), F32)
    return jnp.concatenate([lo, hi], axis=1)


def _pack_pairs_native(x):
    w = x.shape[1] // 2
    return lax.bitcast_convert_type(pltpu.pack_elementwise([x[:, :w], x[:, w:]], packed_dtype=BF16), U32)


def _unpack_pairs_native(p):
    pi = lax.bitcast_convert_type(p, I32)
    halves = [pltpu.unpack_elementwise(pi, index=i, packed_dtype=BF16, unpacked_dtype=F32) for i in range(2)]
    return jnp.concatenate(halves, axis=1)


def _merge_kernel(x_ref, er_ref, ec_ref, yhy_ref, yhg_ref, ghy_ref, ghg_ref, why_ref, whg_ref, wo_ref,
                  g1_ref, n2_ref, sh2_ref, sc2_ref, g2_ref, rwh_ref, rwl_ref, sgu_ref, sd_ref,
                  xres_ref, h2p_ref, lg_ref):
    x = x_ref[...]
    rows, gw, d = x.shape
    half = d // 2
    xp = jnp.concatenate([x[:, :, :half] + er_ref[...], x[:, :, half:] + ec_ref[...]], axis=-1)
    xp = xp.reshape(rows * gw, d)
    m = (jax.nn.sigmoid(ghy_ref[...].astype(F32)) * _dot(yhy_ref[...], why_ref[...])
         + jax.nn.sigmoid(ghg_ref[...].astype(F32)) * _dot(yhg_ref[...], whg_ref[...]))
    x1 = xp + g1_ref[...] * _dot(m.astype(BF16), wo_ref[...])
    ms = jnp.mean(x1 * x1, axis=-1, keepdims=True)
    h2 = x1 * lax.rsqrt(ms + NORM_EPS) * n2_ref[...] * (1.0 + sc2_ref[...]) + sh2_ref[...]
    h_hi, h_lo = _split_bf16(h2)
    lg_ref[...] = _dot_nt(rwh_ref[...], h_hi) + (_dot_nt(rwl_ref[...], h_hi) + _dot_nt(rwh_ref[...], h_lo))
    gu = _dot(h_hi, sgu_ref[...])
    fs = gu.shape[1] // 2
    shared = _dot((_silu(gu[:, :fs]) * gu[:, fs:]).astype(BF16), sd_ref[...])
    xres_ref[...] = x1 + g2_ref[...] * shared
    h2p_ref[...] = _pack_pairs(h2)


def merge_stage(x, emb_r, emb_c, y_hy, y_hg, p, gate_cols, w_hy_out, w_hg_out, w_out, g1, norm2_g,
                sh2, sc2, g2, router_wt, sh_gate_up, sh_down, tm):
    b, s, d = x.shape
    rows_per_batch = s // GRID_W
    rt = tm // GRID_W
    tiles_per_batch = rows_per_batch // rt
    x3 = x.reshape(b * rows_per_batch, GRID_W, d)
    wb = y_hy.shape[1]
    ne = router_wt.shape[0]
    fs2 = sh_gate_up.shape[1]
    rw_hi, rw_lo = _split_bf16(router_wt)
    tok = lambda cb, w: pl.BlockSpec((tm, w), lambda i: (i, cb))
    const = lambda shape: pl.BlockSpec(shape, lambda i: tuple(0 for _ in shape))
    per_b = pl.BlockSpec((None, 1, d), lambda i: (i // tiles_per_batch, 0, 0))
    return pl.pallas_call(
        _merge_kernel,
        out_shape=(jax.ShapeDtypeStruct((b * s, d), F32),
                   jax.ShapeDtypeStruct((b * s, d // 2), U32),
                   jax.ShapeDtypeStruct((ne, b * s), F32)),
        grid=(b * tiles_per_batch,),
        in_specs=[pl.BlockSpec((rt, GRID_W, d), lambda i: (i, 0, 0)),
                  pl.BlockSpec((rt, 1, d // 2), lambda i: (i % tiles_per_batch, 0, 0)),
                  const((GRID_W, d // 2)),
                  tok(0, wb), tok(0, wb), tok(gate_cols[0], d), tok(gate_cols[1], d),
                  const((wb, d)), const((wb, d)), const((d, d)),
                  per_b, const((1, d)), per_b, per_b, per_b,
                  const((ne, d)), const((ne, d)), const((d, fs2)), const((fs2 // 2, d))],
        out_specs=(pl.BlockSpec((tm, d), lambda i: (i, 0)),
                   pl.BlockSpec((tm, d // 2), lambda i: (i, 0)),
                   pl.BlockSpec((ne, tm), lambda i: (0, i))),
        compiler_params=_cparams("arbitrary"),
        name="merge",
    )(x3, emb_r, emb_c, y_hy, y_hg, p, p, w_hy_out, w_hg_out, w_out, g1, norm2_g.reshape(1, d),
      sh2, sc2, g2, rw_hi, rw_lo, sh_gate_up, sh_down)


def _route_kernel(lg_ref, bias_ref, eidx_ref, wsel_ref, rank_ref, cnt_ref, carry):
    ne, tr = lg_ref.shape
    gsz = ne // N_GROUPS
    neg = -jnp.inf

    @pl.when(pl.program_id(0) == 0)
    def _():
        carry[...] = jnp.zeros_like(carry)

    scores = jax.nn.sigmoid(lg_ref[...])
    biased = scores + bias_ref[...]
    riota = lax.broadcasted_iota(I32, (gsz, tr), 0).astype(F32)
    gs = []
    for g in range(N_GROUPS):
        vg = biased[g * gsz:(g + 1) * gsz]
        m1 = jnp.max(vg, axis=0, keepdims=True)
        i1 = jnp.min(jnp.where(vg == m1, riota, float(gsz)), axis=0, keepdims=True)
        m2 = jnp.max(jnp.where(riota == i1, neg, vg), axis=0, keepdims=True)
        gs.append(m1 + m2)
    cur = jnp.concatenate(gs, axis=0)
    giota = lax.broadcasted_iota(I32, (N_GROUPS, tr), 0).astype(F32)
    gsel = jnp.zeros((N_GROUPS, tr), F32)
    for _ in range(TOPK_GROUPS):
        m = jnp.max(cur, axis=0, keepdims=True)
        idx = jnp.min(jnp.where(cur == m, giota, float(N_GROUPS)), axis=0, keepdims=True)
        hit = giota == idx
        gsel = jnp.where(hit, 1.0, gsel)
        cur = jnp.where(hit, neg, cur)
    cur = jnp.concatenate([jnp.where(gsel[g:g + 1] > 0.0, biased[g * gsz:(g + 1) * gsz], neg)
                           for g in range(N_GROUPS)], axis=0)
    eiota = lax.broadcasted_iota(I32, (ne, tr), 0).astype(F32)
    allowed = cur
    idxs, ws = [], []
    for _ in range(TOP_K):
        m = jnp.max(cur, axis=0, keepdims=True)
        idx = jnp.min(jnp.where(cur == m, eiota, float(ne)), axis=0, keepdims=True)
        hit = eiota == idx
        idxs.append(idx)
        ws.append(jnp.sum(jnp.where(hit, scores, 0.0), axis=0, keepdims=True))
        cur = jnp.where(hit, neg, cur)
    chosen = jnp.where((cur == neg) & (allowed > neg), 1.0, 0.0)
    w = jnp.concatenate(ws, axis=0)
    wsel_ref[...] = w / jnp.sum(w, axis=0, keepdims=True) * ROUTED_SCALE
    eidx_ref[...] = jnp.concatenate(idxs, axis=0).astype(I32)
    srow = lax.broadcasted_iota(I32, (tr, tr), 0)
    scol = lax.broadcasted_iota(I32, (tr, tr), 1)
    before = (srow < scol).astype(BF16)
    base = carry[...] + _dot(chosen.astype(BF16), before)
    ranks = [jnp.sum(jnp.where(eiota == idx, base, 0.0), axis=0, keepdims=True) for idx in idxs]
    rank_ref[...] = jnp.concatenate(ranks, axis=0).astype(I32)
    carry[...] += jnp.sum(chosen, axis=1, keepdims=True)
    cnt_ref[...] = carry[...]


def route(logits_t, router_bias, tr):
    ne, t = logits_t.shape
    return pl.pallas_call(
        _route_kernel,
        out_shape=(jax.ShapeDtypeStruct((TOP_K, t), I32),
                   jax.ShapeDtypeStruct((TOP_K, t), F32),
                   jax.ShapeDtypeStruct((TOP_K, t), I32),
                   jax.ShapeDtypeStruct((ne, 1), F32)),
        grid=(t // tr,),
        in_specs=[pl.BlockSpec((ne, tr), lambda i: (0, i)),
                  pl.BlockSpec((ne, 1), lambda i: (0, 0))],
        out_specs=(pl.BlockSpec((TOP_K, tr), lambda i: (0, i)),
                   pl.BlockSpec((TOP_K, tr), lambda i: (0, i)),
                   pl.BlockSpec((TOP_K, tr), lambda i: (0, i)),
                   pl.BlockSpec((ne, 1), lambda i: (0, 0))),
        scratch_shapes=[pltpu.VMEM((ne, 1), F32)],
        compiler_params=_cparams("arbitrary"),
        name="route",
    )(logits_t, router_bias.reshape(ne, 1).astype(F32))


def _dest_kernel(cnt_ref, eidx_ref, rank_ref, dest_ref, be_ref, nv_ref, nb_ref, start_scr):
    ne = cnt_ref.shape[0]
    tr = eidx_ref.shape[1]

    @pl.when(pl.program_id(0) == 0)
    def _():
        cnt = jnp.broadcast_to(cnt_ref[...], (ne, LANES))
        padded = jnp.floor((cnt + float(MOE_ROWS - 1)) / float(MOE_ROWS)) * float(MOE_ROWS)
        r = lax.broadcasted_iota(I32, (ne, ne), 0)
        c = lax.broadcasted_iota(I32, (ne, ne), 1)
        start = _dot_hi((c < r).astype(F32), padded)
        start_scr[...] = start
        end = start[:, 0:1] + padded[:, 0:1]
        used = start[:, 0:1] + cnt[:, 0:1]
        nbl = be_ref.shape[1]
        blk_row = (lax.broadcasted_iota(I32, (1, nbl), 1) * MOE_ROWS).astype(F32)
        total = jnp.max(end, axis=0, keepdims=True)
        last_row = total - float(MOE_ROWS)
        blk_row_c = jnp.minimum(blk_row, last_row)
        e_of = jnp.sum((end <= blk_row_c).astype(F32), axis=0, keepdims=True)
        e_of = jnp.minimum(e_of, float(ne - 1))
        eio = lax.broadcasted_iota(I32, (ne, nbl), 0).astype(F32)
        used_e = jnp.sum(jnp.where(eio == e_of, used, 0.0), axis=0, keepdims=True)
        valid = jnp.clip(used_e - blk_row_c, 0.0, float(MOE_ROWS))
        be_ref[...] = e_of.astype(I32)
        nv_ref[...] = jnp.where(blk_row <= last_row, valid, 0.0).astype(I32)
        nb_ref[...] = jnp.broadcast_to(total / float(MOE_ROWS), nb_ref.shape).astype(I32)

    eiota = lax.broadcasted_iota(I32, (ne, tr), 0)
    start_col = start_scr[:, 0:1]
    rows = []
    for k in range(TOP_K):
        hit = eiota == eidx_ref[k:k + 1, :]
        rows.append(jnp.sum(jnp.where(hit, start_col, 0.0), axis=0, keepdims=True))
    dest_ref[...] = jnp.concatenate(rows, axis=0).astype(I32) + rank_ref[...]


def dispatch_plan(counts, eidx, rank, tr, n_blocks):
    ne = counts.shape[0]
    t = eidx.shape[1]
    nbl = pl.cdiv(n_blocks, LANES) * LANES
    return pl.pallas_call(
        _dest_kernel,
        out_shape=(jax.ShapeDtypeStruct((TOP_K, t), I32),
                   jax.ShapeDtypeStruct((1, nbl), I32),
                   jax.ShapeDtypeStruct((1, nbl), I32),
                   jax.ShapeDtypeStruct((1, LANES), I32)),
        grid=(t // tr,),
        in_specs=[pl.BlockSpec((ne, 1), lambda i: (0, 0)),
                  pl.BlockSpec((TOP_K, tr), lambda i: (0, i)),
                  pl.BlockSpec((TOP_K, tr), lambda i: (0, i))],
        out_specs=(pl.BlockSpec((TOP_K, tr), lambda i: (0, i)),
                   pl.BlockSpec((1, nbl), lambda i: (0, 0)),
                   pl.BlockSpec((1, nbl), lambda i: (0, 0)),
                   pl.BlockSpec((1, LANES), lambda i: (0, 0))),
        scratch_shapes=[pltpu.VMEM((ne, LANES), F32)],
        compiler_params=_cparams("arbitrary"),
        name="dispatch_plan",
    )(counts, eidx, rank)


def _sc_workers():
    info = plsc.get_sparse_core_info()
    return info.num_cores, info.num_cores * info.num_subcores


def scatter_rows(dest_flat, h2p, n_rows):
    t, w = h2p.shape
    n_cores, n_workers = _sc_workers()
    per_worker = t // n_workers
    mesh = plsc.VectorSubcoreMesh(core_axis_name="c", subcore_axis_name="s")

    @functools.partial(
        pl.kernel, mesh=mesh, out_type=jax.ShapeDtypeStruct((n_rows, w), U32),
        scratch_types=[pltpu.VMEM((TOP_K, SC_ROWS), I32), pltpu.VMEM((SC_ROWS, w), U32), pltpu.SemaphoreType.DMA])
    def body(h_hbm, dest_hbm, xs_hbm, idx_v, rows_v, sem):
        base = (lax.axis_index("s") * n_cores + lax.axis_index("c")) * per_worker

        @pl.loop(0, per_worker // SC_ROWS)
        def _(ci):
            off = pl.multiple_of(base + ci * SC_ROWS, SC_ROWS)
            pltpu.sync_copy(h_hbm.at[pl.ds(off, SC_ROWS)], rows_v)
            for k in range(TOP_K):
                pltpu.sync_copy(dest_hbm.at[pl.ds(k * t + off, SC_ROWS)], idx_v.at[k])
            copies = [pltpu.async_copy(rows_v, xs_hbm.at[idx_v.at[k]], sem) for k in range(TOP_K)]
            for c in copies:
                c.wait()

    return body(h2p, dest_flat)


def gather_rows(idx_flat, table):
    n = idx_flat.shape[0]
    w = table.shape[1]
    n_cores, n_workers = _sc_workers()
    per_worker = n // n_workers
    mesh = plsc.VectorSubcoreMesh(core_axis_name="c", subcore_axis_name="s")

    ch = SC_ROWS // 2
    n_chunks = per_worker // ch

    @functools.partial(
        pl.kernel, mesh=mesh, out_type=jax.ShapeDtypeStruct((n, w), table.dtype),
        scratch_types=[pltpu.VMEM((2, ch), I32), pltpu.VMEM((2, ch, w), table.dtype), pltpu.SemaphoreType.DMA((2,))])
    def body(table_hbm, idx_hbm, out_hbm, idx_v, rows_v, sem):
        base = (lax.axis_index("s") * n_cores + lax.axis_index("c")) * per_worker

        def read(b):
            return pltpu.make_async_copy(table_hbm.at[idx_v.at[b]], rows_v.at[b], sem.at[b])

        def start(c, b):
            off = pl.multiple_of(base + c * ch, ch)
            pltpu.sync_copy(idx_hbm.at[pl.ds(off, ch)], idx_v.at[b])
            read(b).start()

        def finish(c, b):
            read(b).wait()
            pltpu.sync_copy(rows_v.at[b], out_hbm.at[pl.ds(pl.multiple_of(base + c * ch, ch), ch)])

        start(0, 0)

        @pl.loop(0, n_chunks, step=2)
        def _(c):
            start(c + 1, 1)
            finish(c, 0)

            @pl.when(c + 2 < n_chunks)
            def _():
                start(c + 2, 0)

            finish(c + 1, 1)

    return body(table, idx_flat)


def _gmm_kernel(be_ref, nv_ref, nb_ref, first_ref, run_ref, slot_ref, rexp_ref, xs_ref, wg_hbm, wu_hbm, wd_hbm,
                y_ref, wg_buf, wu_buf, wd_buf, sem):
    def weight_copies(e, s):
        return (pltpu.make_async_copy(wg_hbm.at[e], wg_buf.at[s], sem.at[s]),
                pltpu.make_async_copy(wu_hbm.at[e], wu_buf.at[s], sem.at[s]),
                pltpu.make_async_copy(wd_hbm.at[e], wd_buf.at[s], sem.at[s]))

    def fetch(run, slot):
        e = rexp_ref[run]

        @pl.when(e >= 0)
        def _():
            for c in weight_copies(e, slot):
                c.start()

    j0 = pl.program_id(0) * MOE_STEP_BLOCKS

    @pl.when(j0 < nb_ref[0])
    def _():
        for sb in range(MOE_STEP_BLOCKS):
            j = j0 + sb

            @pl.when(j == 0)
            def _():
                for a in range(WEIGHT_AHEAD):
                    fetch(a, a)

            @pl.when((j < nb_ref[0]) & (first_ref[j] == 1))
            def _():
                s = slot_ref[j]
                for c in weight_copies(be_ref[j], s):
                    c.wait()
                fetch(run_ref[j] + WEIGHT_AHEAD, (s + WEIGHT_AHEAD) % WEIGHT_SLOTS)

        for sb in range(MOE_STEP_BLOCKS):
            j = j0 + sb
            s = slot_ref[j]
            rows = slice(sb * MOE_ROWS, (sb + 1) * MOE_ROWS)
            x = _unpack_pairs_native(xs_ref[rows, :])
            row = lax.broadcasted_iota(I32, (x.shape[0], 1), 0)
            x = jnp.where(row < nv_ref[j], x, 0.0)
            hmid = _silu(_dot(x, wg_buf[s])) * _dot(x, wu_buf[s])
            y_ref[rows, :] = _pack_pairs_native(_dot(hmid, wd_buf[s]))


def grouped_mlp(block_e, block_valid, n_used, xs, w_gate, w_up, w_down, n_blocks):
    ne, d, f = w_gate.shape
    w = xs.shape[1]
    jj = jnp.arange(block_e.shape[0], dtype=I32)
    active = jj < n_used[0]
    first = (active & ((jj == 0) | (block_e != jnp.roll(block_e, 1)))).astype(I32)
    run = jnp.cumsum(first) - 1
    slot = (run % WEIGHT_SLOTS).astype(I32)
    nbl = block_e.shape[0]
    run_expert = jnp.full((nbl + WEIGHT_SLOTS,), -1, I32).at[jnp.where(first == 1, run, nbl)].set(
        jnp.where(first == 1, block_e, -1))
    run = run.astype(I32)
    step_rows = MOE_STEP_BLOCKS * MOE_ROWS
    last = lambda g, nb: jnp.minimum(g, (nb[0] - 1) // MOE_STEP_BLOCKS)
    row_block = pl.BlockSpec((step_rows, w), lambda g, be, nv, nb, fi, rn, sl, rx: (last(g, nb), 0))
    grid_spec = pltpu.PrefetchScalarGridSpec(
        num_scalar_prefetch=7,
        grid=(n_blocks // MOE_STEP_BLOCKS,),
        in_specs=[row_block,
                  pl.BlockSpec(memory_space=pl.ANY),
                  pl.BlockSpec(memory_space=pl.ANY),
                  pl.BlockSpec(memory_space=pl.ANY)],
        out_specs=row_block,
        scratch_shapes=[pltpu.VMEM((WEIGHT_SLOTS, d, f), F32), pltpu.VMEM((WEIGHT_SLOTS, d, f), F32),
                        pltpu.VMEM((WEIGHT_SLOTS, f, d), F32), pltpu.SemaphoreType.DMA((WEIGHT_SLOTS,))],
    )
    return pl.pallas_call(
        _gmm_kernel,
        out_shape=jax.ShapeDtypeStruct(xs.shape, U32),
        grid_spec=grid_spec,
        compiler_params=_cparams("arbitrary"),
        name="grouped_mlp",
    )(block_e, block_valid, n_used, first, run, slot, run_expert, xs, w_gate, w_up, w_down)


def _combine_kernel(y_ref, xres_ref, wt_ref, g2_ref, fg_ref, o_ref):
    wt = wt_ref[...]
    routed = jnp.zeros(xres_ref.shape, F32)
    for k in range(TOP_K):
        routed = routed + wt[:, k:k + 1] * _unpack_pairs_native(y_ref[k])
    x2 = xres_ref[...] + g2_ref[...] * routed
    ms = jnp.mean(x2 * x2, axis=-1, keepdims=True)
    o_ref[...] = x2 * lax.rsqrt(ms + NORM_EPS) * fg_ref[...]


def _combine_into_kernel(prev_ref, *refs):
    del prev_ref
    _combine_kernel(*refs)


def combine(y_tok, xres, wsel_t, g2, final_g, seq, tm, first_tile, prev_out):
    t, d = xres.shape
    tiles_per_batch = seq // tm
    tile = lambda i: i + first_tile
    in_specs = [pl.BlockSpec((TOP_K, tm, d // 2), lambda i: (0, i, 0)),
                pl.BlockSpec((tm, d), lambda i: (tile(i), 0)),
                pl.BlockSpec((tm, TOP_K), lambda i: (tile(i), 0)),
                pl.BlockSpec((None, 1, d), lambda i: (tile(i) // tiles_per_batch, 0, 0)),
                pl.BlockSpec((1, d), lambda i: (0, 0))]
    args = [y_tok, xres, wsel_t, g2, final_g.reshape(1, d)]
    body, aliases = _combine_kernel, {}
    if prev_out is not None:
        body, aliases = _combine_into_kernel, {0: 0}
        in_specs = [pl.BlockSpec(memory_space=pl.ANY)] + in_specs
        args = [prev_out] + args
    return pl.pallas_call(
        body,
        out_shape=jax.ShapeDtypeStruct((t, d), F32),
        grid=(y_tok.shape[1] // tm,),
        in_specs=in_specs,
        out_specs=pl.BlockSpec((tm, d), lambda i: (tile(i), 0)),
        input_output_aliases=aliases,
        compiler_params=_cparams("arbitrary"),
        name="combine",
    )(*args)


def _pos_tables(rows, cols, dim):
    quarter = dim // 4
    omega = 1.0 / (POS_BASE ** (np.arange(quarter, dtype=np.float32) / quarter))
    ang_r = np.arange(rows, dtype=np.float32)[:, None] * omega
    ang_c = np.arange(cols, dtype=np.float32)[:, None] * omega
    emb_r = np.concatenate([np.sin(ang_r), np.cos(ang_r)], axis=-1).astype(np.float32)
    emb_c = np.concatenate([np.sin(ang_c), np.cos(ang_c)], axis=-1).astype(np.float32)
    return jnp.asarray(emb_r.reshape(rows, 1, dim // 2)), jnp.asarray(emb_c)


def kernel(x, c, ctx, c_ctx, norm1_g, norm2_g, ada_w, ada_b, w_in, hy_conv_w, hy_conv_b, hy_f_w1, hy_f_b1, hy_f_w2, hy_f_b2, hy_f_w3, hy_f_b3, hy_f_w4, hy_f_freq, hy_skip, hg_lb_logits, hg_norm_g, w_hy_out, w_hg_out, w_out, router_w, router_bias, exp_w_gate, exp_w_up, exp_w_down, sh_w_gate, sh_w_up, sh_w_down, final_g):
    bsz, seq, d = x.shape
    n_ctx = ctx.shape[1]
    hy_w = w_hy_out.shape[1]
    hg_w = w_hg_out.shape[1]
    dk = hg_norm_g.shape[1]
    n_heads = hg_w // dk
    ne = router_w.shape[2]
    l = 0

    c_rows = jnp.zeros((SUBLANES, d), F32).at[:bsz].set(c).at[bsz].set(c_ctx)
    mods = ada_vectors(c_rows, ada_w[l], ada_b[l])
    sh1, sc1, g1, sh2, sc2, g2 = [mods[:bsz, j * d:(j + 1) * d].reshape(bsz, 1, d) for j in range(N_ADA)]
    csh1 = jnp.broadcast_to(mods[bsz, 0:d].reshape(1, 1, d), (bsz, 1, d))
    csc1 = jnp.broadcast_to(mods[bsz, d:2 * d].reshape(1, 1, d), (bsz, 1, d))

    emb_r, emb_c = _pos_tables(seq // GRID_W, GRID_W, d)
    w_in_b = w_in[l].astype(BF16)
    hy_proj = 3 * hy_w
    p = in_projection(x, emb_r, emb_c, norm1_g[l], sh1, sc1, w_in_b, TOKEN_TILE)
    hg_cols = slice(hy_proj, hy_proj + 5 * hg_w)
    zero_r = jnp.zeros((n_ctx // GRID_W, 1, d // 2), F32)
    zero_c = jnp.zeros((GRID_W, d // 2), F32)
    pc = in_projection(ctx, zero_r, zero_c, norm1_g[l], csh1, csc1, w_in_b[:, hg_cols], n_ctx)

    lbs = jnp.cumsum(jax.nn.softmax(hg_lb_logits.astype(F32), axis=0), axis=0)
    lb_f, lb_b = lbs[l, 0], lbs[l, 1]
    zero_state = jnp.zeros((bsz, n_heads, dk, dk), F32)
    base = hy_proj // hg_w
    _, st_f = hgrn_scan(pc, (0, 1, 2), lb_f, zero_state, n_ctx, n_ctx, reverse=False)
    _, st_b = hgrn_scan(pc, (0, 1, 3), lb_b, zero_state, n_ctx, n_ctx, reverse=True)
    o_f, _ = hgrn_scan(p, (base, base + 1, base + 2), lb_f, st_f, seq, HG_TIME_BLOCK, reverse=False)
    y_hg, _ = hgrn_scan(p, (base, base + 1, base + 3), lb_b, st_b, seq, HG_TIME_BLOCK, reverse=True,
                        o_fwd=o_f, gate_col=base + 4, norm_g=hg_norm_g[l])

    u = short_conv(p, hy_proj, hy_conv_w[l], hy_conv_b[l], seq, CONV_TILE, hy_w)
    taps, l1 = hyena_filter_taps(seq, hy_f_w1[l], hy_f_b1[l], hy_f_w2[l], hy_f_b2[l], hy_f_w3[l], hy_f_b3[l],
                                 hy_f_w4[l], hy_f_freq[l], hy_w)
    y_hy = hyena_branch(u, bsz, seq, hy_w, taps, l1, hy_skip[l])

    gate_base = (hy_proj + 5 * hg_w) // d
    sh_gu = jnp.concatenate([sh_w_gate[l], sh_w_up[l]], axis=1).astype(BF16)
    xres, h2p, logits_t = merge_stage(
        x, emb_r, emb_c, y_hy, y_hg, p, (gate_base, gate_base + 1),
        w_hy_out[l].astype(BF16), w_hg_out[l].astype(BF16), w_out[l].astype(BF16), g1, norm2_g[l],
        sh2, sc2, g2, router_w[l].T.astype(F32), sh_gu, sh_w_down[l].astype(BF16), TOKEN_TILE)

    t = bsz * seq
    eidx, wsel, rank, counts = route(logits_t, router_bias[l], TOKEN_TILE)
    n_rows = t * TOP_K + ne * (MOE_ROWS - 1)
    n_blocks = pl.cdiv(pl.cdiv(n_rows, MOE_ROWS), MOE_STEP_BLOCKS) * MOE_STEP_BLOCKS
    dest, block_e, block_valid, n_used = dispatch_plan(counts, eidx, rank, TOKEN_TILE, n_blocks)

    dest_flat = dest.reshape(-1)
    xs = scatter_rows(dest_flat, h2p, n_blocks * MOE_ROWS)
    ys = grouped_mlp(block_e.reshape(-1), block_valid.reshape(-1), n_used.reshape(-1)[:1], xs,
                     exp_w_gate[l], exp_w_up[l], exp_w_down[l], n_blocks)
    wsel_t = wsel.T
    out = None
    for h in range(GATHER_SPLIT):
        lo = h * (t // GATHER_SPLIT)
        rng = dest[:, lo:lo + t // GATHER_SPLIT]
        y_tok = gather_rows(rng.reshape(-1), ys).reshape(TOP_K, t // GATHER_SPLIT, d // 2)
        out = combine(y_tok, xres, wsel_t, g2, final_g, seq, TOKEN_TILE, lo // TOKEN_TILE, out)
    return out.reshape(bsz, seq, d)
```
